```python
import jax
import jax.numpy as jnp
from jax import lax
import numpy as np

D_MODEL = 1024
BATCH = 16
SEQ = 256
DEPTH = 4
DEC_BATCH = 2
DEC_SEQ = 1024
PAST_LEN = 512

GRID_W = 64
N_MIXERS = 4
N_LAYERS_A = (DEPTH + 3) // 4
N_LAYERS_B = (DEPTH + 2) // 4
N_LAYERS_C = (DEPTH + 1) // 4
N_LAYERS_D = DEPTH // 4

N_HEADS = 16
QK_NOPE = 64
QK_ROPE = 32
V_HEAD = 64
Q_LORA = 384
KV_LORA = 256
ROPE_THETA = 10000.0
Q_BLOCK = 128
DENSE_KEY_LIMIT = 2048

POOL_WINDOWS = (2, 4, 8, 16)
POOL_GROUP = D_MODEL // len(POOL_WINDOWS)

N_FFT_GROUPS = 4
FFT_GROUP = D_MODEL // N_FFT_GROUPS

D_FF = 2816
CONV_W = 3

EPS = 1e-6

kernel_name = "hybrid_dit_mla_pool_fnet_shortconv_step"


def _rmsnorm(x, w):
    xf = x.astype(jnp.float32)
    y = xf * lax.rsqrt(jnp.mean(xf * xf, axis=-1, keepdims=True) + EPS)
    return (y * w.astype(jnp.float32)).astype(x.dtype)


def _modulate(h, shift, scale):
    return h * (1 + scale[:, None, :]) + shift[:, None, :]


def _dwconv3(x, w):
    xp = jnp.pad(x, ((0, 0), (1, 1), (0, 0)))
    return xp[:, :-2] * w[0] + xp[:, 1:-1] * w[1] + xp[:, 2:] * w[2]


def _axial_rope_tables(rows):
    r, col = jnp.meshgrid(jnp.arange(rows), jnp.arange(GRID_W), indexing="ij")
    r = r.reshape(-1).astype(jnp.float32)
    col = col.reshape(-1).astype(jnp.float32)
    n_freq = QK_ROPE // 4
    inv = ROPE_THETA ** (-jnp.arange(n_freq, dtype=jnp.float32) / n_freq)
    ang = jnp.concatenate([r[:, None] * inv, col[:, None] * inv], axis=-1)
    return jnp.cos(ang), jnp.sin(ang)


def _rope(x, cos, sin):
    xr = x.reshape(x.shape[:-1] + (QK_ROPE // 2, 2))
    x1, x2 = xr[..., 0], xr[..., 1]
    cos = cos.astype(x.dtype)
    sin = sin.astype(x.dtype)
    out = jnp.stack([x1 * cos - x2 * sin, x1 * sin + x2 * cos], axis=-1)
    return out.reshape(x.shape)


def _mla_project(h, P, j):
    B, T, _ = h.shape
    cq = _rmsnorm(h @ P["mla_wdq"][j], P["mla_q_norm"][j])
    q = (cq @ P["mla_wuq"][j]).reshape(B, T, N_HEADS, QK_NOPE + QK_ROPE)
    kv = h @ P["mla_wdkv"][j]
    ckv = _rmsnorm(kv[..., :KV_LORA], P["mla_kv_norm"][j])
    return q[..., :QK_NOPE], q[..., QK_NOPE:], ckv, kv[..., KV_LORA:]


def _mla_expand(ckv, w_ukv):
    B, S, _ = ckv.shape
    kv = (ckv @ w_ukv).reshape(B, S, N_HEADS, QK_NOPE + V_HEAD)
    return kv[..., :QK_NOPE], kv[..., QK_NOPE:]


def _attend_block(q_nope, q_rope, k_nope, k_rope, v):
    s = jnp.einsum("bqhd,bkhd->bhqk", q_nope, k_nope, preferred_element_type=jnp.float32)
    s = s + jnp.einsum("bqhr,bkr->bhqk", q_rope, k_rope, preferred_element_type=jnp.float32)
    p = jax.nn.softmax(s * (1.0 / np.sqrt(QK_NOPE + QK_ROPE)), axis=-1)
    return jnp.einsum("bhqk,bkhd->bqhd", p.astype(v.dtype), v)


def _attention(q_nope, q_rope, k_nope, k_rope, v):
    B, T = q_nope.shape[:2]
    if k_nope.shape[1] < DENSE_KEY_LIMIT or T % Q_BLOCK:
        return _attend_block(q_nope, q_rope, k_nope, k_rope, v)
    nb = T // Q_BLOCK
    qn = q_nope.reshape(B, nb, Q_BLOCK, N_HEADS, QK_NOPE).swapaxes(0, 1)
    qr = q_rope.reshape(B, nb, Q_BLOCK, N_HEADS, QK_ROPE).swapaxes(0, 1)
    out = lax.map(lambda qs: _attend_block(qs[0], qs[1], k_nope, k_rope, v), (qn, qr))
    return out.swapaxes(0, 1).reshape(B, T, N_HEADS, V_HEAD)


def _mla_context(h, P, j):
    B, T, _ = h.shape
    q_nope, q_rope, ckv, k_rope = _mla_project(h, P, j)
    k_nope, v = _mla_expand(ckv, P["mla_wukv"][j])
    o = _attention(q_nope, q_rope, k_nope, k_rope, v)
    return o.reshape(B, T, N_HEADS * V_HEAD) @ P["mla_wo"][j], ckv, k_rope


def _mla_latent(h, P, j, ctx_ckv, ctx_krope, cos, sin):
    B, T, _ = h.shape
    q_nope, q_rope, ckv, k_rope = _mla_project(h, P, j)
    q_rope = _rope(q_rope, cos[:, None, :], sin[:, None, :])
    k_rope = _rope(k_rope, cos, sin)
    ckv_all = jnp.concatenate([ctx_ckv.astype(ckv.dtype), ckv], axis=1)
    krope_all = jnp.concatenate([ctx_krope.astype(k_rope.dtype), k_rope], axis=1)
    k_nope, v = _mla_expand(ckv_all, P["mla_wukv"][j])
    o = _attention(q_nope, q_rope, k_nope, krope_all, v)
    return o.reshape(B, T, N_HEADS * V_HEAD) @ P["mla_wo"][j]


def _pool_mixer(h, w_groups, scale):
    B, T, D = h.shape
    hf = h.astype(jnp.float32)
    cs = jnp.concatenate([jnp.zeros((B, 1, D), jnp.float32), jnp.cumsum(hf, axis=1)], axis=1)
    t = jnp.arange(T)
    outs = []
    for g, w in enumerate(POOL_WINDOWS):
        lo = jnp.clip(t - w // 2, 0, T)
        hi = jnp.clip(t + w - w // 2, 0, T)
        sl = slice(g * POOL_GROUP, (g + 1) * POOL_GROUP)
        seg = cs[:, :, sl]
        cnt = (hi - lo).astype(jnp.float32)[None, :, None]
        outs.append((seg[:, hi] - seg[:, lo]) / cnt - hf[:, :, sl])
    pooled = jnp.stack(outs, axis=2).astype(h.dtype)
    y = jnp.einsum("btgc,gcd->btgd", pooled, w_groups).reshape(B, T, D)
    return y * scale


def _fourier_mixer(h, w_out):
    B, T, D = h.shape
    hg = h.astype(jnp.float32).reshape(B, T, N_FFT_GROUPS, FFT_GROUP)
    f = jnp.fft.fft2(hg, axes=(1, 3), norm="ortho").real
    return f.reshape(B, T, D).astype(h.dtype) @ w_out


def _shortconv_mixer(h, w_in, w_conv, w_out):
    z = h @ w_in
    gb, gc, u = jnp.split(z, 3, axis=-1)
    return (gb * _dwconv3(gc * u, w_conv)) @ w_out


def _conv_ffn(h, w_up, conv_w, conv_b, w_down):
    g, u = jnp.split(h @ w_up, 2, axis=-1)
    g = _dwconv3(g, conv_w) + conv_b
    return (jax.nn.gelu(g, approximate=False) * u) @ w_down


def _layer(P, li, x, cond, latent, ctx_cache, cos, sin):
    mod = jax.nn.silu(cond.astype(jnp.float32)).astype(x.dtype) @ P["ada_w"][li] + P["ada_b"][li]
    sh1, sc1, g1, sh2, sc2, g2 = jnp.split(mod, 6, axis=-1)
    h = _modulate(_rmsnorm(x, P["norm1_w"][li]), sh1, sc1)
    kind, j = li % N_MIXERS, li // N_MIXERS
    new_cache = None
    if kind == 0:
        if latent:
            y = _mla_latent(h, P, j, ctx_cache[0], ctx_cache[1], cos, sin)
        else:
            y, ckv, krope = _mla_context(h, P, j)
            new_cache = (ckv, krope)
    elif kind == 1:
        y = _pool_mixer(h, P["pool_w"][j], P["pool_scale"][j])
    elif kind == 2:
        y = _fourier_mixer(h, P["fnet_w"][j])
    else:
        y = _shortconv_mixer(h, P["sconv_win"][j], P["sconv_conv"][j], P["sconv_wout"][j])
    x = x + g1[:, None, :] * y
    h = _modulate(_rmsnorm(x, P["norm2_w"][li]), sh2, sc2)
    x = x + g2[:, None, :] * _conv_ffn(h, P["ffn_up"][li], P["ffn_conv_w"][li],
                                        P["ffn_conv_b"][li], P["ffn_down"][li])
    return x, new_cache


def setup_inputs(seed: int = 0) -> dict:
    key = jax.random.key(seed)
    ks = iter(jax.random.split(key, 40))

    def nrm(shape, scale):
        return jax.random.normal(next(ks), shape, jnp.float32) * scale

    def gain(shape):
        return 1.0 + nrm(shape, 0.05)

    D = D_MODEL
    H = N_HEADS
    return {
        "x_prompt": nrm((BATCH, SEQ, D), 1.0),
        "x_sample": nrm((DEC_BATCH, DEC_SEQ, D), 1.0),
        "cache_ckv": nrm((DEC_BATCH, N_LAYERS_A, PAST_LEN, KV_LORA), 1.0),
        "cache_krope": nrm((DEC_BATCH, N_LAYERS_A, PAST_LEN, QK_ROPE), 1.0),
        "c": nrm((DEC_BATCH, D), 1.0),
        "c_ctx": nrm((D,), 1.0),
        "norm1_w": gain((DEPTH, D)),
        "norm2_w": gain((DEPTH, D)),
        "ada_w": nrm((DEPTH, D, 6 * D), 0.5 * D ** -0.5),
        "ada_b": nrm((DEPTH, 6 * D), 0.01),
        "mla_wdq": nrm((N_LAYERS_A, D, Q_LORA), D ** -0.5),
        "mla_q_norm": gain((N_LAYERS_A, Q_LORA)),
        "mla_wuq": nrm((N_LAYERS_A, Q_LORA, H * (QK_NOPE + QK_ROPE)), Q_LORA ** -0.5),
        "mla_wdkv": nrm((N_LAYERS_A, D, KV_LORA + QK_ROPE), D ** -0.5),
        "mla_kv_norm": gain((N_LAYERS_A, KV_LORA)),
        "mla_wukv": nrm((N_LAYERS_A, KV_LORA, H * (QK_NOPE + V_HEAD)), KV_LORA ** -0.5),
        "mla_wo": nrm((N_LAYERS_A, H * V_HEAD, D), (H * V_HEAD) ** -0.5),
        "pool_w": nrm((N_LAYERS_B, len(POOL_WINDOWS), POOL_GROUP, POOL_GROUP), POOL_GROUP ** -0.5),
        "pool_scale": gain((N_LAYERS_B, D)),
        "fnet_w": nrm((N_LAYERS_C, D, D), D ** -0.5),
        "sconv_win": nrm((N_LAYERS_D, D, 3 * D), D ** -0.5),
        "sconv_conv": nrm((N_LAYERS_D, CONV_W, D), CONV_W ** -0.5),
        "sconv_wout": nrm((N_LAYERS_D, D, D), D ** -0.5),
        "ffn_up": nrm((DEPTH, D, 2 * D_FF), D ** -0.5),
        "ffn_conv_w": nrm((DEPTH, CONV_W, D_FF), CONV_W ** -0.5),
        "ffn_conv_b": nrm((DEPTH, D_FF), 0.01),
        "ffn_down": nrm((DEPTH, D_FF, D), D_FF ** -0.5),
        "final_norm_w": gain((D,)),
    }


def reference(x_prompt, x_sample, cache_ckv, cache_krope, c, c_ctx,
              norm1_w, norm2_w, ada_w, ada_b,
              mla_wdq, mla_q_norm, mla_wuq, mla_wdkv, mla_kv_norm, mla_wukv, mla_wo,
              pool_w, pool_scale, fnet_w, sconv_win, sconv_conv, sconv_wout,
              ffn_up, ffn_conv_w, ffn_conv_b, ffn_down, final_norm_w):
    P = dict(norm1_w=norm1_w, norm2_w=norm2_w, ada_w=ada_w, ada_b=ada_b,
             mla_wdq=mla_wdq, mla_q_norm=mla_q_norm, mla_wuq=mla_wuq, mla_wdkv=mla_wdkv,
             mla_kv_norm=mla_kv_norm, mla_wukv=mla_wukv, mla_wo=mla_wo,
             pool_w=pool_w, pool_scale=pool_scale, fnet_w=fnet_w,
             sconv_win=sconv_win, sconv_conv=sconv_conv, sconv_wout=sconv_wout,
             ffn_up=ffn_up, ffn_conv_w=ffn_conv_w, ffn_conv_b=ffn_conv_b, ffn_down=ffn_down)

    cond_ctx = jnp.broadcast_to(c_ctx[None, :], (x_prompt.shape[0], D_MODEL))
    xp = x_prompt
    ckv_list, krope_list = [], []
    for li in range(DEPTH):
        xp, cache = _layer(P, li, xp, cond_ctx, False, None, None, None)
        if cache is not None:
            ckv_list.append(cache[0])
            krope_list.append(cache[1])
    y_prompt = _rmsnorm(xp, final_norm_w)
    new_cache_ckv = jnp.stack(ckv_list, axis=1)
    new_cache_krope = jnp.stack(krope_list, axis=1)

    rows = x_sample.shape[1] // GRID_W
    cos, sin = _axial_rope_tables(rows)
    xs = x_sample
    for li in range(DEPTH):
        j = li // N_MIXERS
        ctx = (cache_ckv[:, j], cache_krope[:, j]) if li % N_MIXERS == 0 else None
        xs, _ = _layer(P, li, xs, c, True, ctx, cos, sin)
    y_sample = _rmsnorm(xs, final_norm_w)

    return (y_prompt, y_sample, new_cache_ckv, new_cache_krope)
```

```python
import functools
import math

import numpy as np
import jax
import jax.numpy as jnp
from jax import lax
from jax.experimental import pallas as pl
from jax.experimental.pallas import tpu as pltpu

F32 = jnp.float32
BF16 = jnp.bfloat16

D_MODEL = 1024
N_HEADS = 16
QK_NOPE = 64
QK_ROPE = 32
V_HEAD = 64
Q_LORA = 384
KV_LORA = 256
ROPE_THETA = 10000.0
GRID_W = 64
POOL_WINDOWS = (2, 4, 8, 16)
N_GROUPS = 4
GROUP = D_MODEL // N_GROUPS
D_FF = 2816
EPS = 1e-6
DEPTH = 4

LANES = 128
HEAD_PAIRS = N_HEADS // 2
N_COND = 8
ROW_TILE = 1024
PROJ_TILE = 512
FF_TILE = 256
ADA_TILE = 1536
VMEM_LIMIT = 52 * 1024 * 1024


def _params(sem):
    return pltpu.CompilerParams(dimension_semantics=sem, vmem_limit_bytes=VMEM_LIMIT)


def _dot(a, b):
    return jnp.dot(a, b, preferred_element_type=F32)


def _rms(x, w):
    return x * lax.rsqrt(jnp.mean(x * x, axis=-1, keepdims=True) + EPS) * w


def _norm_mod(x, nw, mod, base):
    return _rms(x, nw) * (1.0 + mod[base + 1:base + 2, :]) + mod[base:base + 1, :]


def _dwconv_rows(g, cw, seq):
    rows = g.shape[0]
    t = lax.broadcasted_iota(jnp.int32, (rows, 1), 0) & (seq - 1)
    prev = jnp.where(t == 0, 0.0, pltpu.roll(g, 1, axis=0))
    nxt = jnp.where(t == seq - 1, 0.0, pltpu.roll(g, rows - 1, axis=0))
    return prev * cw[0:1, :] + g * cw[1:2, :] + nxt * cw[2:3, :]


def _gelu_exact(x):
    return 0.5 * x * (1.0 + lax.erf(x * np.float32(math.sqrt(0.5))))


def _ada_kernel(c_ref, w_ref, b_ref, o_ref):
    s = jax.nn.silu(c_ref[...]).astype(BF16)
    o_ref[...] = _dot(s, w_ref[...].astype(BF16)) + b_ref[...]


def _ada_call(conds, ada_w, ada_b):
    n_out = ada_w.shape[-1]
    return pl.pallas_call(
        _ada_kernel,
        grid=(DEPTH, n_out // ADA_TILE),
        in_specs=[
            pl.BlockSpec((N_COND, D_MODEL), lambda l, j: (0, 0)),
            pl.BlockSpec((None, D_MODEL, ADA_TILE), lambda l, j: (l, 0, j)),
            pl.BlockSpec((None, 1, ADA_TILE), lambda l, j: (l, 0, j)),
        ],
        out_specs=pl.BlockSpec((None, N_COND, ADA_TILE), lambda l, j: (l, 0, j)),
        out_shape=jax.ShapeDtypeStruct((DEPTH, N_COND, n_out), F32),
        compiler_params=_params(("arbitrary", "arbitrary")),
        name="ada",
    )(conds, ada_w, ada_b.reshape(DEPTH, 1, n_out))


def _mod_spec(li, cond_of_row, tile, n_grid):
    if n_grid == 1:
        return pl.BlockSpec((None, None, 6, D_MODEL), lambda i: (li, cond_of_row(i * tile), 0, 0))
    return pl.BlockSpec((None, None, 6, D_MODEL), lambda i, j: (li, cond_of_row(i * tile), 0, 0))


def _ffn_kernel(x_ref, mod_ref, nw_ref, wg_ref, wu_ref, cw_ref, cb_ref, wd_ref, fnw_ref, o_ref,
                h_scr, acc_scr, *, seq, final_norm):
    j = pl.program_id(1)

    @pl.when(j == 0)
    def _():
        h_scr[...] = _norm_mod(x_ref[...], nw_ref[...], mod_ref[...], 3).astype(BF16)
        acc_scr[...] = jnp.zeros_like(acc_scr)

    h = h_scr[...]
    g = _dot(h, wg_ref[...].astype(BF16))
    u = _dot(h, wu_ref[...].astype(BF16))
    g = _dwconv_rows(g, cw_ref[...], seq) + cb_ref[...]
    a = (_gelu_exact(g) * u).astype(BF16)
    acc_scr[...] += _dot(a, wd_ref[...].astype(BF16))

    @pl.when(j == pl.num_programs(1) - 1)
    def _():
        y = x_ref[...] + mod_ref[5:6, :] * acc_scr[...]
        if final_norm:
            y = _rms(y, fnw_ref[...])
        o_ref[...] = y


def _ffn_call(x, mods, li, cond_of_row, seq, P, final_norm):
    m = x.shape[0]
    n_ff = D_FF // FF_TILE
    return pl.pallas_call(
        functools.partial(_ffn_kernel, seq=seq, final_norm=final_norm),
        grid=(m // ROW_TILE, n_ff),
        in_specs=[
            pl.BlockSpec((ROW_TILE, D_MODEL), lambda i, j: (i, 0)),
            _mod_spec(li, cond_of_row, ROW_TILE, 2),
            pl.BlockSpec((None, 1, D_MODEL), lambda i, j: (li, 0, 0)),
            pl.BlockSpec((None, D_MODEL, FF_TILE), lambda i, j: (li, 0, j)),
            pl.BlockSpec((None, D_MODEL, FF_TILE), lambda i, j: (li, 0, n_ff + j)),
            pl.BlockSpec((None, 3, FF_TILE), lambda i, j: (li, 0, j)),
            pl.BlockSpec((None, 1, FF_TILE), lambda i, j: (li, 0, j)),
            pl.BlockSpec((None, FF_TILE, D_MODEL), lambda i, j: (li, j, 0)),
            pl.BlockSpec((1, D_MODEL), lambda i, j: (0, 0)),
        ],
        out_specs=pl.BlockSpec((ROW_TILE, D_MODEL), lambda i, j: (i, 0)),
        out_shape=jax.ShapeDtypeStruct((m, D_MODEL), F32),
        scratch_shapes=[pltpu.VMEM((ROW_TILE, D_MODEL), BF16), pltpu.VMEM((ROW_TILE, D_MODEL), F32)],
        compiler_params=_params(("arbitrary", "arbitrary")),
        name=f"ffn{li}",
    )(x, mods, P["norm2_w"], P["ffn_up"], P["ffn_up"], P["ffn_conv_w"],
      P["ffn_conv_b"].reshape(DEPTH, 1, D_FF), P["ffn_down"], P["final_norm_w"].reshape(1, D_MODEL))


def _sconv_kernel(x_ref, mod_ref, nw_ref, wb_ref, wc_ref, wu_ref, cw_ref, wo_ref, o_ref,
                  h_scr, acc_scr, *, seq):
    j = pl.program_id(1)

    @pl.when(j == 0)
    def _():
        h_scr[...] = _norm_mod(x_ref[...], nw_ref[...], mod_ref[...], 0).astype(BF16)
        acc_scr[...] = jnp.zeros_like(acc_scr)

    h = h_scr[...]
    gb = _dot(h, wb_ref[...].astype(BF16))
    gc = _dot(h, wc_ref[...].astype(BF16))
    u = _dot(h, wu_ref[...].astype(BF16))
    a = (gb * _dwconv_rows(gc * u, cw_ref[...], seq)).astype(BF16)
    acc_scr[...] += _dot(a, wo_ref[...].astype(BF16))

    @pl.when(j == pl.num_programs(1) - 1)
    def _():
        o_ref[...] = x_ref[...] + mod_ref[2:3, :] * acc_scr[...]


def _sconv_call(x, mods, li, cond_of_row, seq, P):
    m = x.shape[0]
    lj = li // 4
    n_c = D_MODEL // FF_TILE
    return pl.pallas_call(
        functools.partial(_sconv_kernel, seq=seq),
        grid=(m // ROW_TILE, n_c),
        in_specs=[
            pl.BlockSpec((ROW_TILE, D_MODEL), lambda i, j: (i, 0)),
            _mod_spec(li, cond_of_row, ROW_TILE, 2),
            pl.BlockSpec((None, 1, D_MODEL), lambda i, j: (li, 0, 0)),
            pl.BlockSpec((None, D_MODEL, FF_TILE), lambda i, j: (lj, 0, j)),
            pl.BlockSpec((None, D_MODEL, FF_TILE), lambda i, j: (lj, 0, n_c + j)),
            pl.BlockSpec((None, D_MODEL, FF_TILE), lambda i, j: (lj, 0, 2 * n_c + j)),
            pl.BlockSpec((None, 3, FF_TILE), lambda i, j: (lj, 0, j)),
            pl.BlockSpec((None, FF_TILE, D_MODEL), lambda i, j: (lj, j, 0)),
        ],
        out_specs=pl.BlockSpec((ROW_TILE, D_MODEL), lambda i, j: (i, 0)),
        out_shape=jax.ShapeDtypeStruct((m, D_MODEL), F32),
        scratch_shapes=[pltpu.VMEM((ROW_TILE, D_MODEL), BF16), pltpu.VMEM((ROW_TILE, D_MODEL), F32)],
        compiler_params=_params(("arbitrary", "arbitrary")),
        name=f"sconv{li}",
    )(x, mods, P["norm1_w"], P["sconv_win"], P["sconv_win"], P["sconv_win"], P["sconv_conv"],
      P["sconv_wout"])


def _emit_kv(ckv, krope_pad, wk_ref, wv_ref, k_ref, v_ref):
    c = ckv.astype(BF16)
    kx = _dot(c, wk_ref[...].astype(BF16))
    vx = _dot(c, wv_ref[...].astype(BF16))
    for h in range(N_HEADS):
        k_ref[h] = (kx[:, h * LANES:(h + 1) * LANES] + krope_pad).astype(BF16)
    for p in range(HEAD_PAIRS):
        v_ref[p] = vx[:, p * LANES:(p + 1) * LANES].astype(BF16)


def _mla_proj_kernel(x_ref, mod_ref, nw_ref, w1_ref, qn_ref, kvn_ref, w2_ref, wk_ref, wv_ref, cos_ref,
                     sin_ref, q_ref, k_ref, v_ref, ckv_ref, kr_ref, *, rope):
    h = _norm_mod(x_ref[...], nw_ref[...], mod_ref[...], 0).astype(BF16)
    z = _dot(h, w1_ref[...].astype(BF16))
    cq = _rms(z[:, :Q_LORA], qn_ref[...]).astype(BF16)
    ckv = _rms(z[:, Q_LORA:Q_LORA + KV_LORA], kvn_ref[...])
    kr = z[:, Q_LORA + KV_LORA:Q_LORA + KV_LORA + LANES]
    if rope:
        cos, sin = cos_ref[...], sin_ref[...]
        kr = kr * cos + z[:, Q_LORA + KV_LORA + LANES:] * sin
    for p in range(HEAD_PAIRS):
        q2 = _dot(cq, w2_ref[:, 2 * p * LANES:(2 * p + 2) * LANES].astype(BF16))
        if rope:
            off = N_HEADS * LANES
            qs = _dot(cq, w2_ref[:, off + 2 * p * LANES:off + (2 * p + 2) * LANES].astype(BF16))
        for e in range(2):
            qh = q2[:, e * LANES:(e + 1) * LANES]
            if rope:
                qh = qh * cos + qs[:, e * LANES:(e + 1) * LANES] * sin
            q_ref[2 * p + e] = qh.astype(BF16)
    ckv_ref[...] = ckv
    kr_ref[...] = kr[:, QK_NOPE:QK_NOPE + QK_ROPE]
    _emit_kv(ckv, kr, wk_ref, wv_ref, k_ref, v_ref)


def _mla_proj_call(x, mods, li, cond_of_row, W, rope):
    m = x.shape[0]
    n1 = W["w1"].shape[1]
    n2 = W["w2"].shape[1]
    const = lambda i: (0, 0)
    pos_tiles = W["cos"].shape[0] // PROJ_TILE
    pos = lambda i: (i % pos_tiles, 0)
    return pl.pallas_call(
        functools.partial(_mla_proj_kernel, rope=rope),
        grid=(m // PROJ_TILE,),
        in_specs=[
            pl.BlockSpec((PROJ_TILE, D_MODEL), lambda i: (i, 0)),
            _mod_spec(li, cond_of_row, PROJ_TILE, 1),
            pl.BlockSpec((None, 1, D_MODEL), lambda i: (li, 0, 0)),
            pl.BlockSpec((D_MODEL, n1), const),
            pl.BlockSpec((1, Q_LORA), const),
            pl.BlockSpec((1, KV_LORA), const),
            pl.BlockSpec((Q_LORA, n2), const),
            pl.BlockSpec((KV_LORA, N_HEADS * LANES), const),
            pl.BlockSpec((KV_LORA, HEAD_PAIRS * LANES), const),
            pl.BlockSpec((PROJ_TILE, LANES), pos),
            pl.BlockSpec((PROJ_TILE, LANES), pos),
        ],
        out_specs=[
            pl.BlockSpec((N_HEADS, PROJ_TILE, LANES), lambda i: (0, i, 0)),
            pl.BlockSpec((N_HEADS, PROJ_TILE, LANES), lambda i: (0, i, 0)),
            pl.BlockSpec((HEAD_PAIRS, PROJ_TILE, LANES), lambda i: (0, i, 0)),
            pl.BlockSpec((PROJ_TILE, KV_LORA), lambda i: (i, 0)),
            pl.BlockSpec((PROJ_TILE, QK_ROPE), lambda i: (i, 0)),
        ],
        out_shape=[
            jax.ShapeDtypeStruct((N_HEADS, m, LANES), BF16),
            jax.ShapeDtypeStruct((N_HEADS, m, LANES), BF16),
            jax.ShapeDtypeStruct((HEAD_PAIRS, m, LANES), BF16),
            jax.ShapeDtypeStruct((m, KV_LORA), F32),
            jax.ShapeDtypeStruct((m, QK_ROPE), F32),
        ],
        compiler_params=_params(("arbitrary",)),
        name="mla_proj_rope" if rope else "mla_proj",
    )(x, mods, W["norm1_w"], W["w1"], W["q_norm"], W["kv_norm"], W["w2"], W["wk"], W["wv"],
      W["cos"], W["sin"])


def _kv_cache_kernel(ckv_ref, kr_ref, wk_ref, wv_ref, k_ref, v_ref):
    _emit_kv(ckv_ref[...], kr_ref[...], wk_ref, wv_ref, k_ref, v_ref)


def _kv_cache_call(ckv, krope_pad, W):
    m = ckv.shape[0]
    const = lambda i: (0, 0)
    return pl.pallas_call(
        _kv_cache_kernel,
        grid=(1,),
        in_specs=[
            pl.BlockSpec((m, KV_LORA), const),
            pl.BlockSpec((m, LANES), const),
            pl.BlockSpec((KV_LORA, N_HEADS * LANES), const),
            pl.BlockSpec((KV_LORA, HEAD_PAIRS * LANES), const),
        ],
        out_specs=[
            pl.BlockSpec((N_HEADS, m, LANES), lambda i: (0, 0, 0)),
            pl.BlockSpec((HEAD_PAIRS, m, LANES), lambda i: (0, 0, 0)),
        ],
        out_shape=[
            jax.ShapeDtypeStruct((N_HEADS, m, LANES), BF16),
            jax.ShapeDtypeStruct((HEAD_PAIRS, m, LANES), BF16),
        ],
        compiler_params=_params(("arbitrary",)),
        name="kv_cache",
    )(ckv, krope_pad, W["wk"], W["wv"])


def _attn_kernel(q_ref, k_ref, v_ref, o_ref, *, n_pairs):
    scale = np.float32(1.0 / math.sqrt(QK_NOPE + QK_ROPE))
    for p in range(n_pairs):
        v = v_ref[p]
        outs = []
        for e in range(2):
            s = lax.dot_general(q_ref[2 * p + e], k_ref[2 * p + e], (((1,), (1,)), ((), ())),
                                preferred_element_type=F32) * scale
            w = jnp.exp(s - jnp.max(s, axis=-1, keepdims=True))
            l = jnp.sum(w, axis=-1, keepdims=True)
            outs.append(_dot(w.astype(BF16), v) / l)
        lane = lax.broadcasted_iota(jnp.int32, outs[0].shape, 1)
        o_ref[:, p * LANES:(p + 1) * LANES] = jnp.where(lane < V_HEAD, outs[0], outs[1]).astype(BF16)


def _attn_call(q, k, v, q_tile, n_pairs):
    _, b, t, _ = q.shape
    s = k.shape[2]
    nq = t // q_tile
    return pl.pallas_call(
        functools.partial(_attn_kernel, n_pairs=n_pairs),
        grid=(b, HEAD_PAIRS // n_pairs, nq),
        in_specs=[
            pl.BlockSpec((2 * n_pairs, None, q_tile, LANES), lambda bi, g, qi: (g, bi, qi, 0)),
            pl.BlockSpec((2 * n_pairs, None, s, LANES), lambda bi, g, qi: (g, bi, 0, 0)),
            pl.BlockSpec((n_pairs, None, s, LANES), lambda bi, g, qi: (g, bi, 0, 0)),
        ],
        out_specs=pl.BlockSpec((q_tile, n_pairs * LANES), lambda bi, g, qi: (bi * nq + qi, g)),
        out_shape=jax.ShapeDtypeStruct((b * t, N_HEADS * V_HEAD), BF16),
        compiler_params=_params(("arbitrary", "arbitrary", "arbitrary")),
        name=f"attn_s{s}",
    )(q, k, v)


def _proj_res_kernel(x_ref, mod_ref, a_ref, w_ref, o_ref):
    o_ref[...] = x_ref[...] + mod_ref[2:3, :] * _dot(a_ref[...], w_ref[...].astype(BF16))


def _proj_res_call(x, mods, li, cond_of_row, a, w, name):
    m = x.shape[0]
    return pl.pallas_call(
        _proj_res_kernel,
        grid=(m // ROW_TILE,),
        in_specs=[
            pl.BlockSpec((ROW_TILE, D_MODEL), lambda i: (i, 0)),
            _mod_spec(li, cond_of_row, ROW_TILE, 1),
            pl.BlockSpec((ROW_TILE, a.shape[1]), lambda i: (i, 0)),
            pl.BlockSpec((None,) + w.shape[1:], lambda i: (0, 0, 0)),
        ],
        out_specs=pl.BlockSpec((ROW_TILE, D_MODEL), lambda i: (i, 0)),
        out_shape=jax.ShapeDtypeStruct((m, D_MODEL), F32),
        compiler_params=_params(("arbitrary",)),
        name=name,
    )(x, mods, a, w)


def _pool_kernel(x_ref, mod_ref, nw_ref, w_ref, sc_ref, o_ref, band_scr, y_scr, *, seq):
    @pl.when(pl.program_id(0) == 0)
    def _():
        r = lax.broadcasted_iota(jnp.int32, (seq, seq), 0)
        c = lax.broadcasted_iota(jnp.int32, (seq, seq), 1)
        for g, win in enumerate(POOL_WINDOWS):
            inside = (c >= r - win // 2) & (c < r + win - win // 2)
            band_scr[g] = jnp.where(inside, 1.0, 0.0).astype(BF16)

    x = x_ref[...]
    h = _norm_mod(x, nw_ref[...], mod_ref[...], 0)
    h_hi = h.astype(BF16)
    h_lo = (h - h_hi.astype(F32)).astype(BF16)
    t = lax.broadcasted_iota(jnp.int32, (seq, 1), 0)
    for g, win in enumerate(POOL_WINDOWS):
        cnt = (jnp.minimum(t + (win - win // 2), seq) - jnp.maximum(t - win // 2, 0)).astype(F32)
        cols = slice(g * GROUP, (g + 1) * GROUP)
        wg = w_ref[g].astype(BF16)
        for s in range(x.shape[0] // seq):
            rows = slice(s * seq, (s + 1) * seq)
            tot = _dot(band_scr[g], h_hi[rows, cols]) + _dot(band_scr[g], h_lo[rows, cols])
            pooled = (tot / cnt - h[rows, cols]).astype(BF16)
            y_scr[rows, cols] = _dot(pooled, wg)
    o_ref[...] = x + mod_ref[2:3, :] * (y_scr[...] * sc_ref[...])


def _pool_call(x, mods, li, cond_of_row, seq, P):
    m = x.shape[0]
    lj = li // 4
    return pl.pallas_call(
        functools.partial(_pool_kernel, seq=seq),
        grid=(m // ROW_TILE,),
        in_specs=[
            pl.BlockSpec((ROW_TILE, D_MODEL), lambda i: (i, 0)),
            _mod_spec(li, cond_of_row, ROW_TILE, 1),
            pl.BlockSpec((None, 1, D_MODEL), lambda i: (li, 0, 0)),
            pl.BlockSpec((None, N_GROUPS, GROUP, GROUP), lambda i: (lj, 0, 0, 0)),
            pl.BlockSpec((None, 1, D_MODEL), lambda i: (lj, 0, 0)),
        ],
        out_specs=pl.BlockSpec((ROW_TILE, D_MODEL), lambda i: (i, 0)),
        out_shape=jax.ShapeDtypeStruct((m, D_MODEL), F32),
        scratch_shapes=[pltpu.VMEM((N_GROUPS, seq, seq), BF16), pltpu.VMEM((ROW_TILE, D_MODEL), F32)],
        compiler_params=_params(("arbitrary",)),
        name=f"pool_t{seq}",
    )(x, mods, P["norm1_w"], P["pool_w"], P["pool_scale"])


def _dft_tables(seq):
    def cs(n):
        k = np.arange(n, dtype=np.int64)
        ang = 2.0 * np.pi * ((k[:, None] * k[None, :]) % n) / n
        return np.cos(ang), np.sin(ang)
    ct, st = cs(seq)
    cc, sc = cs(GROUP)
    e_t = jnp.asarray(np.concatenate([ct, st], axis=0), dtype=F32)
    e_c = jnp.asarray(np.concatenate([cc, -sc], axis=0), dtype=F32)
    return e_t.astype(BF16), e_c.astype(BF16)


def _fourier_kernel(x_ref, mod_ref, nw_ref, et_ref, ec_ref, w_ref, o_ref, f_scr, *, seq):
    x = x_ref[...]
    h = _norm_mod(x, nw_ref[...], mod_ref[...], 0).astype(BF16)
    norm = np.float32(1.0 / math.sqrt(seq * GROUP))
    for s in range(x.shape[0] // seq):
        rows = slice(s * seq, (s + 1) * seq)
        for g in range(N_GROUPS):
            cols = slice(g * GROUP, (g + 1) * GROUP)
            y = _dot(et_ref[...], h[rows, cols]).astype(BF16)
            f = _dot(y[:seq], ec_ref[:GROUP, :]) + _dot(y[seq:], ec_ref[GROUP:, :])
            f_scr[rows, cols] = (f * norm).astype(BF16)
    o_ref[...] = x + mod_ref[2:3, :] * _dot(f_scr[...], w_ref[...].astype(BF16))


def _fourier_call(x, mods, li, cond_of_row, seq, P):
    m = x.shape[0]
    lj = li // 4
    e_t, e_c = _dft_tables(seq)
    return pl.pallas_call(
        functools.partial(_fourier_kernel, seq=seq),
        grid=(m // ROW_TILE,),
        in_specs=[
            pl.BlockSpec((ROW_TILE, D_MODEL), lambda i: (i, 0)),
            _mod_spec(li, cond_of_row, ROW_TILE, 1),
            pl.BlockSpec((None, 1, D_MODEL), lambda i: (li, 0, 0)),
            pl.BlockSpec((2 * seq, seq), lambda i: (0, 0)),
            pl.BlockSpec((2 * GROUP, GROUP), lambda i: (0, 0)),
            pl.BlockSpec((None, D_MODEL, D_MODEL), lambda i: (lj, 0, 0)),
        ],
        out_specs=pl.BlockSpec((ROW_TILE, D_MODEL), lambda i: (i, 0)),
        out_shape=jax.ShapeDtypeStruct((m, D_MODEL), F32),
        scratch_shapes=[pltpu.VMEM((ROW_TILE, D_MODEL), BF16)],
        compiler_params=_params(("arbitrary",)),
        name=f"fourier_t{seq}",
    )(x, mods, P["norm1_w"], e_t, e_c, P["fnet_w"])


def _rope_tables(rows):
    r, col = jnp.meshgrid(jnp.arange(rows), jnp.arange(GRID_W), indexing="ij")
    r = r.reshape(-1).astype(F32)
    col = col.reshape(-1).astype(F32)
    n_freq = QK_ROPE // 4
    inv = ROPE_THETA ** (-jnp.arange(n_freq, dtype=F32) / n_freq)
    ang = jnp.concatenate([r[:, None] * inv, col[:, None] * inv], axis=-1)
    cos = jnp.repeat(jnp.cos(ang), 2, axis=-1)
    sin = jnp.repeat(jnp.sin(ang), 2, axis=-1)
    t = cos.shape[0]
    pad = LANES - QK_NOPE - QK_ROPE
    cos_p = jnp.concatenate([jnp.ones((t, QK_NOPE), F32), cos, jnp.ones((t, pad), F32)], axis=-1)
    sin_p = jnp.concatenate([jnp.zeros((t, QK_NOPE), F32), sin, jnp.zeros((t, pad), F32)], axis=-1)
    return cos_p, sin_p


def _pair_swap(w):
    wr = w.reshape(w.shape[:-1] + (w.shape[-1] // 2, 2))
    return jnp.stack([-wr[..., 1], wr[..., 0]], axis=-1).reshape(w.shape)


def _mla_weights(P, j, rope, seq_rows):
    pad = LANES - QK_NOPE - QK_ROPE
    wuq = P["mla_wuq"][j].reshape(Q_LORA, N_HEADS, QK_NOPE + QK_ROPE)
    q_slots = jnp.pad(wuq, ((0, 0), (0, 0), (0, pad))).reshape(Q_LORA, N_HEADS * LANES)
    wdkv = P["mla_wdkv"][j]
    kr_slot = jnp.pad(wdkv[:, KV_LORA:], ((0, 0), (QK_NOPE, pad)))
    parts1 = [P["mla_wdq"][j], wdkv[:, :KV_LORA], kr_slot]
    parts2 = [q_slots]
    if rope:
        parts1.append(jnp.pad(_pair_swap(wdkv[:, KV_LORA:]), ((0, 0), (QK_NOPE, pad))))
        q_sw = jnp.pad(_pair_swap(wuq[:, :, QK_NOPE:]), ((0, 0), (0, 0), (QK_NOPE, pad)))
        parts2.append(q_sw.reshape(Q_LORA, N_HEADS * LANES))
        cos, sin = _rope_tables(seq_rows // GRID_W)
    else:
        cos = sin = jnp.zeros((PROJ_TILE, LANES), F32)
    wukv = P["mla_wukv"][j].reshape(KV_LORA, N_HEADS, QK_NOPE + V_HEAD)
    wk = jnp.pad(wukv[:, :, :QK_NOPE], ((0, 0), (0, 0), (0, LANES - QK_NOPE)))
    return dict(
        norm1_w=P["norm1_w"], q_norm=P["mla_q_norm"][j][None, :], kv_norm=P["mla_kv_norm"][j][None, :],
        w1=jnp.concatenate(parts1, axis=1), w2=jnp.concatenate(parts2, axis=1),
        wk=wk.reshape(KV_LORA, N_HEADS * LANES),
        wv=wukv[:, :, QK_NOPE:].reshape(KV_LORA, N_HEADS * V_HEAD), cos=cos, sin=sin)


def _mla_layer(x, mods, li, cond_of_row, batch, seq, P, cache):
    j = li // 4
    m = x.shape[0]
    W = _mla_weights(P, j, cache is not None, seq)
    q, k, v, ckv, krope = _mla_proj_call(x, mods, li, cond_of_row, W, cache is not None)
    q = q.reshape(N_HEADS, batch, seq, LANES)
    k = k.reshape(N_HEADS, batch, seq, LANES)
    v = v.reshape(HEAD_PAIRS, batch, seq, LANES)
    if cache is None:
        o = _attn_call(q, k, v, seq, HEAD_PAIRS)
    else:
        c_ckv, c_krope = cache
        past = c_ckv.shape[1]
        pad = LANES - QK_NOPE - QK_ROPE
        kr_pad = jnp.pad(c_krope.reshape(batch * past, QK_ROPE), ((0, 0), (QK_NOPE, pad)))
        kc, vc = _kv_cache_call(c_ckv.reshape(batch * past, KV_LORA), kr_pad, W)
        k = jnp.concatenate([kc.reshape(N_HEADS, batch, past, LANES), k], axis=2)
        v = jnp.concatenate([vc.reshape(HEAD_PAIRS, batch, past, LANES), v], axis=2)
        o = _attn_call(q, k, v, 512, 2)
    x = _proj_res_call(x, mods, li, cond_of_row, o, P["mla_wo"], f"mla_out_{m}")
    return x, ckv, krope


def _run_pass(x, mods, cond_of_row, batch, seq, P, cache):
    ckv = krope = None
    for li in range(DEPTH):
        kind = li % 4
        if kind == 0:
            x, ckv, krope = _mla_layer(x, mods, li, cond_of_row, batch, seq, P, cache)
        elif kind == 1:
            x = _pool_call(x, mods, li, cond_of_row, seq, P)
        elif kind == 2:
            x = _fourier_call(x, mods, li, cond_of_row, seq, P)
        else:
            x = _sconv_call(x, mods, li, cond_of_row, seq, P)
        x = _ffn_call(x, mods, li, cond_of_row, seq, P, final_norm=(li == DEPTH - 1))
    return x, ckv, krope


def kernel(x_prompt, x_sample, cache_ckv, cache_krope, c, c_ctx, norm1_w, norm2_w, ada_w, ada_b, mla_wdq, mla_q_norm, mla_wuq, mla_wdkv, mla_kv_norm, mla_wukv, mla_wo, pool_w, pool_scale, fnet_w, sconv_win, sconv_conv, sconv_wout, ffn_up, ffn_conv_w, ffn_conv_b, ffn_down, final_norm_w):
    row = lambda w: w.reshape(w.shape[0], 1, w.shape[1])
    P = dict(norm1_w=row(norm1_w), norm2_w=row(norm2_w), mla_wdq=mla_wdq, mla_q_norm=mla_q_norm,
             mla_wuq=mla_wuq, mla_wdkv=mla_wdkv, mla_kv_norm=mla_kv_norm, mla_wukv=mla_wukv, mla_wo=mla_wo,
             pool_w=pool_w, pool_scale=row(pool_scale), fnet_w=fnet_w, sconv_win=sconv_win,
             sconv_conv=sconv_conv, sconv_wout=sconv_wout, ffn_up=ffn_up, ffn_conv_w=ffn_conv_w,
             ffn_conv_b=ffn_conv_b, ffn_down=ffn_down, final_norm_w=final_norm_w)
    b_ctx, t_ctx, _ = x_prompt.shape
    b_lat, t_lat, _ = x_sample.shape
    assert ROW_TILE % t_ctx == 0 and t_lat % ROW_TILE == 0 and b_lat + 1 <= N_COND

    conds = jnp.concatenate([c_ctx[None, :], c, jnp.zeros((N_COND - 1 - b_lat, D_MODEL), F32)], axis=0)
    mods = _ada_call(conds, ada_w, ada_b).reshape(DEPTH, N_COND, 6, D_MODEL)

    xp, ckv, krope = _run_pass(x_prompt.reshape(b_ctx * t_ctx, D_MODEL), mods, lambda r: 0,
                               b_ctx, t_ctx, P, None)
    xs, _, _ = _run_pass(x_sample.reshape(b_lat * t_lat, D_MODEL), mods,
                         lambda r: 1 + r // t_lat, b_lat, t_lat, P,
                         (cache_ckv[:, 0], cache_krope[:, 0]))
    return (xp.reshape(b_ctx, t_ctx, D_MODEL), xs.reshape(b_lat, t_lat, D_MODEL),
            ckv.reshape(b_ctx, 1, t_ctx, KV_LORA), krope.reshape(b_ctx, 1, t_ctx, QK_ROPE))
```

```python
import functools
import math

import numpy as np
import jax
import jax.numpy as jnp
from jax import lax
from jax.experimental import pallas as pl
from jax.experimental.pallas import tpu as pltpu

F32 = jnp.float32
BF16 = jnp.bfloat16

D_MODEL = 1024
N_HEADS = 16
QK_NOPE = 64
QK_ROPE = 32
V_HEAD = 64
Q_LORA = 384
KV_LORA = 256
ROPE_THETA = 10000.0
GRID_W = 64
POOL_WINDOWS = (2, 4, 8, 16)
N_GROUPS = 4
GROUP = D_MODEL // N_GROUPS
D_FF = 2816
EPS = 1e-6
DEPTH = 4

LANES = 128
HEAD_PAIRS = N_HEADS // 2
N_COND = 8
ROW_TILE = 1024
PROJ_TILE = 512
FF_TILE = 256
ADA_TILE = 1536
VMEM_LIMIT = 52 * 1024 * 1024


def _params(sem):
    return pltpu.CompilerParams(dimension_semantics=sem, vmem_limit_bytes=VMEM_LIMIT)


def _dot(a, b):
    return jnp.dot(a, b, preferred_element_type=F32)


def _rms(x, w):
    return x * lax.rsqrt(jnp.mean(x * x, axis=-1, keepdims=True) + EPS) * w


def _norm_mod(x, nw, mod, base):
    return _rms(x, nw) * (1.0 + mod[base + 1:base + 2, :]) + mod[base:base + 1, :]


def _dwconv_rows(g, cw, seq):
    rows = g.shape[0]
    t = lax.broadcasted_iota(jnp.int32, (rows, 1), 0) & (seq - 1)
    prev = jnp.where(t == 0, 0.0, pltpu.roll(g, 1, axis=0))
    nxt = jnp.where(t == seq - 1, 0.0, pltpu.roll(g, rows - 1, axis=0))
    return prev * cw[0:1, :] + g * cw[1:2, :] + nxt * cw[2:3, :]


def _gelu_exact(x):
    return 0.5 * x * (1.0 + lax.erf(x * np.float32(math.sqrt(0.5))))


def _ada_kernel(c_ref, w_ref, b_ref, o_ref):
    s = jax.nn.silu(c_ref[...]).astype(BF16)
    o_ref[...] = _dot(s, w_ref[...].astype(BF16)) + b_ref[...]


def _ada_call(conds, ada_w, ada_b):
    n_out = ada_w.shape[-1]
    return pl.pallas_call(
        _ada_kernel,
        grid=(DEPTH, n_out // ADA_TILE),
        in_specs=[
            pl.BlockSpec((N_COND, D_MODEL), lambda l, j: (0, 0)),
            pl.BlockSpec((None, D_MODEL, ADA_TILE), lambda l, j: (l, 0, j)),
            pl.BlockSpec((None, 1, ADA_TILE), lambda l, j: (l, 0, j)),
        ],
        out_specs=pl.BlockSpec((None, N_COND, ADA_TILE), lambda l, j: (l, 0, j)),
        out_shape=jax.ShapeDtypeStruct((DEPTH, N_COND, n_out), F32),
        compiler_params=_params(("arbitrary", "arbitrary")),
        name="ada",
    )(conds, ada_w, ada_b.reshape(DEPTH, 1, n_out))


def _mod_spec(li, cond_of_row, tile, n_grid):
    if n_grid == 1:
        return pl.BlockSpec((None, None, 6, D_MODEL), lambda i: (li, cond_of_row(i * tile), 0, 0))
    return pl.BlockSpec((None, None, 6, D_MODEL), lambda i, j: (li, cond_of_row(i * tile), 0, 0))


def _ffn_act(parts, cw, cb, seq):
    g, u = parts
    return _gelu_exact(_dwconv_rows(g, cw, seq) + cb) * u


def _sconv_act(parts, cw, cb, seq):
    gb, gc, u = parts
    return gb * _dwconv_rows(gc * u, cw, seq)


def _ffn_kernel(x_ref, mod_ref, nw_ref, up_hbm, cw_ref, cb_ref, down_hbm, fnw_ref, o_ref, *scratch, **cfg):
    _glu_body(x_ref, mod_ref, nw_ref, up_hbm, cw_ref, cb_ref, down_hbm, fnw_ref, o_ref, *scratch,
              n_up=2, act=_ffn_act, mod_base=3, **cfg)


def _sconv_kernel(x_ref, mod_ref, nw_ref, up_hbm, cw_ref, down_hbm, o_ref, *scratch, **cfg):
    _glu_body(x_ref, mod_ref, nw_ref, up_hbm, cw_ref, None, down_hbm, None, o_ref, *scratch,
              n_up=3, act=_sconv_act, mod_base=0, final_norm=False, **cfg)


def _glu_body(x_ref, mod_ref, nw_ref, up_hbm, cw_ref, cb_ref, down_hbm, fnw_ref, o_ref,
              h_scr, acc_scr, up0, up1, dn0, dn1, gu0, gu1, sem_up, sem_dn,
              *, layer, n_up, n_chunks, act, seq, mod_base, final_norm):
    up_bufs, dn_bufs, gu = (up0, up1), (dn0, dn1), (gu0, gu1)
    part_cols = n_chunks * FF_TILE

    def aligned(off):
        return off if isinstance(off, int) else pl.multiple_of(off, FF_TILE)

    def up_copy(c, slot, p):
        col = aligned(p * part_cols + c * FF_TILE)
        return pltpu.make_async_copy(up_hbm.at[layer, :, pl.ds(col, FF_TILE)], up_bufs[slot].at[p],
                                     sem_up.at[slot, p])

    def dn_copy(c, slot):
        return pltpu.make_async_copy(down_hbm.at[layer, pl.ds(aligned(c * FF_TILE), FF_TILE), :],
                                     dn_bufs[slot], sem_dn.at[slot])

    def start_up(c, slot):
        for p in range(n_up):
            up_copy(c, slot, p).start()

    def wait_up(c, slot):
        for p in range(n_up):
            up_copy(c, slot, p).wait()

    def up_matmuls(slot):
        h = h_scr[...]
        for p in range(n_up):
            gu[slot][p] = _dot(h, up_bufs[slot][p].astype(BF16))

    def activate(c, slot):
        cb = None if cb_ref is None else cb_ref[c]
        return act([gu[slot][p] for p in range(n_up)], cw_ref[c], cb, seq).astype(BF16)

    def down(a, slot):
        acc_scr[...] += _dot(a, dn_bufs[slot][...].astype(BF16))

    def stage(s, slot, last_up):
        wait_up(s, slot)
        dn_copy(s - 1, 1 - slot).wait()
        if isinstance(last_up, bool):
            if not last_up:
                start_up(s + 1, 1 - slot)
        else:
            pl.when(jnp.logical_not(last_up))(lambda: start_up(s + 1, 1 - slot))
        dn_copy(s, slot).start()
        a = activate(s - 1, 1 - slot)
        up_matmuls(slot)
        down(a, 1 - slot)

    start_up(0, 0)
    dn_copy(0, 0).start()
    h_scr[...] = _norm_mod(x_ref[...], nw_ref[...], mod_ref[...], mod_base).astype(BF16)
    acc_scr[...] = jnp.zeros_like(acc_scr)
    wait_up(0, 0)
    if n_chunks > 1:
        start_up(1, 1)
    up_matmuls(0)

    n_mid = n_chunks - 1

    def pair(i, carry):
        s = 2 * i + 1
        stage(s, 1, False)
        stage(s + 1, 0, s + 2 > n_mid)
        return carry

    lax.fori_loop(0, n_mid // 2, pair, 0)
    if n_mid % 2:
        stage(n_mid, 1, True)
    last = n_chunks - 1
    dn_copy(last, last % 2).wait()
    down(activate(last, last % 2), last % 2)

    y = x_ref[...] + mod_ref[mod_base + 2:mod_base + 3, :] * acc_scr[...]
    if final_norm:
        y = _rms(y, fnw_ref[...])
    o_ref[...] = y


def _chunked(w, n_chunks):
    return w.reshape(w.shape[0], n_chunks, FF_TILE).transpose(1, 0, 2)


def _glu_call(kernel_fn, name, x, mods, li, cond_of_row, n_up, n_chunks, norm_w, operands, specs, cfg):
    m = x.shape[0]
    whole = lambda a: pl.BlockSpec(a.shape, lambda i: (0,) * a.ndim)
    in_specs = [pl.BlockSpec((ROW_TILE, D_MODEL), lambda i: (i, 0)),
                _mod_spec(li, cond_of_row, ROW_TILE, 1),
                pl.BlockSpec((None, 1, D_MODEL), lambda i: (li, 0, 0))]
    in_specs += [pl.BlockSpec(memory_space=pl.ANY) if s == "hbm" else whole(a)
                 for a, s in zip(operands, specs)]
    weights = lambda shape: pltpu.VMEM(shape, F32)
    return pl.pallas_call(
        functools.partial(kernel_fn, n_chunks=n_chunks, **cfg),
        grid=(m // ROW_TILE,),
        in_specs=in_specs,
        out_specs=pl.BlockSpec((ROW_TILE, D_MODEL), lambda i: (i, 0)),
        out_shape=jax.ShapeDtypeStruct((m, D_MODEL), F32),
        scratch_shapes=[
            pltpu.VMEM((ROW_TILE, D_MODEL), BF16),
            pltpu.VMEM((ROW_TILE, D_MODEL), F32),
            weights((n_up, D_MODEL, FF_TILE)), weights((n_up, D_MODEL, FF_TILE)),
            weights((FF_TILE, D_MODEL)), weights((FF_TILE, D_MODEL)),
            pltpu.VMEM((n_up, ROW_TILE, FF_TILE), F32), pltpu.VMEM((n_up, ROW_TILE, FF_TILE), F32),
            pltpu.SemaphoreType.DMA((2, n_up)), pltpu.SemaphoreType.DMA((2,)),
        ],
        compiler_params=_params(("arbitrary",)),
        name=name,
    )(x, mods, norm_w, *operands)


def _ffn_call(x, mods, li, cond_of_row, seq, P, final_norm):
    n = D_FF // FF_TILE
    operands = [P["ffn_up"], _chunked(P["ffn_conv_w"][li], n), _chunked(P["ffn_conv_b"][li][None, :], n),
                P["ffn_down"], P["final_norm_w"].reshape(1, D_MODEL)]
    return _glu_call(_ffn_kernel, f"ffn{li}", x, mods, li, cond_of_row, 2, n, P["norm2_w"],
                     operands, ["hbm", "vmem", "vmem", "hbm", "vmem"],
                     dict(layer=li, seq=seq, final_norm=final_norm))


def _sconv_call(x, mods, li, cond_of_row, seq, P):
    lj = li // 4
    n = D_MODEL // FF_TILE
    operands = [P["sconv_win"], _chunked(P["sconv_conv"][lj], n), P["sconv_wout"]]
    return _glu_call(_sconv_kernel, f"sconv{li}", x, mods, li, cond_of_row, 3, n, P["norm1_w"],
                     operands, ["hbm", "vmem", "hbm"], dict(layer=lj, seq=seq))


def _emit_kv(ckv, krope_pad, wk_ref, wv_ref, k_ref, v_ref):
    c = ckv.astype(BF16)
    kx = _dot(c, wk_ref[...].astype(BF16))
    vx = _dot(c, wv_ref[...].astype(BF16))
    for h in range(N_HEADS):
        k_ref[h] = (kx[:, h * LANES:(h + 1) * LANES] + krope_pad).astype(BF16)
    for p in range(HEAD_PAIRS):
        v_ref[p] = vx[:, p * LANES:(p + 1) * LANES].astype(BF16)


def _mla_proj_kernel(x_ref, mod_ref, nw_ref, w1_ref, qn_ref, kvn_ref, w2_ref, wk_ref, wv_ref, cos_ref,
                     sin_ref, q_ref, k_ref, v_ref, ckv_ref, kr_ref, *, rope):
    h = _norm_mod(x_ref[...], nw_ref[...], mod_ref[...], 0).astype(BF16)
    z = _dot(h, w1_ref[...].astype(BF16))
    cq = _rms(z[:, :Q_LORA], qn_ref[...]).astype(BF16)
    ckv = _rms(z[:, Q_LORA:Q_LORA + KV_LORA], kvn_ref[...])
    kr = z[:, Q_LORA + KV_LORA:Q_LORA + KV_LORA + LANES]
    if rope:
        cos, sin = cos_ref[...], sin_ref[...]
        kr = kr * cos + z[:, Q_LORA + KV_LORA + LANES:] * sin
    for p in range(HEAD_PAIRS):
        q2 = _dot(cq, w2_ref[:, 2 * p * LANES:(2 * p + 2) * LANES].astype(BF16))
        if rope:
            off = N_HEADS * LANES
            qs = _dot(cq, w2_ref[:, off + 2 * p * LANES:off + (2 * p + 2) * LANES].astype(BF16))
        for e in range(2):
            qh = q2[:, e * LANES:(e + 1) * LANES]
            if rope:
                qh = qh * cos + qs[:, e * LANES:(e + 1) * LANES] * sin
            q_ref[2 * p + e] = qh.astype(BF16)
    ckv_ref[...] = ckv
    kr_ref[...] = kr[:, QK_NOPE:QK_NOPE + QK_ROPE]
    _emit_kv(ckv, kr, wk_ref, wv_ref, k_ref, v_ref)


def _mla_proj_call(x, mods, li, cond_of_row, W, rope):
    m = x.shape[0]
    n1 = W["w1"].shape[1]
    n2 = W["w2"].shape[1]
    const = lambda i: (0, 0)
    pos_tiles = W["cos"].shape[0] // PROJ_TILE
    pos = lambda i: (i % pos_tiles, 0)
    return pl.pallas_call(
        functools.partial(_mla_proj_kernel, rope=rope),
        grid=(m // PROJ_TILE,),
        in_specs=[
            pl.BlockSpec((PROJ_TILE, D_MODEL), lambda i: (i, 0)),
            _mod_spec(li, cond_of_row, PROJ_TILE, 1),
            pl.BlockSpec((None, 1, D_MODEL), lambda i: (li, 0, 0)),
            pl.BlockSpec((D_MODEL, n1), const),
            pl.BlockSpec((1, Q_LORA), const),
            pl.BlockSpec((1, KV_LORA), const),
            pl.BlockSpec((Q_LORA, n2), const),
            pl.BlockSpec((KV_LORA, N_HEADS * LANES), const),
            pl.BlockSpec((KV_LORA, HEAD_PAIRS * LANES), const),
            pl.BlockSpec((PROJ_TILE, LANES), pos),
            pl.BlockSpec((PROJ_TILE, LANES), pos),
        ],
        out_specs=[
            pl.BlockSpec((N_HEADS, PROJ_TILE, LANES), lambda i: (0, i, 0)),
            pl.BlockSpec((N_HEADS, PROJ_TILE, LANES), lambda i: (0, i, 0)),
            pl.BlockSpec((HEAD_PAIRS, PROJ_TILE, LANES), lambda i: (0, i, 0)),
            pl.BlockSpec((PROJ_TILE, KV_LORA), lambda i: (i, 0)),
            pl.BlockSpec((PROJ_TILE, QK_ROPE), lambda i: (i, 0)),
        ],
        out_shape=[
            jax.ShapeDtypeStruct((N_HEADS, m, LANES), BF16),
            jax.ShapeDtypeStruct((N_HEADS, m, LANES), BF16),
            jax.ShapeDtypeStruct((HEAD_PAIRS, m, LANES), BF16),
            jax.ShapeDtypeStruct((m, KV_LORA), F32),
            jax.ShapeDtypeStruct((m, QK_ROPE), F32),
        ],
        compiler_params=_params(("arbitrary",)),
        name="mla_proj_rope" if rope else "mla_proj",
    )(x, mods, W["norm1_w"], W["w1"], W["q_norm"], W["kv_norm"], W["w2"], W["wk"], W["wv"],
      W["cos"], W["sin"])


def _kv_cache_kernel(ckv_ref, kr_ref, wk_ref, wv_ref, k_ref, v_ref):
    _emit_kv(ckv_ref[...], kr_ref[...], wk_ref, wv_ref, k_ref, v_ref)


def _kv_cache_call(ckv, krope_pad, W):
    m = ckv.shape[0]
    const = lambda i: (0, 0)
    return pl.pallas_call(
        _kv_cache_kernel,
        grid=(1,),
        in_specs=[
            pl.BlockSpec((m, KV_LORA), const),
            pl.BlockSpec((m, LANES), const),
            pl.BlockSpec((KV_LORA, N_HEADS * LANES), const),
            pl.BlockSpec((KV_LORA, HEAD_PAIRS * LANES), const),
        ],
        out_specs=[
            pl.BlockSpec((N_HEADS, m, LANES), lambda i: (0, 0, 0)),
            pl.BlockSpec((HEAD_PAIRS, m, LANES), lambda i: (0, 0, 0)),
        ],
        out_shape=[
            jax.ShapeDtypeStruct((N_HEADS, m, LANES), BF16),
            jax.ShapeDtypeStruct((HEAD_PAIRS, m, LANES), BF16),
        ],
        compiler_params=_params(("arbitrary",)),
        name="kv_cache",
    )(ckv, krope_pad, W["wk"], W["wv"])


def _attn_kernel(q_ref, k_ref, v_ref, o_ref, *, n_pairs):
    scale = np.float32(1.0 / math.sqrt(QK_NOPE + QK_ROPE))
    for p in range(n_pairs):
        v = v_ref[p]
        outs = []
        for e in range(2):
            s = lax.dot_general(q_ref[2 * p + e], k_ref[2 * p + e], (((1,), (1,)), ((), ())),
                                preferred_element_type=F32) * scale
            w = jnp.exp(s - jnp.max(s, axis=-1, keepdims=True))
            l = jnp.sum(w, axis=-1, keepdims=True)
            outs.append(_dot(w.astype(BF16), v) / l)
        lane = lax.broadcasted_iota(jnp.int32, outs[0].shape, 1)
        o_ref[:, p * LANES:(p + 1) * LANES] = jnp.where(lane < V_HEAD, outs[0], outs[1]).astype(BF16)


def _attn_call(q, k, v, q_tile, n_pairs):
    _, b, t, _ = q.shape
    s = k.shape[2]
    nq = t // q_tile
    return pl.pallas_call(
        functools.partial(_attn_kernel, n_pairs=n_pairs),
        grid=(b, HEAD_PAIRS // n_pairs, nq),
        in_specs=[
            pl.BlockSpec((2 * n_pairs, None, q_tile, LANES), lambda bi, g, qi: (g, bi, qi, 0)),
            pl.BlockSpec((2 * n_pairs, None, s, LANES), lambda bi, g, qi: (g, bi, 0, 0)),
            pl.BlockSpec((n_pairs, None, s, LANES), lambda bi, g, qi: (g, bi, 0, 0)),
        ],
        out_specs=pl.BlockSpec((q_tile, n_pairs * LANES), lambda bi, g, qi: (bi * nq + qi, g)),
        out_shape=jax.ShapeDtypeStruct((b * t, N_HEADS * V_HEAD), BF16),
        compiler_params=_params(("arbitrary", "arbitrary", "arbitrary")),
        name=f"attn_s{s}",
    )(q, k, v)


def _proj_res_kernel(x_ref, mod_ref, a_ref, w_ref, o_ref):
    o_ref[...] = x_ref[...] + mod_ref[2:3, :] * _dot(a_ref[...], w_ref[...].astype(BF16))


def _proj_res_call(x, mods, li, cond_of_row, a, w, name):
    m = x.shape[0]
    return pl.pallas_call(
        _proj_res_kernel,
        grid=(m // ROW_TILE,),
        in_specs=[
            pl.BlockSpec((ROW_TILE, D_MODEL), lambda i: (i, 0)),
            _mod_spec(li, cond_of_row, ROW_TILE, 1),
            pl.BlockSpec((ROW_TILE, a.shape[1]), lambda i: (i, 0)),
            pl.BlockSpec((None,) + w.shape[1:], lambda i: (0, 0, 0)),
        ],
        out_specs=pl.BlockSpec((ROW_TILE, D_MODEL), lambda i: (i, 0)),
        out_shape=jax.ShapeDtypeStruct((m, D_MODEL), F32),
        compiler_params=_params(("arbitrary",)),
        name=name,
    )(x, mods, a, w)


def _pool_kernel(x_ref, mod_ref, nw_ref, w_ref, sc_ref, o_ref, band_scr, y_scr, *, seq):
    @pl.when(pl.program_id(0) == 0)
    def _():
        r = lax.broadcasted_iota(jnp.int32, (seq, seq), 0)
        c = lax.broadcasted_iota(jnp.int32, (seq, seq), 1)
        for g, win in enumerate(POOL_WINDOWS):
            inside = (c >= r - win // 2) & (c < r + win - win // 2)
            band_scr[g] = jnp.where(inside, 1.0, 0.0).astype(BF16)

    x = x_ref[...]
    h = _norm_mod(x, nw_ref[...], mod_ref[...], 0)
    h_hi = h.astype(BF16)
    h_lo = (h - h_hi.astype(F32)).astype(BF16)
    t = lax.broadcasted_iota(jnp.int32, (seq, 1), 0)
    for g, win in enumerate(POOL_WINDOWS):
        cnt = (jnp.minimum(t + (win - win // 2), seq) - jnp.maximum(t - win // 2, 0)).astype(F32)
        cols = slice(g * GROUP, (g + 1) * GROUP)
        wg = w_ref[g].astype(BF16)
        for s in range(x.shape[0] // seq):
            rows = slice(s * seq, (s + 1) * seq)
            tot = _dot(band_scr[g], h_hi[rows, cols]) + _dot(band_scr[g], h_lo[rows, cols])
            pooled = (tot / cnt - h[rows, cols]).astype(BF16)
            y_scr[rows, cols] = _dot(pooled, wg)
    o_ref[...] = x + mod_ref[2:3, :] * (y_scr[...] * sc_ref[...])


def _pool_call(x, mods, li, cond_of_row, seq, P):
    m = x.shape[0]
    lj = li // 4
    return pl.pallas_call(
        functools.partial(_pool_kernel, seq=seq),
        grid=(m // ROW_TILE,),
        in_specs=[
            pl.BlockSpec((ROW_TILE, D_MODEL), lambda i: (i, 0)),
            _mod_spec(li, cond_of_row, ROW_TILE, 1),
            pl.BlockSpec((None, 1, D_MODEL), lambda i: (li, 0, 0)),
            pl.BlockSpec((None, N_GROUPS, GROUP, GROUP), lambda i: (lj, 0, 0, 0)),
            pl.BlockSpec((None, 1, D_MODEL), lambda i: (lj, 0, 0)),
        ],
        out_specs=pl.BlockSpec((ROW_TILE, D_MODEL), lambda i: (i, 0)),
        out_shape=jax.ShapeDtypeStruct((m, D_MODEL), F32),
        scratch_shapes=[pltpu.VMEM((N_GROUPS, seq, seq), BF16), pltpu.VMEM((ROW_TILE, D_MODEL), F32)],
        compiler_params=_params(("arbitrary",)),
        name=f"pool_t{seq}",
    )(x, mods, P["norm1_w"], P["pool_w"], P["pool_scale"])


def _dft_tables(seq):
    def cs(n):
        k = np.arange(n, dtype=np.int64)
        ang = 2.0 * np.pi * ((k[:, None] * k[None, :]) % n) / n
        return np.cos(ang), np.sin(ang)
    ct, st = cs(seq)
    cc, sc = cs(GROUP)
    e_t = jnp.asarray(np.concatenate([ct, st], axis=0), dtype=F32)
    e_c = jnp.asarray(np.concatenate([cc, -sc], axis=0), dtype=F32)
    return e_t.astype(BF16), e_c.astype(BF16)


def _fourier_kernel(x_ref, mod_ref, nw_ref, et_ref, ec_ref, w_ref, o_ref, f_scr, *, seq):
    x = x_ref[...]
    h = _norm_mod(x, nw_ref[...], mod_ref[...], 0).astype(BF16)
    norm = np.float32(1.0 / math.sqrt(seq * GROUP))
    for s in range(x.shape[0] // seq):
        rows = slice(s * seq, (s + 1) * seq)
        for g in range(N_GROUPS):
            cols = slice(g * GROUP, (g + 1) * GROUP)
            y = _dot(et_ref[...], h[rows, cols]).astype(BF16)
            f = _dot(y[:seq], ec_ref[:GROUP, :]) + _dot(y[seq:], ec_ref[GROUP:, :])
            f_scr[rows, cols] = (f * norm).astype(BF16)
    o_ref[...] = x + mod_ref[2:3, :] * _dot(f_scr[...], w_ref[...].astype(BF16))


def _fourier_call(x, mods, li, cond_of_row, seq, P):
    m = x.shape[0]
    lj = li // 4
    e_t, e_c = _dft_tables(seq)
    return pl.pallas_call(
        functools.partial(_fourier_kernel, seq=seq),
        grid=(m // ROW_TILE,),
        in_specs=[
            pl.BlockSpec((ROW_TILE, D_MODEL), lambda i: (i, 0)),
            _mod_spec(li, cond_of_row, ROW_TILE, 1),
            pl.BlockSpec((None, 1, D_MODEL), lambda i: (li, 0, 0)),
            pl.BlockSpec((2 * seq, seq), lambda i: (0, 0)),
            pl.BlockSpec((2 * GROUP, GROUP), lambda i: (0, 0)),
            pl.BlockSpec((None, D_MODEL, D_MODEL), lambda i: (lj, 0, 0)),
        ],
        out_specs=pl.BlockSpec((ROW_TILE, D_MODEL), lambda i: (i, 0)),
        out_shape=jax.ShapeDtypeStruct((m, D_MODEL), F32),
        scratch_shapes=[pltpu.VMEM((ROW_TILE, D_MODEL), BF16)],
        compiler_params=_params(("arbitrary",)),
        name=f"fourier_t{seq}",
    )(x, mods, P["norm1_w"], e_t, e_c, P["fnet_w"])


def _rope_tables(rows):
    r, col = jnp.meshgrid(jnp.arange(rows), jnp.arange(GRID_W), indexing="ij")
    r = r.reshape(-1).astype(F32)
    col = col.reshape(-1).astype(F32)
    n_freq = QK_ROPE // 4
    inv = ROPE_THETA ** (-jnp.arange(n_freq, dtype=F32) / n_freq)
    ang = jnp.concatenate([r[:, None] * inv, col[:, None] * inv], axis=-1)
    cos = jnp.repeat(jnp.cos(ang), 2, axis=-1)
    sin = jnp.repeat(jnp.sin(ang), 2, axis=-1)
    t = cos.shape[0]
    pad = LANES - QK_NOPE - QK_ROPE
    cos_p = jnp.concatenate([jnp.ones((t, QK_NOPE), F32), cos, jnp.ones((t, pad), F32)], axis=-1)
    sin_p = jnp.concatenate([jnp.zeros((t, QK_NOPE), F32), sin, jnp.zeros((t, pad), F32)], axis=-1)
    return cos_p, sin_p


def _pair_swap(w):
    wr = w.reshape(w.shape[:-1] + (w.shape[-1] // 2, 2))
    return jnp.stack([-wr[..., 1], wr[..., 0]], axis=-1).reshape(w.shape)


def _mla_weights(P, j, rope, seq_rows):
    pad = LANES - QK_NOPE - QK_ROPE
    wuq = P["mla_wuq"][j].reshape(Q_LORA, N_HEADS, QK_NOPE + QK_ROPE)
    q_slots = jnp.pad(wuq, ((0, 0), (0, 0), (0, pad))).reshape(Q_LORA, N_HEADS * LANES)
    wdkv = P["mla_wdkv"][j]
    kr_slot = jnp.pad(wdkv[:, KV_LORA:], ((0, 0), (QK_NOPE, pad)))
    parts1 = [P["mla_wdq"][j], wdkv[:, :KV_LORA], kr_slot]
    parts2 = [q_slots]
    if rope:
        parts1.append(jnp.pad(_pair_swap(wdkv[:, KV_LORA:]), ((0, 0), (QK_NOPE, pad))))
        q_sw = jnp.pad(_pair_swap(wuq[:, :, QK_NOPE:]), ((0, 0), (0, 0), (QK_NOPE, pad)))
        parts2.append(q_sw.reshape(Q_LORA, N_HEADS * LANES))
        cos, sin = _rope_tables(seq_rows // GRID_W)
    else:
        cos = sin = jnp.zeros((PROJ_TILE, LANES), F32)
    wukv = P["mla_wukv"][j].reshape(KV_LORA, N_HEADS, QK_NOPE + V_HEAD)
    wk = jnp.pad(wukv[:, :, :QK_NOPE], ((0, 0), (0, 0), (0, LANES - QK_NOPE)))
    return dict(
        norm1_w=P["norm1_w"], q_norm=P["mla_q_norm"][j][None, :], kv_norm=P["mla_kv_norm"][j][None, :],
        w1=jnp.concatenate(parts1, axis=1), w2=jnp.concatenate(parts2, axis=1),
        wk=wk.reshape(KV_LORA, N_HEADS * LANES),
        wv=wukv[:, :, QK_NOPE:].reshape(KV_LORA, N_HEADS * V_HEAD), cos=cos, sin=sin)


def _mla_layer(x, mods, li, cond_of_row, batch, seq, P, cache):
    j = li // 4
    m = x.shape[0]
    W = _mla_weights(P, j, cache is not None, seq)
    q, k, v, ckv, krope = _mla_proj_call(x, mods, li, cond_of_row, W, cache is not None)
    q = q.reshape(N_HEADS, batch, seq, LANES)
    k = k.reshape(N_HEADS, batch, seq, LANES)
    v = v.reshape(HEAD_PAIRS, batch, seq, LANES)
    if cache is None:
        o = _attn_call(q, k, v, seq, HEAD_PAIRS)
    else:
        c_ckv, c_krope = cache
        past = c_ckv.shape[1]
        pad = LANES - QK_NOPE - QK_ROPE
        kr_pad = jnp.pad(c_krope.reshape(batch * past, QK_ROPE), ((0, 0), (QK_NOPE, pad)))
        kc, vc = _kv_cache_call(c_ckv.reshape(batch * past, KV_LORA), kr_pad, W)
        k = jnp.concatenate([kc.reshape(N_HEADS, batch, past, LANES), k], axis=2)
        v = jnp.concatenate([vc.reshape(HEAD_PAIRS, batch, past, LANES), v], axis=2)
        o = _attn_call(q, k, v, 512, 2)
    x = _proj_res_call(x, mods, li, cond_of_row, o, P["mla_wo"], f"mla_out_{m}")
    return x, ckv, krope


def _run_pass(x, mods, cond_of_row, batch, seq, P, cache):
    ckv = krope = None
    for li in range(DEPTH):
        kind = li % 4
        if kind == 0:
            x, ckv, krope = _mla_layer(x, mods, li, cond_of_row, batch, seq, P, cache)
        elif kind == 1:
            x = _pool_call(x, mods, li, cond_of_row, seq, P)
        elif kind == 2:
            x = _fourier_call(x, mods, li, cond_of_row, seq, P)
        else:
            x = _sconv_call(x, mods, li, cond_of_row, seq, P)
        x = _ffn_call(x, mods, li, cond_of_row, seq, P, final_norm=(li == DEPTH - 1))
    return x, ckv, krope


def kernel(x_prompt, x_sample, cache_ckv, cache_krope, c, c_ctx, norm1_w, norm2_w, ada_w, ada_b, mla_wdq, mla_q_norm, mla_wuq, mla_wdkv, mla_kv_norm, mla_wukv, mla_wo, pool_w, pool_scale, fnet_w, sconv_win, sconv_conv, sconv_wout, ffn_up, ffn_conv_w, ffn_conv_b, ffn_down, final_norm_w):
    row = lambda w: w.reshape(w.shape[0], 1, w.shape[1])
    P = dict(norm1_w=row(norm1_w), norm2_w=row(norm2_w), mla_wdq=mla_wdq, mla_q_norm=mla_q_norm,
             mla_wuq=mla_wuq, mla_wdkv=mla_wdkv, mla_kv_norm=mla_kv_norm, mla_wukv=mla_wukv, mla_wo=mla_wo,
             pool_w=pool_w, pool_scale=row(pool_scale), fnet_w=fnet_w, sconv_win=sconv_win,
             sconv_conv=sconv_conv, sconv_wout=sconv_wout, ffn_up=ffn_up, ffn_conv_w=ffn_conv_w,
             ffn_conv_b=ffn_conv_b, ffn_down=ffn_down, final_norm_w=final_norm_w)
    b_ctx, t_ctx, _ = x_prompt.shape
    b_lat, t_lat, _ = x_sample.shape
    assert ROW_TILE % t_ctx == 0 and t_lat % ROW_TILE == 0 and b_lat + 1 <= N_COND

    conds = jnp.concatenate([c_ctx[None, :], c, jnp.zeros((N_COND - 1 - b_lat, D_MODEL), F32)], axis=0)
    mods = _ada_call(conds, ada_w, ada_b).reshape(DEPTH, N_COND, 6, D_MODEL)

    xp, ckv, krope = _run_pass(x_prompt.reshape(b_ctx * t_ctx, D_MODEL), mods, lambda r: 0,
                               b_ctx, t_ctx, P, None)
    xs, _, _ = _run_pass(x_sample.reshape(b_lat * t_lat, D_MODEL), mods,
                         lambda r: 1 + r // t_lat, b_lat, t_lat, P,
                         (cache_ckv[:, 0], cache_krope[:, 0]))
    return (xp.reshape(b_ctx, t_ctx, D_MODEL), xs.reshape(b_lat, t_lat, D_MODEL),
            ckv.reshape(b_ctx, 1, t_ctx, KV_LORA), krope.reshape(b_ctx, 1, t_ctx, QK_ROPE))
```

```python
import functools
import math

import numpy as np
import jax
import jax.numpy as jnp
from jax import lax
from jax.experimental import pallas as pl
from jax.experimental.pallas import tpu as pltpu

F32 = jnp.float32
BF16 = jnp.bfloat16

D_MODEL = 1024
N_HEADS = 16
QK_NOPE = 64
QK_ROPE = 32
V_HEAD = 64
Q_LORA = 384
KV_LORA = 256
ROPE_THETA = 10000.0
GRID_W = 64
POOL_WINDOWS = (2, 4, 8, 16)
N_GROUPS = 4
GROUP = D_MODEL // N_GROUPS
D_FF = 2816
EPS = 1e-6
DEPTH = 4

LANES = 128
HEAD_PAIRS = N_HEADS // 2
N_COND = 8
ROW_TILE = 1024
PROJ_TILE = 512
FF_TILE = 256
ROW_BLOCK = 256
HALO = 8
ADA_TILE = 1536
VMEM_LIMIT = 52 * 1024 * 1024


def _params(sem):
    return pltpu.CompilerParams(dimension_semantics=sem, vmem_limit_bytes=VMEM_LIMIT)


def _dot(a, b):
    return jnp.dot(a, b, preferred_element_type=F32)


def _rms(x, w):
    return x * lax.rsqrt(jnp.mean(x * x, axis=-1, keepdims=True) + EPS) * w


def _norm_mod(x, nw, mod, base):
    return _rms(x, nw) * (1.0 + mod[base + 1:base + 2, :]) + mod[base:base + 1, :]


def _dwconv_rows(g, cw, seq):
    rows = g.shape[0]
    t = lax.broadcasted_iota(jnp.int32, (rows, 1), 0) & (seq - 1)
    prev = jnp.where(t == 0, 0.0, pltpu.roll(g, 1, axis=0))
    nxt = jnp.where(t == seq - 1, 0.0, pltpu.roll(g, rows - 1, axis=0))
    return prev * cw[0:1, :] + g * cw[1:2, :] + nxt * cw[2:3, :]


def _gelu_exact(x):
    return 0.5 * x * (1.0 + lax.erf(x * np.float32(math.sqrt(0.5))))


def _ada_kernel(c_ref, w_ref, b_ref, o_ref):
    s = jax.nn.silu(c_ref[...]).astype(BF16)
    o_ref[...] = _dot(s, w_ref[...].astype(BF16)) + b_ref[...]


def _ada_call(conds, ada_w, ada_b):
    n_out = ada_w.shape[-1]
    return pl.pallas_call(
        _ada_kernel,
        grid=(DEPTH, n_out // ADA_TILE),
        in_specs=[
            pl.BlockSpec((N_COND, D_MODEL), lambda l, j: (0, 0)),
            pl.BlockSpec((None, D_MODEL, ADA_TILE), lambda l, j: (l, 0, j)),
            pl.BlockSpec((None, 1, ADA_TILE), lambda l, j: (l, 0, j)),
        ],
        out_specs=pl.BlockSpec((None, N_COND, ADA_TILE), lambda l, j: (l, 0, j)),
        out_shape=jax.ShapeDtypeStruct((DEPTH, N_COND, n_out), F32),
        compiler_params=_params(("arbitrary", "arbitrary")),
        name="ada",
    )(conds, ada_w, ada_b.reshape(DEPTH, 1, n_out))


def _mod_spec(li, cond_of_row, tile, n_grid):
    if n_grid == 1:
        return pl.BlockSpec((None, None, 6, D_MODEL), lambda i: (li, cond_of_row(i * tile), 0, 0))
    return pl.BlockSpec((None, None, 6, D_MODEL), lambda i, j: (li, cond_of_row(i * tile), 0, 0))


def _conv_window(w, cw, t0, seq):
    n = w.shape[0]
    mid = slice(HALO, HALO + ROW_BLOCK)
    t = (t0 + lax.broadcasted_iota(jnp.int32, (ROW_BLOCK, 1), 0)) & (seq - 1)
    prev = jnp.where(t == 0, 0.0, pltpu.roll(w, 1, axis=0)[mid])
    nxt = jnp.where(t == seq - 1, 0.0, pltpu.roll(w, n - 1, axis=0)[mid])
    return prev * cw[0:1, :] + w[mid] * cw[1:2, :] + nxt * cw[2:3, :]


def _ffn_act(win, mid, cw, cb, t0, seq):
    return _gelu_exact(_conv_window(win(0), cw, t0, seq) + cb) * mid(1)


def _sconv_act(win, mid, cw, cb, t0, seq):
    return mid(0) * _conv_window(win(1) * win(2), cw, t0, seq)


def _ffn_kernel(x_ref, mod_ref, nw_ref, up_hbm, cw_ref, cb_ref, down_hbm, fnw_ref, o_ref, *scratch, **cfg):
    _glu_body(x_ref, mod_ref, nw_ref, up_hbm, cw_ref, cb_ref, down_hbm, fnw_ref, o_ref, *scratch,
              n_up=2, act=_ffn_act, mod_base=3, **cfg)


def _sconv_kernel(x_ref, mod_ref, nw_ref, up_hbm, cw_ref, down_hbm, o_ref, *scratch, **cfg):
    _glu_body(x_ref, mod_ref, nw_ref, up_hbm, cw_ref, None, down_hbm, None, o_ref, *scratch,
              n_up=3, act=_sconv_act, mod_base=0, final_norm=False, **cfg)


def _glu_body(x_ref, mod_ref, nw_ref, up_hbm, cw_ref, cb_ref, down_hbm, fnw_ref, o_ref,
              h_scr, acc_scr, up0, up1, dst0, dst1, dn0, dn1, gu0, gu1, sem_up, sem_dn,
              *, layer, n_up, n_chunks, act, seq, mod_base, final_norm):
    up_bufs, dn_stage, dn_bufs, gu = (up0, up1), (dst0, dst1), (dn0, dn1), (gu0, gu1)
    n_blocks = ROW_TILE // ROW_BLOCK
    dn_rows = FF_TILE // n_blocks
    assert n_chunks >= 2

    def up_copy(c, slot):
        return pltpu.make_async_copy(up_hbm.at[layer, c], up_bufs[slot], sem_up.at[slot])

    def dn_copy(c, slot):
        row = c * FF_TILE if isinstance(c, int) else pl.multiple_of(c * FF_TILE, FF_TILE)
        return pltpu.make_async_copy(down_hbm.at[layer, pl.ds(row, FF_TILE), :], dn_stage[slot],
                                     sem_dn.at[slot])

    def fetch(c, slot):
        up_copy(c, slot).start()
        dn_copy(c, slot).start()

    def arrive(c, slot):
        up_copy(c, slot).wait()
        dn_copy(c, slot).wait()

    def up(slot, r):
        rows = slice(r * ROW_BLOCK, (r + 1) * ROW_BLOCK)
        res = _dot(h_scr[rows, :], up_bufs[slot][...])
        for p in range(n_up):
            gu[slot][p, HALO + r * ROW_BLOCK:HALO + (r + 1) * ROW_BLOCK, :] = res[:, p * FF_TILE:(p + 1) * FF_TILE]

    def activate(c, slot, r):
        win = lambda p: gu[slot][p, r * ROW_BLOCK:(r + 1) * ROW_BLOCK + 2 * HALO, :]
        mid = lambda p: gu[slot][p, HALO + r * ROW_BLOCK:HALO + (r + 1) * ROW_BLOCK, :]
        cb = None if cb_ref is None else cb_ref[c]
        return act(win, mid, cw_ref[c], cb, r * ROW_BLOCK, seq).astype(BF16)

    def down(a, slot, r):
        rows = slice(r * ROW_BLOCK, (r + 1) * ROW_BLOCK)
        acc_scr[rows, :] += _dot(a, dn_bufs[slot][...])

    def cast_down(slot, r):
        rows = slice(r * dn_rows, (r + 1) * dn_rows)
        dn_bufs[slot][rows, :] = dn_stage[slot][rows, :].astype(BF16)

    def stage(s, slot, more):
        arrive(s, slot)
        if more:
            fetch(s + 1, 1 - slot)
        for r in range(n_blocks):
            a = activate(s - 1, 1 - slot, r)
            up(slot, r)
            down(a, 1 - slot, r)
            cast_down(slot, r)

    fetch(0, 0)
    h_scr[...] = _norm_mod(x_ref[...], nw_ref[...], mod_ref[...], mod_base).astype(BF16)
    acc_scr[...] = jnp.zeros_like(acc_scr)
    for g in gu:
        g[:, :HALO, :] = jnp.zeros((n_up, HALO, FF_TILE), F32)
        g[:, HALO + ROW_TILE:, :] = jnp.zeros((n_up, HALO, FF_TILE), F32)

    arrive(0, 0)
    fetch(1, 1)
    for r in range(n_blocks):
        up(0, r)
        cast_down(0, r)

    n_pairs = (n_chunks - 2) // 2

    def pair(i, carry):
        s = 2 * i + 1
        stage(s, 1, True)
        stage(s + 1, 0, True)
        return carry

    lax.fori_loop(0, n_pairs, pair, 0)
    for s in range(2 * n_pairs + 1, n_chunks):
        stage(s, s % 2, s + 1 < n_chunks)
    last = n_chunks - 1
    for r in range(n_blocks):
        down(activate(last, last % 2, r), last % 2, r)

    y = x_ref[...] + mod_ref[mod_base + 2:mod_base + 3, :] * acc_scr[...]
    if final_norm:
        y = _rms(y, fnw_ref[...])
    o_ref[...] = y


def _chunked(w, n_chunks):
    return w.reshape(w.shape[0], n_chunks, FF_TILE).transpose(1, 0, 2)


def _glu_call(kernel_fn, name, x, mods, li, cond_of_row, n_up, n_chunks, norm_w, operands, specs, cfg):
    m = x.shape[0]
    whole = lambda a: pl.BlockSpec(a.shape, lambda i: (0,) * a.ndim)
    in_specs = [pl.BlockSpec((ROW_TILE, D_MODEL), lambda i: (i, 0)),
                _mod_spec(li, cond_of_row, ROW_TILE, 1),
                pl.BlockSpec((None, 1, D_MODEL), lambda i: (li, 0, 0))]
    in_specs += [pl.BlockSpec(memory_space=pl.ANY) if s == "hbm" else whole(a)
                 for a, s in zip(operands, specs)]
    gu_shape = (n_up, ROW_TILE + 2 * HALO, FF_TILE)
    return pl.pallas_call(
        functools.partial(kernel_fn, n_chunks=n_chunks, **cfg),
        grid=(m // ROW_TILE,),
        in_specs=in_specs,
        out_specs=pl.BlockSpec((ROW_TILE, D_MODEL), lambda i: (i, 0)),
        out_shape=jax.ShapeDtypeStruct((m, D_MODEL), F32),
        scratch_shapes=[
            pltpu.VMEM((ROW_TILE, D_MODEL), BF16),
            pltpu.VMEM((ROW_TILE, D_MODEL), F32),
            pltpu.VMEM((D_MODEL, n_up * FF_TILE), BF16), pltpu.VMEM((D_MODEL, n_up * FF_TILE), BF16),
            pltpu.VMEM((FF_TILE, D_MODEL), F32), pltpu.VMEM((FF_TILE, D_MODEL), F32),
            pltpu.VMEM((FF_TILE, D_MODEL), BF16), pltpu.VMEM((FF_TILE, D_MODEL), BF16),
            pltpu.VMEM(gu_shape, F32), pltpu.VMEM(gu_shape, F32),
            pltpu.SemaphoreType.DMA((2,)), pltpu.SemaphoreType.DMA((2,)),
        ],
        compiler_params=_params(("arbitrary",)),
        name=name,
    )(x, mods, norm_w, *operands)


def _chunk_major(w, n_up):
    l, d, cols = w.shape
    n_chunks = cols // (n_up * FF_TILE)
    w = w.reshape(l, d, n_up, n_chunks, FF_TILE).transpose(0, 3, 1, 2, 4)
    return w.reshape(l, n_chunks, d, n_up * FF_TILE).astype(BF16)


def _ffn_call(x, mods, li, cond_of_row, seq, P, final_norm):
    n = D_FF // FF_TILE
    operands = [P["ffn_up_cm"], _chunked(P["ffn_conv_w"][li], n), _chunked(P["ffn_conv_b"][li][None, :], n),
                P["ffn_down"], P["final_norm_w"].reshape(1, D_MODEL)]
    return _glu_call(_ffn_kernel, f"ffn{li}", x, mods, li, cond_of_row, 2, n, P["norm2_w"],
                     operands, ["hbm", "vmem", "vmem", "hbm", "vmem"],
                     dict(layer=li, seq=seq, final_norm=final_norm))


def _sconv_call(x, mods, li, cond_of_row, seq, P):
    lj = li // 4
    n = D_MODEL // FF_TILE
    operands = [P["sconv_win_cm"], _chunked(P["sconv_conv"][lj], n), P["sconv_wout"]]
    return _glu_call(_sconv_kernel, f"sconv{li}", x, mods, li, cond_of_row, 3, n, P["norm1_w"],
                     operands, ["hbm", "vmem", "hbm"], dict(layer=lj, seq=seq))


def _emit_kv(ckv, krope_pad, wk_ref, wv_ref, k_ref, v_ref):
    c = ckv.astype(BF16)
    kx = _dot(c, wk_ref[...].astype(BF16))
    vx = _dot(c, wv_ref[...].astype(BF16))
    for h in range(N_HEADS):
        k_ref[h] = (kx[:, h * LANES:(h + 1) * LANES] + krope_pad).astype(BF16)
    for p in range(HEAD_PAIRS):
        v_ref[p] = vx[:, p * LANES:(p + 1) * LANES].astype(BF16)


def _mla_proj_kernel(x_ref, mod_ref, nw_ref, w1_ref, qn_ref, kvn_ref, w2_ref, wk_ref, wv_ref, cos_ref,
                     sin_ref, q_ref, k_ref, v_ref, ckv_ref, kr_ref, *, rope):
    h = _norm_mod(x_ref[...], nw_ref[...], mod_ref[...], 0).astype(BF16)
    z = _dot(h, w1_ref[...].astype(BF16))
    cq = _rms(z[:, :Q_LORA], qn_ref[...]).astype(BF16)
    ckv = _rms(z[:, Q_LORA:Q_LORA + KV_LORA], kvn_ref[...])
    kr = z[:, Q_LORA + KV_LORA:Q_LORA + KV_LORA + LANES]
    if rope:
        cos, sin = cos_ref[...], sin_ref[...]
        kr = kr * cos + z[:, Q_LORA + KV_LORA + LANES:] * sin
    for p in range(HEAD_PAIRS):
        q2 = _dot(cq, w2_ref[:, 2 * p * LANES:(2 * p + 2) * LANES].astype(BF16))
        if rope:
            off = N_HEADS * LANES
            qs = _dot(cq, w2_ref[:, off + 2 * p * LANES:off + (2 * p + 2) * LANES].astype(BF16))
        for e in range(2):
            qh = q2[:, e * LANES:(e + 1) * LANES]
            if rope:
                qh = qh * cos + qs[:, e * LANES:(e + 1) * LANES] * sin
            q_ref[2 * p + e] = qh.astype(BF16)
    ckv_ref[...] = ckv
    kr_ref[...] = kr[:, QK_NOPE:QK_NOPE + QK_ROPE]
    _emit_kv(ckv, kr, wk_ref, wv_ref, k_ref, v_ref)


def _mla_proj_call(x, mods, li, cond_of_row, W, rope):
    m = x.shape[0]
    n1 = W["w1"].shape[1]
    n2 = W["w2"].shape[1]
    const = lambda i: (0, 0)
    pos_tiles = W["cos"].shape[0] // PROJ_TILE
    pos = lambda i: (i % pos_tiles, 0)
    return pl.pallas_call(
        functools.partial(_mla_proj_kernel, rope=rope),
        grid=(m // PROJ_TILE,),
        in_specs=[
            pl.BlockSpec((PROJ_TILE, D_MODEL), lambda i: (i, 0)),
            _mod_spec(li, cond_of_row, PROJ_TILE, 1),
            pl.BlockSpec((None, 1, D_MODEL), lambda i: (li, 0, 0)),
            pl.BlockSpec((D_MODEL, n1), const),
            pl.BlockSpec((1, Q_LORA), const),
            pl.BlockSpec((1, KV_LORA), const),
            pl.BlockSpec((Q_LORA, n2), const),
            pl.BlockSpec((KV_LORA, N_HEADS * LANES), const),
            pl.BlockSpec((KV_LORA, HEAD_PAIRS * LANES), const),
            pl.BlockSpec((PROJ_TILE, LANES), pos),
            pl.BlockSpec((PROJ_TILE, LANES), pos),
        ],
        out_specs=[
            pl.BlockSpec((N_HEADS, PROJ_TILE, LANES), lambda i: (0, i, 0)),
            pl.BlockSpec((N_HEADS, PROJ_TILE, LANES), lambda i: (0, i, 0)),
            pl.BlockSpec((HEAD_PAIRS, PROJ_TILE, LANES), lambda i: (0, i, 0)),
            pl.BlockSpec((PROJ_TILE, KV_LORA), lambda i: (i, 0)),
            pl.BlockSpec((PROJ_TILE, QK_ROPE), lambda i: (i, 0)),
        ],
        out_shape=[
            jax.ShapeDtypeStruct((N_HEADS, m, LANES), BF16),
            jax.ShapeDtypeStruct((N_HEADS, m, LANES), BF16),
            jax.ShapeDtypeStruct((HEAD_PAIRS, m, LANES), BF16),
            jax.ShapeDtypeStruct((m, KV_LORA), F32),
            jax.ShapeDtypeStruct((m, QK_ROPE), F32),
        ],
        compiler_params=_params(("arbitrary",)),
        name="mla_proj_rope" if rope else "mla_proj",
    )(x, mods, W["norm1_w"], W["w1"], W["q_norm"], W["kv_norm"], W["w2"], W["wk"], W["wv"],
      W["cos"], W["sin"])


def _kv_cache_kernel(ckv_ref, kr_ref, wk_ref, wv_ref, k_ref, v_ref):
    _emit_kv(ckv_ref[...], kr_ref[...], wk_ref, wv_ref, k_ref, v_ref)


def _kv_cache_call(ckv, krope_pad, W):
    m = ckv.shape[0]
    const = lambda i: (0, 0)
    return pl.pallas_call(
        _kv_cache_kernel,
        grid=(1,),
        in_specs=[
            pl.BlockSpec((m, KV_LORA), const),
            pl.BlockSpec((m, LANES), const),
            pl.BlockSpec((KV_LORA, N_HEADS * LANES), const),
            pl.BlockSpec((KV_LORA, HEAD_PAIRS * LANES), const),
        ],
        out_specs=[
            pl.BlockSpec((N_HEADS, m, LANES), lambda i: (0, 0, 0)),
            pl.BlockSpec((HEAD_PAIRS, m, LANES), lambda i: (0, 0, 0)),
        ],
        out_shape=[
            jax.ShapeDtypeStruct((N_HEADS, m, LANES), BF16),
            jax.ShapeDtypeStruct((HEAD_PAIRS, m, LANES), BF16),
        ],
        compiler_params=_params(("arbitrary",)),
        name="kv_cache",
    )(ckv, krope_pad, W["wk"], W["wv"])


def _attn_kernel(q_ref, k_ref, v_ref, o_ref, *, n_pairs):
    scale = np.float32(1.0 / math.sqrt(QK_NOPE + QK_ROPE))
    for p in range(n_pairs):
        v = v_ref[p]
        outs = []
        for e in range(2):
            s = lax.dot_general(q_ref[2 * p + e], k_ref[2 * p + e], (((1,), (1,)), ((), ())),
                                preferred_element_type=F32) * scale
            w = jnp.exp(s - jnp.max(s, axis=-1, keepdims=True))
            l = jnp.sum(w, axis=-1, keepdims=True)
            outs.append(_dot(w.astype(BF16), v) / l)
        lane = lax.broadcasted_iota(jnp.int32, outs[0].shape, 1)
        o_ref[:, p * LANES:(p + 1) * LANES] = jnp.where(lane < V_HEAD, outs[0], outs[1]).astype(BF16)


def _attn_call(q, k, v, q_tile, n_pairs):
    _, b, t, _ = q.shape
    s = k.shape[2]
    nq = t // q_tile
    return pl.pallas_call(
        functools.partial(_attn_kernel, n_pairs=n_pairs),
        grid=(b, HEAD_PAIRS // n_pairs, nq),
        in_specs=[
            pl.BlockSpec((2 * n_pairs, None, q_tile, LANES), lambda bi, g, qi: (g, bi, qi, 0)),
            pl.BlockSpec((2 * n_pairs, None, s, LANES), lambda bi, g, qi: (g, bi, 0, 0)),
            pl.BlockSpec((n_pairs, None, s, LANES), lambda bi, g, qi: (g, bi, 0, 0)),
        ],
        out_specs=pl.BlockSpec((q_tile, n_pairs * LANES), lambda bi, g, qi: (bi * nq + qi, g)),
        out_shape=jax.ShapeDtypeStruct((b * t, N_HEADS * V_HEAD), BF16),
        compiler_params=_params(("arbitrary", "arbitrary", "arbitrary")),
        name=f"attn_s{s}",
    )(q, k, v)


def _proj_res_kernel(x_ref, mod_ref, a_ref, w_ref, o_ref):
    o_ref[...] = x_ref[...] + mod_ref[2:3, :] * _dot(a_ref[...], w_ref[...].astype(BF16))


def _proj_res_call(x, mods, li, cond_of_row, a, w, name):
    m = x.shape[0]
    return pl.pallas_call(
        _proj_res_kernel,
        grid=(m // ROW_TILE,),
        in_specs=[
            pl.BlockSpec((ROW_TILE, D_MODEL), lambda i: (i, 0)),
            _mod_spec(li, cond_of_row, ROW_TILE, 1),
            pl.BlockSpec((ROW_TILE, a.shape[1]), lambda i: (i, 0)),
            pl.BlockSpec((None,) + w.shape[1:], lambda i: (0, 0, 0)),
        ],
        out_specs=pl.BlockSpec((ROW_TILE, D_MODEL), lambda i: (i, 0)),
        out_shape=jax.ShapeDtypeStruct((m, D_MODEL), F32),
        compiler_params=_params(("arbitrary",)),
        name=name,
    )(x, mods, a, w)


def _pool_kernel(x_ref, mod_ref, nw_ref, w_ref, sc_ref, o_ref, band_scr, y_scr, *, seq):
    @pl.when(pl.program_id(0) == 0)
    def _():
        r = lax.broadcasted_iota(jnp.int32, (seq, seq), 0)
        c = lax.broadcasted_iota(jnp.int32, (seq, seq), 1)
        for g, win in enumerate(POOL_WINDOWS):
            inside = (c >= r - win // 2) & (c < r + win - win // 2)
            band_scr[g] = jnp.where(inside, 1.0, 0.0).astype(BF16)

    x = x_ref[...]
    h = _norm_mod(x, nw_ref[...], mod_ref[...], 0)
    h_hi = h.astype(BF16)
    h_lo = (h - h_hi.astype(F32)).astype(BF16)
    t = lax.broadcasted_iota(jnp.int32, (seq, 1), 0)
    for g, win in enumerate(POOL_WINDOWS):
        cnt = (jnp.minimum(t + (win - win // 2), seq) - jnp.maximum(t - win // 2, 0)).astype(F32)
        cols = slice(g * GROUP, (g + 1) * GROUP)
        wg = w_ref[g].astype(BF16)
        for s in range(x.shape[0] // seq):
            rows = slice(s * seq, (s + 1) * seq)
            tot = _dot(band_scr[g], h_hi[rows, cols]) + _dot(band_scr[g], h_lo[rows, cols])
            pooled = (tot / cnt - h[rows, cols]).astype(BF16)
            y_scr[rows, cols] = _dot(pooled, wg)
    o_ref[...] = x + mod_ref[2:3, :] * (y_scr[...] * sc_ref[...])


def _pool_call(x, mods, li, cond_of_row, seq, P):
    m = x.shape[0]
    lj = li // 4
    return pl.pallas_call(
        functools.partial(_pool_kernel, seq=seq),
        grid=(m // ROW_TILE,),
        in_specs=[
            pl.BlockSpec((ROW_TILE, D_MODEL), lambda i: (i, 0)),
            _mod_spec(li, cond_of_row, ROW_TILE, 1),
            pl.BlockSpec((None, 1, D_MODEL), lambda i: (li, 0, 0)),
            pl.BlockSpec((None, N_GROUPS, GROUP, GROUP), lambda i: (lj, 0, 0, 0)),
            pl.BlockSpec((None, 1, D_MODEL), lambda i: (lj, 0, 0)),
        ],
        out_specs=pl.BlockSpec((ROW_TILE, D_MODEL), lambda i: (i, 0)),
        out_shape=jax.ShapeDtypeStruct((m, D_MODEL), F32),
        scratch_shapes=[pltpu.VMEM((N_GROUPS, seq, seq), BF16), pltpu.VMEM((ROW_TILE, D_MODEL), F32)],
        compiler_params=_params(("arbitrary",)),
        name=f"pool_t{seq}",
    )(x, mods, P["norm1_w"], P["pool_w"], P["pool_scale"])


def _dft_tables(seq):
    def cs(n):
        k = np.arange(n, dtype=np.int64)
        ang = 2.0 * np.pi * ((k[:, None] * k[None, :]) % n) / n
        return np.cos(ang), np.sin(ang)
    ct, st = cs(seq)
    cc, sc = cs(GROUP)
    e_t = jnp.asarray(np.concatenate([ct, st], axis=0), dtype=F32)
    e_c = jnp.asarray(np.concatenate([cc, -sc], axis=0), dtype=F32)
    return e_t.astype(BF16), e_c.astype(BF16)


def _fourier_kernel(x_ref, mod_ref, nw_ref, et_ref, ec_ref, w_ref, o_ref, f_scr, *, seq):
    x = x_ref[...]
    h = _norm_mod(x, nw_ref[...], mod_ref[...], 0).astype(BF16)
    norm = np.float32(1.0 / math.sqrt(seq * GROUP))
    for s in range(x.shape[0] // seq):
        rows = slice(s * seq, (s + 1) * seq)
        for g in range(N_GROUPS):
            cols = slice(g * GROUP, (g + 1) * GROUP)
            y = _dot(et_ref[...], h[rows, cols]).astype(BF16)
            f = _dot(y[:seq], ec_ref[:GROUP, :]) + _dot(y[seq:], ec_ref[GROUP:, :])
            f_scr[rows, cols] = (f * norm).astype(BF16)
    o_ref[...] = x + mod_ref[2:3, :] * _dot(f_scr[...], w_ref[...].astype(BF16))


def _fourier_call(x, mods, li, cond_of_row, seq, P):
    m = x.shape[0]
    lj = li // 4
    e_t, e_c = _dft_tables(seq)
    return pl.pallas_call(
        functools.partial(_fourier_kernel, seq=seq),
        grid=(m // ROW_TILE,),
        in_specs=[
            pl.BlockSpec((ROW_TILE, D_MODEL), lambda i: (i, 0)),
            _mod_spec(li, cond_of_row, ROW_TILE, 1),
            pl.BlockSpec((None, 1, D_MODEL), lambda i: (li, 0, 0)),
            pl.BlockSpec((2 * seq, seq), lambda i: (0, 0)),
            pl.BlockSpec((2 * GROUP, GROUP), lambda i: (0, 0)),
            pl.BlockSpec((None, D_MODEL, D_MODEL), lambda i: (lj, 0, 0)),
        ],
        out_specs=pl.BlockSpec((ROW_TILE, D_MODEL), lambda i: (i, 0)),
        out_shape=jax.ShapeDtypeStruct((m, D_MODEL), F32),
        scratch_shapes=[pltpu.VMEM((ROW_TILE, D_MODEL), BF16)],
        compiler_params=_params(("arbitrary",)),
        name=f"fourier_t{seq}",
    )(x, mods, P["norm1_w"], e_t, e_c, P["fnet_w"])


def _rope_tables(rows):
    r, col = jnp.meshgrid(jnp.arange(rows), jnp.arange(GRID_W), indexing="ij")
    r = r.reshape(-1).astype(F32)
    col = col.reshape(-1).astype(F32)
    n_freq = QK_ROPE // 4
    inv = ROPE_THETA ** (-jnp.arange(n_freq, dtype=F32) / n_freq)
    ang = jnp.concatenate([r[:, None] * inv, col[:, None] * inv], axis=-1)
    cos = jnp.repeat(jnp.cos(ang), 2, axis=-1)
    sin = jnp.repeat(jnp.sin(ang), 2, axis=-1)
    t = cos.shape[0]
    pad = LANES - QK_NOPE - QK_ROPE
    cos_p = jnp.concatenate([jnp.ones((t, QK_NOPE), F32), cos, jnp.ones((t, pad), F32)], axis=-1)
    sin_p = jnp.concatenate([jnp.zeros((t, QK_NOPE), F32), sin, jnp.zeros((t, pad), F32)], axis=-1)
    return cos_p, sin_p


def _pair_swap(w):
    wr = w.reshape(w.shape[:-1] + (w.shape[-1] // 2, 2))
    return jnp.stack([-wr[..., 1], wr[..., 0]], axis=-1).reshape(w.shape)


def _mla_weights(P, j, rope, seq_rows):
    pad = LANES - QK_NOPE - QK_ROPE
    wuq = P["mla_wuq"][j].reshape(Q_LORA, N_HEADS, QK_NOPE + QK_ROPE)
    q_slots = jnp.pad(wuq, ((0, 0), (0, 0), (0, pad))).reshape(Q_LORA, N_HEADS * LANES)
    wdkv = P["mla_wdkv"][j]
    kr_slot = jnp.pad(wdkv[:, KV_LORA:], ((0, 0), (QK_NOPE, pad)))
    parts1 = [P["mla_wdq"][j], wdkv[:, :KV_LORA], kr_slot]
    parts2 = [q_slots]
    if rope:
        parts1.append(jnp.pad(_pair_swap(wdkv[:, KV_LORA:]), ((0, 0), (QK_NOPE, pad))))
        q_sw = jnp.pad(_pair_swap(wuq[:, :, QK_NOPE:]), ((0, 0), (0, 0), (QK_NOPE, pad)))
        parts2.append(q_sw.reshape(Q_LORA, N_HEADS * LANES))
        cos, sin = _rope_tables(seq_rows // GRID_W)
    else:
        cos = sin = jnp.zeros((PROJ_TILE, LANES), F32)
    wukv = P["mla_wukv"][j].reshape(KV_LORA, N_HEADS, QK_NOPE + V_HEAD)
    wk = jnp.pad(wukv[:, :, :QK_NOPE], ((0, 0), (0, 0), (0, LANES - QK_NOPE)))
    return dict(
        norm1_w=P["norm1_w"], q_norm=P["mla_q_norm"][j][None, :], kv_norm=P["mla_kv_norm"][j][None, :],
        w1=jnp.concatenate(parts1, axis=1), w2=jnp.concatenate(parts2, axis=1),
        wk=wk.reshape(KV_LORA, N_HEADS * LANES),
        wv=wukv[:, :, QK_NOPE:].reshape(KV_LORA, N_HEADS * V_HEAD), cos=cos, sin=sin)


def _mla_layer(x, mods, li, cond_of_row, batch, seq, P, cache):
    j = li // 4
    m = x.shape[0]
    W = _mla_weights(P, j, cache is not None, seq)
    q, k, v, ckv, krope = _mla_proj_call(x, mods, li, cond_of_row, W, cache is not None)
    q = q.reshape(N_HEADS, batch, seq, LANES)
    k = k.reshape(N_HEADS, batch, seq, LANES)
    v = v.reshape(HEAD_PAIRS, batch, seq, LANES)
    if cache is None:
        o = _attn_call(q, k, v, seq, HEAD_PAIRS)
    else:
        c_ckv, c_krope = cache
        past = c_ckv.shape[1]
        pad = LANES - QK_NOPE - QK_ROPE
        kr_pad = jnp.pad(c_krope.reshape(batch * past, QK_ROPE), ((0, 0), (QK_NOPE, pad)))
        kc, vc = _kv_cache_call(c_ckv.reshape(batch * past, KV_LORA), kr_pad, W)
        k = jnp.concatenate([kc.reshape(N_HEADS, batch, past, LANES), k], axis=2)
        v = jnp.concatenate([vc.reshape(HEAD_PAIRS, batch, past, LANES), v], axis=2)
        o = _attn_call(q, k, v, 512, 2)
    x = _proj_res_call(x, mods, li, cond_of_row, o, P["mla_wo"], f"mla_out_{m}")
    return x, ckv, krope


def _run_pass(x, mods, cond_of_row, batch, seq, P, cache):
    ckv = krope = None
    for li in range(DEPTH):
        kind = li % 4
        if kind == 0:
            x, ckv, krope = _mla_layer(x, mods, li, cond_of_row, batch, seq, P, cache)
        elif kind == 1:
            x = _pool_call(x, mods, li, cond_of_row, seq, P)
        elif kind == 2:
            x = _fourier_call(x, mods, li, cond_of_row, seq, P)
        else:
            x = _sconv_call(x, mods, li, cond_of_row, seq, P)
        x = _ffn_call(x, mods, li, cond_of_row, seq, P, final_norm=(li == DEPTH - 1))
    return x, ckv, krope


def kernel(x_prompt, x_sample, cache_ckv, cache_krope, c, c_ctx, norm1_w, norm2_w, ada_w, ada_b, mla_wdq, mla_q_norm, mla_wuq, mla_wdkv, mla_kv_norm, mla_wukv, mla_wo, pool_w, pool_scale, fnet_w, sconv_win, sconv_conv, sconv_wout, ffn_up, ffn_conv_w, ffn_conv_b, ffn_down, final_norm_w):
    row = lambda w: w.reshape(w.shape[0], 1, w.shape[1])
    P = dict(norm1_w=row(norm1_w), norm2_w=row(norm2_w), mla_wdq=mla_wdq, mla_q_norm=mla_q_norm,
             mla_wuq=mla_wuq, mla_wdkv=mla_wdkv, mla_kv_norm=mla_kv_norm, mla_wukv=mla_wukv, mla_wo=mla_wo,
             pool_w=pool_w, pool_scale=row(pool_scale), fnet_w=fnet_w, sconv_win=sconv_win,
             sconv_conv=sconv_conv, sconv_wout=sconv_wout, ffn_conv_w=ffn_conv_w,
             ffn_conv_b=ffn_conv_b, ffn_down=ffn_down, final_norm_w=final_norm_w,
             ffn_up_cm=_chunk_major(ffn_up, 2), sconv_win_cm=_chunk_major(sconv_win, 3))
    b_ctx, t_ctx, _ = x_prompt.shape
    b_lat, t_lat, _ = x_sample.shape
    assert ROW_TILE % t_ctx == 0 and t_lat % ROW_TILE == 0 and b_lat + 1 <= N_COND

    conds = jnp.concatenate([c_ctx[None, :], c, jnp.zeros((N_COND - 1 - b_lat, D_MODEL), F32)], axis=0)
    mods = _ada_call(conds, ada_w, ada_b).reshape(DEPTH, N_COND, 6, D_MODEL)

    xp, ckv, krope = _run_pass(x_prompt.reshape(b_ctx * t_ctx, D_MODEL), mods, lambda r: 0,
                               b_ctx, t_ctx, P, None)
    xs, _, _ = _run_pass(x_sample.reshape(b_lat * t_lat, D_MODEL), mods,
                         lambda r: 1 + r // t_lat, b_lat, t_lat, P,
                         (cache_ckv[:, 0], cache_krope[:, 0]))
    return (xp.reshape(b_ctx, t_ctx, D_MODEL), xs.reshape(b_lat, t_lat, D_MODEL),
            ckv.reshape(b_ctx, 1, t_ctx, KV_LORA), krope.reshape(b_ctx, 1, t_ctx, QK_ROPE))
```

```python
import functools
import math

import numpy as np
import jax
import jax.numpy as jnp
from jax import lax
from jax.experimental import pallas as pl
from jax.experimental.pallas import tpu as pltpu

F32 = jnp.float32
BF16 = jnp.bfloat16

D_MODEL = 1024
N_HEADS = 16
QK_NOPE = 64
QK_ROPE = 32
V_HEAD = 64
Q_LORA = 384
KV_LORA = 256
ROPE_THETA = 10000.0
GRID_W = 64
POOL_WINDOWS = (2, 4, 8, 16)
N_GROUPS = 4
GROUP = D_MODEL // N_GROUPS
D_FF = 2816
EPS = 1e-6
DEPTH = 4

LANES = 128
HEAD_PAIRS = N_HEADS // 2
N_COND = 8
ROW_TILE = 1024
PROJ_TILE = 512
FF_TILE = 256
ROW_BLOCK = 256
HALO = 8
ADA_TILE = 1536
VMEM_LIMIT = 52 * 1024 * 1024


def _params(sem):
    return pltpu.CompilerParams(dimension_semantics=sem, vmem_limit_bytes=VMEM_LIMIT)


def _dot(a, b):
    return jnp.dot(a, b, preferred_element_type=F32)


def _rms(x, w):
    return x * lax.rsqrt(jnp.mean(x * x, axis=-1, keepdims=True) + EPS) * w


def _norm_mod(x, nw, mod, base):
    return _rms(x, nw) * (1.0 + mod[base + 1:base + 2, :]) + mod[base:base + 1, :]


def _dwconv_rows(g, cw, seq):
    rows = g.shape[0]
    t = lax.broadcasted_iota(jnp.int32, (rows, 1), 0) & (seq - 1)
    prev = jnp.where(t == 0, 0.0, pltpu.roll(g, 1, axis=0))
    nxt = jnp.where(t == seq - 1, 0.0, pltpu.roll(g, rows - 1, axis=0))
    return prev * cw[0:1, :] + g * cw[1:2, :] + nxt * cw[2:3, :]


def _gelu_exact(x):
    return 0.5 * x * (1.0 + lax.erf(x * np.float32(math.sqrt(0.5))))


def _ada_kernel(c_ref, w_ref, b_ref, o_ref):
    s = jax.nn.silu(c_ref[...]).astype(BF16)
    o_ref[...] = _dot(s, w_ref[...].astype(BF16)) + b_ref[...]


def _ada_call(conds, ada_w, ada_b):
    n_out = ada_w.shape[-1]
    return pl.pallas_call(
        _ada_kernel,
        grid=(DEPTH, n_out // ADA_TILE),
        in_specs=[
            pl.BlockSpec((N_COND, D_MODEL), lambda l, j: (0, 0)),
            pl.BlockSpec((None, D_MODEL, ADA_TILE), lambda l, j: (l, 0, j)),
            pl.BlockSpec((None, 1, ADA_TILE), lambda l, j: (l, 0, j)),
        ],
        out_specs=pl.BlockSpec((None, N_COND, ADA_TILE), lambda l, j: (l, 0, j)),
        out_shape=jax.ShapeDtypeStruct((DEPTH, N_COND, n_out), F32),
        compiler_params=_params(("arbitrary", "arbitrary")),
        name="ada",
    )(conds, ada_w, ada_b.reshape(DEPTH, 1, n_out))


def _mod_spec(li, cond_of_row, tile, n_grid):
    if n_grid == 1:
        return pl.BlockSpec((None, None, 6, D_MODEL), lambda i: (li, cond_of_row(i * tile), 0, 0))
    return pl.BlockSpec((None, None, 6, D_MODEL), lambda i, j: (li, cond_of_row(i * tile), 0, 0))


def _conv_window(w, cw, t0, seq):
    n = w.shape[0]
    mid = slice(HALO, HALO + ROW_BLOCK)
    t = (t0 + lax.broadcasted_iota(jnp.int32, (ROW_BLOCK, 1), 0)) & (seq - 1)
    prev = jnp.where(t == 0, 0.0, pltpu.roll(w, 1, axis=0)[mid])
    nxt = jnp.where(t == seq - 1, 0.0, pltpu.roll(w, n - 1, axis=0)[mid])
    return prev * cw[0:1, :] + w[mid] * cw[1:2, :] + nxt * cw[2:3, :]


def _ffn_act(win, mid, cw, cb, t0, seq):
    return _gelu_exact(_conv_window(win(0), cw, t0, seq) + cb) * mid(1)


def _sconv_act(win, mid, cw, cb, t0, seq):
    return mid(0) * _conv_window(win(1) * win(2), cw, t0, seq)


def _ffn_kernel(x_ref, mod_ref, nw_ref, up_hbm, cw_ref, cb_ref, down_hbm, fnw_ref, o_ref, *scratch, **cfg):
    _glu_body(x_ref, mod_ref, nw_ref, up_hbm, cw_ref, cb_ref, down_hbm, fnw_ref, o_ref, *scratch,
              n_up=2, act=_ffn_act, mod_base=3, **cfg)


def _sconv_kernel(x_ref, mod_ref, nw_ref, up_hbm, cw_ref, down_hbm, o_ref, *scratch, **cfg):
    _glu_body(x_ref, mod_ref, nw_ref, up_hbm, cw_ref, None, down_hbm, None, o_ref, *scratch,
              n_up=3, act=_sconv_act, mod_base=0, final_norm=False, **cfg)


def _glu_body(x_ref, mod_ref, nw_ref, up_ref, cw_ref, cb_ref, down_ref, fnw_ref, o_ref,
              h_scr, acc_scr, gu0, gu1, *, n_up, n_chunks, act, seq, mod_base, final_norm):
    gu = (gu0, gu1)
    n_blocks = ROW_TILE // ROW_BLOCK

    def up(c, slot, r):
        rows = slice(r * ROW_BLOCK, (r + 1) * ROW_BLOCK)
        res = _dot(h_scr[rows, :], up_ref[c])
        for p in range(n_up):
            gu[slot][p, HALO + r * ROW_BLOCK:HALO + (r + 1) * ROW_BLOCK, :] = res[:, p * FF_TILE:(p + 1) * FF_TILE]

    def activate(c, slot, r):
        win = lambda p: gu[slot][p, r * ROW_BLOCK:(r + 1) * ROW_BLOCK + 2 * HALO, :]
        mid = lambda p: gu[slot][p, HALO + r * ROW_BLOCK:HALO + (r + 1) * ROW_BLOCK, :]
        cb = None if cb_ref is None else cb_ref[c]
        return act(win, mid, cw_ref[c], cb, r * ROW_BLOCK, seq).astype(BF16)

    def down(a, c, r):
        rows = slice(r * ROW_BLOCK, (r + 1) * ROW_BLOCK)
        acc_scr[rows, :] += _dot(a, down_ref[c])

    def stage(s, slot):
        for r in range(n_blocks):
            a = activate(s - 1, 1 - slot, r)
            up(s, slot, r)
            down(a, s - 1, r)

    h_scr[...] = _norm_mod(x_ref[...], nw_ref[...], mod_ref[...], mod_base).astype(BF16)
    acc_scr[...] = jnp.zeros_like(acc_scr)
    for g in gu:
        g[:, :HALO, :] = jnp.zeros((n_up, HALO, FF_TILE), F32)
        g[:, HALO + ROW_TILE:, :] = jnp.zeros((n_up, HALO, FF_TILE), F32)

    for r in range(n_blocks):
        up(0, 0, r)

    n_pairs = (n_chunks - 1) // 2

    def pair(i, carry):
        s = 2 * i + 1
        stage(s, 1)
        stage(s + 1, 0)
        return carry

    lax.fori_loop(0, n_pairs, pair, 0)
    for s in range(2 * n_pairs + 1, n_chunks):
        stage(s, s % 2)
    last = n_chunks - 1
    for r in range(n_blocks):
        down(activate(last, last % 2, r), last, r)

    y = x_ref[...] + mod_ref[mod_base + 2:mod_base + 3, :] * acc_scr[...]
    if final_norm:
        y = _rms(y, fnw_ref[...])
    o_ref[...] = y


def _chunked(w, n_chunks):
    return w.reshape(w.shape[0], n_chunks, FF_TILE).transpose(1, 0, 2)


def _glu_call(kernel_fn, name, x, mods, li, cond_of_row, n_up, n_chunks, norm_w, operands, specs, cfg):
    m = x.shape[0]
    whole = lambda a: pl.BlockSpec(a.shape, lambda i: (0,) * a.ndim)
    in_specs = [pl.BlockSpec((ROW_TILE, D_MODEL), lambda i: (i, 0)),
                _mod_spec(li, cond_of_row, ROW_TILE, 1),
                pl.BlockSpec((None, 1, D_MODEL), lambda i: (li, 0, 0))]
    layer_w = lambda a, l: pl.BlockSpec((None,) + a.shape[1:], lambda i: (l,) + (0,) * (a.ndim - 1),
                                        pipeline_mode=pl.Buffered(1))
    in_specs += [whole(a) if s is None else layer_w(a, s) for a, s in zip(operands, specs)]
    gu_shape = (n_up, ROW_TILE + 2 * HALO, FF_TILE)
    return pl.pallas_call(
        functools.partial(kernel_fn, n_chunks=n_chunks, **cfg),
        grid=(m // ROW_TILE,),
        in_specs=in_specs,
        out_specs=pl.BlockSpec((ROW_TILE, D_MODEL), lambda i: (i, 0)),
        out_shape=jax.ShapeDtypeStruct((m, D_MODEL), F32),
        scratch_shapes=[
            pltpu.VMEM((ROW_TILE, D_MODEL), BF16),
            pltpu.VMEM((ROW_TILE, D_MODEL), F32),
            pltpu.VMEM(gu_shape, F32), pltpu.VMEM(gu_shape, F32),
        ],
        compiler_params=_params(("arbitrary",)),
        name=name,
    )(x, mods, norm_w, *operands)


def _chunk_major(w, n_up):
    l, d, cols = w.shape
    n_chunks = cols // (n_up * FF_TILE)
    w = w.reshape(l, d, n_up, n_chunks, FF_TILE).transpose(0, 3, 1, 2, 4)
    return w.reshape(l, n_chunks, d, n_up * FF_TILE).astype(BF16)


def _down_chunks(w):
    l, rows, d = w.shape
    return w.reshape(l, rows // FF_TILE, FF_TILE, d).astype(BF16)


def _ffn_call(x, mods, li, cond_of_row, seq, P, final_norm):
    n = D_FF // FF_TILE
    operands = [P["ffn_up_cm"], _chunked(P["ffn_conv_w"][li], n), _chunked(P["ffn_conv_b"][li][None, :], n),
                P["ffn_down_cm"], P["final_norm_w"].reshape(1, D_MODEL)]
    return _glu_call(_ffn_kernel, f"ffn{li}", x, mods, li, cond_of_row, 2, n, P["norm2_w"],
                     operands, [li, None, None, li, None], dict(seq=seq, final_norm=final_norm))


def _sconv_call(x, mods, li, cond_of_row, seq, P):
    lj = li // 4
    n = D_MODEL // FF_TILE
    operands = [P["sconv_win_cm"], _chunked(P["sconv_conv"][lj], n), P["sconv_wout_cm"]]
    return _glu_call(_sconv_kernel, f"sconv{li}", x, mods, li, cond_of_row, 3, n, P["norm1_w"],
                     operands, [lj, None, lj], dict(seq=seq))


def _emit_kv(ckv, krope_pad, wk_ref, wv_ref, k_ref, v_ref):
    c = ckv.astype(BF16)
    kx = _dot(c, wk_ref[...].astype(BF16))
    vx = _dot(c, wv_ref[...].astype(BF16))
    for h in range(N_HEADS):
        k_ref[h] = (kx[:, h * LANES:(h + 1) * LANES] + krope_pad).astype(BF16)
    for p in range(HEAD_PAIRS):
        v_ref[p] = vx[:, p * LANES:(p + 1) * LANES].astype(BF16)


def _mla_proj_kernel(x_ref, mod_ref, nw_ref, w1_ref, qn_ref, kvn_ref, w2_ref, wk_ref, wv_ref, cos_ref,
                     sin_ref, q_ref, k_ref, v_ref, ckv_ref, kr_ref, *, rope):
    h = _norm_mod(x_ref[...], nw_ref[...], mod_ref[...], 0).astype(BF16)
    z = _dot(h, w1_ref[...].astype(BF16))
    cq = _rms(z[:, :Q_LORA], qn_ref[...]).astype(BF16)
    ckv = _rms(z[:, Q_LORA:Q_LORA + KV_LORA], kvn_ref[...])
    kr = z[:, Q_LORA + KV_LORA:Q_LORA + KV_LORA + LANES]
    if rope:
        cos, sin = cos_ref[...], sin_ref[...]
        kr = kr * cos + z[:, Q_LORA + KV_LORA + LANES:] * sin
    for p in range(HEAD_PAIRS):
        q2 = _dot(cq, w2_ref[:, 2 * p * LANES:(2 * p + 2) * LANES].astype(BF16))
        if rope:
            off = N_HEADS * LANES
            qs = _dot(cq, w2_ref[:, off + 2 * p * LANES:off + (2 * p + 2) * LANES].astype(BF16))
        for e in range(2):
            qh = q2[:, e * LANES:(e + 1) * LANES]
            if rope:
                qh = qh * cos + qs[:, e * LANES:(e + 1) * LANES] * sin
            q_ref[2 * p + e] = qh.astype(BF16)
    ckv_ref[...] = ckv
    kr_ref[...] = kr[:, QK_NOPE:QK_NOPE + QK_ROPE]
    _emit_kv(ckv, kr, wk_ref, wv_ref, k_ref, v_ref)


def _mla_proj_call(x, mods, li, cond_of_row, W, rope):
    m = x.shape[0]
    n1 = W["w1"].shape[1]
    n2 = W["w2"].shape[1]
    const = lambda i: (0, 0)
    pos_tiles = W["cos"].shape[0] // PROJ_TILE
    pos = lambda i: (i % pos_tiles, 0)
    return pl.pallas_call(
        functools.partial(_mla_proj_kernel, rope=rope),
        grid=(m // PROJ_TILE,),
        in_specs=[
            pl.BlockSpec((PROJ_TILE, D_MODEL), lambda i: (i, 0)),
            _mod_spec(li, cond_of_row, PROJ_TILE, 1),
            pl.BlockSpec((None, 1, D_MODEL), lambda i: (li, 0, 0)),
            pl.BlockSpec((D_MODEL, n1), const),
            pl.BlockSpec((1, Q_LORA), const),
            pl.BlockSpec((1, KV_LORA), const),
            pl.BlockSpec((Q_LORA, n2), const),
            pl.BlockSpec((KV_LORA, N_HEADS * LANES), const),
            pl.BlockSpec((KV_LORA, HEAD_PAIRS * LANES), const),
            pl.BlockSpec((PROJ_TILE, LANES), pos),
            pl.BlockSpec((PROJ_TILE, LANES), pos),
        ],
        out_specs=[
            pl.BlockSpec((N_HEADS, PROJ_TILE, LANES), lambda i: (0, i, 0)),
            pl.BlockSpec((N_HEADS, PROJ_TILE, LANES), lambda i: (0, i, 0)),
            pl.BlockSpec((HEAD_PAIRS, PROJ_TILE, LANES), lambda i: (0, i, 0)),
            pl.BlockSpec((PROJ_TILE, KV_LORA), lambda i: (i, 0)),
            pl.BlockSpec((PROJ_TILE, QK_ROPE), lambda i: (i, 0)),
        ],
        out_shape=[
            jax.ShapeDtypeStruct((N_HEADS, m, LANES), BF16),
            jax.ShapeDtypeStruct((N_HEADS, m, LANES), BF16),
            jax.ShapeDtypeStruct((HEAD_PAIRS, m, LANES), BF16),
            jax.ShapeDtypeStruct((m, KV_LORA), F32),
            jax.ShapeDtypeStruct((m, QK_ROPE), F32),
        ],
        compiler_params=_params(("arbitrary",)),
        name="mla_proj_rope" if rope else "mla_proj",
    )(x, mods, W["norm1_w"], W["w1"], W["q_norm"], W["kv_norm"], W["w2"], W["wk"], W["wv"],
      W["cos"], W["sin"])


def _kv_cache_kernel(ckv_ref, kr_ref, wk_ref, wv_ref, k_ref, v_ref):
    _emit_kv(ckv_ref[...], kr_ref[...], wk_ref, wv_ref, k_ref, v_ref)


def _kv_cache_call(ckv, krope_pad, W):
    m = ckv.shape[0]
    const = lambda i: (0, 0)
    return pl.pallas_call(
        _kv_cache_kernel,
        grid=(1,),
        in_specs=[
            pl.BlockSpec((m, KV_LORA), const),
            pl.BlockSpec((m, LANES), const),
            pl.BlockSpec((KV_LORA, N_HEADS * LANES), const),
            pl.BlockSpec((KV_LORA, HEAD_PAIRS * LANES), const),
        ],
        out_specs=[
            pl.BlockSpec((N_HEADS, m, LANES), lambda i: (0, 0, 0)),
            pl.BlockSpec((HEAD_PAIRS, m, LANES), lambda i: (0, 0, 0)),
        ],
        out_shape=[
            jax.ShapeDtypeStruct((N_HEADS, m, LANES), BF16),
            jax.ShapeDtypeStruct((HEAD_PAIRS, m, LANES), BF16),
        ],
        compiler_params=_params(("arbitrary",)),
        name="kv_cache",
    )(ckv, krope_pad, W["wk"], W["wv"])


def _attn_kernel(q_ref, k_ref, v_ref, o_ref, *, n_pairs):
    scale = np.float32(1.0 / math.sqrt(QK_NOPE + QK_ROPE))
    for p in range(n_pairs):
        v = v_ref[p]
        outs = []
        for e in range(2):
            s = lax.dot_general(q_ref[2 * p + e], k_ref[2 * p + e], (((1,), (1,)), ((), ())),
                                preferred_element_type=F32) * scale
            w = jnp.exp(s - jnp.max(s, axis=-1, keepdims=True))
            l = jnp.sum(w, axis=-1, keepdims=True)
            outs.append(_dot(w.astype(BF16), v) / l)
        lane = lax.broadcasted_iota(jnp.int32, outs[0].shape, 1)
        o_ref[:, p * LANES:(p + 1) * LANES] = jnp.where(lane < V_HEAD, outs[0], outs[1]).astype(BF16)


def _attn_call(q, k, v, q_tile, n_pairs):
    _, b, t, _ = q.shape
    s = k.shape[2]
    nq = t // q_tile
    return pl.pallas_call(
        functools.partial(_attn_kernel, n_pairs=n_pairs),
        grid=(b, HEAD_PAIRS // n_pairs, nq),
        in_specs=[
            pl.BlockSpec((2 * n_pairs, None, q_tile, LANES), lambda bi, g, qi: (g, bi, qi, 0)),
            pl.BlockSpec((2 * n_pairs, None, s, LANES), lambda bi, g, qi: (g, bi, 0, 0)),
            pl.BlockSpec((n_pairs, None, s, LANES), lambda bi, g, qi: (g, bi, 0, 0)),
        ],
        out_specs=pl.BlockSpec((q_tile, n_pairs * LANES), lambda bi, g, qi: (bi * nq + qi, g)),
        out_shape=jax.ShapeDtypeStruct((b * t, N_HEADS * V_HEAD), BF16),
        compiler_params=_params(("arbitrary", "arbitrary", "arbitrary")),
        name=f"attn_s{s}",
    )(q, k, v)


def _proj_res_kernel(x_ref, mod_ref, a_ref, w_ref, o_ref):
    o_ref[...] = x_ref[...] + mod_ref[2:3, :] * _dot(a_ref[...], w_ref[...].astype(BF16))


def _proj_res_call(x, mods, li, cond_of_row, a, w, name):
    m = x.shape[0]
    return pl.pallas_call(
        _proj_res_kernel,
        grid=(m // ROW_TILE,),
        in_specs=[
            pl.BlockSpec((ROW_TILE, D_MODEL), lambda i: (i, 0)),
            _mod_spec(li, cond_of_row, ROW_TILE, 1),
            pl.BlockSpec((ROW_TILE, a.shape[1]), lambda i: (i, 0)),
            pl.BlockSpec((None,) + w.shape[1:], lambda i: (0, 0, 0)),
        ],
        out_specs=pl.BlockSpec((ROW_TILE, D_MODEL), lambda i: (i, 0)),
        out_shape=jax.ShapeDtypeStruct((m, D_MODEL), F32),
        compiler_params=_params(("arbitrary",)),
        name=name,
    )(x, mods, a, w)


def _pool_kernel(x_ref, mod_ref, nw_ref, w_ref, sc_ref, o_ref, band_scr, y_scr, *, seq):
    @pl.when(pl.program_id(0) == 0)
    def _():
        r = lax.broadcasted_iota(jnp.int32, (seq, seq), 0)
        c = lax.broadcasted_iota(jnp.int32, (seq, seq), 1)
        for g, win in enumerate(POOL_WINDOWS):
            inside = (c >= r - win // 2) & (c < r + win - win // 2)
            band_scr[g] = jnp.where(inside, 1.0, 0.0).astype(BF16)

    x = x_ref[...]
    h = _norm_mod(x, nw_ref[...], mod_ref[...], 0)
    h_hi = h.astype(BF16)
    h_lo = (h - h_hi.astype(F32)).astype(BF16)
    t = lax.broadcasted_iota(jnp.int32, (seq, 1), 0)
    for g, win in enumerate(POOL_WINDOWS):
        cnt = (jnp.minimum(t + (win - win // 2), seq) - jnp.maximum(t - win // 2, 0)).astype(F32)
        cols = slice(g * GROUP, (g + 1) * GROUP)
        wg = w_ref[g].astype(BF16)
        for s in range(x.shape[0] // seq):
            rows = slice(s * seq, (s + 1) * seq)
            tot = _dot(band_scr[g], h_hi[rows, cols]) + _dot(band_scr[g], h_lo[rows, cols])
            pooled = (tot / cnt - h[rows, cols]).astype(BF16)
            y_scr[rows, cols] = _dot(pooled, wg)
    o_ref[...] = x + mod_ref[2:3, :] * (y_scr[...] * sc_ref[...])


def _pool_call(x, mods, li, cond_of_row, seq, P):
    m = x.shape[0]
    lj = li // 4
    return pl.pallas_call(
        functools.partial(_pool_kernel, seq=seq),
        grid=(m // ROW_TILE,),
        in_specs=[
            pl.BlockSpec((ROW_TILE, D_MODEL), lambda i: (i, 0)),
            _mod_spec(li, cond_of_row, ROW_TILE, 1),
            pl.BlockSpec((None, 1, D_MODEL), lambda i: (li, 0, 0)),
            pl.BlockSpec((None, N_GROUPS, GROUP, GROUP), lambda i: (lj, 0, 0, 0)),
            pl.BlockSpec((None, 1, D_MODEL), lambda i: (lj, 0, 0)),
        ],
        out_specs=pl.BlockSpec((ROW_TILE, D_MODEL), lambda i: (i, 0)),
        out_shape=jax.ShapeDtypeStruct((m, D_MODEL), F32),
        scratch_shapes=[pltpu.VMEM((N_GROUPS, seq, seq), BF16), pltpu.VMEM((ROW_TILE, D_MODEL), F32)],
        compiler_params=_params(("arbitrary",)),
        name=f"pool_t{seq}",
    )(x, mods, P["norm1_w"], P["pool_w"], P["pool_scale"])


def _dft_tables(seq):
    def cs(n):
        k = np.arange(n, dtype=np.int64)
        ang = 2.0 * np.pi * ((k[:, None] * k[None, :]) % n) / n
        return np.cos(ang), np.sin(ang)
    ct, st = cs(seq)
    cc, sc = cs(GROUP)
    e_t = jnp.asarray(np.concatenate([ct, st], axis=0), dtype=F32)
    e_c = jnp.asarray(np.concatenate([cc, -sc], axis=0), dtype=F32)
    return e_t.astype(BF16), e_c.astype(BF16)


def _fourier_kernel(x_ref, mod_ref, nw_ref, et_ref, ec_ref, w_ref, o_ref, f_scr, *, seq):
    x = x_ref[...]
    h = _norm_mod(x, nw_ref[...], mod_ref[...], 0).astype(BF16)
    norm = np.float32(1.0 / math.sqrt(seq * GROUP))
    for s in range(x.shape[0] // seq):
        rows = slice(s * seq, (s + 1) * seq)
        for g in range(N_GROUPS):
            cols = slice(g * GROUP, (g + 1) * GROUP)
            y = _dot(et_ref[...], h[rows, cols]).astype(BF16)
            f = _dot(y[:seq], ec_ref[:GROUP, :]) + _dot(y[seq:], ec_ref[GROUP:, :])
            f_scr[rows, cols] = (f * norm).astype(BF16)
    o_ref[...] = x + mod_ref[2:3, :] * _dot(f_scr[...], w_ref[...].astype(BF16))


def _fourier_call(x, mods, li, cond_of_row, seq, P):
    m = x.shape[0]
    lj = li // 4
    e_t, e_c = _dft_tables(seq)
    return pl.pallas_call(
        functools.partial(_fourier_kernel, seq=seq),
        grid=(m // ROW_TILE,),
        in_specs=[
            pl.BlockSpec((ROW_TILE, D_MODEL), lambda i: (i, 0)),
            _mod_spec(li, cond_of_row, ROW_TILE, 1),
            pl.BlockSpec((None, 1, D_MODEL), lambda i: (li, 0, 0)),
            pl.BlockSpec((2 * seq, seq), lambda i: (0, 0)),
            pl.BlockSpec((2 * GROUP, GROUP), lambda i: (0, 0)),
            pl.BlockSpec((None, D_MODEL, D_MODEL), lambda i: (lj, 0, 0)),
        ],
        out_specs=pl.BlockSpec((ROW_TILE, D_MODEL), lambda i: (i, 0)),
        out_shape=jax.ShapeDtypeStruct((m, D_MODEL), F32),
        scratch_shapes=[pltpu.VMEM((ROW_TILE, D_MODEL), BF16)],
        compiler_params=_params(("arbitrary",)),
        name=f"fourier_t{seq}",
    )(x, mods, P["norm1_w"], e_t, e_c, P["fnet_w"])


def _rope_tables(rows):
    r, col = jnp.meshgrid(jnp.arange(rows), jnp.arange(GRID_W), indexing="ij")
    r = r.reshape(-1).astype(F32)
    col = col.reshape(-1).astype(F32)
    n_freq = QK_ROPE // 4
    inv = ROPE_THETA ** (-jnp.arange(n_freq, dtype=F32) / n_freq)
    ang = jnp.concatenate([r[:, None] * inv, col[:, None] * inv], axis=-1)
    cos = jnp.repeat(jnp.cos(ang), 2, axis=-1)
    sin = jnp.repeat(jnp.sin(ang), 2, axis=-1)
    t = cos.shape[0]
    pad = LANES - QK_NOPE - QK_ROPE
    cos_p = jnp.concatenate([jnp.ones((t, QK_NOPE), F32), cos, jnp.ones((t, pad), F32)], axis=-1)
    sin_p = jnp.concatenate([jnp.zeros((t, QK_NOPE), F32), sin, jnp.zeros((t, pad), F32)], axis=-1)
    return cos_p, sin_p


def _pair_swap(w):
    wr = w.reshape(w.shape[:-1] + (w.shape[-1] // 2, 2))
    return jnp.stack([-wr[..., 1], wr[..., 0]], axis=-1).reshape(w.shape)


def _mla_weights(P, j, rope, seq_rows):
    pad = LANES - QK_NOPE - QK_ROPE
    wuq = P["mla_wuq"][j].reshape(Q_LORA, N_HEADS, QK_NOPE + QK_ROPE)
    q_slots = jnp.pad(wuq, ((0, 0), (0, 0), (0, pad))).reshape(Q_LORA, N_HEADS * LANES)
    wdkv = P["mla_wdkv"][j]
    kr_slot = jnp.pad(wdkv[:, KV_LORA:], ((0, 0), (QK_NOPE, pad)))
    parts1 = [P["mla_wdq"][j], wdkv[:, :KV_LORA], kr_slot]
    parts2 = [q_slots]
    if rope:
        parts1.append(jnp.pad(_pair_swap(wdkv[:, KV_LORA:]), ((0, 0), (QK_NOPE, pad))))
        q_sw = jnp.pad(_pair_swap(wuq[:, :, QK_NOPE:]), ((0, 0), (0, 0), (QK_NOPE, pad)))
        parts2.append(q_sw.reshape(Q_LORA, N_HEADS * LANES))
        cos, sin = _rope_tables(seq_rows // GRID_W)
    else:
        cos = sin = jnp.zeros((PROJ_TILE, LANES), F32)
    wukv = P["mla_wukv"][j].reshape(KV_LORA, N_HEADS, QK_NOPE + V_HEAD)
    wk = jnp.pad(wukv[:, :, :QK_NOPE], ((0, 0), (0, 0), (0, LANES - QK_NOPE)))
    return dict(
        norm1_w=P["norm1_w"], q_norm=P["mla_q_norm"][j][None, :], kv_norm=P["mla_kv_norm"][j][None, :],
        w1=jnp.concatenate(parts1, axis=1), w2=jnp.concatenate(parts2, axis=1),
        wk=wk.reshape(KV_LORA, N_HEADS * LANES),
        wv=wukv[:, :, QK_NOPE:].reshape(KV_LORA, N_HEADS * V_HEAD), cos=cos, sin=sin)


def _mla_layer(x, mods, li, cond_of_row, batch, seq, P, cache):
    j = li // 4
    m = x.shape[0]
    W = _mla_weights(P, j, cache is not None, seq)
    q, k, v, ckv, krope = _mla_proj_call(x, mods, li, cond_of_row, W, cache is not None)
    q = q.reshape(N_HEADS, batch, seq, LANES)
    k = k.reshape(N_HEADS, batch, seq, LANES)
    v = v.reshape(HEAD_PAIRS, batch, seq, LANES)
    if cache is None:
        o = _attn_call(q, k, v, seq, HEAD_PAIRS)
    else:
        c_ckv, c_krope = cache
        past = c_ckv.shape[1]
        pad = LANES - QK_NOPE - QK_ROPE
        kr_pad = jnp.pad(c_krope.reshape(batch * past, QK_ROPE), ((0, 0), (QK_NOPE, pad)))
        kc, vc = _kv_cache_call(c_ckv.reshape(batch * past, KV_LORA), kr_pad, W)
        k = jnp.concatenate([kc.reshape(N_HEADS, batch, past, LANES), k], axis=2)
        v = jnp.concatenate([vc.reshape(HEAD_PAIRS, batch, past, LANES), v], axis=2)
        o = _attn_call(q, k, v, 512, 2)
    x = _proj_res_call(x, mods, li, cond_of_row, o, P["mla_wo"], f"mla_out_{m}")
    return x, ckv, krope


def _run_pass(x, mods, cond_of_row, batch, seq, P, cache):
    ckv = krope = None
    for li in range(DEPTH):
        kind = li % 4
        if kind == 0:
            x, ckv, krope = _mla_layer(x, mods, li, cond_of_row, batch, seq, P, cache)
        elif kind == 1:
            x = _pool_call(x, mods, li, cond_of_row, seq, P)
        elif kind == 2:
            x = _fourier_call(x, mods, li, cond_of_row, seq, P)
        else:
            x = _sconv_call(x, mods, li, cond_of_row, seq, P)
        x = _ffn_call(x, mods, li, cond_of_row, seq, P, final_norm=(li == DEPTH - 1))
    return x, ckv, krope


def kernel(x_prompt, x_sample, cache_ckv, cache_krope, c, c_ctx, norm1_w, norm2_w, ada_w, ada_b, mla_wdq, mla_q_norm, mla_wuq, mla_wdkv, mla_kv_norm, mla_wukv, mla_wo, pool_w, pool_scale, fnet_w, sconv_win, sconv_conv, sconv_wout, ffn_up, ffn_conv_w, ffn_conv_b, ffn_down, final_norm_w):
    row = lambda w: w.reshape(w.shape[0], 1, w.shape[1])
    P = dict(norm1_w=row(norm1_w), norm2_w=row(norm2_w), mla_wdq=mla_wdq, mla_q_norm=mla_q_norm,
             mla_wuq=mla_wuq, mla_wdkv=mla_wdkv, mla_kv_norm=mla_kv_norm, mla_wukv=mla_wukv, mla_wo=mla_wo,
             pool_w=pool_w, pool_scale=row(pool_scale), fnet_w=fnet_w, sconv_win=sconv_win,
             sconv_conv=sconv_conv, sconv_wout=sconv_wout, ffn_conv_w=ffn_conv_w,
             ffn_conv_b=ffn_conv_b, ffn_down=ffn_down, final_norm_w=final_norm_w,
             ffn_up_cm=_chunk_major(ffn_up, 2), sconv_win_cm=_chunk_major(sconv_win, 3),
             ffn_down_cm=_down_chunks(ffn_down), sconv_wout_cm=_down_chunks(sconv_wout))
    b_ctx, t_ctx, _ = x_prompt.shape
    b_lat, t_lat, _ = x_sample.shape
    assert ROW_TILE % t_ctx == 0 and t_lat % ROW_TILE == 0 and b_lat + 1 <= N_COND

    conds = jnp.concatenate([c_ctx[None, :], c, jnp.zeros((N_COND - 1 - b_lat, D_MODEL), F32)], axis=0)
    mods = _ada_call(conds, ada_w, ada_b).reshape(DEPTH, N_COND, 6, D_MODEL)

    xp, ckv, krope = _run_pass(x_prompt.reshape(b_ctx * t_ctx, D_MODEL), mods, lambda r: 0,
                               b_ctx, t_ctx, P, None)
    xs, _, _ = _run_pass(x_sample.reshape(b_lat * t_lat, D_MODEL), mods,
                         lambda r: 1 + r // t_lat, b_lat, t_lat, P,
                         (cache_ckv[:, 0], cache_krope[:, 0]))
    return (xp.reshape(b_ctx, t_ctx, D_MODEL), xs.reshape(b_lat, t_lat, D_MODEL),
            ckv.reshape(b_ctx, 1, t_ctx, KV_LORA), krope.reshape(b_ctx, 1, t_ctx, QK_ROPE))
```

```python
import functools
import math

import numpy as np
import jax
import jax.numpy as jnp
from jax import lax
from jax.experimental import pallas as pl
from jax.experimental.pallas import tpu as pltpu

F32 = jnp.float32
BF16 = jnp.bfloat16

D_MODEL = 1024
N_HEADS = 16
QK_NOPE = 64
QK_ROPE = 32
V_HEAD = 64
Q_LORA = 384
KV_LORA = 256
ROPE_THETA = 10000.0
GRID_W = 64
POOL_WINDOWS = (2, 4, 8, 16)
N_GROUPS = 4
GROUP = D_MODEL // N_GROUPS
D_FF = 2816
EPS = 1e-6
DEPTH = 4

LANES = 128
HEAD_PAIRS = N_HEADS // 2
N_COND = 8
ROW_TILE = 1024
PROJ_TILE = 512
FF_TILE = 256
ROW_BLOCK = 256
HALO = 8
RELAYOUT_ROWS = 128
ADA_TILE = 1536
VMEM_LIMIT = 52 * 1024 * 1024


def _params(sem):
    return pltpu.CompilerParams(dimension_semantics=sem, vmem_limit_bytes=VMEM_LIMIT)


def _dot(a, b):
    return jnp.dot(a, b, preferred_element_type=F32)


def _rms(x, w):
    return x * lax.rsqrt(jnp.mean(x * x, axis=-1, keepdims=True) + EPS) * w


def _norm_mod(x, nw, mod, base):
    return _rms(x, nw) * (1.0 + mod[base + 1:base + 2, :]) + mod[base:base + 1, :]


def _dwconv_rows(g, cw, seq):
    rows = g.shape[0]
    t = lax.broadcasted_iota(jnp.int32, (rows, 1), 0) & (seq - 1)
    prev = jnp.where(t == 0, 0.0, pltpu.roll(g, 1, axis=0))
    nxt = jnp.where(t == seq - 1, 0.0, pltpu.roll(g, rows - 1, axis=0))
    return prev * cw[0:1, :] + g * cw[1:2, :] + nxt * cw[2:3, :]


def _gelu_exact(x):
    return 0.5 * x * (1.0 + lax.erf(x * np.float32(math.sqrt(0.5))))


def _ada_kernel(c_ref, w_ref, b_ref, o_ref):
    s = jax.nn.silu(c_ref[...]).astype(BF16)
    o_ref[...] = _dot(s, w_ref[...].astype(BF16)) + b_ref[...]


def _ada_call(conds, ada_w, ada_b):
    n_out = ada_w.shape[-1]
    return pl.pallas_call(
        _ada_kernel,
        grid=(DEPTH, n_out // ADA_TILE),
        in_specs=[
            pl.BlockSpec((N_COND, D_MODEL), lambda l, j: (0, 0)),
            pl.BlockSpec((None, D_MODEL, ADA_TILE), lambda l, j: (l, 0, j)),
            pl.BlockSpec((None, 1, ADA_TILE), lambda l, j: (l, 0, j)),
        ],
        out_specs=pl.BlockSpec((None, N_COND, ADA_TILE), lambda l, j: (l, 0, j)),
        out_shape=jax.ShapeDtypeStruct((DEPTH, N_COND, n_out), F32),
        compiler_params=_params(("arbitrary", "arbitrary")),
        name="ada",
    )(conds, ada_w, ada_b.reshape(DEPTH, 1, n_out))


def _mod_spec(li, cond_of_row, tile, n_grid):
    if n_grid == 1:
        return pl.BlockSpec((None, None, 6, D_MODEL), lambda i: (li, cond_of_row(i * tile), 0, 0))
    return pl.BlockSpec((None, None, 6, D_MODEL), lambda i, j: (li, cond_of_row(i * tile), 0, 0))


def _conv_window(w, cw, t0, seq):
    n = w.shape[0]
    mid = slice(HALO, HALO + ROW_BLOCK)
    t = (t0 + lax.broadcasted_iota(jnp.int32, (ROW_BLOCK, 1), 0)) & (seq - 1)
    prev = jnp.where(t == 0, 0.0, pltpu.roll(w, 1, axis=0)[mid])
    nxt = jnp.where(t == seq - 1, 0.0, pltpu.roll(w, n - 1, axis=0)[mid])
    return prev * cw[0:1, :] + w[mid] * cw[1:2, :] + nxt * cw[2:3, :]


def _ffn_act(win, mid, cw, cb, t0, seq):
    return _gelu_exact(_conv_window(win(0), cw, t0, seq) + cb) * mid(1)


def _sconv_act(win, mid, cw, cb, t0, seq):
    return mid(0) * _conv_window(win(1) * win(2), cw, t0, seq)


def _ffn_kernel(x_ref, mod_ref, nw_ref, up_hbm, cw_ref, cb_ref, down_hbm, fnw_ref, o_ref, *scratch, **cfg):
    _glu_body(x_ref, mod_ref, nw_ref, up_hbm, cw_ref, cb_ref, down_hbm, fnw_ref, o_ref, *scratch,
              n_up=2, act=_ffn_act, mod_base=3, **cfg)


def _sconv_kernel(x_ref, mod_ref, nw_ref, up_hbm, cw_ref, down_hbm, o_ref, *scratch, **cfg):
    _glu_body(x_ref, mod_ref, nw_ref, up_hbm, cw_ref, None, down_hbm, None, o_ref, *scratch,
              n_up=3, act=_sconv_act, mod_base=0, final_norm=False, **cfg)


def _glu_body(x_ref, mod_ref, nw_ref, up_ref, cw_ref, cb_ref, down_ref, fnw_ref, o_ref,
              h_scr, acc_scr, gu0, gu1, *, n_up, n_chunks, act, seq, mod_base, final_norm):
    gu = (gu0, gu1)
    n_blocks = ROW_TILE // ROW_BLOCK

    def up(c, slot, r):
        rows = slice(r * ROW_BLOCK, (r + 1) * ROW_BLOCK)
        res = _dot(h_scr[rows, :], up_ref[c])
        for p in range(n_up):
            gu[slot][p, HALO + r * ROW_BLOCK:HALO + (r + 1) * ROW_BLOCK, :] = res[:, p * FF_TILE:(p + 1) * FF_TILE]

    def activate(c, slot, r):
        win = lambda p: gu[slot][p, r * ROW_BLOCK:(r + 1) * ROW_BLOCK + 2 * HALO, :]
        mid = lambda p: gu[slot][p, HALO + r * ROW_BLOCK:HALO + (r + 1) * ROW_BLOCK, :]
        cb = None if cb_ref is None else cb_ref[c]
        return act(win, mid, cw_ref[c], cb, r * ROW_BLOCK, seq).astype(BF16)

    def down(a, c, r):
        rows = slice(r * ROW_BLOCK, (r + 1) * ROW_BLOCK)
        acc_scr[rows, :] += _dot(a, down_ref[c])

    def stage(s, slot):
        for r in range(n_blocks):
            a = activate(s - 1, 1 - slot, r)
            up(s, slot, r)
            down(a, s - 1, r)

    h_scr[...] = _norm_mod(x_ref[...], nw_ref[...], mod_ref[...], mod_base).astype(BF16)
    acc_scr[...] = jnp.zeros_like(acc_scr)
    for g in gu:
        g[:, :HALO, :] = jnp.zeros((n_up, HALO, FF_TILE), F32)
        g[:, HALO + ROW_TILE:, :] = jnp.zeros((n_up, HALO, FF_TILE), F32)

    for r in range(n_blocks):
        up(0, 0, r)

    n_pairs = (n_chunks - 1) // 2

    def pair(i, carry):
        s = 2 * i + 1
        stage(s, 1)
        stage(s + 1, 0)
        return carry

    lax.fori_loop(0, n_pairs, pair, 0)
    for s in range(2 * n_pairs + 1, n_chunks):
        stage(s, s % 2)
    last = n_chunks - 1
    for r in range(n_blocks):
        down(activate(last, last % 2, r), last, r)

    y = x_ref[...] + mod_ref[mod_base + 2:mod_base + 3, :] * acc_scr[...]
    if final_norm:
        y = _rms(y, fnw_ref[...])
    o_ref[...] = y


def _chunked(w, n_chunks):
    return w.reshape(w.shape[0], n_chunks, FF_TILE).transpose(1, 0, 2)


def _glu_call(kernel_fn, name, x, mods, li, cond_of_row, n_up, n_chunks, norm_w, operands, specs, cfg):
    m = x.shape[0]
    whole = lambda a: pl.BlockSpec(a.shape, lambda i: (0,) * a.ndim)
    in_specs = [pl.BlockSpec((ROW_TILE, D_MODEL), lambda i: (i, 0)),
                _mod_spec(li, cond_of_row, ROW_TILE, 1),
                pl.BlockSpec((None, 1, D_MODEL), lambda i: (li, 0, 0))]
    layer_w = lambda a, l: pl.BlockSpec((None,) + a.shape[1:], lambda i: (l,) + (0,) * (a.ndim - 1),
                                        pipeline_mode=pl.Buffered(1))
    in_specs += [whole(a) if s is None else layer_w(a, s) for a, s in zip(operands, specs)]
    gu_shape = (n_up, ROW_TILE + 2 * HALO, FF_TILE)
    return pl.pallas_call(
        functools.partial(kernel_fn, n_chunks=n_chunks, **cfg),
        grid=(m // ROW_TILE,),
        in_specs=in_specs,
        out_specs=pl.BlockSpec((ROW_TILE, D_MODEL), lambda i: (i, 0)),
        out_shape=jax.ShapeDtypeStruct((m, D_MODEL), F32),
        scratch_shapes=[
            pltpu.VMEM((ROW_TILE, D_MODEL), BF16),
            pltpu.VMEM((ROW_TILE, D_MODEL), F32),
            pltpu.VMEM(gu_shape, F32), pltpu.VMEM(gu_shape, F32),
        ],
        compiler_params=_params(("arbitrary",)),
        name=name,
    )(x, mods, norm_w, *operands)


def _chunk_major_kernel(w_ref, o_ref, *, n_up, n_chunks):
    for c in range(n_chunks):
        for p in range(n_up):
            col = (p * n_chunks + c) * FF_TILE
            o_ref[c, :, p * FF_TILE:(p + 1) * FF_TILE] = w_ref[:, col:col + FF_TILE].astype(BF16)


def _chunk_major(w, n_up, name):
    l, d, cols = w.shape
    n_chunks = cols // (n_up * FF_TILE)
    return pl.pallas_call(
        functools.partial(_chunk_major_kernel, n_up=n_up, n_chunks=n_chunks),
        grid=(l, d // RELAYOUT_ROWS),
        in_specs=[pl.BlockSpec((None, RELAYOUT_ROWS, cols), lambda li, k: (li, k, 0))],
        out_specs=pl.BlockSpec((None, n_chunks, RELAYOUT_ROWS, n_up * FF_TILE), lambda li, k: (li, 0, k, 0)),
        out_shape=jax.ShapeDtypeStruct((l, n_chunks, d, n_up * FF_TILE), BF16),
        compiler_params=_params(("arbitrary", "arbitrary")),
        name=name,
    )(w)


def _down_chunks(w):
    l, rows, d = w.shape
    return w.reshape(l, rows // FF_TILE, FF_TILE, d).astype(BF16)


def _ffn_call(x, mods, li, cond_of_row, seq, P, final_norm):
    n = D_FF // FF_TILE
    operands = [P["ffn_up_cm"], _chunked(P["ffn_conv_w"][li], n), _chunked(P["ffn_conv_b"][li][None, :], n),
                P["ffn_down_cm"], P["final_norm_w"].reshape(1, D_MODEL)]
    return _glu_call(_ffn_kernel, f"ffn{li}", x, mods, li, cond_of_row, 2, n, P["norm2_w"],
                     operands, [li, None, None, li, None], dict(seq=seq, final_norm=final_norm))


def _sconv_call(x, mods, li, cond_of_row, seq, P):
    lj = li // 4
    n = D_MODEL // FF_TILE
    operands = [P["sconv_win_cm"], _chunked(P["sconv_conv"][lj], n), P["sconv_wout_cm"]]
    return _glu_call(_sconv_kernel, f"sconv{li}", x, mods, li, cond_of_row, 3, n, P["norm1_w"],
                     operands, [lj, None, lj], dict(seq=seq))


def _emit_kv(ckv, krope_pad, wk_ref, wv_ref, k_ref, v_ref):
    c = ckv.astype(BF16)
    kx = _dot(c, wk_ref[...].astype(BF16))
    vx = _dot(c, wv_ref[...].astype(BF16))
    for h in range(N_HEADS):
        k_ref[h] = (kx[:, h * LANES:(h + 1) * LANES] + krope_pad).astype(BF16)
    for p in range(HEAD_PAIRS):
        v_ref[p] = vx[:, p * LANES:(p + 1) * LANES].astype(BF16)


def _mla_proj_kernel(x_ref, mod_ref, nw_ref, w1_ref, qn_ref, kvn_ref, w2_ref, wk_ref, wv_ref, cos_ref,
                     sin_ref, q_ref, k_ref, v_ref, ckv_ref, kr_ref, *, rope):
    h = _norm_mod(x_ref[...], nw_ref[...], mod_ref[...], 0).astype(BF16)
    z = _dot(h, w1_ref[...].astype(BF16))
    cq = _rms(z[:, :Q_LORA], qn_ref[...]).astype(BF16)
    ckv = _rms(z[:, Q_LORA:Q_LORA + KV_LORA], kvn_ref[...])
    kr = z[:, Q_LORA + KV_LORA:Q_LORA + KV_LORA + LANES]
    if rope:
        cos, sin = cos_ref[...], sin_ref[...]
        kr = kr * cos + z[:, Q_LORA + KV_LORA + LANES:] * sin
    for p in range(HEAD_PAIRS):
        q2 = _dot(cq, w2_ref[:, 2 * p * LANES:(2 * p + 2) * LANES].astype(BF16))
        if rope:
            off = N_HEADS * LANES
            qs = _dot(cq, w2_ref[:, off + 2 * p * LANES:off + (2 * p + 2) * LANES].astype(BF16))
        for e in range(2):
            qh = q2[:, e * LANES:(e + 1) * LANES]
            if rope:
                qh = qh * cos + qs[:, e * LANES:(e + 1) * LANES] * sin
            q_ref[2 * p + e] = qh.astype(BF16)
    ckv_ref[...] = ckv
    kr_ref[...] = kr[:, QK_NOPE:QK_NOPE + QK_ROPE]
    _emit_kv(ckv, kr, wk_ref, wv_ref, k_ref, v_ref)


def _mla_proj_call(x, mods, li, cond_of_row, W, rope):
    m = x.shape[0]
    n1 = W["w1"].shape[1]
    n2 = W["w2"].shape[1]
    const = lambda i: (0, 0)
    pos_tiles = W["cos"].shape[0] // PROJ_TILE
    pos = lambda i: (i % pos_tiles, 0)
    return pl.pallas_call(
        functools.partial(_mla_proj_kernel, rope=rope),
        grid=(m // PROJ_TILE,),
        in_specs=[
            pl.BlockSpec((PROJ_TILE, D_MODEL), lambda i: (i, 0)),
            _mod_spec(li, cond_of_row, PROJ_TILE, 1),
            pl.BlockSpec((None, 1, D_MODEL), lambda i: (li, 0, 0)),
            pl.BlockSpec((D_MODEL, n1), const),
            pl.BlockSpec((1, Q_LORA), const),
            pl.BlockSpec((1, KV_LORA), const),
            pl.BlockSpec((Q_LORA, n2), const),
            pl.BlockSpec((KV_LORA, N_HEADS * LANES), const),
            pl.BlockSpec((KV_LORA, HEAD_PAIRS * LANES), const),
            pl.BlockSpec((PROJ_TILE, LANES), pos),
            pl.BlockSpec((PROJ_TILE, LANES), pos),
        ],
        out_specs=[
            pl.BlockSpec((N_HEADS, PROJ_TILE, LANES), lambda i: (0, i, 0)),
            pl.BlockSpec((N_HEADS, PROJ_TILE, LANES), lambda i: (0, i, 0)),
            pl.BlockSpec((HEAD_PAIRS, PROJ_TILE, LANES), lambda i: (0, i, 0)),
            pl.BlockSpec((PROJ_TILE, KV_LORA), lambda i: (i, 0)),
            pl.BlockSpec((PROJ_TILE, QK_ROPE), lambda i: (i, 0)),
        ],
        out_shape=[
            jax.ShapeDtypeStruct((N_HEADS, m, LANES), BF16),
            jax.ShapeDtypeStruct((N_HEADS, m, LANES), BF16),
            jax.ShapeDtypeStruct((HEAD_PAIRS, m, LANES), BF16),
            jax.ShapeDtypeStruct((m, KV_LORA), F32),
            jax.ShapeDtypeStruct((m, QK_ROPE), F32),
        ],
        compiler_params=_params(("arbitrary",)),
        name="mla_proj_rope" if rope else "mla_proj",
    )(x, mods, W["norm1_w"], W["w1"], W["q_norm"], W["kv_norm"], W["w2"], W["wk"], W["wv"],
      W["cos"], W["sin"])


def _kv_cache_kernel(ckv_ref, kr_ref, wk_ref, wv_ref, k_ref, v_ref):
    _emit_kv(ckv_ref[...], kr_ref[...], wk_ref, wv_ref, k_ref, v_ref)


def _kv_cache_call(ckv, krope_pad, W):
    m = ckv.shape[0]
    const = lambda i: (0, 0)
    return pl.pallas_call(
        _kv_cache_kernel,
        grid=(1,),
        in_specs=[
            pl.BlockSpec((m, KV_LORA), const),
            pl.BlockSpec((m, LANES), const),
            pl.BlockSpec((KV_LORA, N_HEADS * LANES), const),
            pl.BlockSpec((KV_LORA, HEAD_PAIRS * LANES), const),
        ],
        out_specs=[
            pl.BlockSpec((N_HEADS, m, LANES), lambda i: (0, 0, 0)),
            pl.BlockSpec((HEAD_PAIRS, m, LANES), lambda i: (0, 0, 0)),
        ],
        out_shape=[
            jax.ShapeDtypeStruct((N_HEADS, m, LANES), BF16),
            jax.ShapeDtypeStruct((HEAD_PAIRS, m, LANES), BF16),
        ],
        compiler_params=_params(("arbitrary",)),
        name="kv_cache",
    )(ckv, krope_pad, W["wk"], W["wv"])


def _attn_kernel(q_ref, k_ref, v_ref, o_ref, *, n_pairs):
    scale = np.float32(1.0 / math.sqrt(QK_NOPE + QK_ROPE))
    for p in range(n_pairs):
        v = v_ref[p]
        outs = []
        for e in range(2):
            s = lax.dot_general(q_ref[2 * p + e], k_ref[2 * p + e], (((1,), (1,)), ((), ())),
                                preferred_element_type=F32) * scale
            w = jnp.exp(s - jnp.max(s, axis=-1, keepdims=True))
            l = jnp.sum(w, axis=-1, keepdims=True)
            outs.append(_dot(w.astype(BF16), v) / l)
        lane = lax.broadcasted_iota(jnp.int32, outs[0].shape, 1)
        o_ref[:, p * LANES:(p + 1) * LANES] = jnp.where(lane < V_HEAD, outs[0], outs[1]).astype(BF16)


def _attn_call(q, k, v, q_tile, n_pairs):
    _, b, t, _ = q.shape
    s = k.shape[2]
    nq = t // q_tile
    return pl.pallas_call(
        functools.partial(_attn_kernel, n_pairs=n_pairs),
        grid=(b, HEAD_PAIRS // n_pairs, nq),
        in_specs=[
            pl.BlockSpec((2 * n_pairs, None, q_tile, LANES), lambda bi, g, qi: (g, bi, qi, 0)),
            pl.BlockSpec((2 * n_pairs, None, s, LANES), lambda bi, g, qi: (g, bi, 0, 0)),
            pl.BlockSpec((n_pairs, None, s, LANES), lambda bi, g, qi: (g, bi, 0, 0)),
        ],
        out_specs=pl.BlockSpec((q_tile, n_pairs * LANES), lambda bi, g, qi: (bi * nq + qi, g)),
        out_shape=jax.ShapeDtypeStruct((b * t, N_HEADS * V_HEAD), BF16),
        compiler_params=_params(("arbitrary", "arbitrary", "arbitrary")),
        name=f"attn_s{s}",
    )(q, k, v)


def _proj_res_kernel(x_ref, mod_ref, a_ref, w_ref, o_ref):
    o_ref[...] = x_ref[...] + mod_ref[2:3, :] * _dot(a_ref[...], w_ref[...].astype(BF16))


def _proj_res_call(x, mods, li, cond_of_row, a, w, name):
    m = x.shape[0]
    return pl.pallas_call(
        _proj_res_kernel,
        grid=(m // ROW_TILE,),
        in_specs=[
            pl.BlockSpec((ROW_TILE, D_MODEL), lambda i: (i, 0)),
            _mod_spec(li, cond_of_row, ROW_TILE, 1),
            pl.BlockSpec((ROW_TILE, a.shape[1]), lambda i: (i, 0)),
            pl.BlockSpec((None,) + w.shape[1:], lambda i: (0, 0, 0)),
        ],
        out_specs=pl.BlockSpec((ROW_TILE, D_MODEL), lambda i: (i, 0)),
        out_shape=jax.ShapeDtypeStruct((m, D_MODEL), F32),
        compiler_params=_params(("arbitrary",)),
        name=name,
    )(x, mods, a, w)


def _pool_kernel(x_ref, mod_ref, nw_ref, w_ref, sc_ref, o_ref, band_scr, y_scr, *, seq):
    @pl.when(pl.program_id(0) == 0)
    def _():
        r = lax.broadcasted_iota(jnp.int32, (seq, seq), 0)
        c = lax.broadcasted_iota(jnp.int32, (seq, seq), 1)
        for g, win in enumerate(POOL_WINDOWS):
            inside = (c >= r - win // 2) & (c < r + win - win // 2)
            band_scr[g] = jnp.where(inside, 1.0, 0.0).astype(BF16)

    x = x_ref[...]
    h = _norm_mod(x, nw_ref[...], mod_ref[...], 0)
    h_hi = h.astype(BF16)
    h_lo = (h - h_hi.astype(F32)).astype(BF16)
    t = lax.broadcasted_iota(jnp.int32, (seq, 1), 0)
    for g, win in enumerate(POOL_WINDOWS):
        cnt = (jnp.minimum(t + (win - win // 2), seq) - jnp.maximum(t - win // 2, 0)).astype(F32)
        cols = slice(g * GROUP, (g + 1) * GROUP)
        wg = w_ref[g].astype(BF16)
        for s in range(x.shape[0] // seq):
            rows = slice(s * seq, (s + 1) * seq)
            tot = _dot(band_scr[g], h_hi[rows, cols]) + _dot(band_scr[g], h_lo[rows, cols])
            pooled = (tot / cnt - h[rows, cols]).astype(BF16)
            y_scr[rows, cols] = _dot(pooled, wg)
    o_ref[...] = x + mod_ref[2:3, :] * (y_scr[...] * sc_ref[...])


def _pool_call(x, mods, li, cond_of_row, seq, P):
    m = x.shape[0]
    lj = li // 4
    return pl.pallas_call(
        functools.partial(_pool_kernel, seq=seq),
        grid=(m // ROW_TILE,),
        in_specs=[
            pl.BlockSpec((ROW_TILE, D_MODEL), lambda i: (i, 0)),
            _mod_spec(li, cond_of_row, ROW_TILE, 1),
            pl.BlockSpec((None, 1, D_MODEL), lambda i: (li, 0, 0)),
            pl.BlockSpec((None, N_GROUPS, GROUP, GROUP), lambda i: (lj, 0, 0, 0)),
            pl.BlockSpec((None, 1, D_MODEL), lambda i: (lj, 0, 0)),
        ],
        out_specs=pl.BlockSpec((ROW_TILE, D_MODEL), lambda i: (i, 0)),
        out_shape=jax.ShapeDtypeStruct((m, D_MODEL), F32),
        scratch_shapes=[pltpu.VMEM((N_GROUPS, seq, seq), BF16), pltpu.VMEM((ROW_TILE, D_MODEL), F32)],
        compiler_params=_params(("arbitrary",)),
        name=f"pool_t{seq}",
    )(x, mods, P["norm1_w"], P["pool_w"], P["pool_scale"])


def _dft_tables(seq):
    def cs(n):
        k = np.arange(n, dtype=np.int64)
        ang = 2.0 * np.pi * ((k[:, None] * k[None, :]) % n) / n
        return np.cos(ang), np.sin(ang)
    ct, st = cs(seq)
    cc, sc = cs(GROUP)
    e_t = jnp.asarray(np.concatenate([ct, st], axis=0), dtype=F32)
    e_c = jnp.asarray(np.concatenate([cc, -sc], axis=0), dtype=F32)
    return e_t.astype(BF16), e_c.astype(BF16)


def _fourier_kernel(x_ref, mod_ref, nw_ref, et_ref, ec_ref, w_ref, o_ref, f_scr, *, seq):
    x = x_ref[...]
    h = _norm_mod(x, nw_ref[...], mod_ref[...], 0).astype(BF16)
    norm = np.float32(1.0 / math.sqrt(seq * GROUP))
    for s in range(x.shape[0] // seq):
        rows = slice(s * seq, (s + 1) * seq)
        for g in range(N_GROUPS):
            cols = slice(g * GROUP, (g + 1) * GROUP)
            y = _dot(et_ref[...], h[rows, cols]).astype(BF16)
            f = _dot(y[:seq], ec_ref[:GROUP, :]) + _dot(y[seq:], ec_ref[GROUP:, :])
            f_scr[rows, cols] = (f * norm).astype(BF16)
    o_ref[...] = x + mod_ref[2:3, :] * _dot(f_scr[...], w_ref[...].astype(BF16))


def _fourier_call(x, mods, li, cond_of_row, seq, P):
    m = x.shape[0]
    lj = li // 4
    e_t, e_c = _dft_tables(seq)
    return pl.pallas_call(
        functools.partial(_fourier_kernel, seq=seq),
        grid=(m // ROW_TILE,),
        in_specs=[
            pl.BlockSpec((ROW_TILE, D_MODEL), lambda i: (i, 0)),
            _mod_spec(li, cond_of_row, ROW_TILE, 1),
            pl.BlockSpec((None, 1, D_MODEL), lambda i: (li, 0, 0)),
            pl.BlockSpec((2 * seq, seq), lambda i: (0, 0)),
            pl.BlockSpec((2 * GROUP, GROUP), lambda i: (0, 0)),
            pl.BlockSpec((None, D_MODEL, D_MODEL), lambda i: (lj, 0, 0)),
        ],
        out_specs=pl.BlockSpec((ROW_TILE, D_MODEL), lambda i: (i, 0)),
        out_shape=jax.ShapeDtypeStruct((m, D_MODEL), F32),
        scratch_shapes=[pltpu.VMEM((ROW_TILE, D_MODEL), BF16)],
        compiler_params=_params(("arbitrary",)),
        name=f"fourier_t{seq}",
    )(x, mods, P["norm1_w"], e_t, e_c, P["fnet_w"])


def _rope_tables(rows):
    r, col = jnp.meshgrid(jnp.arange(rows), jnp.arange(GRID_W), indexing="ij")
    r = r.reshape(-1).astype(F32)
    col = col.reshape(-1).astype(F32)
    n_freq = QK_ROPE // 4
    inv = ROPE_THETA ** (-jnp.arange(n_freq, dtype=F32) / n_freq)
    ang = jnp.concatenate([r[:, None] * inv, col[:, None] * inv], axis=-1)
    cos = jnp.repeat(jnp.cos(ang), 2, axis=-1)
    sin = jnp.repeat(jnp.sin(ang), 2, axis=-1)
    t = cos.shape[0]
    pad = LANES - QK_NOPE - QK_ROPE
    cos_p = jnp.concatenate([jnp.ones((t, QK_NOPE), F32), cos, jnp.ones((t, pad), F32)], axis=-1)
    sin_p = jnp.concatenate([jnp.zeros((t, QK_NOPE), F32), sin, jnp.zeros((t, pad), F32)], axis=-1)
    return cos_p, sin_p


def _pair_swap(w):
    wr = w.reshape(w.shape[:-1] + (w.shape[-1] // 2, 2))
    return jnp.stack([-wr[..., 1], wr[..., 0]], axis=-1).reshape(w.shape)


def _mla_weights(P, j, rope, seq_rows):
    pad = LANES - QK_NOPE - QK_ROPE
    wuq = P["mla_wuq"][j].reshape(Q_LORA, N_HEADS, QK_NOPE + QK_ROPE)
    q_slots = jnp.pad(wuq, ((0, 0), (0, 0), (0, pad))).reshape(Q_LORA, N_HEADS * LANES)
    wdkv = P["mla_wdkv"][j]
    kr_slot = jnp.pad(wdkv[:, KV_LORA:], ((0, 0), (QK_NOPE, pad)))
    parts1 = [P["mla_wdq"][j], wdkv[:, :KV_LORA], kr_slot]
    parts2 = [q_slots]
    if rope:
        parts1.append(jnp.pad(_pair_swap(wdkv[:, KV_LORA:]), ((0, 0), (QK_NOPE, pad))))
        q_sw = jnp.pad(_pair_swap(wuq[:, :, QK_NOPE:]), ((0, 0), (0, 0), (QK_NOPE, pad)))
        parts2.append(q_sw.reshape(Q_LORA, N_HEADS * LANES))
        cos, sin = _rope_tables(seq_rows // GRID_W)
    else:
        cos = sin = jnp.zeros((PROJ_TILE, LANES), F32)
    wukv = P["mla_wukv"][j].reshape(KV_LORA, N_HEADS, QK_NOPE + V_HEAD)
    wk = jnp.pad(wukv[:, :, :QK_NOPE], ((0, 0), (0, 0), (0, LANES - QK_NOPE)))
    return dict(
        norm1_w=P["norm1_w"], q_norm=P["mla_q_norm"][j][None, :], kv_norm=P["mla_kv_norm"][j][None, :],
        w1=jnp.concatenate(parts1, axis=1), w2=jnp.concatenate(parts2, axis=1),
        wk=wk.reshape(KV_LORA, N_HEADS * LANES),
        wv=wukv[:, :, QK_NOPE:].reshape(KV_LORA, N_HEADS * V_HEAD), cos=cos, sin=sin)


def _mla_layer(x, mods, li, cond_of_row, batch, seq, P, cache):
    j = li // 4
    m = x.shape[0]
    W = _mla_weights(P, j, cache is not None, seq)
    q, k, v, ckv, krope = _mla_proj_call(x, mods, li, cond_of_row, W, cache is not None)
    q = q.reshape(N_HEADS, batch, seq, LANES)
    k = k.reshape(N_HEADS, batch, seq, LANES)
    v = v.reshape(HEAD_PAIRS, batch, seq, LANES)
    if cache is None:
        o = _attn_call(q, k, v, seq, HEAD_PAIRS)
    else:
        c_ckv, c_krope = cache
        past = c_ckv.shape[1]
        pad = LANES - QK_NOPE - QK_ROPE
        kr_pad = jnp.pad(c_krope.reshape(batch * past, QK_ROPE), ((0, 0), (QK_NOPE, pad)))
        kc, vc = _kv_cache_call(c_ckv.reshape(batch * past, KV_LORA), kr_pad, W)
        k = jnp.concatenate([kc.reshape(N_HEADS, batch, past, LANES), k], axis=2)
        v = jnp.concatenate([vc.reshape(HEAD_PAIRS, batch, past, LANES), v], axis=2)
        o = _attn_call(q, k, v, 512, 2)
    x = _proj_res_call(x, mods, li, cond_of_row, o, P["mla_wo"], f"mla_out_{m}")
    return x, ckv, krope


def _run_pass(x, mods, cond_of_row, batch, seq, P, cache):
    ckv = krope = None
    for li in range(DEPTH):
        kind = li % 4
        if kind == 0:
            x, ckv, krope = _mla_layer(x, mods, li, cond_of_row, batch, seq, P, cache)
        elif kind == 1:
            x = _pool_call(x, mods, li, cond_of_row, seq, P)
        elif kind == 2:
            x = _fourier_call(x, mods, li, cond_of_row, seq, P)
        else:
            x = _sconv_call(x, mods, li, cond_of_row, seq, P)
        x = _ffn_call(x, mods, li, cond_of_row, seq, P, final_norm=(li == DEPTH - 1))
    return x, ckv, krope


def kernel(x_prompt, x_sample, cache_ckv, cache_krope, c, c_ctx, norm1_w, norm2_w, ada_w, ada_b, mla_wdq, mla_q_norm, mla_wuq, mla_wdkv, mla_kv_norm, mla_wukv, mla_wo, pool_w, pool_scale, fnet_w, sconv_win, sconv_conv, sconv_wout, ffn_up, ffn_conv_w, ffn_conv_b, ffn_down, final_norm_w):
    row = lambda w: w.reshape(w.shape[0], 1, w.shape[1])
    P = dict(norm1_w=row(norm1_w), norm2_w=row(norm2_w), mla_wdq=mla_wdq, mla_q_norm=mla_q_norm,
             mla_wuq=mla_wuq, mla_wdkv=mla_wdkv, mla_kv_norm=mla_kv_norm, mla_wukv=mla_wukv, mla_wo=mla_wo,
             pool_w=pool_w, pool_scale=row(pool_scale), fnet_w=fnet_w, sconv_win=sconv_win,
             sconv_conv=sconv_conv, sconv_wout=sconv_wout, ffn_conv_w=ffn_conv_w,
             ffn_conv_b=ffn_conv_b, ffn_down=ffn_down, final_norm_w=final_norm_w,
             ffn_up_cm=_chunk_major(ffn_up, 2, "ffn_up_layout"),
             sconv_win_cm=_chunk_major(sconv_win, 3, "sconv_win_layout"),
             ffn_down_cm=_down_chunks(ffn_down), sconv_wout_cm=_down_chunks(sconv_wout))
    b_ctx, t_ctx, _ = x_prompt.shape
    b_lat, t_lat, _ = x_sample.shape
    assert ROW_TILE % t_ctx == 0 and t_lat % ROW_TILE == 0 and b_lat + 1 <= N_COND

    conds = jnp.concatenate([c_ctx[None, :], c, jnp.zeros((N_COND - 1 - b_lat, D_MODEL), F32)], axis=0)
    mods = _ada_call(conds, ada_w, ada_b).reshape(DEPTH, N_COND, 6, D_MODEL)

    xp, ckv, krope = _run_pass(x_prompt.reshape(b_ctx * t_ctx, D_MODEL), mods, lambda r: 0,
                               b_ctx, t_ctx, P, None)
    xs, _, _ = _run_pass(x_sample.reshape(b_lat * t_lat, D_MODEL), mods,
                         lambda r: 1 + r // t_lat, b_lat, t_lat, P,
                         (cache_ckv[:, 0], cache_krope[:, 0]))
    return (xp.reshape(b_ctx, t_ctx, D_MODEL), xs.reshape(b_lat, t_lat, D_MODEL),
            ckv.reshape(b_ctx, 1, t_ctx, KV_LORA), krope.reshape(b_ctx, 1, t_ctx, QK_ROPE))
```

```python
import functools
import math

import numpy as np
import jax
import jax.numpy as jnp
from jax import lax
from jax.experimental import pallas as pl
from jax.experimental.pallas import tpu as pltpu

F32 = jnp.float32
BF16 = jnp.bfloat16

D_MODEL = 1024
N_HEADS = 16
QK_NOPE = 64
QK_ROPE = 32
V_HEAD = 64
Q_LORA = 384
KV_LORA = 256
ROPE_THETA = 10000.0
GRID_W = 64
POOL_WINDOWS = (2, 4, 8, 16)
N_GROUPS = 4
GROUP = D_MODEL // N_GROUPS
D_FF = 2816
EPS = 1e-6
DEPTH = 4

LANES = 128
HEAD_PAIRS = N_HEADS // 2
N_COND = 8
ROW_TILE = 1024
PROJ_TILE = 512
FF_TILE = 256
ROW_BLOCK = 512
HALO = 8
RELAYOUT_ROWS = 128
ADA_TILE = 1536
VMEM_LIMIT = 52 * 1024 * 1024


def _params(sem):
    return pltpu.CompilerParams(dimension_semantics=sem, vmem_limit_bytes=VMEM_LIMIT)


def _dot(a, b):
    return jnp.dot(a, b, preferred_element_type=F32)


def _rms(x, w):
    return x * lax.rsqrt(jnp.mean(x * x, axis=-1, keepdims=True) + EPS) * w


def _norm_mod(x, nw, mod, base):
    return _rms(x, nw) * (1.0 + mod[base + 1:base + 2, :]) + mod[base:base + 1, :]


def _dwconv_rows(g, cw, seq):
    rows = g.shape[0]
    t = lax.broadcasted_iota(jnp.int32, (rows, 1), 0) & (seq - 1)
    prev = jnp.where(t == 0, 0.0, pltpu.roll(g, 1, axis=0))
    nxt = jnp.where(t == seq - 1, 0.0, pltpu.roll(g, rows - 1, axis=0))
    return prev * cw[0:1, :] + g * cw[1:2, :] + nxt * cw[2:3, :]


def _gelu_exact(x):
    return 0.5 * x * (1.0 + lax.erf(x * np.float32(math.sqrt(0.5))))


def _ada_kernel(c_ref, w_ref, b_ref, o_ref):
    s = jax.nn.silu(c_ref[...]).astype(BF16)
    o_ref[...] = _dot(s, w_ref[...].astype(BF16)) + b_ref[...]


def _ada_call(conds, ada_w, ada_b):
    n_out = ada_w.shape[-1]
    return pl.pallas_call(
        _ada_kernel,
        grid=(DEPTH, n_out // ADA_TILE),
        in_specs=[
            pl.BlockSpec((N_COND, D_MODEL), lambda l, j: (0, 0)),
            pl.BlockSpec((None, D_MODEL, ADA_TILE), lambda l, j: (l, 0, j)),
            pl.BlockSpec((None, 1, ADA_TILE), lambda l, j: (l, 0, j)),
        ],
        out_specs=pl.BlockSpec((None, N_COND, ADA_TILE), lambda l, j: (l, 0, j)),
        out_shape=jax.ShapeDtypeStruct((DEPTH, N_COND, n_out), F32),
        compiler_params=_params(("arbitrary", "arbitrary")),
        name="ada",
    )(conds, ada_w, ada_b.reshape(DEPTH, 1, n_out))


def _mod_spec(li, cond_of_row, tile, n_grid):
    if n_grid == 1:
        return pl.BlockSpec((None, None, 6, D_MODEL), lambda i: (li, cond_of_row(i * tile), 0, 0))
    return pl.BlockSpec((None, None, 6, D_MODEL), lambda i, j: (li, cond_of_row(i * tile), 0, 0))


def _conv_window(w, cw, t0, seq):
    n = w.shape[0]
    mid = slice(HALO, HALO + ROW_BLOCK)
    t = (t0 + lax.broadcasted_iota(jnp.int32, (ROW_BLOCK, 1), 0)) & (seq - 1)
    prev = jnp.where(t == 0, 0.0, pltpu.roll(w, 1, axis=0)[mid])
    nxt = jnp.where(t == seq - 1, 0.0, pltpu.roll(w, n - 1, axis=0)[mid])
    return prev * cw[0:1, :] + w[mid] * cw[1:2, :] + nxt * cw[2:3, :]


def _ffn_act(win, mid, cw, cb, t0, seq):
    return _gelu_exact(_conv_window(win(0), cw, t0, seq) + cb) * mid(1)


def _sconv_act(win, mid, cw, cb, t0, seq):
    return mid(0) * _conv_window(win(1) * win(2), cw, t0, seq)


def _ffn_kernel(x_ref, mod_ref, nw_ref, up_hbm, cw_ref, cb_ref, down_hbm, fnw_ref, o_ref, *scratch, **cfg):
    _glu_body(x_ref, mod_ref, nw_ref, up_hbm, cw_ref, cb_ref, down_hbm, fnw_ref, o_ref, *scratch,
              n_up=2, act=_ffn_act, mod_base=3, **cfg)


def _sconv_kernel(x_ref, mod_ref, nw_ref, up_hbm, cw_ref, down_hbm, o_ref, *scratch, **cfg):
    _glu_body(x_ref, mod_ref, nw_ref, up_hbm, cw_ref, None, down_hbm, None, o_ref, *scratch,
              n_up=3, act=_sconv_act, mod_base=0, final_norm=False, **cfg)


def _glu_body(x_ref, mod_ref, nw_ref, up_ref, cw_ref, cb_ref, down_ref, fnw_ref, o_ref,
              h_scr, acc_scr, gu0, gu1, *, n_up, n_chunks, act, seq, mod_base, final_norm):
    gu = (gu0, gu1)
    n_blocks = ROW_TILE // ROW_BLOCK

    def up(c, slot, r):
        rows = slice(r * ROW_BLOCK, (r + 1) * ROW_BLOCK)
        res = _dot(h_scr[rows, :], up_ref[c])
        for p in range(n_up):
            gu[slot][p, HALO + r * ROW_BLOCK:HALO + (r + 1) * ROW_BLOCK, :] = res[:, p * FF_TILE:(p + 1) * FF_TILE]

    def activate(c, slot, r):
        win = lambda p: gu[slot][p, r * ROW_BLOCK:(r + 1) * ROW_BLOCK + 2 * HALO, :]
        mid = lambda p: gu[slot][p, HALO + r * ROW_BLOCK:HALO + (r + 1) * ROW_BLOCK, :]
        cb = None if cb_ref is None else cb_ref[c]
        return act(win, mid, cw_ref[c], cb, r * ROW_BLOCK, seq).astype(BF16)

    def down(a, c, r):
        rows = slice(r * ROW_BLOCK, (r + 1) * ROW_BLOCK)
        acc_scr[rows, :] += _dot(a, down_ref[c])

    def stage(s, slot):
        for r in range(n_blocks):
            a = activate(s - 1, 1 - slot, r)
            up(s, slot, r)
            down(a, s - 1, r)

    h_scr[...] = _norm_mod(x_ref[...], nw_ref[...], mod_ref[...], mod_base).astype(BF16)
    acc_scr[...] = jnp.zeros_like(acc_scr)
    for g in gu:
        g[:, :HALO, :] = jnp.zeros((n_up, HALO, FF_TILE), F32)
        g[:, HALO + ROW_TILE:, :] = jnp.zeros((n_up, HALO, FF_TILE), F32)

    for r in range(n_blocks):
        up(0, 0, r)

    n_pairs = (n_chunks - 1) // 2

    def pair(i, carry):
        s = 2 * i + 1
        stage(s, 1)
        stage(s + 1, 0)
        return carry

    lax.fori_loop(0, n_pairs, pair, 0)
    for s in range(2 * n_pairs + 1, n_chunks):
        stage(s, s % 2)
    last = n_chunks - 1
    for r in range(n_blocks):
        down(activate(last, last % 2, r), last, r)

    y = x_ref[...] + mod_ref[mod_base + 2:mod_base + 3, :] * acc_scr[...]
    if final_norm:
        y = _rms(y, fnw_ref[...])
    o_ref[...] = y


def _chunked(w, n_chunks):
    return w.reshape(w.shape[0], n_chunks, FF_TILE).transpose(1, 0, 2)


def _glu_call(kernel_fn, name, x, mods, li, cond_of_row, n_up, n_chunks, norm_w, operands, specs, cfg):
    m = x.shape[0]
    whole = lambda a: pl.BlockSpec(a.shape, lambda i: (0,) * a.ndim)
    in_specs = [pl.BlockSpec((ROW_TILE, D_MODEL), lambda i: (i, 0)),
                _mod_spec(li, cond_of_row, ROW_TILE, 1),
                pl.BlockSpec((None, 1, D_MODEL), lambda i: (li, 0, 0))]
    layer_w = lambda a, l: pl.BlockSpec((None,) + a.shape[1:], lambda i: (l,) + (0,) * (a.ndim - 1),
                                        pipeline_mode=pl.Buffered(1))
    in_specs += [whole(a) if s is None else layer_w(a, s) for a, s in zip(operands, specs)]
    gu_shape = (n_up, ROW_TILE + 2 * HALO, FF_TILE)
    return pl.pallas_call(
        functools.partial(kernel_fn, n_chunks=n_chunks, **cfg),
        grid=(m // ROW_TILE,),
        in_specs=in_specs,
        out_specs=pl.BlockSpec((ROW_TILE, D_MODEL), lambda i: (i, 0)),
        out_shape=jax.ShapeDtypeStruct((m, D_MODEL), F32),
        scratch_shapes=[
            pltpu.VMEM((ROW_TILE, D_MODEL), BF16),
            pltpu.VMEM((ROW_TILE, D_MODEL), F32),
            pltpu.VMEM(gu_shape, F32), pltpu.VMEM(gu_shape, F32),
        ],
        compiler_params=_params(("arbitrary",)),
        name=name,
    )(x, mods, norm_w, *operands)


def _chunk_major_kernel(w_ref, o_ref, *, n_up, n_chunks):
    for c in range(n_chunks):
        for p in range(n_up):
            col = (p * n_chunks + c) * FF_TILE
            o_ref[c, :, p * FF_TILE:(p + 1) * FF_TILE] = w_ref[:, col:col + FF_TILE].astype(BF16)


def _chunk_major(w, n_up, name):
    l, d, cols = w.shape
    n_chunks = cols // (n_up * FF_TILE)
    return pl.pallas_call(
        functools.partial(_chunk_major_kernel, n_up=n_up, n_chunks=n_chunks),
        grid=(l, d // RELAYOUT_ROWS),
        in_specs=[pl.BlockSpec((None, RELAYOUT_ROWS, cols), lambda li, k: (li, k, 0))],
        out_specs=pl.BlockSpec((None, n_chunks, RELAYOUT_ROWS, n_up * FF_TILE), lambda li, k: (li, 0, k, 0)),
        out_shape=jax.ShapeDtypeStruct((l, n_chunks, d, n_up * FF_TILE), BF16),
        compiler_params=_params(("arbitrary", "arbitrary")),
        name=name,
    )(w)


def _down_chunks(w):
    l, rows, d = w.shape
    return w.reshape(l, rows // FF_TILE, FF_TILE, d).astype(BF16)


def _ffn_call(x, mods, li, cond_of_row, seq, P, final_norm):
    n = D_FF // FF_TILE
    operands = [P["ffn_up_cm"], _chunked(P["ffn_conv_w"][li], n), _chunked(P["ffn_conv_b"][li][None, :], n),
                P["ffn_down_cm"], P["final_norm_w"].reshape(1, D_MODEL)]
    return _glu_call(_ffn_kernel, f"ffn{li}", x, mods, li, cond_of_row, 2, n, P["norm2_w"],
                     operands, [li, None, None, li, None], dict(seq=seq, final_norm=final_norm))


def _sconv_call(x, mods, li, cond_of_row, seq, P):
    lj = li // 4
    n = D_MODEL // FF_TILE
    operands = [P["sconv_win_cm"], _chunked(P["sconv_conv"][lj], n), P["sconv_wout_cm"]]
    return _glu_call(_sconv_kernel, f"sconv{li}", x, mods, li, cond_of_row, 3, n, P["norm1_w"],
                     operands, [lj, None, lj], dict(seq=seq))


def _emit_kv(ckv, krope_pad, wk_ref, wv_ref, k_ref, v_ref):
    c = ckv.astype(BF16)
    kx = _dot(c, wk_ref[...].astype(BF16))
    vx = _dot(c, wv_ref[...].astype(BF16))
    for h in range(N_HEADS):
        k_ref[h] = (kx[:, h * LANES:(h + 1) * LANES] + krope_pad).astype(BF16)
    for p in range(HEAD_PAIRS):
        v_ref[p] = vx[:, p * LANES:(p + 1) * LANES].astype(BF16)


def _mla_proj_kernel(x_ref, mod_ref, nw_ref, w1_ref, qn_ref, kvn_ref, w2_ref, wk_ref, wv_ref, cos_ref,
                     sin_ref, q_ref, k_ref, v_ref, ckv_ref, kr_ref, *, rope):
    h = _norm_mod(x_ref[...], nw_ref[...], mod_ref[...], 0).astype(BF16)
    z = _dot(h, w1_ref[...].astype(BF16))
    cq = _rms(z[:, :Q_LORA], qn_ref[...]).astype(BF16)
    ckv = _rms(z[:, Q_LORA:Q_LORA + KV_LORA], kvn_ref[...])
    kr = z[:, Q_LORA + KV_LORA:Q_LORA + KV_LORA + LANES]
    if rope:
        cos, sin = cos_ref[...], sin_ref[...]
        kr = kr * cos + z[:, Q_LORA + KV_LORA + LANES:] * sin
    for p in range(HEAD_PAIRS):
        q2 = _dot(cq, w2_ref[:, 2 * p * LANES:(2 * p + 2) * LANES].astype(BF16))
        if rope:
            off = N_HEADS * LANES
            qs = _dot(cq, w2_ref[:, off + 2 * p * LANES:off + (2 * p + 2) * LANES].astype(BF16))
        for e in range(2):
            qh = q2[:, e * LANES:(e + 1) * LANES]
            if rope:
                qh = qh * cos + qs[:, e * LANES:(e + 1) * LANES] * sin
            q_ref[2 * p + e] = qh.astype(BF16)
    ckv_ref[...] = ckv
    kr_ref[...] = kr[:, QK_NOPE:QK_NOPE + QK_ROPE]
    _emit_kv(ckv, kr, wk_ref, wv_ref, k_ref, v_ref)


def _mla_proj_call(x, mods, li, cond_of_row, W, rope):
    m = x.shape[0]
    n1 = W["w1"].shape[1]
    n2 = W["w2"].shape[1]
    const = lambda i: (0, 0)
    pos_tiles = W["cos"].shape[0] // PROJ_TILE
    pos = lambda i: (i % pos_tiles, 0)
    return pl.pallas_call(
        functools.partial(_mla_proj_kernel, rope=rope),
        grid=(m // PROJ_TILE,),
        in_specs=[
            pl.BlockSpec((PROJ_TILE, D_MODEL), lambda i: (i, 0)),
            _mod_spec(li, cond_of_row, PROJ_TILE, 1),
            pl.BlockSpec((None, 1, D_MODEL), lambda i: (li, 0, 0)),
            pl.BlockSpec((D_MODEL, n1), const),
            pl.BlockSpec((1, Q_LORA), const),
            pl.BlockSpec((1, KV_LORA), const),
            pl.BlockSpec((Q_LORA, n2), const),
            pl.BlockSpec((KV_LORA, N_HEADS * LANES), const),
            pl.BlockSpec((KV_LORA, HEAD_PAIRS * LANES), const),
            pl.BlockSpec((PROJ_TILE, LANES), pos),
            pl.BlockSpec((PROJ_TILE, LANES), pos),
        ],
        out_specs=[
            pl.BlockSpec((N_HEADS, PROJ_TILE, LANES), lambda i: (0, i, 0)),
            pl.BlockSpec((N_HEADS, PROJ_TILE, LANES), lambda i: (0, i, 0)),
            pl.BlockSpec((HEAD_PAIRS, PROJ_TILE, LANES), lambda i: (0, i, 0)),
            pl.BlockSpec((PROJ_TILE, KV_LORA), lambda i: (i, 0)),
            pl.BlockSpec((PROJ_TILE, QK_ROPE), lambda i: (i, 0)),
        ],
        out_shape=[
            jax.ShapeDtypeStruct((N_HEADS, m, LANES), BF16),
            jax.ShapeDtypeStruct((N_HEADS, m, LANES), BF16),
            jax.ShapeDtypeStruct((HEAD_PAIRS, m, LANES), BF16),
            jax.ShapeDtypeStruct((m, KV_LORA), F32),
            jax.ShapeDtypeStruct((m, QK_ROPE), F32),
        ],
        compiler_params=_params(("arbitrary",)),
        name="mla_proj_rope" if rope else "mla_proj",
    )(x, mods, W["norm1_w"], W["w1"], W["q_norm"], W["kv_norm"], W["w2"], W["wk"], W["wv"],
      W["cos"], W["sin"])


def _kv_cache_kernel(ckv_ref, kr_ref, wk_ref, wv_ref, k_ref, v_ref):
    _emit_kv(ckv_ref[...], kr_ref[...], wk_ref, wv_ref, k_ref, v_ref)


def _kv_cache_call(ckv, krope_pad, W):
    m = ckv.shape[0]
    const = lambda i: (0, 0)
    return pl.pallas_call(
        _kv_cache_kernel,
        grid=(1,),
        in_specs=[
            pl.BlockSpec((m, KV_LORA), const),
            pl.BlockSpec((m, LANES), const),
            pl.BlockSpec((KV_LORA, N_HEADS * LANES), const),
            pl.BlockSpec((KV_LORA, HEAD_PAIRS * LANES), const),
        ],
        out_specs=[
            pl.BlockSpec((N_HEADS, m, LANES), lambda i: (0, 0, 0)),
            pl.BlockSpec((HEAD_PAIRS, m, LANES), lambda i: (0, 0, 0)),
        ],
        out_shape=[
            jax.ShapeDtypeStruct((N_HEADS, m, LANES), BF16),
            jax.ShapeDtypeStruct((HEAD_PAIRS, m, LANES), BF16),
        ],
        compiler_params=_params(("arbitrary",)),
        name="kv_cache",
    )(ckv, krope_pad, W["wk"], W["wv"])


def _attn_kernel(q_ref, k_ref, v_ref, o_ref, *, n_pairs):
    scale = np.float32(1.0 / math.sqrt(QK_NOPE + QK_ROPE))
    for p in range(n_pairs):
        v = v_ref[p]
        outs = []
        for e in range(2):
            s = lax.dot_general(q_ref[2 * p + e], k_ref[2 * p + e], (((1,), (1,)), ((), ())),
                                preferred_element_type=F32) * scale
            w = jnp.exp(s - jnp.max(s, axis=-1, keepdims=True))
            l = jnp.sum(w, axis=-1, keepdims=True)
            outs.append(_dot(w.astype(BF16), v) / l)
        lane = lax.broadcasted_iota(jnp.int32, outs[0].shape, 1)
        o_ref[:, p * LANES:(p + 1) * LANES] = jnp.where(lane < V_HEAD, outs[0], outs[1]).astype(BF16)


def _attn_call(q, k, v, q_tile, n_pairs):
    _, b, t, _ = q.shape
    s = k.shape[2]
    nq = t // q_tile
    return pl.pallas_call(
        functools.partial(_attn_kernel, n_pairs=n_pairs),
        grid=(b, HEAD_PAIRS // n_pairs, nq),
        in_specs=[
            pl.BlockSpec((2 * n_pairs, None, q_tile, LANES), lambda bi, g, qi: (g, bi, qi, 0)),
            pl.BlockSpec((2 * n_pairs, None, s, LANES), lambda bi, g, qi: (g, bi, 0, 0)),
            pl.BlockSpec((n_pairs, None, s, LANES), lambda bi, g, qi: (g, bi, 0, 0)),
        ],
        out_specs=pl.BlockSpec((q_tile, n_pairs * LANES), lambda bi, g, qi: (bi * nq + qi, g)),
        out_shape=jax.ShapeDtypeStruct((b * t, N_HEADS * V_HEAD), BF16),
        compiler_params=_params(("arbitrary", "arbitrary", "arbitrary")),
        name=f"attn_s{s}",
    )(q, k, v)


def _proj_res_kernel(x_ref, mod_ref, a_ref, w_ref, o_ref):
    o_ref[...] = x_ref[...] + mod_ref[2:3, :] * _dot(a_ref[...], w_ref[...].astype(BF16))


def _proj_res_call(x, mods, li, cond_of_row, a, w, name):
    m = x.shape[0]
    return pl.pallas_call(
        _proj_res_kernel,
        grid=(m // ROW_TILE,),
        in_specs=[
            pl.BlockSpec((ROW_TILE, D_MODEL), lambda i: (i, 0)),
            _mod_spec(li, cond_of_row, ROW_TILE, 1),
            pl.BlockSpec((ROW_TILE, a.shape[1]), lambda i: (i, 0)),
            pl.BlockSpec((None,) + w.shape[1:], lambda i: (0, 0, 0)),
        ],
        out_specs=pl.BlockSpec((ROW_TILE, D_MODEL), lambda i: (i, 0)),
        out_shape=jax.ShapeDtypeStruct((m, D_MODEL), F32),
        compiler_params=_params(("arbitrary",)),
        name=name,
    )(x, mods, a, w)


def _pool_kernel(x_ref, mod_ref, nw_ref, w_ref, sc_ref, o_ref, band_scr, y_scr, *, seq):
    @pl.when(pl.program_id(0) == 0)
    def _():
        r = lax.broadcasted_iota(jnp.int32, (seq, seq), 0)
        c = lax.broadcasted_iota(jnp.int32, (seq, seq), 1)
        for g, win in enumerate(POOL_WINDOWS):
            inside = (c >= r - win // 2) & (c < r + win - win // 2)
            band_scr[g] = jnp.where(inside, 1.0, 0.0).astype(BF16)

    x = x_ref[...]
    h = _norm_mod(x, nw_ref[...], mod_ref[...], 0)
    h_hi = h.astype(BF16)
    h_lo = (h - h_hi.astype(F32)).astype(BF16)
    t = lax.broadcasted_iota(jnp.int32, (seq, 1), 0)
    for g, win in enumerate(POOL_WINDOWS):
        cnt = (jnp.minimum(t + (win - win // 2), seq) - jnp.maximum(t - win // 2, 0)).astype(F32)
        cols = slice(g * GROUP, (g + 1) * GROUP)
        wg = w_ref[g].astype(BF16)
        for s in range(x.shape[0] // seq):
            rows = slice(s * seq, (s + 1) * seq)
            tot = _dot(band_scr[g], h_hi[rows, cols]) + _dot(band_scr[g], h_lo[rows, cols])
            pooled = (tot / cnt - h[rows, cols]).astype(BF16)
            y_scr[rows, cols] = _dot(pooled, wg)
    o_ref[...] = x + mod_ref[2:3, :] * (y_scr[...] * sc_ref[...])


def _pool_call(x, mods, li, cond_of_row, seq, P):
    m = x.shape[0]
    lj = li // 4
    return pl.pallas_call(
        functools.partial(_pool_kernel, seq=seq),
        grid=(m // ROW_TILE,),
        in_specs=[
            pl.BlockSpec((ROW_TILE, D_MODEL), lambda i: (i, 0)),
            _mod_spec(li, cond_of_row, ROW_TILE, 1),
            pl.BlockSpec((None, 1, D_MODEL), lambda i: (li, 0, 0)),
            pl.BlockSpec((None, N_GROUPS, GROUP, GROUP), lambda i: (lj, 0, 0, 0)),
            pl.BlockSpec((None, 1, D_MODEL), lambda i: (lj, 0, 0)),
        ],
        out_specs=pl.BlockSpec((ROW_TILE, D_MODEL), lambda i: (i, 0)),
        out_shape=jax.ShapeDtypeStruct((m, D_MODEL), F32),
        scratch_shapes=[pltpu.VMEM((N_GROUPS, seq, seq), BF16), pltpu.VMEM((ROW_TILE, D_MODEL), F32)],
        compiler_params=_params(("arbitrary",)),
        name=f"pool_t{seq}",
    )(x, mods, P["norm1_w"], P["pool_w"], P["pool_scale"])


def _dft_tables(seq):
    def cs(n):
        k = np.arange(n, dtype=np.int64)
        ang = 2.0 * np.pi * ((k[:, None] * k[None, :]) % n) / n
        return np.cos(ang), np.sin(ang)
    ct, st = cs(seq)
    cc, sc = cs(GROUP)
    e_t = jnp.asarray(np.concatenate([ct, st], axis=0), dtype=F32)
    e_c = jnp.asarray(np.concatenate([cc, -sc], axis=0), dtype=F32)
    return e_t.astype(BF16), e_c.astype(BF16)


def _fourier_kernel(x_ref, mod_ref, nw_ref, et_ref, ec_ref, w_ref, o_ref, f_scr, *, seq):
    x = x_ref[...]
    h = _norm_mod(x, nw_ref[...], mod_ref[...], 0).astype(BF16)
    norm = np.float32(1.0 / math.sqrt(seq * GROUP))
    for s in range(x.shape[0] // seq):
        rows = slice(s * seq, (s + 1) * seq)
        for g in range(N_GROUPS):
            cols = slice(g * GROUP, (g + 1) * GROUP)
            y = _dot(et_ref[...], h[rows, cols]).astype(BF16)
            f = _dot(y[:seq], ec_ref[:GROUP, :]) + _dot(y[seq:], ec_ref[GROUP:, :])
            f_scr[rows, cols] = (f * norm).astype(BF16)
    o_ref[...] = x + mod_ref[2:3, :] * _dot(f_scr[...], w_ref[...].astype(BF16))


def _fourier_call(x, mods, li, cond_of_row, seq, P):
    m = x.shape[0]
    lj = li // 4
    e_t, e_c = _dft_tables(seq)
    return pl.pallas_call(
        functools.partial(_fourier_kernel, seq=seq),
        grid=(m // ROW_TILE,),
        in_specs=[
            pl.BlockSpec((ROW_TILE, D_MODEL), lambda i: (i, 0)),
            _mod_spec(li, cond_of_row, ROW_TILE, 1),
            pl.BlockSpec((None, 1, D_MODEL), lambda i: (li, 0, 0)),
            pl.BlockSpec((2 * seq, seq), lambda i: (0, 0)),
            pl.BlockSpec((2 * GROUP, GROUP), lambda i: (0, 0)),
            pl.BlockSpec((None, D_MODEL, D_MODEL), lambda i: (lj, 0, 0)),
        ],
        out_specs=pl.BlockSpec((ROW_TILE, D_MODEL), lambda i: (i, 0)),
        out_shape=jax.ShapeDtypeStruct((m, D_MODEL), F32),
        scratch_shapes=[pltpu.VMEM((ROW_TILE, D_MODEL), BF16)],
        compiler_params=_params(("arbitrary",)),
        name=f"fourier_t{seq}",
    )(x, mods, P["norm1_w"], e_t, e_c, P["fnet_w"])


def _rope_tables(rows):
    r, col = jnp.meshgrid(jnp.arange(rows), jnp.arange(GRID_W), indexing="ij")
    r = r.reshape(-1).astype(F32)
    col = col.reshape(-1).astype(F32)
    n_freq = QK_ROPE // 4
    inv = ROPE_THETA ** (-jnp.arange(n_freq, dtype=F32) / n_freq)
    ang = jnp.concatenate([r[:, None] * inv, col[:, None] * inv], axis=-1)
    cos = jnp.repeat(jnp.cos(ang), 2, axis=-1)
    sin = jnp.repeat(jnp.sin(ang), 2, axis=-1)
    t = cos.shape[0]
    pad = LANES - QK_NOPE - QK_ROPE
    cos_p = jnp.concatenate([jnp.ones((t, QK_NOPE), F32), cos, jnp.ones((t, pad), F32)], axis=-1)
    sin_p = jnp.concatenate([jnp.zeros((t, QK_NOPE), F32), sin, jnp.zeros((t, pad), F32)], axis=-1)
    return cos_p, sin_p


def _pair_swap(w):
    wr = w.reshape(w.shape[:-1] + (w.shape[-1] // 2, 2))
    return jnp.stack([-wr[..., 1], wr[..., 0]], axis=-1).reshape(w.shape)


def _mla_weights(P, j, rope, seq_rows):
    pad = LANES - QK_NOPE - QK_ROPE
    wuq = P["mla_wuq"][j].reshape(Q_LORA, N_HEADS, QK_NOPE + QK_ROPE)
    q_slots = jnp.pad(wuq, ((0, 0), (0, 0), (0, pad))).reshape(Q_LORA, N_HEADS * LANES)
    wdkv = P["mla_wdkv"][j]
    kr_slot = jnp.pad(wdkv[:, KV_LORA:], ((0, 0), (QK_NOPE, pad)))
    parts1 = [P["mla_wdq"][j], wdkv[:, :KV_LORA], kr_slot]
    parts2 = [q_slots]
    if rope:
        parts1.append(jnp.pad(_pair_swap(wdkv[:, KV_LORA:]), ((0, 0), (QK_NOPE, pad))))
        q_sw = jnp.pad(_pair_swap(wuq[:, :, QK_NOPE:]), ((0, 0), (0, 0), (QK_NOPE, pad)))
        parts2.append(q_sw.reshape(Q_LORA, N_HEADS * LANES))
        cos, sin = _rope_tables(seq_rows // GRID_W)
    else:
        cos = sin = jnp.zeros((PROJ_TILE, LANES), F32)
    wukv = P["mla_wukv"][j].reshape(KV_LORA, N_HEADS, QK_NOPE + V_HEAD)
    wk = jnp.pad(wukv[:, :, :QK_NOPE], ((0, 0), (0, 0), (0, LANES - QK_NOPE)))
    return dict(
        norm1_w=P["norm1_w"], q_norm=P["mla_q_norm"][j][None, :], kv_norm=P["mla_kv_norm"][j][None, :],
        w1=jnp.concatenate(parts1, axis=1), w2=jnp.concatenate(parts2, axis=1),
        wk=wk.reshape(KV_LORA, N_HEADS * LANES),
        wv=wukv[:, :, QK_NOPE:].reshape(KV_LORA, N_HEADS * V_HEAD), cos=cos, sin=sin)


def _mla_layer(x, mods, li, cond_of_row, batch, seq, P, cache):
    j = li // 4
    m = x.shape[0]
    W = _mla_weights(P, j, cache is not None, seq)
    q, k, v, ckv, krope = _mla_proj_call(x, mods, li, cond_of_row, W, cache is not None)
    q = q.reshape(N_HEADS, batch, seq, LANES)
    k = k.reshape(N_HEADS, batch, seq, LANES)
    v = v.reshape(HEAD_PAIRS, batch, seq, LANES)
    if cache is None:
        o = _attn_call(q, k, v, seq, HEAD_PAIRS)
    else:
        c_ckv, c_krope = cache
        past = c_ckv.shape[1]
        pad = LANES - QK_NOPE - QK_ROPE
        kr_pad = jnp.pad(c_krope.reshape(batch * past, QK_ROPE), ((0, 0), (QK_NOPE, pad)))
        kc, vc = _kv_cache_call(c_ckv.reshape(batch * past, KV_LORA), kr_pad, W)
        k = jnp.concatenate([kc.reshape(N_HEADS, batch, past, LANES), k], axis=2)
        v = jnp.concatenate([vc.reshape(HEAD_PAIRS, batch, past, LANES), v], axis=2)
        o = _attn_call(q, k, v, 512, 2)
    x = _proj_res_call(x, mods, li, cond_of_row, o, P["mla_wo"], f"mla_out_{m}")
    return x, ckv, krope


def _run_pass(x, mods, cond_of_row, batch, seq, P, cache):
    ckv = krope = None
    for li in range(DEPTH):
        kind = li % 4
        if kind == 0:
            x, ckv, krope = _mla_layer(x, mods, li, cond_of_row, batch, seq, P, cache)
        elif kind == 1:
            x = _pool_call(x, mods, li, cond_of_row, seq, P)
        elif kind == 2:
            x = _fourier_call(x, mods, li, cond_of_row, seq, P)
        else:
            x = _sconv_call(x, mods, li, cond_of_row, seq, P)
        x = _ffn_call(x, mods, li, cond_of_row, seq, P, final_norm=(li == DEPTH - 1))
    return x, ckv, krope


def kernel(x_prompt, x_sample, cache_ckv, cache_krope, c, c_ctx, norm1_w, norm2_w, ada_w, ada_b, mla_wdq, mla_q_norm, mla_wuq, mla_wdkv, mla_kv_norm, mla_wukv, mla_wo, pool_w, pool_scale, fnet_w, sconv_win, sconv_conv, sconv_wout, ffn_up, ffn_conv_w, ffn_conv_b, ffn_down, final_norm_w):
    row = lambda w: w.reshape(w.shape[0], 1, w.shape[1])
    P = dict(norm1_w=row(norm1_w), norm2_w=row(norm2_w), mla_wdq=mla_wdq, mla_q_norm=mla_q_norm,
             mla_wuq=mla_wuq, mla_wdkv=mla_wdkv, mla_kv_norm=mla_kv_norm, mla_wukv=mla_wukv, mla_wo=mla_wo,
             pool_w=pool_w, pool_scale=row(pool_scale), fnet_w=fnet_w, sconv_win=sconv_win,
             sconv_conv=sconv_conv, sconv_wout=sconv_wout, ffn_conv_w=ffn_conv_w,
             ffn_conv_b=ffn_conv_b, ffn_down=ffn_down, final_norm_w=final_norm_w,
             ffn_up_cm=_chunk_major(ffn_up, 2, "ffn_up_layout"),
             sconv_win_cm=_chunk_major(sconv_win, 3, "sconv_win_layout"),
             ffn_down_cm=_down_chunks(ffn_down), sconv_wout_cm=_down_chunks(sconv_wout))
    b_ctx, t_ctx, _ = x_prompt.shape
    b_lat, t_lat, _ = x_sample.shape
    assert ROW_TILE % t_ctx == 0 and t_lat % ROW_TILE == 0 and b_lat + 1 <= N_COND

    conds = jnp.concatenate([c_ctx[None, :], c, jnp.zeros((N_COND - 1 - b_lat, D_MODEL), F32)], axis=0)
    mods = _ada_call(conds, ada_w, ada_b).reshape(DEPTH, N_COND, 6, D_MODEL)

    xp, ckv, krope = _run_pass(x_prompt.reshape(b_ctx * t_ctx, D_MODEL), mods, lambda r: 0,
                               b_ctx, t_ctx, P, None)
    xs, _, _ = _run_pass(x_sample.reshape(b_lat * t_lat, D_MODEL), mods,
                         lambda r: 1 + r // t_lat, b_lat, t_lat, P,
                         (cache_ckv[:, 0], cache_krope[:, 0]))
    return (xp.reshape(b_ctx, t_ctx, D_MODEL), xs.reshape(b_lat, t_lat, D_MODEL),
            ckv.reshape(b_ctx, 1, t_ctx, KV_LORA), krope.reshape(b_ctx, 1, t_ctx, QK_ROPE))
```

```python
import functools
import math

import numpy as np
import jax
import jax.numpy as jnp
from jax import lax
from jax.experimental import pallas as pl
from jax.experimental.pallas import tpu as pltpu

F32 = jnp.float32
BF16 = jnp.bfloat16

D_MODEL = 1024
N_HEADS = 16
QK_NOPE = 64
QK_ROPE = 32
V_HEAD = 64
Q_LORA = 384
KV_LORA = 256
ROPE_THETA = 10000.0
GRID_W = 64
POOL_WINDOWS = (2, 4, 8, 16)
N_GROUPS = 4
GROUP = D_MODEL // N_GROUPS
D_FF = 2816
EPS = 1e-6
DEPTH = 4

LANES = 128
HEAD_PAIRS = N_HEADS // 2
N_COND = 8
ROW_TILE = 1024
PROJ_TILE = 512
FF_TILE = 256
ROW_BLOCK = 256
HALO = 8
RELAYOUT_ROWS = 128
ADA_TILE = 1536
VMEM_LIMIT = 56 * 1024 * 1024


def _params(sem):
    return pltpu.CompilerParams(dimension_semantics=sem, vmem_limit_bytes=VMEM_LIMIT)


def _dot(a, b):
    return jnp.dot(a, b, preferred_element_type=F32)


def _rms(x, w):
    return x * lax.rsqrt(jnp.mean(x * x, axis=-1, keepdims=True) + EPS) * w


def _norm_mod(x, nw, mod, base):
    return _rms(x, nw) * (1.0 + mod[base + 1:base + 2, :]) + mod[base:base + 1, :]


def _gelu_exact(x):
    return 0.5 * x * (1.0 + lax.erf(x * np.float32(math.sqrt(0.5))))


def _ada_kernel(c_ref, w_ref, b_ref, o_ref):
    s = jax.nn.silu(c_ref[...]).astype(BF16)
    o_ref[...] = _dot(s, w_ref[...].astype(BF16)) + b_ref[...]


def _ada_call(conds, ada_w, ada_b):
    n_out = ada_w.shape[-1]
    return pl.pallas_call(
        _ada_kernel,
        grid=(DEPTH, n_out // ADA_TILE),
        in_specs=[
            pl.BlockSpec((N_COND, D_MODEL), lambda l, j: (0, 0)),
            pl.BlockSpec((None, D_MODEL, ADA_TILE), lambda l, j: (l, 0, j)),
            pl.BlockSpec((None, 1, ADA_TILE), lambda l, j: (l, 0, j)),
        ],
        out_specs=pl.BlockSpec((None, N_COND, ADA_TILE), lambda l, j: (l, 0, j)),
        out_shape=jax.ShapeDtypeStruct((DEPTH, N_COND, n_out), F32),
        compiler_params=_params(("arbitrary", "arbitrary")),
        name="ada",
    )(conds, ada_w, ada_b.reshape(DEPTH, 1, n_out))


def _mod_spec(li, cond_of_row, tile, n_grid):
    if n_grid == 1:
        return pl.BlockSpec((None, None, 6, D_MODEL), lambda i: (li, cond_of_row(i * tile), 0, 0))
    return pl.BlockSpec((None, None, 6, D_MODEL), lambda i, j: (li, cond_of_row(i * tile), 0, 0))


def _conv_window(w, cw, t0, seq):
    n = w.shape[0]
    mid = slice(HALO, HALO + ROW_BLOCK)
    t = (t0 + lax.broadcasted_iota(jnp.int32, (ROW_BLOCK, 1), 0)) & (seq - 1)
    prev = jnp.where(t == 0, 0.0, pltpu.roll(w, 1, axis=0)[mid])
    nxt = jnp.where(t == seq - 1, 0.0, pltpu.roll(w, n - 1, axis=0)[mid])
    return prev * cw[0:1, :] + w[mid] * cw[1:2, :] + nxt * cw[2:3, :]


def _ffn_act(win, mid, cw, cb, t0, seq):
    return _gelu_exact(_conv_window(win(0), cw, t0, seq) + cb) * mid(1)


def _sconv_act(win, mid, cw, cb, t0, seq):
    return mid(0) * _conv_window(win(1) * win(2), cw, t0, seq)


def _ffn_kernel(x_ref, mod_ref, nw_ref, up_ref, cw_ref, cb_ref, down_ref, fnw_ref, o_ref, *scratch, **cfg):
    _glu_body(x_ref, mod_ref, nw_ref, up_ref, cw_ref, cb_ref, down_ref, fnw_ref, None, o_ref, *scratch,
              n_up=2, act=_ffn_act, mod_base=3, **cfg)


def _ffn_pre_kernel(x_ref, mod_ref, nw_ref, up_ref, cw_ref, cb_ref, down_ref, fnw_ref, a_ref, wa_ref, o_ref,
                    *scratch, **cfg):
    _glu_body(x_ref, mod_ref, nw_ref, up_ref, cw_ref, cb_ref, down_ref, fnw_ref, (a_ref, wa_ref), o_ref,
              *scratch, n_up=2, act=_ffn_act, mod_base=3, **cfg)


def _sconv_kernel(x_ref, mod_ref, nw_ref, up_ref, cw_ref, down_ref, o_ref, *scratch, **cfg):
    _glu_body(x_ref, mod_ref, nw_ref, up_ref, cw_ref, None, down_ref, None, None, o_ref, *scratch,
              n_up=3, act=_sconv_act, mod_base=0, final_norm=False, **cfg)


def _glu_body(x_ref, mod_ref, nw_ref, up_ref, cw_ref, cb_ref, down_ref, fnw_ref, pre, o_ref,
              h_scr, acc_scr, gu0, gu1, *, n_up, n_chunks, act, seq, mod_base, final_norm):
    gu = (gu0, gu1)
    n_blocks = ROW_TILE // ROW_BLOCK

    def up(c, slot, r):
        rows = slice(r * ROW_BLOCK, (r + 1) * ROW_BLOCK)
        res = _dot(h_scr[rows, :], up_ref[c])
        for p in range(n_up):
            gu[slot][p, HALO + r * ROW_BLOCK:HALO + (r + 1) * ROW_BLOCK, :] = res[:, p * FF_TILE:(p + 1) * FF_TILE]

    def activate(c, slot, r):
        win = lambda p: gu[slot][p, r * ROW_BLOCK:(r + 1) * ROW_BLOCK + 2 * HALO, :]
        mid = lambda p: gu[slot][p, HALO + r * ROW_BLOCK:HALO + (r + 1) * ROW_BLOCK, :]
        cb = None if cb_ref is None else cb_ref[c]
        return act(win, mid, cw_ref[c], cb, r * ROW_BLOCK, seq).astype(BF16)

    def down(a, c, r):
        rows = slice(r * ROW_BLOCK, (r + 1) * ROW_BLOCK)
        acc_scr[rows, :] += _dot(a, down_ref[c])

    def stage(s, slot):
        for r in range(n_blocks):
            a = activate(s - 1, 1 - slot, r)
            up(s, slot, r)
            down(a, s - 1, r)

    x = x_ref[...]
    if pre is not None:
        a_ref, wa_ref = pre
        x = x + mod_ref[2:3, :] * _dot(a_ref[...], wa_ref[...].astype(BF16))
        o_ref[...] = x
    h_scr[...] = _norm_mod(x, nw_ref[...], mod_ref[...], mod_base).astype(BF16)
    acc_scr[...] = jnp.zeros_like(acc_scr)
    for g in gu:
        g[:, :HALO, :] = jnp.zeros((n_up, HALO, FF_TILE), F32)
        g[:, HALO + ROW_TILE:, :] = jnp.zeros((n_up, HALO, FF_TILE), F32)

    for r in range(n_blocks):
        up(0, 0, r)

    n_pairs = (n_chunks - 1) // 2

    def pair(i, carry):
        s = 2 * i + 1
        stage(s, 1)
        stage(s + 1, 0)
        return carry

    lax.fori_loop(0, n_pairs, pair, 0)
    for s in range(2 * n_pairs + 1, n_chunks):
        stage(s, s % 2)
    last = n_chunks - 1
    for r in range(n_blocks):
        down(activate(last, last % 2, r), last, r)

    res = x_ref if pre is None else o_ref
    y = res[...] + mod_ref[mod_base + 2:mod_base + 3, :] * acc_scr[...]
    if final_norm:
        y = _rms(y, fnw_ref[...])
    o_ref[...] = y


def _chunked(w, n_chunks):
    return w.reshape(w.shape[0], n_chunks, FF_TILE).transpose(1, 0, 2)


def _glu_call(kernel_fn, name, x, mods, li, cond_of_row, n_up, n_chunks, norm_w, operands, specs, cfg):
    m = x.shape[0]
    whole = lambda a: pl.BlockSpec(a.shape, lambda i: (0,) * a.ndim)
    in_specs = [pl.BlockSpec((ROW_TILE, D_MODEL), lambda i: (i, 0)),
                _mod_spec(li, cond_of_row, ROW_TILE, 1),
                pl.BlockSpec((None, 1, D_MODEL), lambda i: (li, 0, 0))]
    layer_w = lambda a, l: pl.BlockSpec((None,) + a.shape[1:], lambda i: (l,) + (0,) * (a.ndim - 1),
                                        pipeline_mode=pl.Buffered(1))
    rows = lambda a: pl.BlockSpec((ROW_TILE, a.shape[1]), lambda i: (i, 0))
    in_specs += [whole(a) if s is None else rows(a) if s == "rows" else layer_w(a, s)
                 for a, s in zip(operands, specs)]
    gu_shape = (n_up, ROW_TILE + 2 * HALO, FF_TILE)
    return pl.pallas_call(
        functools.partial(kernel_fn, n_chunks=n_chunks, **cfg),
        grid=(m // ROW_TILE,),
        in_specs=in_specs,
        out_specs=pl.BlockSpec((ROW_TILE, D_MODEL), lambda i: (i, 0)),
        out_shape=jax.ShapeDtypeStruct((m, D_MODEL), F32),
        scratch_shapes=[
            pltpu.VMEM((ROW_TILE, D_MODEL), BF16),
            pltpu.VMEM((ROW_TILE, D_MODEL), F32),
            pltpu.VMEM(gu_shape, F32), pltpu.VMEM(gu_shape, F32),
        ],
        compiler_params=_params(("arbitrary",)),
        name=name,
    )(x, mods, norm_w, *operands)


def _chunk_major_kernel(w_ref, o_ref, *, n_up, n_chunks):
    for c in range(n_chunks):
        for p in range(n_up):
            col = (p * n_chunks + c) * FF_TILE
            o_ref[c, :, p * FF_TILE:(p + 1) * FF_TILE] = w_ref[:, col:col + FF_TILE].astype(BF16)


def _chunk_major(w, n_up, name):
    l, d, cols = w.shape
    n_chunks = cols // (n_up * FF_TILE)
    return pl.pallas_call(
        functools.partial(_chunk_major_kernel, n_up=n_up, n_chunks=n_chunks),
        grid=(l, d // RELAYOUT_ROWS),
        in_specs=[pl.BlockSpec((None, RELAYOUT_ROWS, cols), lambda li, k: (li, k, 0))],
        out_specs=pl.BlockSpec((None, n_chunks, RELAYOUT_ROWS, n_up * FF_TILE), lambda li, k: (li, 0, k, 0)),
        out_shape=jax.ShapeDtypeStruct((l, n_chunks, d, n_up * FF_TILE), BF16),
        compiler_params=_params(("arbitrary", "arbitrary")),
        name=name,
    )(w)


def _down_chunks(w):
    l, rows, d = w.shape
    return w.reshape(l, rows // FF_TILE, FF_TILE, d).astype(BF16)


def _ffn_call(x, mods, li, cond_of_row, seq, P, pre, final_norm):
    n = D_FF // FF_TILE
    operands = [P["ffn_up_cm"], _chunked(P["ffn_conv_w"][li], n), _chunked(P["ffn_conv_b"][li][None, :], n),
                P["ffn_down_cm"], P["final_norm_w"].reshape(1, D_MODEL)]
    specs = [li, None, None, li, None]
    kernel_fn = _ffn_kernel
    if pre is not None:
        a, w_a, l_a = pre
        operands += [a, w_a]
        specs += ["rows", l_a]
        kernel_fn = _ffn_pre_kernel
    return _glu_call(kernel_fn, f"ffn{li}", x, mods, li, cond_of_row, 2, n, P["norm2_w"],
                     operands, specs, dict(seq=seq, final_norm=final_norm))


def _sconv_call(x, mods, li, cond_of_row, seq, P):
    lj = li // 4
    n = D_MODEL // FF_TILE
    operands = [P["sconv_win_cm"], _chunked(P["sconv_conv"][lj], n), P["sconv_wout_cm"]]
    return _glu_call(_sconv_kernel, f"sconv{li}", x, mods, li, cond_of_row, 3, n, P["norm1_w"],
                     operands, [lj, None, lj], dict(seq=seq))


def _emit_kv(ckv, krope_pad, wk_ref, wv_ref, k_ref, v_ref):
    c = ckv.astype(BF16)
    kx = _dot(c, wk_ref[...].astype(BF16))
    vx = _dot(c, wv_ref[...].astype(BF16))
    for h in range(N_HEADS):
        k_ref[h] = (kx[:, h * LANES:(h + 1) * LANES] + krope_pad).astype(BF16)
    for p in range(HEAD_PAIRS):
        v_ref[p] = vx[:, p * LANES:(p + 1) * LANES].astype(BF16)


def _mla_proj_kernel(x_ref, mod_ref, nw_ref, w1_ref, qn_ref, kvn_ref, w2_ref, wk_ref, wv_ref, cos_ref,
                     sin_ref, q_ref, k_ref, v_ref, ckv_ref, kr_ref, *, rope):
    h = _norm_mod(x_ref[...], nw_ref[...], mod_ref[...], 0).astype(BF16)
    z = _dot(h, w1_ref[...].astype(BF16))
    cq = _rms(z[:, :Q_LORA], qn_ref[...]).astype(BF16)
    ckv = _rms(z[:, Q_LORA:Q_LORA + KV_LORA], kvn_ref[...])
    kr = z[:, Q_LORA + KV_LORA:Q_LORA + KV_LORA + LANES]
    if rope:
        cos, sin = cos_ref[...], sin_ref[...]
        kr = kr * cos + z[:, Q_LORA + KV_LORA + LANES:] * sin
    for p in range(HEAD_PAIRS):
        q2 = _dot(cq, w2_ref[:, 2 * p * LANES:(2 * p + 2) * LANES].astype(BF16))
        if rope:
            off = N_HEADS * LANES
            qs = _dot(cq, w2_ref[:, off + 2 * p * LANES:off + (2 * p + 2) * LANES].astype(BF16))
        for e in range(2):
            qh = q2[:, e * LANES:(e + 1) * LANES]
            if rope:
                qh = qh * cos + qs[:, e * LANES:(e + 1) * LANES] * sin
            q_ref[2 * p + e] = qh.astype(BF16)
    ckv_ref[...] = ckv
    kr_ref[...] = kr[:, QK_NOPE:QK_NOPE + QK_ROPE]
    _emit_kv(ckv, kr, wk_ref, wv_ref, k_ref, v_ref)


def _mla_proj_call(x, mods, li, cond_of_row, W, rope):
    m = x.shape[0]
    n1 = W["w1"].shape[1]
    n2 = W["w2"].shape[1]
    const = lambda i: (0, 0)
    pos_tiles = W["cos"].shape[0] // PROJ_TILE
    pos = lambda i: (i % pos_tiles, 0)
    return pl.pallas_call(
        functools.partial(_mla_proj_kernel, rope=rope),
        grid=(m // PROJ_TILE,),
        in_specs=[
            pl.BlockSpec((PROJ_TILE, D_MODEL), lambda i: (i, 0)),
            _mod_spec(li, cond_of_row, PROJ_TILE, 1),
            pl.BlockSpec((None, 1, D_MODEL), lambda i: (li, 0, 0)),
            pl.BlockSpec((D_MODEL, n1), const),
            pl.BlockSpec((1, Q_LORA), const),
            pl.BlockSpec((1, KV_LORA), const),
            pl.BlockSpec((Q_LORA, n2), const),
            pl.BlockSpec((KV_LORA, N_HEADS * LANES), const),
            pl.BlockSpec((KV_LORA, HEAD_PAIRS * LANES), const),
            pl.BlockSpec((PROJ_TILE, LANES), pos),
            pl.BlockSpec((PROJ_TILE, LANES), pos),
        ],
        out_specs=[
            pl.BlockSpec((N_HEADS, PROJ_TILE, LANES), lambda i: (0, i, 0)),
            pl.BlockSpec((N_HEADS, PROJ_TILE, LANES), lambda i: (0, i, 0)),
            pl.BlockSpec((HEAD_PAIRS, PROJ_TILE, LANES), lambda i: (0, i, 0)),
            pl.BlockSpec((PROJ_TILE, KV_LORA), lambda i: (i, 0)),
            pl.BlockSpec((PROJ_TILE, QK_ROPE), lambda i: (i, 0)),
        ],
        out_shape=[
            jax.ShapeDtypeStruct((N_HEADS, m, LANES), BF16),
            jax.ShapeDtypeStruct((N_HEADS, m, LANES), BF16),
            jax.ShapeDtypeStruct((HEAD_PAIRS, m, LANES), BF16),
            jax.ShapeDtypeStruct((m, KV_LORA), F32),
            jax.ShapeDtypeStruct((m, QK_ROPE), F32),
        ],
        compiler_params=_params(("arbitrary",)),
        name="mla_proj_rope" if rope else "mla_proj",
    )(x, mods, W["norm1_w"], W["w1"], W["q_norm"], W["kv_norm"], W["w2"], W["wk"], W["wv"],
      W["cos"], W["sin"])


def _kv_cache_kernel(ckv_ref, kr_ref, wk_ref, wv_ref, k_ref, v_ref):
    _emit_kv(ckv_ref[...], kr_ref[...], wk_ref, wv_ref, k_ref, v_ref)


def _kv_cache_call(ckv, krope_pad, W):
    m = ckv.shape[0]
    const = lambda i: (0, 0)
    return pl.pallas_call(
        _kv_cache_kernel,
        grid=(1,),
        in_specs=[
            pl.BlockSpec((m, KV_LORA), const),
            pl.BlockSpec((m, LANES), const),
            pl.BlockSpec((KV_LORA, N_HEADS * LANES), const),
            pl.BlockSpec((KV_LORA, HEAD_PAIRS * LANES), const),
        ],
        out_specs=[
            pl.BlockSpec((N_HEADS, m, LANES), lambda i: (0, 0, 0)),
            pl.BlockSpec((HEAD_PAIRS, m, LANES), lambda i: (0, 0, 0)),
        ],
        out_shape=[
            jax.ShapeDtypeStruct((N_HEADS, m, LANES), BF16),
            jax.ShapeDtypeStruct((HEAD_PAIRS, m, LANES), BF16),
        ],
        compiler_params=_params(("arbitrary",)),
        name="kv_cache",
    )(ckv, krope_pad, W["wk"], W["wv"])


def _attn_kernel(q_ref, *refs, n_pairs):
    o_ref, kv = refs[-1], refs[:-1]
    parts = [(kv[i], kv[i + 1]) for i in range(0, len(kv), 2)]
    scale = np.float32(1.0 / math.sqrt(QK_NOPE + QK_ROPE))
    for p in range(n_pairs):
        outs = []
        for e in range(2):
            q = q_ref[2 * p + e]
            s = [lax.dot_general(q, k_ref[2 * p + e], (((1,), (1,)), ((), ())),
                                 preferred_element_type=F32) * scale for k_ref, _ in parts]
            m = functools.reduce(jnp.maximum, [jnp.max(x, axis=-1, keepdims=True) for x in s])
            w = [jnp.exp(x - m) for x in s]
            l = sum(jnp.sum(x, axis=-1, keepdims=True) for x in w)
            pv = sum(_dot(x.astype(BF16), v_ref[p]) for x, (_, v_ref) in zip(w, parts))
            outs.append(pv / l)
        lane = lax.broadcasted_iota(jnp.int32, outs[0].shape, 1)
        o_ref[:, p * LANES:(p + 1) * LANES] = jnp.where(lane < V_HEAD, outs[0], outs[1]).astype(BF16)


def _attn_call(q, kv_parts, q_tile, n_pairs):
    _, b, t, _ = q.shape
    nq = t // q_tile
    in_specs = [pl.BlockSpec((2 * n_pairs, None, q_tile, LANES), lambda bi, g, qi: (g, bi, qi, 0))]
    operands = [q]
    for k, v in kv_parts:
        s = k.shape[2]
        in_specs += [pl.BlockSpec((2 * n_pairs, None, s, LANES), lambda bi, g, qi: (g, bi, 0, 0)),
                     pl.BlockSpec((n_pairs, None, s, LANES), lambda bi, g, qi: (g, bi, 0, 0))]
        operands += [k, v]
    return pl.pallas_call(
        functools.partial(_attn_kernel, n_pairs=n_pairs),
        grid=(b, HEAD_PAIRS // n_pairs, nq),
        in_specs=in_specs,
        out_specs=pl.BlockSpec((q_tile, n_pairs * LANES), lambda bi, g, qi: (bi * nq + qi, g)),
        out_shape=jax.ShapeDtypeStruct((b * t, N_HEADS * V_HEAD), BF16),
        compiler_params=_params(("arbitrary", "arbitrary", "arbitrary")),
        name=f"attn_{len(kv_parts)}part",
    )(*operands)


def _pool_kernel(x_ref, mod_ref, nw_ref, w_ref, sc_ref, o_ref, band_scr, y_scr, *, seq):
    @pl.when(pl.program_id(0) == 0)
    def _():
        r = lax.broadcasted_iota(jnp.int32, (seq, seq), 0)
        c = lax.broadcasted_iota(jnp.int32, (seq, seq), 1)
        for g, win in enumerate(POOL_WINDOWS):
            inside = (c >= r - win // 2) & (c < r + win - win // 2)
            band_scr[g] = jnp.where(inside, 1.0, 0.0).astype(BF16)

    x = x_ref[...]
    h = _norm_mod(x, nw_ref[...], mod_ref[...], 0)
    h_hi = h.astype(BF16)
    h_lo = (h - h_hi.astype(F32)).astype(BF16)
    t = lax.broadcasted_iota(jnp.int32, (seq, 1), 0)
    for g, win in enumerate(POOL_WINDOWS):
        cnt = (jnp.minimum(t + (win - win // 2), seq) - jnp.maximum(t - win // 2, 0)).astype(F32)
        cols = slice(g * GROUP, (g + 1) * GROUP)
        wg = w_ref[g].astype(BF16)
        for s in range(x.shape[0] // seq):
            rows = slice(s * seq, (s + 1) * seq)
            tot = _dot(band_scr[g], h_hi[rows, cols]) + _dot(band_scr[g], h_lo[rows, cols])
            pooled = (tot / cnt - h[rows, cols]).astype(BF16)
            y_scr[rows, cols] = _dot(pooled, wg)
    o_ref[...] = x + mod_ref[2:3, :] * (y_scr[...] * sc_ref[...])


def _pool_call(x, mods, li, cond_of_row, seq, P):
    m = x.shape[0]
    lj = li // 4
    return pl.pallas_call(
        functools.partial(_pool_kernel, seq=seq),
        grid=(m // ROW_TILE,),
        in_specs=[
            pl.BlockSpec((ROW_TILE, D_MODEL), lambda i: (i, 0)),
            _mod_spec(li, cond_of_row, ROW_TILE, 1),
            pl.BlockSpec((None, 1, D_MODEL), lambda i: (li, 0, 0)),
            pl.BlockSpec((None, N_GROUPS, GROUP, GROUP), lambda i: (lj, 0, 0, 0)),
            pl.BlockSpec((None, 1, D_MODEL), lambda i: (lj, 0, 0)),
        ],
        out_specs=pl.BlockSpec((ROW_TILE, D_MODEL), lambda i: (i, 0)),
        out_shape=jax.ShapeDtypeStruct((m, D_MODEL), F32),
        scratch_shapes=[pltpu.VMEM((N_GROUPS, seq, seq), BF16), pltpu.VMEM((ROW_TILE, D_MODEL), F32)],
        compiler_params=_params(("arbitrary",)),
        name=f"pool_t{seq}",
    )(x, mods, P["norm1_w"], P["pool_w"], P["pool_scale"])


def _dft_tables(seq):
    def cs(n):
        k = np.arange(n, dtype=np.int64)
        ang = 2.0 * np.pi * ((k[:, None] * k[None, :]) % n) / n
        return np.cos(ang), np.sin(ang)
    ct, st = cs(seq)
    cc, sc = cs(GROUP)
    e_t = jnp.asarray(np.concatenate([ct, st], axis=0), dtype=F32)
    e_c = jnp.asarray(np.concatenate([cc, -sc], axis=0), dtype=F32)
    return e_t.astype(BF16), e_c.astype(BF16)


def _fourier_kernel(x_ref, mod_ref, nw_ref, et_ref, ec_ref, w_ref, o_ref, f_scr, *, seq):
    x = x_ref[...]
    h = _norm_mod(x, nw_ref[...], mod_ref[...], 0).astype(BF16)
    norm = np.float32(1.0 / math.sqrt(seq * GROUP))
    for s in range(x.shape[0] // seq):
        rows = slice(s * seq, (s + 1) * seq)
        for g in range(N_GROUPS):
            cols = slice(g * GROUP, (g + 1) * GROUP)
            y = _dot(et_ref[...], h[rows, cols]).astype(BF16)
            f = _dot(y[:seq], ec_ref[:GROUP, :]) + _dot(y[seq:], ec_ref[GROUP:, :])
            f_scr[rows, cols] = (f * norm).astype(BF16)
    o_ref[...] = x + mod_ref[2:3, :] * _dot(f_scr[...], w_ref[...].astype(BF16))


def _fourier_call(x, mods, li, cond_of_row, seq, P):
    m = x.shape[0]
    lj = li // 4
    e_t, e_c = _dft_tables(seq)
    return pl.pallas_call(
        functools.partial(_fourier_kernel, seq=seq),
        grid=(m // ROW_TILE,),
        in_specs=[
            pl.BlockSpec((ROW_TILE, D_MODEL), lambda i: (i, 0)),
            _mod_spec(li, cond_of_row, ROW_TILE, 1),
            pl.BlockSpec((None, 1, D_MODEL), lambda i: (li, 0, 0)),
            pl.BlockSpec((2 * seq, seq), lambda i: (0, 0)),
            pl.BlockSpec((2 * GROUP, GROUP), lambda i: (0, 0)),
            pl.BlockSpec((None, D_MODEL, D_MODEL), lambda i: (lj, 0, 0)),
        ],
        out_specs=pl.BlockSpec((ROW_TILE, D_MODEL), lambda i: (i, 0)),
        out_shape=jax.ShapeDtypeStruct((m, D_MODEL), F32),
        scratch_shapes=[pltpu.VMEM((ROW_TILE, D_MODEL), BF16)],
        compiler_params=_params(("arbitrary",)),
        name=f"fourier_t{seq}",
    )(x, mods, P["norm1_w"], e_t, e_c, P["fnet_w"])


def _rope_tables(rows):
    r, col = jnp.meshgrid(jnp.arange(rows), jnp.arange(GRID_W), indexing="ij")
    r = r.reshape(-1).astype(F32)
    col = col.reshape(-1).astype(F32)
    n_freq = QK_ROPE // 4
    inv = ROPE_THETA ** (-jnp.arange(n_freq, dtype=F32) / n_freq)
    ang = jnp.concatenate([r[:, None] * inv, col[:, None] * inv], axis=-1)
    cos = jnp.repeat(jnp.cos(ang), 2, axis=-1)
    sin = jnp.repeat(jnp.sin(ang), 2, axis=-1)
    t = cos.shape[0]
    pad = LANES - QK_NOPE - QK_ROPE
    cos_p = jnp.concatenate([jnp.ones((t, QK_NOPE), F32), cos, jnp.ones((t, pad), F32)], axis=-1)
    sin_p = jnp.concatenate([jnp.zeros((t, QK_NOPE), F32), sin, jnp.zeros((t, pad), F32)], axis=-1)
    return cos_p, sin_p


def _pair_swap(w):
    wr = w.reshape(w.shape[:-1] + (w.shape[-1] // 2, 2))
    return jnp.stack([-wr[..., 1], wr[..., 0]], axis=-1).reshape(w.shape)


def _mla_weights(P, j, rope, seq_rows):
    pad = LANES - QK_NOPE - QK_ROPE
    wuq = P["mla_wuq"][j].reshape(Q_LORA, N_HEADS, QK_NOPE + QK_ROPE)
    q_slots = jnp.pad(wuq, ((0, 0), (0, 0), (0, pad))).reshape(Q_LORA, N_HEADS * LANES)
    wdkv = P["mla_wdkv"][j]
    kr_slot = jnp.pad(wdkv[:, KV_LORA:], ((0, 0), (QK_NOPE, pad)))
    parts1 = [P["mla_wdq"][j], wdkv[:, :KV_LORA], kr_slot]
    parts2 = [q_slots]
    if rope:
        parts1.append(jnp.pad(_pair_swap(wdkv[:, KV_LORA:]), ((0, 0), (QK_NOPE, pad))))
        q_sw = jnp.pad(_pair_swap(wuq[:, :, QK_NOPE:]), ((0, 0), (0, 0), (QK_NOPE, pad)))
        parts2.append(q_sw.reshape(Q_LORA, N_HEADS * LANES))
        cos, sin = _rope_tables(seq_rows // GRID_W)
    else:
        cos = sin = jnp.zeros((PROJ_TILE, LANES), F32)
    wukv = P["mla_wukv"][j].reshape(KV_LORA, N_HEADS, QK_NOPE + V_HEAD)
    wk = jnp.pad(wukv[:, :, :QK_NOPE], ((0, 0), (0, 0), (0, LANES - QK_NOPE)))
    return dict(
        norm1_w=P["norm1_w"], q_norm=P["mla_q_norm"][j][None, :], kv_norm=P["mla_kv_norm"][j][None, :],
        w1=jnp.concatenate(parts1, axis=1), w2=jnp.concatenate(parts2, axis=1),
        wk=wk.reshape(KV_LORA, N_HEADS * LANES),
        wv=wukv[:, :, QK_NOPE:].reshape(KV_LORA, N_HEADS * V_HEAD), cos=cos, sin=sin)


def _mla_layer(x, mods, li, cond_of_row, batch, seq, P, cache):
    j = li // 4
    W = _mla_weights(P, j, cache is not None, seq)
    q, k, v, ckv, krope = _mla_proj_call(x, mods, li, cond_of_row, W, cache is not None)
    q = q.reshape(N_HEADS, batch, seq, LANES)
    k = k.reshape(N_HEADS, batch, seq, LANES)
    v = v.reshape(HEAD_PAIRS, batch, seq, LANES)
    if cache is None:
        o = _attn_call(q, [(k, v)], seq, HEAD_PAIRS)
    else:
        c_ckv, c_krope = cache
        past = c_ckv.shape[1]
        pad = LANES - QK_NOPE - QK_ROPE
        kr_pad = jnp.pad(c_krope.reshape(batch * past, QK_ROPE), ((0, 0), (QK_NOPE, pad)))
        kc, vc = _kv_cache_call(c_ckv.reshape(batch * past, KV_LORA), kr_pad, W)
        cached = (kc.reshape(N_HEADS, batch, past, LANES), vc.reshape(HEAD_PAIRS, batch, past, LANES))
        o = _attn_call(q, [cached, (k, v)], 512, 2)
    return o, ckv, krope


def _run_pass(x, mods, cond_of_row, batch, seq, P, cache):
    ckv = krope = None
    for li in range(DEPTH):
        kind = li % 4
        pre = None
        if kind == 0:
            o, ckv, krope = _mla_layer(x, mods, li, cond_of_row, batch, seq, P, cache)
            pre = (o, P["mla_wo"], li // 4)
        elif kind == 1:
            x = _pool_call(x, mods, li, cond_of_row, seq, P)
        elif kind == 2:
            x = _fourier_call(x, mods, li, cond_of_row, seq, P)
        else:
            x = _sconv_call(x, mods, li, cond_of_row, seq, P)
        x = _ffn_call(x, mods, li, cond_of_row, seq, P, pre, final_norm=(li == DEPTH - 1))
    return x, ckv, krope


def kernel(x_prompt, x_sample, cache_ckv, cache_krope, c, c_ctx, norm1_w, norm2_w, ada_w, ada_b, mla_wdq, mla_q_norm, mla_wuq, mla_wdkv, mla_kv_norm, mla_wukv, mla_wo, pool_w, pool_scale, fnet_w, sconv_win, sconv_conv, sconv_wout, ffn_up, ffn_conv_w, ffn_conv_b, ffn_down, final_norm_w):
    row = lambda w: w.reshape(w.shape[0], 1, w.shape[1])
    P = dict(norm1_w=row(norm1_w), norm2_w=row(norm2_w), mla_wdq=mla_wdq, mla_q_norm=mla_q_norm,
             mla_wuq=mla_wuq, mla_wdkv=mla_wdkv, mla_kv_norm=mla_kv_norm, mla_wukv=mla_wukv, mla_wo=mla_wo,
             pool_w=pool_w, pool_scale=row(pool_scale), fnet_w=fnet_w, sconv_win=sconv_win,
             sconv_conv=sconv_conv, sconv_wout=sconv_wout, ffn_conv_w=ffn_conv_w,
             ffn_conv_b=ffn_conv_b, ffn_down=ffn_down, final_norm_w=final_norm_w,
             ffn_up_cm=_chunk_major(ffn_up, 2, "ffn_up_layout"),
             sconv_win_cm=_chunk_major(sconv_win, 3, "sconv_win_layout"),
             ffn_down_cm=_down_chunks(ffn_down), sconv_wout_cm=_down_chunks(sconv_wout))
    b_ctx, t_ctx, _ = x_prompt.shape
    b_lat, t_lat, _ = x_sample.shape
    assert ROW_TILE % t_ctx == 0 and t_lat % ROW_TILE == 0 and b_lat + 1 <= N_COND

    conds = jnp.concatenate([c_ctx[None, :], c, jnp.zeros((N_COND - 1 - b_lat, D_MODEL), F32)], axis=0)
    mods = _ada_call(conds, ada_w, ada_b).reshape(DEPTH, N_COND, 6, D_MODEL)

    xp, ckv, krope = _run_pass(x_prompt.reshape(b_ctx * t_ctx, D_MODEL), mods, lambda r: 0,
                               b_ctx, t_ctx, P, None)
    xs, _, _ = _run_pass(x_sample.reshape(b_lat * t_lat, D_MODEL), mods,
                         lambda r: 1 + r // t_lat, b_lat, t_lat, P,
                         (cache_ckv[:, 0], cache_krope[:, 0]))
    return (xp.reshape(b_ctx, t_ctx, D_MODEL), xs.reshape(b_lat, t_lat, D_MODEL),
            ckv.reshape(b_ctx, 1, t_ctx, KV_LORA), krope.reshape(b_ctx, 1, t_ctx, QK_ROPE))
```

```python
import functools
import math

import numpy as np
import jax
import jax.numpy as jnp
from jax import lax
from jax.experimental import pallas as pl
from jax.experimental.pallas import tpu as pltpu

F32 = jnp.float32
BF16 = jnp.bfloat16

D_MODEL = 1024
N_HEADS = 16
QK_NOPE = 64
QK_ROPE = 32
V_HEAD = 64
Q_LORA = 384
KV_LORA = 256
ROPE_THETA = 10000.0
GRID_W = 64
POOL_WINDOWS = (2, 4, 8, 16)
N_GROUPS = 4
GROUP = D_MODEL // N_GROUPS
D_FF = 2816
EPS = 1e-6
DEPTH = 4

LANES = 128
HEAD_PAIRS = N_HEADS // 2
N_COND = 8
ROW_TILE = 1024
PROJ_TILE = 512
FF_TILE = 256
ROW_BLOCK = 256
HALO = 8
RELAYOUT_ROWS = 128
ADA_TILE = 1536
MIB = 1024 * 1024
VMEM_DEFAULT = 48 * MIB
VMEM_HEADROOM = 4 * MIB
ATTN_VMEM = 32 * MIB


def _params(sem, vmem_bytes=VMEM_DEFAULT):
    return pltpu.CompilerParams(dimension_semantics=sem, vmem_limit_bytes=int(vmem_bytes))


def _nbytes(a):
    return math.prod(a.shape) * jnp.dtype(a.dtype).itemsize


def _dot(a, b):
    return jnp.dot(a, b, preferred_element_type=F32)


def _rms(x, w):
    return x * lax.rsqrt(jnp.mean(x * x, axis=-1, keepdims=True) + EPS) * w


def _norm_mod(x, nw, mod, base):
    return _rms(x, nw) * (1.0 + mod[base + 1:base + 2, :]) + mod[base:base + 1, :]


def _gelu_exact(x):
    return 0.5 * x * (1.0 + lax.erf(x * np.float32(math.sqrt(0.5))))


def _ada_kernel(c_ref, w_ref, b_ref, o_ref):
    s = jax.nn.silu(c_ref[...]).astype(BF16)
    o_ref[...] = _dot(s, w_ref[...].astype(BF16)) + b_ref[...]


def _ada_call(conds, ada_w, ada_b):
    n_out = ada_w.shape[-1]
    return pl.pallas_call(
        _ada_kernel,
        grid=(DEPTH, n_out // ADA_TILE),
        in_specs=[
            pl.BlockSpec((N_COND, D_MODEL), lambda l, j: (0, 0)),
            pl.BlockSpec((None, D_MODEL, ADA_TILE), lambda l, j: (l, 0, j)),
            pl.BlockSpec((None, 1, ADA_TILE), lambda l, j: (l, 0, j)),
        ],
        out_specs=pl.BlockSpec((None, N_COND, ADA_TILE), lambda l, j: (l, 0, j)),
        out_shape=jax.ShapeDtypeStruct((DEPTH, N_COND, n_out), F32),
        compiler_params=_params(("arbitrary", "arbitrary"), 2 * D_MODEL * ADA_TILE * 4 + VMEM_HEADROOM),
        name="ada",
    )(conds, ada_w, ada_b.reshape(DEPTH, 1, n_out))


def _mod_spec(li, cond_of_row, tile, n_grid):
    if n_grid == 1:
        return pl.BlockSpec((None, None, 6, D_MODEL), lambda i: (li, cond_of_row(i * tile), 0, 0))
    return pl.BlockSpec((None, None, 6, D_MODEL), lambda i, j: (li, cond_of_row(i * tile), 0, 0))


def _conv_window(w, cw, t0, seq):
    n = w.shape[0]
    mid = slice(HALO, HALO + ROW_BLOCK)
    t = (t0 + lax.broadcasted_iota(jnp.int32, (ROW_BLOCK, 1), 0)) & (seq - 1)
    prev = jnp.where(t == 0, 0.0, pltpu.roll(w, 1, axis=0)[mid])
    nxt = jnp.where(t == seq - 1, 0.0, pltpu.roll(w, n - 1, axis=0)[mid])
    return prev * cw[0:1, :] + w[mid] * cw[1:2, :] + nxt * cw[2:3, :]


def _ffn_act(win, mid, cw, cb, t0, seq):
    return _gelu_exact(_conv_window(win(0), cw, t0, seq) + cb) * mid(1)


def _sconv_act(win, mid, cw, cb, t0, seq):
    return mid(0) * _conv_window(win(1) * win(2), cw, t0, seq)


def _ffn_kernel(x_ref, mod_ref, nw_ref, up_ref, cw_ref, cb_ref, down_ref, fnw_ref, o_ref, *scratch, **cfg):
    _glu_body(x_ref, mod_ref, nw_ref, up_ref, cw_ref, cb_ref, down_ref, fnw_ref, None, o_ref, *scratch,
              n_up=2, act=_ffn_act, mod_base=3, **cfg)


def _ffn_pre_kernel(x_ref, mod_ref, nw_ref, up_ref, cw_ref, cb_ref, down_ref, fnw_ref, a_ref, wa_ref, o_ref,
                    *scratch, **cfg):
    _glu_body(x_ref, mod_ref, nw_ref, up_ref, cw_ref, cb_ref, down_ref, fnw_ref, (a_ref, wa_ref), o_ref,
              *scratch, n_up=2, act=_ffn_act, mod_base=3, **cfg)


def _sconv_kernel(x_ref, mod_ref, nw_ref, up_ref, cw_ref, down_ref, o_ref, *scratch, **cfg):
    _glu_body(x_ref, mod_ref, nw_ref, up_ref, cw_ref, None, down_ref, None, None, o_ref, *scratch,
              n_up=3, act=_sconv_act, mod_base=0, final_norm=False, **cfg)


def _glu_body(x_ref, mod_ref, nw_ref, up_ref, cw_ref, cb_ref, down_ref, fnw_ref, pre, o_ref,
              h_scr, acc_scr, gu0, gu1, *, n_up, n_chunks, act, seq, mod_base, final_norm):
    gu = (gu0, gu1)
    n_blocks = ROW_TILE // ROW_BLOCK

    def up(c, slot, r):
        rows = slice(r * ROW_BLOCK, (r + 1) * ROW_BLOCK)
        res = _dot(h_scr[rows, :], up_ref[c])
        for p in range(n_up):
            gu[slot][p, HALO + r * ROW_BLOCK:HALO + (r + 1) * ROW_BLOCK, :] = res[:, p * FF_TILE:(p + 1) * FF_TILE]

    def activate(c, slot, r):
        win = lambda p: gu[slot][p, r * ROW_BLOCK:(r + 1) * ROW_BLOCK + 2 * HALO, :]
        mid = lambda p: gu[slot][p, HALO + r * ROW_BLOCK:HALO + (r + 1) * ROW_BLOCK, :]
        cb = None if cb_ref is None else cb_ref[c]
        return act(win, mid, cw_ref[c], cb, r * ROW_BLOCK, seq).astype(BF16)

    def down(a, c, r):
        rows = slice(r * ROW_BLOCK, (r + 1) * ROW_BLOCK)
        acc_scr[rows, :] += _dot(a, down_ref[c])

    def stage(s, slot):
        for r in range(n_blocks):
            a = activate(s - 1, 1 - slot, r)
            up(s, slot, r)
            down(a, s - 1, r)

    x = x_ref[...]
    if pre is not None:
        a_ref, wa_ref = pre
        x = x + mod_ref[2:3, :] * _dot(a_ref[...], wa_ref[...])
        o_ref[...] = x
    h_scr[...] = _norm_mod(x, nw_ref[...], mod_ref[...], mod_base).astype(BF16)
    acc_scr[...] = jnp.zeros_like(acc_scr)
    for g in gu:
        g[:, :HALO, :] = jnp.zeros((n_up, HALO, FF_TILE), F32)
        g[:, HALO + ROW_TILE:, :] = jnp.zeros((n_up, HALO, FF_TILE), F32)

    for r in range(n_blocks):
        up(0, 0, r)

    n_pairs = (n_chunks - 1) // 2

    def pair(i, carry):
        s = 2 * i + 1
        stage(s, 1)
        stage(s + 1, 0)
        return carry

    lax.fori_loop(0, n_pairs, pair, 0)
    for s in range(2 * n_pairs + 1, n_chunks):
        stage(s, s % 2)
    last = n_chunks - 1
    for r in range(n_blocks):
        down(activate(last, last % 2, r), last, r)

    res = x_ref if pre is None else o_ref
    y = res[...] + mod_ref[mod_base + 2:mod_base + 3, :] * acc_scr[...]
    if final_norm:
        y = _rms(y, fnw_ref[...])
    o_ref[...] = y


def _chunked(w, n_chunks):
    return w.reshape(w.shape[0], n_chunks, FF_TILE).transpose(1, 0, 2)


def _glu_call(kernel_fn, name, x, mods, li, cond_of_row, n_up, n_chunks, norm_w, operands, specs, cfg):
    m = x.shape[0]
    whole = lambda a: pl.BlockSpec(a.shape, lambda i: (0,) * a.ndim)
    in_specs = [pl.BlockSpec((ROW_TILE, D_MODEL), lambda i: (i, 0)),
                _mod_spec(li, cond_of_row, ROW_TILE, 1),
                pl.BlockSpec((None, 1, D_MODEL), lambda i: (li, 0, 0))]
    layer_w = lambda a, l: pl.BlockSpec((None,) + a.shape[1:], lambda i: (l,) + (0,) * (a.ndim - 1),
                                        pipeline_mode=pl.Buffered(1))
    rows = lambda a: pl.BlockSpec((ROW_TILE, a.shape[1]), lambda i: (i, 0))
    in_specs += [whole(a) if s is None else rows(a) if s == "rows" else layer_w(a, s)
                 for a, s in zip(operands, specs)]
    gu_shape = (n_up, ROW_TILE + 2 * HALO, FF_TILE)
    tile = ROW_TILE * D_MODEL
    vmem = (2 * 2 * tile * 4 + tile * 2 + tile * 4 + 2 * math.prod(gu_shape) * 4 + VMEM_HEADROOM
            + sum(_nbytes(a) if s is None else 2 * ROW_TILE * a.shape[1] * a.dtype.itemsize if s == "rows"
                  else _nbytes(a) // a.shape[0] for a, s in zip(operands, specs)))
    return pl.pallas_call(
        functools.partial(kernel_fn, n_chunks=n_chunks, **cfg),
        grid=(m // ROW_TILE,),
        in_specs=in_specs,
        out_specs=pl.BlockSpec((ROW_TILE, D_MODEL), lambda i: (i, 0)),
        out_shape=jax.ShapeDtypeStruct((m, D_MODEL), F32),
        scratch_shapes=[
            pltpu.VMEM((ROW_TILE, D_MODEL), BF16),
            pltpu.VMEM((ROW_TILE, D_MODEL), F32),
            pltpu.VMEM(gu_shape, F32), pltpu.VMEM(gu_shape, F32),
        ],
        compiler_params=_params(("arbitrary",), vmem),
        name=name,
    )(x, mods, norm_w, *operands)


def _chunk_major_kernel(w_ref, o_ref, *, n_up, n_chunks):
    for c in range(n_chunks):
        for p in range(n_up):
            col = (p * n_chunks + c) * FF_TILE
            o_ref[c, :, p * FF_TILE:(p + 1) * FF_TILE] = w_ref[:, col:col + FF_TILE].astype(BF16)


def _chunk_major(w, n_up, name):
    l, d, cols = w.shape
    n_chunks = cols // (n_up * FF_TILE)
    return pl.pallas_call(
        functools.partial(_chunk_major_kernel, n_up=n_up, n_chunks=n_chunks),
        grid=(l, d // RELAYOUT_ROWS),
        in_specs=[pl.BlockSpec((None, RELAYOUT_ROWS, cols), lambda li, k: (li, k, 0))],
        out_specs=pl.BlockSpec((None, n_chunks, RELAYOUT_ROWS, n_up * FF_TILE), lambda li, k: (li, 0, k, 0)),
        out_shape=jax.ShapeDtypeStruct((l, n_chunks, d, n_up * FF_TILE), BF16),
        compiler_params=_params(("arbitrary", "arbitrary"), 2 * RELAYOUT_ROWS * cols * (4 + 2) + VMEM_HEADROOM),
        name=name,
    )(w)


def _down_chunks(w):
    l, rows, d = w.shape
    return w.reshape(l, rows // FF_TILE, FF_TILE, d).astype(BF16)


def _ffn_call(x, mods, li, cond_of_row, seq, P, pre, final_norm):
    n = D_FF // FF_TILE
    operands = [P["ffn_up_cm"], _chunked(P["ffn_conv_w"][li], n), _chunked(P["ffn_conv_b"][li][None, :], n),
                P["ffn_down_cm"], P["final_norm_w"].reshape(1, D_MODEL)]
    specs = [li, None, None, li, None]
    kernel_fn = _ffn_kernel
    if pre is not None:
        a, w_a, l_a = pre
        operands += [a, w_a]
        specs += ["rows", l_a]
        kernel_fn = _ffn_pre_kernel
    return _glu_call(kernel_fn, f"ffn{li}", x, mods, li, cond_of_row, 2, n, P["norm2_w"],
                     operands, specs, dict(seq=seq, final_norm=final_norm))


def _sconv_call(x, mods, li, cond_of_row, seq, P):
    lj = li // 4
    n = D_MODEL // FF_TILE
    operands = [P["sconv_win_cm"], _chunked(P["sconv_conv"][lj], n), P["sconv_wout_cm"]]
    return _glu_call(_sconv_kernel, f"sconv{li}", x, mods, li, cond_of_row, 3, n, P["norm1_w"],
                     operands, [lj, None, lj], dict(seq=seq))


def _emit_kv(ckv, krope_pad, wk_ref, wv_ref, k_ref, v_ref):
    c = ckv.astype(BF16)
    kx = _dot(c, wk_ref[...].astype(BF16))
    vx = _dot(c, wv_ref[...].astype(BF16))
    for h in range(N_HEADS):
        k_ref[h] = (kx[:, h * LANES:(h + 1) * LANES] + krope_pad).astype(BF16)
    for p in range(HEAD_PAIRS):
        v_ref[p] = vx[:, p * LANES:(p + 1) * LANES].astype(BF16)


def _mla_proj_kernel(x_ref, mod_ref, nw_ref, w1_ref, qn_ref, kvn_ref, w2_ref, wk_ref, wv_ref, cos_ref,
                     sin_ref, q_ref, k_ref, v_ref, ckv_ref, kr_ref, *, rope):
    h = _norm_mod(x_ref[...], nw_ref[...], mod_ref[...], 0).astype(BF16)
    z = _dot(h, w1_ref[...].astype(BF16))
    cq = _rms(z[:, :Q_LORA], qn_ref[...]).astype(BF16)
    ckv = _rms(z[:, Q_LORA:Q_LORA + KV_LORA], kvn_ref[...])
    kr = z[:, Q_LORA + KV_LORA:Q_LORA + KV_LORA + LANES]
    if rope:
        cos, sin = cos_ref[...], sin_ref[...]
        kr = kr * cos + z[:, Q_LORA + KV_LORA + LANES:] * sin
    for p in range(HEAD_PAIRS):
        q2 = _dot(cq, w2_ref[:, 2 * p * LANES:(2 * p + 2) * LANES].astype(BF16))
        if rope:
            off = N_HEADS * LANES
            qs = _dot(cq, w2_ref[:, off + 2 * p * LANES:off + (2 * p + 2) * LANES].astype(BF16))
        for e in range(2):
            qh = q2[:, e * LANES:(e + 1) * LANES]
            if rope:
                qh = qh * cos + qs[:, e * LANES:(e + 1) * LANES] * sin
            q_ref[2 * p + e] = qh.astype(BF16)
    ckv_ref[...] = ckv
    kr_ref[...] = kr[:, QK_NOPE:QK_NOPE + QK_ROPE]
    _emit_kv(ckv, kr, wk_ref, wv_ref, k_ref, v_ref)


def _mla_proj_call(x, mods, li, cond_of_row, W, rope):
    m = x.shape[0]
    n1 = W["w1"].shape[1]
    n2 = W["w2"].shape[1]
    const = lambda i: (0, 0)
    pos_tiles = W["cos"].shape[0] // PROJ_TILE
    pos = lambda i: (i % pos_tiles, 0)
    return pl.pallas_call(
        functools.partial(_mla_proj_kernel, rope=rope),
        grid=(m // PROJ_TILE,),
        in_specs=[
            pl.BlockSpec((PROJ_TILE, D_MODEL), lambda i: (i, 0)),
            _mod_spec(li, cond_of_row, PROJ_TILE, 1),
            pl.BlockSpec((None, 1, D_MODEL), lambda i: (li, 0, 0)),
            pl.BlockSpec((D_MODEL, n1), const),
            pl.BlockSpec((1, Q_LORA), const),
            pl.BlockSpec((1, KV_LORA), const),
            pl.BlockSpec((Q_LORA, n2), const),
            pl.BlockSpec((KV_LORA, N_HEADS * LANES), const),
            pl.BlockSpec((KV_LORA, HEAD_PAIRS * LANES), const),
            pl.BlockSpec((PROJ_TILE, LANES), pos),
            pl.BlockSpec((PROJ_TILE, LANES), pos),
        ],
        out_specs=[
            pl.BlockSpec((N_HEADS, PROJ_TILE, LANES), lambda i: (0, i, 0)),
            pl.BlockSpec((N_HEADS, PROJ_TILE, LANES), lambda i: (0, i, 0)),
            pl.BlockSpec((HEAD_PAIRS, PROJ_TILE, LANES), lambda i: (0, i, 0)),
            pl.BlockSpec((PROJ_TILE, KV_LORA), lambda i: (i, 0)),
            pl.BlockSpec((PROJ_TILE, QK_ROPE), lambda i: (i, 0)),
        ],
        out_shape=[
            jax.ShapeDtypeStruct((N_HEADS, m, LANES), BF16),
            jax.ShapeDtypeStruct((N_HEADS, m, LANES), BF16),
            jax.ShapeDtypeStruct((HEAD_PAIRS, m, LANES), BF16),
            jax.ShapeDtypeStruct((m, KV_LORA), F32),
            jax.ShapeDtypeStruct((m, QK_ROPE), F32),
        ],
        compiler_params=_params(("arbitrary",)),
        name="mla_proj_rope" if rope else "mla_proj",
    )(x, mods, W["norm1_w"], W["w1"], W["q_norm"], W["kv_norm"], W["w2"], W["wk"], W["wv"],
      W["cos"], W["sin"])


def _kv_cache_kernel(ckv_ref, kr_ref, wk_ref, wv_ref, k_ref, v_ref):
    _emit_kv(ckv_ref[...], kr_ref[...], wk_ref, wv_ref, k_ref, v_ref)


def _kv_cache_call(ckv, krope_pad, W):
    m = ckv.shape[0]
    const = lambda i: (0, 0)
    return pl.pallas_call(
        _kv_cache_kernel,
        grid=(1,),
        in_specs=[
            pl.BlockSpec((m, KV_LORA), const),
            pl.BlockSpec((m, LANES), const),
            pl.BlockSpec((KV_LORA, N_HEADS * LANES), const),
            pl.BlockSpec((KV_LORA, HEAD_PAIRS * LANES), const),
        ],
        out_specs=[
            pl.BlockSpec((N_HEADS, m, LANES), lambda i: (0, 0, 0)),
            pl.BlockSpec((HEAD_PAIRS, m, LANES), lambda i: (0, 0, 0)),
        ],
        out_shape=[
            jax.ShapeDtypeStruct((N_HEADS, m, LANES), BF16),
            jax.ShapeDtypeStruct((HEAD_PAIRS, m, LANES), BF16),
        ],
        compiler_params=_params(("arbitrary",)),
        name="kv_cache",
    )(ckv, krope_pad, W["wk"], W["wv"])


def _attn_kernel(q_ref, *refs, n_pairs):
    o_ref, kv = refs[-1], refs[:-1]
    parts = [(kv[i], kv[i + 1]) for i in range(0, len(kv), 2)]
    scale = np.float32(1.0 / math.sqrt(QK_NOPE + QK_ROPE))
    for p in range(n_pairs):
        outs = []
        for e in range(2):
            q = q_ref[2 * p + e]
            s = [lax.dot_general(q, k_ref[2 * p + e], (((1,), (1,)), ((), ())),
                                 preferred_element_type=F32) * scale for k_ref, _ in parts]
            m = functools.reduce(jnp.maximum, [jnp.max(x, axis=-1, keepdims=True) for x in s])
            w = [jnp.exp(x - m) for x in s]
            l = sum(jnp.sum(x, axis=-1, keepdims=True) for x in w)
            pv = sum(_dot(x.astype(BF16), v_ref[p]) for x, (_, v_ref) in zip(w, parts))
            outs.append(pv / l)
        lane = lax.broadcasted_iota(jnp.int32, outs[0].shape, 1)
        o_ref[:, p * LANES:(p + 1) * LANES] = jnp.where(lane < V_HEAD, outs[0], outs[1]).astype(BF16)


def _attn_call(q, kv_parts, q_tile, n_pairs):
    _, b, t, _ = q.shape
    nq = t // q_tile
    in_specs = [pl.BlockSpec((2 * n_pairs, None, q_tile, LANES), lambda bi, g, qi: (g, bi, qi, 0))]
    operands = [q]
    for k, v in kv_parts:
        s = k.shape[2]
        in_specs += [pl.BlockSpec((2 * n_pairs, None, s, LANES), lambda bi, g, qi: (g, bi, 0, 0)),
                     pl.BlockSpec((n_pairs, None, s, LANES), lambda bi, g, qi: (g, bi, 0, 0))]
        operands += [k, v]
    return pl.pallas_call(
        functools.partial(_attn_kernel, n_pairs=n_pairs),
        grid=(b, HEAD_PAIRS // n_pairs, nq),
        in_specs=in_specs,
        out_specs=pl.BlockSpec((q_tile, n_pairs * LANES), lambda bi, g, qi: (bi * nq + qi, g)),
        out_shape=jax.ShapeDtypeStruct((b * t, N_HEADS * V_HEAD), BF16),
        compiler_params=_params(("arbitrary", "arbitrary", "arbitrary"), ATTN_VMEM),
        name=f"attn_{len(kv_parts)}part",
    )(*operands)


def _pool_kernel(x_ref, mod_ref, nw_ref, w_ref, sc_ref, o_ref, band_scr, y_scr, *, seq):
    @pl.when(pl.program_id(0) == 0)
    def _():
        r = lax.broadcasted_iota(jnp.int32, (seq, seq), 0)
        c = lax.broadcasted_iota(jnp.int32, (seq, seq), 1)
        for g, win in enumerate(POOL_WINDOWS):
            inside = (c >= r - win // 2) & (c < r + win - win // 2)
            band_scr[g] = jnp.where(inside, 1.0, 0.0).astype(BF16)

    x = x_ref[...]
    h = _norm_mod(x, nw_ref[...], mod_ref[...], 0)
    h_hi = h.astype(BF16)
    h_lo = (h - h_hi.astype(F32)).astype(BF16)
    t = lax.broadcasted_iota(jnp.int32, (seq, 1), 0)
    for g, win in enumerate(POOL_WINDOWS):
        cnt = (jnp.minimum(t + (win - win // 2), seq) - jnp.maximum(t - win // 2, 0)).astype(F32)
        cols = slice(g * GROUP, (g + 1) * GROUP)
        wg = w_ref[g].astype(BF16)
        for s in range(x.shape[0] // seq):
            rows = slice(s * seq, (s + 1) * seq)
            tot = _dot(band_scr[g], h_hi[rows, cols]) + _dot(band_scr[g], h_lo[rows, cols])
            pooled = (tot / cnt - h[rows, cols]).astype(BF16)
            y_scr[rows, cols] = _dot(pooled, wg)
    o_ref[...] = x + mod_ref[2:3, :] * (y_scr[...] * sc_ref[...])


def _pool_call(x, mods, li, cond_of_row, seq, P):
    m = x.shape[0]
    lj = li // 4
    return pl.pallas_call(
        functools.partial(_pool_kernel, seq=seq),
        grid=(m // ROW_TILE,),
        in_specs=[
            pl.BlockSpec((ROW_TILE, D_MODEL), lambda i: (i, 0)),
            _mod_spec(li, cond_of_row, ROW_TILE, 1),
            pl.BlockSpec((None, 1, D_MODEL), lambda i: (li, 0, 0)),
            pl.BlockSpec((None, N_GROUPS, GROUP, GROUP), lambda i: (lj, 0, 0, 0)),
            pl.BlockSpec((None, 1, D_MODEL), lambda i: (lj, 0, 0)),
        ],
        out_specs=pl.BlockSpec((ROW_TILE, D_MODEL), lambda i: (i, 0)),
        out_shape=jax.ShapeDtypeStruct((m, D_MODEL), F32),
        scratch_shapes=[pltpu.VMEM((N_GROUPS, seq, seq), BF16), pltpu.VMEM((ROW_TILE, D_MODEL), F32)],
        compiler_params=_params(("arbitrary",)),
        name=f"pool_t{seq}",
    )(x, mods, P["norm1_w"], P["pool_w"], P["pool_scale"])


def _dft_tables(seq):
    def cs(n):
        k = np.arange(n, dtype=np.int64)
        ang = 2.0 * np.pi * ((k[:, None] * k[None, :]) % n) / n
        return np.cos(ang), np.sin(ang)
    ct, st = cs(seq)
    cc, sc = cs(GROUP)
    e_t = jnp.asarray(np.concatenate([ct, st], axis=0), dtype=F32)
    e_c = jnp.asarray(np.concatenate([cc, -sc], axis=0), dtype=F32)
    return e_t.astype(BF16), e_c.astype(BF16)


def _fourier_kernel(x_ref, mod_ref, nw_ref, et_ref, ec_ref, w_ref, o_ref, f_scr, *, seq):
    x = x_ref[...]
    h = _norm_mod(x, nw_ref[...], mod_ref[...], 0).astype(BF16)
    norm = np.float32(1.0 / math.sqrt(seq * GROUP))
    for s in range(x.shape[0] // seq):
        rows = slice(s * seq, (s + 1) * seq)
        for g in range(N_GROUPS):
            cols = slice(g * GROUP, (g + 1) * GROUP)
            y = _dot(et_ref[...], h[rows, cols]).astype(BF16)
            f = _dot(y[:seq], ec_ref[:GROUP, :]) + _dot(y[seq:], ec_ref[GROUP:, :])
            f_scr[rows, cols] = (f * norm).astype(BF16)
    o_ref[...] = x + mod_ref[2:3, :] * _dot(f_scr[...], w_ref[...].astype(BF16))


def _fourier_call(x, mods, li, cond_of_row, seq, P):
    m = x.shape[0]
    lj = li // 4
    e_t, e_c = _dft_tables(seq)
    return pl.pallas_call(
        functools.partial(_fourier_kernel, seq=seq),
        grid=(m // ROW_TILE,),
        in_specs=[
            pl.BlockSpec((ROW_TILE, D_MODEL), lambda i: (i, 0)),
            _mod_spec(li, cond_of_row, ROW_TILE, 1),
            pl.BlockSpec((None, 1, D_MODEL), lambda i: (li, 0, 0)),
            pl.BlockSpec((2 * seq, seq), lambda i: (0, 0)),
            pl.BlockSpec((2 * GROUP, GROUP), lambda i: (0, 0)),
            pl.BlockSpec((None, D_MODEL, D_MODEL), lambda i: (lj, 0, 0)),
        ],
        out_specs=pl.BlockSpec((ROW_TILE, D_MODEL), lambda i: (i, 0)),
        out_shape=jax.ShapeDtypeStruct((m, D_MODEL), F32),
        scratch_shapes=[pltpu.VMEM((ROW_TILE, D_MODEL), BF16)],
        compiler_params=_params(("arbitrary",)),
        name=f"fourier_t{seq}",
    )(x, mods, P["norm1_w"], e_t, e_c, P["fnet_w"])


def _rope_tables(rows):
    r, col = jnp.meshgrid(jnp.arange(rows), jnp.arange(GRID_W), indexing="ij")
    r = r.reshape(-1).astype(F32)
    col = col.reshape(-1).astype(F32)
    n_freq = QK_ROPE // 4
    inv = ROPE_THETA ** (-jnp.arange(n_freq, dtype=F32) / n_freq)
    ang = jnp.concatenate([r[:, None] * inv, col[:, None] * inv], axis=-1)
    cos = jnp.repeat(jnp.cos(ang), 2, axis=-1)
    sin = jnp.repeat(jnp.sin(ang), 2, axis=-1)
    t = cos.shape[0]
    pad = LANES - QK_NOPE - QK_ROPE
    cos_p = jnp.concatenate([jnp.ones((t, QK_NOPE), F32), cos, jnp.ones((t, pad), F32)], axis=-1)
    sin_p = jnp.concatenate([jnp.zeros((t, QK_NOPE), F32), sin, jnp.zeros((t, pad), F32)], axis=-1)
    return cos_p, sin_p


def _pair_swap(w):
    wr = w.reshape(w.shape[:-1] + (w.shape[-1] // 2, 2))
    return jnp.stack([-wr[..., 1], wr[..., 0]], axis=-1).reshape(w.shape)


def _mla_weights(P, j, rope, seq_rows):
    pad = LANES - QK_NOPE - QK_ROPE
    wuq = P["mla_wuq"][j].reshape(Q_LORA, N_HEADS, QK_NOPE + QK_ROPE)
    q_slots = jnp.pad(wuq, ((0, 0), (0, 0), (0, pad))).reshape(Q_LORA, N_HEADS * LANES)
    wdkv = P["mla_wdkv"][j]
    kr_slot = jnp.pad(wdkv[:, KV_LORA:], ((0, 0), (QK_NOPE, pad)))
    parts1 = [P["mla_wdq"][j], wdkv[:, :KV_LORA], kr_slot]
    parts2 = [q_slots]
    if rope:
        parts1.append(jnp.pad(_pair_swap(wdkv[:, KV_LORA:]), ((0, 0), (QK_NOPE, pad))))
        q_sw = jnp.pad(_pair_swap(wuq[:, :, QK_NOPE:]), ((0, 0), (0, 0), (QK_NOPE, pad)))
        parts2.append(q_sw.reshape(Q_LORA, N_HEADS * LANES))
        cos, sin = _rope_tables(seq_rows // GRID_W)
    else:
        cos = sin = jnp.zeros((PROJ_TILE, LANES), F32)
    wukv = P["mla_wukv"][j].reshape(KV_LORA, N_HEADS, QK_NOPE + V_HEAD)
    wk = jnp.pad(wukv[:, :, :QK_NOPE], ((0, 0), (0, 0), (0, LANES - QK_NOPE)))
    return dict(
        norm1_w=P["norm1_w"], q_norm=P["mla_q_norm"][j][None, :], kv_norm=P["mla_kv_norm"][j][None, :],
        w1=jnp.concatenate(parts1, axis=1), w2=jnp.concatenate(parts2, axis=1),
        wk=wk.reshape(KV_LORA, N_HEADS * LANES),
        wv=wukv[:, :, QK_NOPE:].reshape(KV_LORA, N_HEADS * V_HEAD), cos=cos, sin=sin)


def _mla_layer(x, mods, li, cond_of_row, batch, seq, P, cache):
    j = li // 4
    W = _mla_weights(P, j, cache is not None, seq)
    q, k, v, ckv, krope = _mla_proj_call(x, mods, li, cond_of_row, W, cache is not None)
    q = q.reshape(N_HEADS, batch, seq, LANES)
    k = k.reshape(N_HEADS, batch, seq, LANES)
    v = v.reshape(HEAD_PAIRS, batch, seq, LANES)
    if cache is None:
        o = _attn_call(q, [(k, v)], seq, HEAD_PAIRS)
    else:
        c_ckv, c_krope = cache
        past = c_ckv.shape[1]
        pad = LANES - QK_NOPE - QK_ROPE
        kr_pad = jnp.pad(c_krope.reshape(batch * past, QK_ROPE), ((0, 0), (QK_NOPE, pad)))
        kc, vc = _kv_cache_call(c_ckv.reshape(batch * past, KV_LORA), kr_pad, W)
        cached = (kc.reshape(N_HEADS, batch, past, LANES), vc.reshape(HEAD_PAIRS, batch, past, LANES))
        o = _attn_call(q, [cached, (k, v)], 512, 2)
    return o, ckv, krope


def _run_pass(x, mods, cond_of_row, batch, seq, P, cache):
    ckv = krope = None
    for li in range(DEPTH):
        kind = li % 4
        pre = None
        if kind == 0:
            o, ckv, krope = _mla_layer(x, mods, li, cond_of_row, batch, seq, P, cache)
            pre = (o, P["mla_wo_bf16"], li // 4)
        elif kind == 1:
            x = _pool_call(x, mods, li, cond_of_row, seq, P)
        elif kind == 2:
            x = _fourier_call(x, mods, li, cond_of_row, seq, P)
        else:
            x = _sconv_call(x, mods, li, cond_of_row, seq, P)
        x = _ffn_call(x, mods, li, cond_of_row, seq, P, pre, final_norm=(li == DEPTH - 1))
    return x, ckv, krope


def kernel(x_prompt, x_sample, cache_ckv, cache_krope, c, c_ctx, norm1_w, norm2_w, ada_w, ada_b, mla_wdq, mla_q_norm, mla_wuq, mla_wdkv, mla_kv_norm, mla_wukv, mla_wo, pool_w, pool_scale, fnet_w, sconv_win, sconv_conv, sconv_wout, ffn_up, ffn_conv_w, ffn_conv_b, ffn_down, final_norm_w):
    row = lambda w: w.reshape(w.shape[0], 1, w.shape[1])
    P = dict(norm1_w=row(norm1_w), norm2_w=row(norm2_w), mla_wdq=mla_wdq, mla_q_norm=mla_q_norm,
             mla_wuq=mla_wuq, mla_wdkv=mla_wdkv, mla_kv_norm=mla_kv_norm, mla_wukv=mla_wukv,
             mla_wo_bf16=mla_wo.astype(BF16),
             pool_w=pool_w, pool_scale=row(pool_scale), fnet_w=fnet_w, sconv_win=sconv_win,
             sconv_conv=sconv_conv, sconv_wout=sconv_wout, ffn_conv_w=ffn_conv_w,
             ffn_conv_b=ffn_conv_b, ffn_down=ffn_down, final_norm_w=final_norm_w,
             ffn_up_cm=_chunk_major(ffn_up, 2, "ffn_up_layout"),
             sconv_win_cm=_chunk_major(sconv_win, 3, "sconv_win_layout"),
             ffn_down_cm=_down_chunks(ffn_down), sconv_wout_cm=_down_chunks(sconv_wout))
    b_ctx, t_ctx, _ = x_prompt.shape
    b_lat, t_lat, _ = x_sample.shape
    assert ROW_TILE % t_ctx == 0 and t_lat % ROW_TILE == 0 and b_lat + 1 <= N_COND

    conds = jnp.concatenate([c_ctx[None, :], c, jnp.zeros((N_COND - 1 - b_lat, D_MODEL), F32)], axis=0)
    mods = _ada_call(conds, ada_w, ada_b).reshape(DEPTH, N_COND, 6, D_MODEL)

    xp, ckv, krope = _run_pass(x_prompt.reshape(b_ctx * t_ctx, D_MODEL), mods, lambda r: 0,
                               b_ctx, t_ctx, P, None)
    xs, _, _ = _run_pass(x_sample.reshape(b_lat * t_lat, D_MODEL), mods,
                         lambda r: 1 + r // t_lat, b_lat, t_lat, P,
                         (cache_ckv[:, 0], cache_krope[:, 0]))
    return (xp.reshape(b_ctx, t_ctx, D_MODEL), xs.reshape(b_lat, t_lat, D_MODEL),
            ckv.reshape(b_ctx, 1, t_ctx, KV_LORA), krope.reshape(b_ctx, 1, t_ctx, QK_ROPE))
```

```python
import functools
import math

import numpy as np
import jax
import jax.numpy as jnp
from jax import lax
from jax.experimental import pallas as pl
from jax.experimental.pallas import tpu as pltpu

F32 = jnp.float32
BF16 = jnp.bfloat16

D_MODEL = 1024
N_HEADS = 16
QK_NOPE = 64
QK_ROPE = 32
V_HEAD = 64
Q_LORA = 384
KV_LORA = 256
ROPE_THETA = 10000.0
GRID_W = 64
POOL_WINDOWS = (2, 4, 8, 16)
N_GROUPS = 4
GROUP = D_MODEL // N_GROUPS
D_FF = 2816
EPS = 1e-6
DEPTH = 4

LANES = 128
Q_SCALE = np.float32(math.log2(math.e) / math.sqrt(QK_NOPE + QK_ROPE))
HEAD_PAIRS = N_HEADS // 2
N_COND = 8
ROW_TILE = 1024
PROJ_TILE = 512
FF_TILE = 256
ROW_BLOCK = 256
HALO = 8
RELAYOUT_ROWS = 128
ADA_TILE = 1536
MIB = 1024 * 1024
VMEM_DEFAULT = 48 * MIB
VMEM_HEADROOM = 4 * MIB
ATTN_VMEM = 32 * MIB


def _params(sem, vmem_bytes=VMEM_DEFAULT):
    return pltpu.CompilerParams(dimension_semantics=sem, vmem_limit_bytes=int(vmem_bytes))


def _nbytes(a):
    return math.prod(a.shape) * jnp.dtype(a.dtype).itemsize


def _dot(a, b):
    return jnp.dot(a, b, preferred_element_type=F32)


def _rms(x, w):
    return x * lax.rsqrt(jnp.mean(x * x, axis=-1, keepdims=True) + EPS) * w


def _norm_mod(x, nw, mod, base):
    return _rms(x, nw) * (1.0 + mod[base + 1:base + 2, :]) + mod[base:base + 1, :]


def _gelu_exact(x):
    return 0.5 * x * (1.0 + lax.erf(x * np.float32(math.sqrt(0.5))))


def _ada_kernel(c_ref, w_ref, b_ref, o_ref):
    s = jax.nn.silu(c_ref[...]).astype(BF16)
    o_ref[...] = _dot(s, w_ref[...].astype(BF16)) + b_ref[...]


def _ada_call(conds, ada_w, ada_b):
    n_out = ada_w.shape[-1]
    return pl.pallas_call(
        _ada_kernel,
        grid=(DEPTH, n_out // ADA_TILE),
        in_specs=[
            pl.BlockSpec((N_COND, D_MODEL), lambda l, j: (0, 0)),
            pl.BlockSpec((None, D_MODEL, ADA_TILE), lambda l, j: (l, 0, j)),
            pl.BlockSpec((None, 1, ADA_TILE), lambda l, j: (l, 0, j)),
        ],
        out_specs=pl.BlockSpec((None, N_COND, ADA_TILE), lambda l, j: (l, 0, j)),
        out_shape=jax.ShapeDtypeStruct((DEPTH, N_COND, n_out), F32),
        compiler_params=_params(("arbitrary", "arbitrary"), 2 * D_MODEL * ADA_TILE * 4 + VMEM_HEADROOM),
        name="ada",
    )(conds, ada_w, ada_b.reshape(DEPTH, 1, n_out))


def _mod_spec(li, cond_of_row, tile, n_grid):
    if n_grid == 1:
        return pl.BlockSpec((None, None, 6, D_MODEL), lambda i: (li, cond_of_row(i * tile), 0, 0))
    return pl.BlockSpec((None, None, 6, D_MODEL), lambda i, j: (li, cond_of_row(i * tile), 0, 0))


def _conv_window(w, cw, t0, seq):
    n = w.shape[0]
    mid = slice(HALO, HALO + ROW_BLOCK)
    t = (t0 + lax.broadcasted_iota(jnp.int32, (ROW_BLOCK, 1), 0)) & (seq - 1)
    prev = jnp.where(t == 0, 0.0, pltpu.roll(w, 1, axis=0)[mid])
    nxt = jnp.where(t == seq - 1, 0.0, pltpu.roll(w, n - 1, axis=0)[mid])
    return prev * cw[0:1, :] + w[mid] * cw[1:2, :] + nxt * cw[2:3, :]


def _ffn_act(win, mid, cw, cb, t0, seq):
    return _gelu_exact(_conv_window(win(0), cw, t0, seq) + cb) * mid(1)


def _sconv_act(win, mid, cw, cb, t0, seq):
    return mid(0) * _conv_window(win(1) * win(2), cw, t0, seq)


def _ffn_kernel(x_ref, mod_ref, nw_ref, up_ref, cw_ref, cb_ref, down_ref, fnw_ref, o_ref, *scratch, **cfg):
    _glu_body(x_ref, mod_ref, nw_ref, up_ref, cw_ref, cb_ref, down_ref, fnw_ref, None, o_ref, *scratch,
              n_up=2, act=_ffn_act, mod_base=3, **cfg)


def _ffn_pre_kernel(x_ref, mod_ref, nw_ref, up_ref, cw_ref, cb_ref, down_ref, fnw_ref, a_ref, wa_ref, o_ref,
                    *scratch, **cfg):
    _glu_body(x_ref, mod_ref, nw_ref, up_ref, cw_ref, cb_ref, down_ref, fnw_ref, (a_ref, wa_ref), o_ref,
              *scratch, n_up=2, act=_ffn_act, mod_base=3, **cfg)


def _sconv_kernel(x_ref, mod_ref, nw_ref, up_ref, cw_ref, down_ref, o_ref, *scratch, **cfg):
    _glu_body(x_ref, mod_ref, nw_ref, up_ref, cw_ref, None, down_ref, None, None, o_ref, *scratch,
              n_up=3, act=_sconv_act, mod_base=0, final_norm=False, **cfg)


def _glu_body(x_ref, mod_ref, nw_ref, up_ref, cw_ref, cb_ref, down_ref, fnw_ref, pre, o_ref,
              h_scr, acc_scr, gu0, gu1, *, n_up, n_chunks, act, seq, mod_base, final_norm):
    gu = (gu0, gu1)
    n_blocks = ROW_TILE // ROW_BLOCK

    def up(c, slot, r):
        rows = slice(r * ROW_BLOCK, (r + 1) * ROW_BLOCK)
        res = _dot(h_scr[rows, :], up_ref[c])
        for p in range(n_up):
            gu[slot][p, HALO + r * ROW_BLOCK:HALO + (r + 1) * ROW_BLOCK, :] = res[:, p * FF_TILE:(p + 1) * FF_TILE]

    def activate(c, slot, r):
        win = lambda p: gu[slot][p, r * ROW_BLOCK:(r + 1) * ROW_BLOCK + 2 * HALO, :]
        mid = lambda p: gu[slot][p, HALO + r * ROW_BLOCK:HALO + (r + 1) * ROW_BLOCK, :]
        cb = None if cb_ref is None else cb_ref[c]
        return act(win, mid, cw_ref[c], cb, r * ROW_BLOCK, seq).astype(BF16)

    def down(a, c, r):
        rows = slice(r * ROW_BLOCK, (r + 1) * ROW_BLOCK)
        acc_scr[rows, :] += _dot(a, down_ref[c])

    def stage(s, slot):
        for r in range(n_blocks):
            a = activate(s - 1, 1 - slot, r)
            up(s, slot, r)
            down(a, s - 1, r)

    x = x_ref[...]
    if pre is not None:
        a_ref, wa_ref = pre
        x = x + mod_ref[2:3, :] * _dot(a_ref[...], wa_ref[...])
        o_ref[...] = x
    h_scr[...] = _norm_mod(x, nw_ref[...], mod_ref[...], mod_base).astype(BF16)
    acc_scr[...] = jnp.zeros_like(acc_scr)
    for g in gu:
        g[:, :HALO, :] = jnp.zeros((n_up, HALO, FF_TILE), F32)
        g[:, HALO + ROW_TILE:, :] = jnp.zeros((n_up, HALO, FF_TILE), F32)

    for r in range(n_blocks):
        up(0, 0, r)

    n_pairs = (n_chunks - 1) // 2

    def pair(i, carry):
        s = 2 * i + 1
        stage(s, 1)
        stage(s + 1, 0)
        return carry

    lax.fori_loop(0, n_pairs, pair, 0)
    for s in range(2 * n_pairs + 1, n_chunks):
        stage(s, s % 2)
    last = n_chunks - 1
    for r in range(n_blocks):
        down(activate(last, last % 2, r), last, r)

    res = x_ref if pre is None else o_ref
    y = res[...] + mod_ref[mod_base + 2:mod_base + 3, :] * acc_scr[...]
    if final_norm:
        y = _rms(y, fnw_ref[...])
    o_ref[...] = y


def _chunked(w, n_chunks):
    return w.reshape(w.shape[0], n_chunks, FF_TILE).transpose(1, 0, 2)


def _glu_call(kernel_fn, name, x, mods, li, cond_of_row, n_up, n_chunks, norm_w, operands, specs, cfg):
    m = x.shape[0]
    whole = lambda a: pl.BlockSpec(a.shape, lambda i: (0,) * a.ndim)
    in_specs = [pl.BlockSpec((ROW_TILE, D_MODEL), lambda i: (i, 0)),
                _mod_spec(li, cond_of_row, ROW_TILE, 1),
                pl.BlockSpec((None, 1, D_MODEL), lambda i: (li, 0, 0))]
    layer_w = lambda a, l: pl.BlockSpec((None,) + a.shape[1:], lambda i: (l,) + (0,) * (a.ndim - 1),
                                        pipeline_mode=pl.Buffered(1))
    rows = lambda a: pl.BlockSpec((ROW_TILE, a.shape[1]), lambda i: (i, 0))
    in_specs += [whole(a) if s is None else rows(a) if s == "rows" else layer_w(a, s)
                 for a, s in zip(operands, specs)]
    gu_shape = (n_up, ROW_TILE + 2 * HALO, FF_TILE)
    tile = ROW_TILE * D_MODEL
    vmem = (2 * 2 * tile * 4 + tile * 2 + tile * 4 + 2 * math.prod(gu_shape) * 4 + VMEM_HEADROOM
            + sum(_nbytes(a) if s is None else 2 * ROW_TILE * a.shape[1] * a.dtype.itemsize if s == "rows"
                  else _nbytes(a) // a.shape[0] for a, s in zip(operands, specs)))
    return pl.pallas_call(
        functools.partial(kernel_fn, n_chunks=n_chunks, **cfg),
        grid=(m // ROW_TILE,),
        in_specs=in_specs,
        out_specs=pl.BlockSpec((ROW_TILE, D_MODEL), lambda i: (i, 0)),
        out_shape=jax.ShapeDtypeStruct((m, D_MODEL), F32),
        scratch_shapes=[
            pltpu.VMEM((ROW_TILE, D_MODEL), BF16),
            pltpu.VMEM((ROW_TILE, D_MODEL), F32),
            pltpu.VMEM(gu_shape, F32), pltpu.VMEM(gu_shape, F32),
        ],
        compiler_params=_params(("arbitrary",), vmem),
        name=name,
    )(x, mods, norm_w, *operands)


def _chunk_major_kernel(w_ref, o_ref, *, n_up, n_chunks):
    for c in range(n_chunks):
        for p in range(n_up):
            col = (p * n_chunks + c) * FF_TILE
            o_ref[c, :, p * FF_TILE:(p + 1) * FF_TILE] = w_ref[:, col:col + FF_TILE].astype(BF16)


def _chunk_major(w, n_up, name):
    l, d, cols = w.shape
    n_chunks = cols // (n_up * FF_TILE)
    return pl.pallas_call(
        functools.partial(_chunk_major_kernel, n_up=n_up, n_chunks=n_chunks),
        grid=(l, d // RELAYOUT_ROWS),
        in_specs=[pl.BlockSpec((None, RELAYOUT_ROWS, cols), lambda li, k: (li, k, 0))],
        out_specs=pl.BlockSpec((None, n_chunks, RELAYOUT_ROWS, n_up * FF_TILE), lambda li, k: (li, 0, k, 0)),
        out_shape=jax.ShapeDtypeStruct((l, n_chunks, d, n_up * FF_TILE), BF16),
        compiler_params=_params(("arbitrary", "arbitrary"), 2 * RELAYOUT_ROWS * cols * (4 + 2) + VMEM_HEADROOM),
        name=name,
    )(w)


def _down_chunks(w):
    l, rows, d = w.shape
    return w.reshape(l, rows // FF_TILE, FF_TILE, d).astype(BF16)


def _ffn_call(x, mods, li, cond_of_row, seq, P, pre, final_norm):
    n = D_FF // FF_TILE
    operands = [P["ffn_up_cm"], _chunked(P["ffn_conv_w"][li], n), _chunked(P["ffn_conv_b"][li][None, :], n),
                P["ffn_down_cm"], P["final_norm_w"].reshape(1, D_MODEL)]
    specs = [li, None, None, li, None]
    kernel_fn = _ffn_kernel
    if pre is not None:
        a, w_a, l_a = pre
        operands += [a, w_a]
        specs += ["rows", l_a]
        kernel_fn = _ffn_pre_kernel
    return _glu_call(kernel_fn, f"ffn{li}", x, mods, li, cond_of_row, 2, n, P["norm2_w"],
                     operands, specs, dict(seq=seq, final_norm=final_norm))


def _sconv_call(x, mods, li, cond_of_row, seq, P):
    lj = li // 4
    n = D_MODEL // FF_TILE
    operands = [P["sconv_win_cm"], _chunked(P["sconv_conv"][lj], n), P["sconv_wout_cm"]]
    return _glu_call(_sconv_kernel, f"sconv{li}", x, mods, li, cond_of_row, 3, n, P["norm1_w"],
                     operands, [lj, None, lj], dict(seq=seq))


def _emit_kv(ckv, krope_pad, wk_ref, wv_ref, k_ref, v_ref):
    c = ckv.astype(BF16)
    kx = _dot(c, wk_ref[...].astype(BF16))
    vx = _dot(c, wv_ref[...].astype(BF16))
    for h in range(N_HEADS):
        k_ref[h] = (kx[:, h * LANES:(h + 1) * LANES] + krope_pad).astype(BF16)
    for p in range(HEAD_PAIRS):
        v_ref[p] = vx[:, p * LANES:(p + 1) * LANES].astype(BF16)


def _mla_proj_kernel(x_ref, mod_ref, nw_ref, w1_ref, qn_ref, kvn_ref, w2_ref, wk_ref, wv_ref, cos_ref,
                     sin_ref, q_ref, k_ref, v_ref, ckv_ref, kr_ref, *, rope):
    h = _norm_mod(x_ref[...], nw_ref[...], mod_ref[...], 0).astype(BF16)
    z = _dot(h, w1_ref[...].astype(BF16))
    cq = _rms(z[:, :Q_LORA], qn_ref[...]).astype(BF16)
    ckv = _rms(z[:, Q_LORA:Q_LORA + KV_LORA], kvn_ref[...])
    kr = z[:, Q_LORA + KV_LORA:Q_LORA + KV_LORA + LANES]
    if rope:
        cos, sin = cos_ref[...], sin_ref[...]
        kr = kr * cos + z[:, Q_LORA + KV_LORA + LANES:] * sin
    for p in range(HEAD_PAIRS):
        q2 = _dot(cq, w2_ref[:, 2 * p * LANES:(2 * p + 2) * LANES].astype(BF16))
        if rope:
            off = N_HEADS * LANES
            qs = _dot(cq, w2_ref[:, off + 2 * p * LANES:off + (2 * p + 2) * LANES].astype(BF16))
        for e in range(2):
            qh = q2[:, e * LANES:(e + 1) * LANES]
            if rope:
                qh = qh * cos + qs[:, e * LANES:(e + 1) * LANES] * sin
            q_ref[2 * p + e] = (qh * Q_SCALE).astype(BF16)
    ckv_ref[...] = ckv
    kr_ref[...] = kr[:, QK_NOPE:QK_NOPE + QK_ROPE]
    _emit_kv(ckv, kr, wk_ref, wv_ref, k_ref, v_ref)


def _mla_proj_call(x, mods, li, cond_of_row, W, rope):
    m = x.shape[0]
    n1 = W["w1"].shape[1]
    n2 = W["w2"].shape[1]
    const = lambda i: (0, 0)
    pos_tiles = W["cos"].shape[0] // PROJ_TILE
    pos = lambda i: (i % pos_tiles, 0)
    return pl.pallas_call(
        functools.partial(_mla_proj_kernel, rope=rope),
        grid=(m // PROJ_TILE,),
        in_specs=[
            pl.BlockSpec((PROJ_TILE, D_MODEL), lambda i: (i, 0)),
            _mod_spec(li, cond_of_row, PROJ_TILE, 1),
            pl.BlockSpec((None, 1, D_MODEL), lambda i: (li, 0, 0)),
            pl.BlockSpec((D_MODEL, n1), const),
            pl.BlockSpec((1, Q_LORA), const),
            pl.BlockSpec((1, KV_LORA), const),
            pl.BlockSpec((Q_LORA, n2), const),
            pl.BlockSpec((KV_LORA, N_HEADS * LANES), const),
            pl.BlockSpec((KV_LORA, HEAD_PAIRS * LANES), const),
            pl.BlockSpec((PROJ_TILE, LANES), pos),
            pl.BlockSpec((PROJ_TILE, LANES), pos),
        ],
        out_specs=[
            pl.BlockSpec((N_HEADS, PROJ_TILE, LANES), lambda i: (0, i, 0)),
            pl.BlockSpec((N_HEADS, PROJ_TILE, LANES), lambda i: (0, i, 0)),
            pl.BlockSpec((HEAD_PAIRS, PROJ_TILE, LANES), lambda i: (0, i, 0)),
            pl.BlockSpec((PROJ_TILE, KV_LORA), lambda i: (i, 0)),
            pl.BlockSpec((PROJ_TILE, QK_ROPE), lambda i: (i, 0)),
        ],
        out_shape=[
            jax.ShapeDtypeStruct((N_HEADS, m, LANES), BF16),
            jax.ShapeDtypeStruct((N_HEADS, m, LANES), BF16),
            jax.ShapeDtypeStruct((HEAD_PAIRS, m, LANES), BF16),
            jax.ShapeDtypeStruct((m, KV_LORA), F32),
            jax.ShapeDtypeStruct((m, QK_ROPE), F32),
        ],
        compiler_params=_params(("arbitrary",)),
        name="mla_proj_rope" if rope else "mla_proj",
    )(x, mods, W["norm1_w"], W["w1"], W["q_norm"], W["kv_norm"], W["w2"], W["wk"], W["wv"],
      W["cos"], W["sin"])


def _kv_cache_kernel(ckv_ref, kr_ref, wk_ref, wv_ref, k_ref, v_ref):
    _emit_kv(ckv_ref[...], kr_ref[...], wk_ref, wv_ref, k_ref, v_ref)


def _kv_cache_call(ckv, krope_pad, W):
    m = ckv.shape[0]
    const = lambda i: (0, 0)
    return pl.pallas_call(
        _kv_cache_kernel,
        grid=(1,),
        in_specs=[
            pl.BlockSpec((m, KV_LORA), const),
            pl.BlockSpec((m, LANES), const),
            pl.BlockSpec((KV_LORA, N_HEADS * LANES), const),
            pl.BlockSpec((KV_LORA, HEAD_PAIRS * LANES), const),
        ],
        out_specs=[
            pl.BlockSpec((N_HEADS, m, LANES), lambda i: (0, 0, 0)),
            pl.BlockSpec((HEAD_PAIRS, m, LANES), lambda i: (0, 0, 0)),
        ],
        out_shape=[
            jax.ShapeDtypeStruct((N_HEADS, m, LANES), BF16),
            jax.ShapeDtypeStruct((HEAD_PAIRS, m, LANES), BF16),
        ],
        compiler_params=_params(("arbitrary",)),
        name="kv_cache",
    )(ckv, krope_pad, W["wk"], W["wv"])


def _attn_kernel(q_ref, *refs, n_pairs):
    o_ref, kv = refs[-1], refs[:-1]
    parts = [(kv[i], kv[i + 1]) for i in range(0, len(kv), 2)]
    heads = [(p, e) for p in range(n_pairs) for e in range(2)]

    def scores(p, e):
        q = q_ref[2 * p + e]
        return [lax.dot_general(q, k_ref[2 * p + e], (((1,), (1,)), ((), ())), preferred_element_type=F32)
                for k_ref, _ in parts]

    outs = []
    s_next = scores(*heads[0])
    for i, (p, e) in enumerate(heads):
        s = s_next
        if i + 1 < len(heads):
            s_next = scores(*heads[i + 1])
        m = functools.reduce(jnp.maximum, [jnp.max(x, axis=-1, keepdims=True) for x in s])
        w = [jnp.exp2(x - m) for x in s]
        l = sum(jnp.sum(x, axis=-1, keepdims=True) for x in w)
        pv = sum(_dot(x.astype(BF16), v_ref[p]) for x, (_, v_ref) in zip(w, parts))
        outs.append(pv / l)
        if e == 1:
            lane = lax.broadcasted_iota(jnp.int32, pv.shape, 1)
            o_ref[:, p * LANES:(p + 1) * LANES] = jnp.where(lane < V_HEAD, outs[-2], outs[-1]).astype(BF16)


def _attn_call(q, kv_parts, q_tile, n_pairs):
    _, b, t, _ = q.shape
    nq = t // q_tile
    in_specs = [pl.BlockSpec((2 * n_pairs, None, q_tile, LANES), lambda bi, g, qi: (g, bi, qi, 0))]
    operands = [q]
    for k, v in kv_parts:
        s = k.shape[2]
        in_specs += [pl.BlockSpec((2 * n_pairs, None, s, LANES), lambda bi, g, qi: (g, bi, 0, 0)),
                     pl.BlockSpec((n_pairs, None, s, LANES), lambda bi, g, qi: (g, bi, 0, 0))]
        operands += [k, v]
    return pl.pallas_call(
        functools.partial(_attn_kernel, n_pairs=n_pairs),
        grid=(b, HEAD_PAIRS // n_pairs, nq),
        in_specs=in_specs,
        out_specs=pl.BlockSpec((q_tile, n_pairs * LANES), lambda bi, g, qi: (bi * nq + qi, g)),
        out_shape=jax.ShapeDtypeStruct((b * t, N_HEADS * V_HEAD), BF16),
        compiler_params=_params(("arbitrary", "arbitrary", "arbitrary"), ATTN_VMEM),
        name=f"attn_{len(kv_parts)}part",
    )(*operands)


def _pool_kernel(x_ref, mod_ref, nw_ref, w_ref, sc_ref, o_ref, band_scr, y_scr, *, seq):
    @pl.when(pl.program_id(0) == 0)
    def _():
        r = lax.broadcasted_iota(jnp.int32, (seq, seq), 0)
        c = lax.broadcasted_iota(jnp.int32, (seq, seq), 1)
        for g, win in enumerate(POOL_WINDOWS):
            inside = (c >= r - win // 2) & (c < r + win - win // 2)
            band_scr[g] = jnp.where(inside, 1.0, 0.0).astype(BF16)

    x = x_ref[...]
    h = _norm_mod(x, nw_ref[...], mod_ref[...], 0)
    h_hi = h.astype(BF16)
    h_lo = (h - h_hi.astype(F32)).astype(BF16)
    t = lax.broadcasted_iota(jnp.int32, (seq, 1), 0)
    for g, win in enumerate(POOL_WINDOWS):
        cnt = (jnp.minimum(t + (win - win // 2), seq) - jnp.maximum(t - win // 2, 0)).astype(F32)
        cols = slice(g * GROUP, (g + 1) * GROUP)
        wg = w_ref[g].astype(BF16)
        for s in range(x.shape[0] // seq):
            rows = slice(s * seq, (s + 1) * seq)
            tot = _dot(band_scr[g], h_hi[rows, cols]) + _dot(band_scr[g], h_lo[rows, cols])
            pooled = (tot / cnt - h[rows, cols]).astype(BF16)
            y_scr[rows, cols] = _dot(pooled, wg)
    o_ref[...] = x + mod_ref[2:3, :] * (y_scr[...] * sc_ref[...])


def _pool_call(x, mods, li, cond_of_row, seq, P):
    m = x.shape[0]
    lj = li // 4
    return pl.pallas_call(
        functools.partial(_pool_kernel, seq=seq),
        grid=(m // ROW_TILE,),
        in_specs=[
            pl.BlockSpec((ROW_TILE, D_MODEL), lambda i: (i, 0)),
            _mod_spec(li, cond_of_row, ROW_TILE, 1),
            pl.BlockSpec((None, 1, D_MODEL), lambda i: (li, 0, 0)),
            pl.BlockSpec((None, N_GROUPS, GROUP, GROUP), lambda i: (lj, 0, 0, 0)),
            pl.BlockSpec((None, 1, D_MODEL), lambda i: (lj, 0, 0)),
        ],
        out_specs=pl.BlockSpec((ROW_TILE, D_MODEL), lambda i: (i, 0)),
        out_shape=jax.ShapeDtypeStruct((m, D_MODEL), F32),
        scratch_shapes=[pltpu.VMEM((N_GROUPS, seq, seq), BF16), pltpu.VMEM((ROW_TILE, D_MODEL), F32)],
        compiler_params=_params(("arbitrary",)),
        name=f"pool_t{seq}",
    )(x, mods, P["norm1_w"], P["pool_w"], P["pool_scale"])


def _dft_tables(seq):
    def cs(n):
        k = np.arange(n, dtype=np.int64)
        ang = 2.0 * np.pi * ((k[:, None] * k[None, :]) % n) / n
        return np.cos(ang), np.sin(ang)
    ct, st = cs(seq)
    cc, sc = cs(GROUP)
    e_t = jnp.asarray(np.concatenate([ct, st], axis=0), dtype=F32)
    e_c = jnp.asarray(np.concatenate([cc, -sc], axis=0), dtype=F32)
    return e_t.astype(BF16), e_c.astype(BF16)


def _fourier_kernel(x_ref, mod_ref, nw_ref, et_ref, ec_ref, w_ref, o_ref, f_scr, *, seq):
    x = x_ref[...]
    h = _norm_mod(x, nw_ref[...], mod_ref[...], 0).astype(BF16)
    norm = np.float32(1.0 / math.sqrt(seq * GROUP))
    for s in range(x.shape[0] // seq):
        rows = slice(s * seq, (s + 1) * seq)
        for g in range(N_GROUPS):
            cols = slice(g * GROUP, (g + 1) * GROUP)
            y = _dot(et_ref[...], h[rows, cols]).astype(BF16)
            f = _dot(y[:seq], ec_ref[:GROUP, :]) + _dot(y[seq:], ec_ref[GROUP:, :])
            f_scr[rows, cols] = (f * norm).astype(BF16)
    o_ref[...] = x + mod_ref[2:3, :] * _dot(f_scr[...], w_ref[...].astype(BF16))


def _fourier_call(x, mods, li, cond_of_row, seq, P):
    m = x.shape[0]
    lj = li // 4
    e_t, e_c = _dft_tables(seq)
    return pl.pallas_call(
        functools.partial(_fourier_kernel, seq=seq),
        grid=(m // ROW_TILE,),
        in_specs=[
            pl.BlockSpec((ROW_TILE, D_MODEL), lambda i: (i, 0)),
            _mod_spec(li, cond_of_row, ROW_TILE, 1),
            pl.BlockSpec((None, 1, D_MODEL), lambda i: (li, 0, 0)),
            pl.BlockSpec((2 * seq, seq), lambda i: (0, 0)),
            pl.BlockSpec((2 * GROUP, GROUP), lambda i: (0, 0)),
            pl.BlockSpec((None, D_MODEL, D_MODEL), lambda i: (lj, 0, 0)),
        ],
        out_specs=pl.BlockSpec((ROW_TILE, D_MODEL), lambda i: (i, 0)),
        out_shape=jax.ShapeDtypeStruct((m, D_MODEL), F32),
        scratch_shapes=[pltpu.VMEM((ROW_TILE, D_MODEL), BF16)],
        compiler_params=_params(("arbitrary",)),
        name=f"fourier_t{seq}",
    )(x, mods, P["norm1_w"], e_t, e_c, P["fnet_w"])


def _rope_tables(rows):
    r, col = jnp.meshgrid(jnp.arange(rows), jnp.arange(GRID_W), indexing="ij")
    r = r.reshape(-1).astype(F32)
    col = col.reshape(-1).astype(F32)
    n_freq = QK_ROPE // 4
    inv = ROPE_THETA ** (-jnp.arange(n_freq, dtype=F32) / n_freq)
    ang = jnp.concatenate([r[:, None] * inv, col[:, None] * inv], axis=-1)
    cos = jnp.repeat(jnp.cos(ang), 2, axis=-1)
    sin = jnp.repeat(jnp.sin(ang), 2, axis=-1)
    t = cos.shape[0]
    pad = LANES - QK_NOPE - QK_ROPE
    cos_p = jnp.concatenate([jnp.ones((t, QK_NOPE), F32), cos, jnp.ones((t, pad), F32)], axis=-1)
    sin_p = jnp.concatenate([jnp.zeros((t, QK_NOPE), F32), sin, jnp.zeros((t, pad), F32)], axis=-1)
    return cos_p, sin_p


def _pair_swap(w):
    wr = w.reshape(w.shape[:-1] + (w.shape[-1] // 2, 2))
    return jnp.stack([-wr[..., 1], wr[..., 0]], axis=-1).reshape(w.shape)


def _mla_weights(P, j, rope, seq_rows):
    pad = LANES - QK_NOPE - QK_ROPE
    wuq = P["mla_wuq"][j].reshape(Q_LORA, N_HEADS, QK_NOPE + QK_ROPE)
    q_slots = jnp.pad(wuq, ((0, 0), (0, 0), (0, pad))).reshape(Q_LORA, N_HEADS * LANES)
    wdkv = P["mla_wdkv"][j]
    kr_slot = jnp.pad(wdkv[:, KV_LORA:], ((0, 0), (QK_NOPE, pad)))
    parts1 = [P["mla_wdq"][j], wdkv[:, :KV_LORA], kr_slot]
    parts2 = [q_slots]
    if rope:
        parts1.append(jnp.pad(_pair_swap(wdkv[:, KV_LORA:]), ((0, 0), (QK_NOPE, pad))))
        q_sw = jnp.pad(_pair_swap(wuq[:, :, QK_NOPE:]), ((0, 0), (0, 0), (QK_NOPE, pad)))
        parts2.append(q_sw.reshape(Q_LORA, N_HEADS * LANES))
        cos, sin = _rope_tables(seq_rows // GRID_W)
    else:
        cos = sin = jnp.zeros((PROJ_TILE, LANES), F32)
    wukv = P["mla_wukv"][j].reshape(KV_LORA, N_HEADS, QK_NOPE + V_HEAD)
    wk = jnp.pad(wukv[:, :, :QK_NOPE], ((0, 0), (0, 0), (0, LANES - QK_NOPE)))
    return dict(
        norm1_w=P["norm1_w"], q_norm=P["mla_q_norm"][j][None, :], kv_norm=P["mla_kv_norm"][j][None, :],
        w1=jnp.concatenate(parts1, axis=1), w2=jnp.concatenate(parts2, axis=1),
        wk=wk.reshape(KV_LORA, N_HEADS * LANES),
        wv=wukv[:, :, QK_NOPE:].reshape(KV_LORA, N_HEADS * V_HEAD), cos=cos, sin=sin)


def _mla_layer(x, mods, li, cond_of_row, batch, seq, P, cache):
    j = li // 4
    W = _mla_weights(P, j, cache is not None, seq)
    q, k, v, ckv, krope = _mla_proj_call(x, mods, li, cond_of_row, W, cache is not None)
    q = q.reshape(N_HEADS, batch, seq, LANES)
    k = k.reshape(N_HEADS, batch, seq, LANES)
    v = v.reshape(HEAD_PAIRS, batch, seq, LANES)
    if cache is None:
        o = _attn_call(q, [(k, v)], seq, HEAD_PAIRS)
    else:
        c_ckv, c_krope = cache
        past = c_ckv.shape[1]
        pad = LANES - QK_NOPE - QK_ROPE
        kr_pad = jnp.pad(c_krope.reshape(batch * past, QK_ROPE), ((0, 0), (QK_NOPE, pad)))
        kc, vc = _kv_cache_call(c_ckv.reshape(batch * past, KV_LORA), kr_pad, W)
        cached = (kc.reshape(N_HEADS, batch, past, LANES), vc.reshape(HEAD_PAIRS, batch, past, LANES))
        o = _attn_call(q, [cached, (k, v)], 512, 2)
    return o, ckv, krope


def _run_pass(x, mods, cond_of_row, batch, seq, P, cache):
    ckv = krope = None
    for li in range(DEPTH):
        kind = li % 4
        pre = None
        if kind == 0:
            o, ckv, krope = _mla_layer(x, mods, li, cond_of_row, batch, seq, P, cache)
            pre = (o, P["mla_wo_bf16"], li // 4)
        elif kind == 1:
            x = _pool_call(x, mods, li, cond_of_row, seq, P)
        elif kind == 2:
            x = _fourier_call(x, mods, li, cond_of_row, seq, P)
        else:
            x = _sconv_call(x, mods, li, cond_of_row, seq, P)
        x = _ffn_call(x, mods, li, cond_of_row, seq, P, pre, final_norm=(li == DEPTH - 1))
    return x, ckv, krope


def kernel(x_prompt, x_sample, cache_ckv, cache_krope, c, c_ctx, norm1_w, norm2_w, ada_w, ada_b, mla_wdq, mla_q_norm, mla_wuq, mla_wdkv, mla_kv_norm, mla_wukv, mla_wo, pool_w, pool_scale, fnet_w, sconv_win, sconv_conv, sconv_wout, ffn_up, ffn_conv_w, ffn_conv_b, ffn_down, final_norm_w):
    row = lambda w: w.reshape(w.shape[0], 1, w.shape[1])
    P = dict(norm1_w=row(norm1_w), norm2_w=row(norm2_w), mla_wdq=mla_wdq, mla_q_norm=mla_q_norm,
             mla_wuq=mla_wuq, mla_wdkv=mla_wdkv, mla_kv_norm=mla_kv_norm, mla_wukv=mla_wukv,
             mla_wo_bf16=mla_wo.astype(BF16),
             pool_w=pool_w, pool_scale=row(pool_scale), fnet_w=fnet_w, sconv_win=sconv_win,
             sconv_conv=sconv_conv, sconv_wout=sconv_wout, ffn_conv_w=ffn_conv_w,
             ffn_conv_b=ffn_conv_b, ffn_down=ffn_down, final_norm_w=final_norm_w,
             ffn_up_cm=_chunk_major(ffn_up, 2, "ffn_up_layout"),
             sconv_win_cm=_chunk_major(sconv_win, 3, "sconv_win_layout"),
             ffn_down_cm=_down_chunks(ffn_down), sconv_wout_cm=_down_chunks(sconv_wout))
    b_ctx, t_ctx, _ = x_prompt.shape
    b_lat, t_lat, _ = x_sample.shape
    assert ROW_TILE % t_ctx == 0 and t_lat % ROW_TILE == 0 and b_lat + 1 <= N_COND

    conds = jnp.concatenate([c_ctx[None, :], c, jnp.zeros((N_COND - 1 - b_lat, D_MODEL), F32)], axis=0)
    mods = _ada_call(conds, ada_w, ada_b).reshape(DEPTH, N_COND, 6, D_MODEL)

    xp, ckv, krope = _run_pass(x_prompt.reshape(b_ctx * t_ctx, D_MODEL), mods, lambda r: 0,
                               b_ctx, t_ctx, P, None)
    xs, _, _ = _run_pass(x_sample.reshape(b_lat * t_lat, D_MODEL), mods,
                         lambda r: 1 + r // t_lat, b_lat, t_lat, P,
                         (cache_ckv[:, 0], cache_krope[:, 0]))
    return (xp.reshape(b_ctx, t_ctx, D_MODEL), xs.reshape(b_lat, t_lat, D_MODEL),
            ckv.reshape(b_ctx, 1, t_ctx, KV_LORA), krope.reshape(b_ctx, 1, t_ctx, QK_ROPE))
```

```python
import functools
import math

import numpy as np
import jax
import jax.numpy as jnp
from jax import lax
from jax.experimental import pallas as pl
from jax.experimental.pallas import tpu as pltpu

F32 = jnp.float32
BF16 = jnp.bfloat16

D_MODEL = 1024
N_HEADS = 16
QK_NOPE = 64
QK_ROPE = 32
V_HEAD = 64
Q_LORA = 384
KV_LORA = 256
ROPE_THETA = 10000.0
GRID_W = 64
POOL_WINDOWS = (2, 4, 8, 16)
N_GROUPS = 4
GROUP = D_MODEL // N_GROUPS
D_FF = 2816
EPS = 1e-6
DEPTH = 4

LANES = 128
Q_SCALE = np.float32(math.log2(math.e) / math.sqrt(QK_NOPE + QK_ROPE))
HEAD_PAIRS = N_HEADS // 2
N_COND = 8
ROW_TILE = 1024
PROJ_TILE = 512
FF_TILE = 256
ROW_BLOCK = 256
HALO = 8
RELAYOUT_ROWS = 128
ADA_TILE = 1536
MIB = 1024 * 1024
VMEM_DEFAULT = 48 * MIB
VMEM_HEADROOM = 4 * MIB
ATTN_VMEM = 32 * MIB


def _params(sem, vmem_bytes=VMEM_DEFAULT):
    return pltpu.CompilerParams(dimension_semantics=sem, vmem_limit_bytes=int(vmem_bytes))


def _nbytes(a):
    return math.prod(a.shape) * jnp.dtype(a.dtype).itemsize


def _dot(a, b):
    return jnp.dot(a, b, preferred_element_type=F32)


def _rms(x, w):
    return x * lax.rsqrt(jnp.mean(x * x, axis=-1, keepdims=True) + EPS) * w


def _norm_mod(x, nw, mod, base):
    return _rms(x, nw) * (1.0 + mod[base + 1:base + 2, :]) + mod[base:base + 1, :]


def _gelu_exact(x):
    return 0.5 * x * (1.0 + lax.erf(x * np.float32(math.sqrt(0.5))))


def _ada_kernel(c_ref, w_ref, b_ref, o_ref):
    s = jax.nn.silu(c_ref[...]).astype(BF16)
    o_ref[...] = _dot(s, w_ref[...].astype(BF16)) + b_ref[...]


def _ada_call(conds, ada_w, ada_b):
    n_out = ada_w.shape[-1]
    return pl.pallas_call(
        _ada_kernel,
        grid=(DEPTH, n_out // ADA_TILE),
        in_specs=[
            pl.BlockSpec((N_COND, D_MODEL), lambda l, j: (0, 0)),
            pl.BlockSpec((None, D_MODEL, ADA_TILE), lambda l, j: (l, 0, j)),
            pl.BlockSpec((None, 1, ADA_TILE), lambda l, j: (l, 0, j)),
        ],
        out_specs=pl.BlockSpec((None, N_COND, ADA_TILE), lambda l, j: (l, 0, j)),
        out_shape=jax.ShapeDtypeStruct((DEPTH, N_COND, n_out), F32),
        compiler_params=_params(("arbitrary", "arbitrary"), 2 * D_MODEL * ADA_TILE * 4 + VMEM_HEADROOM),
        name="ada",
    )(conds, ada_w, ada_b.reshape(DEPTH, 1, n_out))


def _mod_spec(li, cond_of_row, tile, n_grid):
    if n_grid == 1:
        return pl.BlockSpec((None, None, 6, D_MODEL), lambda i: (li, cond_of_row(i * tile), 0, 0))
    return pl.BlockSpec((None, None, 6, D_MODEL), lambda i, j: (li, cond_of_row(i * tile), 0, 0))


def _conv_window(w, cw, t0, seq):
    n = w.shape[0]
    mid = slice(HALO, HALO + ROW_BLOCK)
    t = (t0 + lax.broadcasted_iota(jnp.int32, (ROW_BLOCK, 1), 0)) & (seq - 1)
    prev = jnp.where(t == 0, 0.0, pltpu.roll(w, 1, axis=0)[mid])
    nxt = jnp.where(t == seq - 1, 0.0, pltpu.roll(w, n - 1, axis=0)[mid])
    return prev * cw[0:1, :] + w[mid] * cw[1:2, :] + nxt * cw[2:3, :]


def _ffn_act(win, mid, cw, cb, t0, seq):
    return _gelu_exact(_conv_window(win(0), cw, t0, seq) + cb) * mid(1)


def _sconv_act(win, mid, cw, cb, t0, seq):
    return mid(0) * _conv_window(win(1) * win(2), cw, t0, seq)


def _ffn_kernel(x_ref, mod_ref, nw_ref, up_ref, cw_ref, cb_ref, down_ref, fnw_ref, o_ref, *scratch, **cfg):
    _glu_body(x_ref, mod_ref, nw_ref, up_ref, cw_ref, cb_ref, down_ref, fnw_ref, None, o_ref, *scratch,
              n_up=2, act=_ffn_act, mod_base=3, **cfg)


def _ffn_pre_kernel(x_ref, mod_ref, nw_ref, up_ref, cw_ref, cb_ref, down_ref, fnw_ref, a_ref, wa_ref, o_ref,
                    *scratch, **cfg):
    _glu_body(x_ref, mod_ref, nw_ref, up_ref, cw_ref, cb_ref, down_ref, fnw_ref, (a_ref, wa_ref), o_ref,
              *scratch, n_up=2, act=_ffn_act, mod_base=3, **cfg)


def _sconv_kernel(x_ref, mod_ref, nw_ref, up_ref, cw_ref, down_ref, o_ref, *scratch, **cfg):
    _glu_body(x_ref, mod_ref, nw_ref, up_ref, cw_ref, None, down_ref, None, None, o_ref, *scratch,
              n_up=3, act=_sconv_act, mod_base=0, final_norm=False, **cfg)


def _glu_body(x_ref, mod_ref, nw_ref, up_ref, cw_ref, cb_ref, down_ref, fnw_ref, pre, o_ref,
              h_scr, acc_scr, gu0, gu1, *, n_up, n_chunks, act, seq, mod_base, final_norm):
    gu = (gu0, gu1)
    n_blocks = ROW_TILE // ROW_BLOCK

    def up(c, slot, r):
        rows = slice(r * ROW_BLOCK, (r + 1) * ROW_BLOCK)
        res = _dot(h_scr[rows, :], up_ref[c])
        for p in range(n_up):
            gu[slot][p, HALO + r * ROW_BLOCK:HALO + (r + 1) * ROW_BLOCK, :] = res[:, p * FF_TILE:(p + 1) * FF_TILE]

    def activate(c, slot, r):
        win = lambda p: gu[slot][p, r * ROW_BLOCK:(r + 1) * ROW_BLOCK + 2 * HALO, :]
        mid = lambda p: gu[slot][p, HALO + r * ROW_BLOCK:HALO + (r + 1) * ROW_BLOCK, :]
        cb = None if cb_ref is None else cb_ref[c]
        return act(win, mid, cw_ref[c], cb, r * ROW_BLOCK, seq).astype(BF16)

    def down(a, c, r):
        rows = slice(r * ROW_BLOCK, (r + 1) * ROW_BLOCK)
        acc_scr[rows, :] += _dot(a, down_ref[c])

    def stage(s, slot):
        for r in range(n_blocks):
            a = activate(s - 1, 1 - slot, r)
            up(s, slot, r)
            down(a, s - 1, r)

    def block(r):
        return slice(r * ROW_BLOCK, (r + 1) * ROW_BLOCK)

    def mixer_proj(r):
        a_ref, wa_ref = pre
        return _dot(a_ref[block(r), :], wa_ref[...])

    for g in gu:
        g[:, :HALO, :] = jnp.zeros((n_up, HALO, FF_TILE), F32)
        g[:, HALO + ROW_TILE:, :] = jnp.zeros((n_up, HALO, FF_TILE), F32)

    proj = mixer_proj(0) if pre is not None else None
    for r in range(n_blocks):
        x = x_ref[block(r), :]
        if pre is not None:
            x = x + mod_ref[2:3, :] * proj
            o_ref[block(r), :] = x
            if r + 1 < n_blocks:
                proj = mixer_proj(r + 1)
        h_scr[block(r), :] = _norm_mod(x, nw_ref[...], mod_ref[...], mod_base).astype(BF16)
        acc_scr[block(r), :] = jnp.zeros((ROW_BLOCK, D_MODEL), F32)
        up(0, 0, r)

    n_pairs = (n_chunks - 1) // 2

    def pair(i, carry):
        s = 2 * i + 1
        stage(s, 1)
        stage(s + 1, 0)
        return carry

    lax.fori_loop(0, n_pairs, pair, 0)
    for s in range(2 * n_pairs + 1, n_chunks):
        stage(s, s % 2)
    last = n_chunks - 1
    res = x_ref if pre is None else o_ref
    for r in range(n_blocks):
        down(activate(last, last % 2, r), last, r)
        y = res[block(r), :] + mod_ref[mod_base + 2:mod_base + 3, :] * acc_scr[block(r), :]
        if final_norm:
            y = _rms(y, fnw_ref[...])
        o_ref[block(r), :] = y


def _chunked(w, n_chunks):
    return w.reshape(w.shape[0], n_chunks, FF_TILE).transpose(1, 0, 2)


def _glu_call(kernel_fn, name, x, mods, li, cond_of_row, n_up, n_chunks, norm_w, operands, specs, cfg):
    m = x.shape[0]
    whole = lambda a: pl.BlockSpec(a.shape, lambda i: (0,) * a.ndim)
    in_specs = [pl.BlockSpec((ROW_TILE, D_MODEL), lambda i: (i, 0)),
                _mod_spec(li, cond_of_row, ROW_TILE, 1),
                pl.BlockSpec((None, 1, D_MODEL), lambda i: (li, 0, 0))]
    layer_w = lambda a, l: pl.BlockSpec((None,) + a.shape[1:], lambda i: (l,) + (0,) * (a.ndim - 1),
                                        pipeline_mode=pl.Buffered(1))
    rows = lambda a: pl.BlockSpec((ROW_TILE, a.shape[1]), lambda i: (i, 0))
    in_specs += [whole(a) if s is None else rows(a) if s == "rows" else layer_w(a, s)
                 for a, s in zip(operands, specs)]
    gu_shape = (n_up, ROW_TILE + 2 * HALO, FF_TILE)
    tile = ROW_TILE * D_MODEL
    vmem = (2 * 2 * tile * 4 + tile * 2 + tile * 4 + 2 * math.prod(gu_shape) * 4 + VMEM_HEADROOM
            + sum(_nbytes(a) if s is None else 2 * ROW_TILE * a.shape[1] * a.dtype.itemsize if s == "rows"
                  else _nbytes(a) // a.shape[0] for a, s in zip(operands, specs)))
    return pl.pallas_call(
        functools.partial(kernel_fn, n_chunks=n_chunks, **cfg),
        grid=(m // ROW_TILE,),
        in_specs=in_specs,
        out_specs=pl.BlockSpec((ROW_TILE, D_MODEL), lambda i: (i, 0)),
        out_shape=jax.ShapeDtypeStruct((m, D_MODEL), F32),
        scratch_shapes=[
            pltpu.VMEM((ROW_TILE, D_MODEL), BF16),
            pltpu.VMEM((ROW_TILE, D_MODEL), F32),
            pltpu.VMEM(gu_shape, F32), pltpu.VMEM(gu_shape, F32),
        ],
        compiler_params=_params(("arbitrary",), vmem),
        name=name,
    )(x, mods, norm_w, *operands)


def _chunk_major_kernel(w_ref, o_ref, *, n_up, n_chunks):
    for c in range(n_chunks):
        for p in range(n_up):
            col = (p * n_chunks + c) * FF_TILE
            o_ref[c, :, p * FF_TILE:(p + 1) * FF_TILE] = w_ref[:, col:col + FF_TILE].astype(BF16)


def _chunk_major(w, n_up, name):
    l, d, cols = w.shape
    n_chunks = cols // (n_up * FF_TILE)
    return pl.pallas_call(
        functools.partial(_chunk_major_kernel, n_up=n_up, n_chunks=n_chunks),
        grid=(l, d // RELAYOUT_ROWS),
        in_specs=[pl.BlockSpec((None, RELAYOUT_ROWS, cols), lambda li, k: (li, k, 0))],
        out_specs=pl.BlockSpec((None, n_chunks, RELAYOUT_ROWS, n_up * FF_TILE), lambda li, k: (li, 0, k, 0)),
        out_shape=jax.ShapeDtypeStruct((l, n_chunks, d, n_up * FF_TILE), BF16),
        compiler_params=_params(("arbitrary", "arbitrary"), 2 * RELAYOUT_ROWS * cols * (4 + 2) + VMEM_HEADROOM),
        name=name,
    )(w)


def _down_chunks(w):
    l, rows, d = w.shape
    return w.reshape(l, rows // FF_TILE, FF_TILE, d).astype(BF16)


def _ffn_call(x, mods, li, cond_of_row, seq, P, pre, final_norm):
    n = D_FF // FF_TILE
    operands = [P["ffn_up_cm"], _chunked(P["ffn_conv_w"][li], n), _chunked(P["ffn_conv_b"][li][None, :], n),
                P["ffn_down_cm"], P["final_norm_w"].reshape(1, D_MODEL)]
    specs = [li, None, None, li, None]
    kernel_fn = _ffn_kernel
    if pre is not None:
        a, w_a, l_a = pre
        operands += [a, w_a]
        specs += ["rows", l_a]
        kernel_fn = _ffn_pre_kernel
    return _glu_call(kernel_fn, f"ffn{li}", x, mods, li, cond_of_row, 2, n, P["norm2_w"],
                     operands, specs, dict(seq=seq, final_norm=final_norm))


def _sconv_call(x, mods, li, cond_of_row, seq, P):
    lj = li // 4
    n = D_MODEL // FF_TILE
    operands = [P["sconv_win_cm"], _chunked(P["sconv_conv"][lj], n), P["sconv_wout_cm"]]
    return _glu_call(_sconv_kernel, f"sconv{li}", x, mods, li, cond_of_row, 3, n, P["norm1_w"],
                     operands, [lj, None, lj], dict(seq=seq))


def _emit_kv(ckv, krope_pad, wk_ref, wv_ref, k_ref, v_ref):
    c = ckv.astype(BF16)
    kx = _dot(c, wk_ref[...].astype(BF16))
    vx = _dot(c, wv_ref[...].astype(BF16))
    for h in range(N_HEADS):
        k_ref[h] = (kx[:, h * LANES:(h + 1) * LANES] + krope_pad).astype(BF16)
    for p in range(HEAD_PAIRS):
        v_ref[p] = vx[:, p * LANES:(p + 1) * LANES].astype(BF16)


def _mla_proj_kernel(x_ref, mod_ref, nw_ref, w1_ref, qn_ref, kvn_ref, w2_ref, wk_ref, wv_ref, cos_ref,
                     sin_ref, q_ref, k_ref, v_ref, ckv_ref, kr_ref, *, rope):
    h = _norm_mod(x_ref[...], nw_ref[...], mod_ref[...], 0).astype(BF16)
    z = _dot(h, w1_ref[...].astype(BF16))
    cq = _rms(z[:, :Q_LORA], qn_ref[...]).astype(BF16)
    ckv = _rms(z[:, Q_LORA:Q_LORA + KV_LORA], kvn_ref[...])
    kr = z[:, Q_LORA + KV_LORA:Q_LORA + KV_LORA + LANES]
    if rope:
        cos, sin = cos_ref[...], sin_ref[...]
        kr = kr * cos + z[:, Q_LORA + KV_LORA + LANES:] * sin
    for p in range(HEAD_PAIRS):
        q2 = _dot(cq, w2_ref[:, 2 * p * LANES:(2 * p + 2) * LANES].astype(BF16))
        if rope:
            off = N_HEADS * LANES
            qs = _dot(cq, w2_ref[:, off + 2 * p * LANES:off + (2 * p + 2) * LANES].astype(BF16))
        for e in range(2):
            qh = q2[:, e * LANES:(e + 1) * LANES]
            if rope:
                qh = qh * cos + qs[:, e * LANES:(e + 1) * LANES] * sin
            q_ref[2 * p + e] = (qh * Q_SCALE).astype(BF16)
    ckv_ref[...] = ckv
    kr_ref[...] = kr[:, QK_NOPE:QK_NOPE + QK_ROPE]
    _emit_kv(ckv, kr, wk_ref, wv_ref, k_ref, v_ref)


def _mla_proj_call(x, mods, li, cond_of_row, W, rope):
    m = x.shape[0]
    n1 = W["w1"].shape[1]
    n2 = W["w2"].shape[1]
    const = lambda i: (0, 0)
    pos_tiles = W["cos"].shape[0] // PROJ_TILE
    pos = lambda i: (i % pos_tiles, 0)
    return pl.pallas_call(
        functools.partial(_mla_proj_kernel, rope=rope),
        grid=(m // PROJ_TILE,),
        in_specs=[
            pl.BlockSpec((PROJ_TILE, D_MODEL), lambda i: (i, 0)),
            _mod_spec(li, cond_of_row, PROJ_TILE, 1),
            pl.BlockSpec((None, 1, D_MODEL), lambda i: (li, 0, 0)),
            pl.BlockSpec((D_MODEL, n1), const),
            pl.BlockSpec((1, Q_LORA), const),
            pl.BlockSpec((1, KV_LORA), const),
            pl.BlockSpec((Q_LORA, n2), const),
            pl.BlockSpec((KV_LORA, N_HEADS * LANES), const),
            pl.BlockSpec((KV_LORA, HEAD_PAIRS * LANES), const),
            pl.BlockSpec((PROJ_TILE, LANES), pos),
            pl.BlockSpec((PROJ_TILE, LANES), pos),
        ],
        out_specs=[
            pl.BlockSpec((N_HEADS, PROJ_TILE, LANES), lambda i: (0, i, 0)),
            pl.BlockSpec((N_HEADS, PROJ_TILE, LANES), lambda i: (0, i, 0)),
            pl.BlockSpec((HEAD_PAIRS, PROJ_TILE, LANES), lambda i: (0, i, 0)),
            pl.BlockSpec((PROJ_TILE, KV_LORA), lambda i: (i, 0)),
            pl.BlockSpec((PROJ_TILE, QK_ROPE), lambda i: (i, 0)),
        ],
        out_shape=[
            jax.ShapeDtypeStruct((N_HEADS, m, LANES), BF16),
            jax.ShapeDtypeStruct((N_HEADS, m, LANES), BF16),
            jax.ShapeDtypeStruct((HEAD_PAIRS, m, LANES), BF16),
            jax.ShapeDtypeStruct((m, KV_LORA), F32),
            jax.ShapeDtypeStruct((m, QK_ROPE), F32),
        ],
        compiler_params=_params(("arbitrary",)),
        name="mla_proj_rope" if rope else "mla_proj",
    )(x, mods, W["norm1_w"], W["w1"], W["q_norm"], W["kv_norm"], W["w2"], W["wk"], W["wv"],
      W["cos"], W["sin"])


def _kv_cache_kernel(ckv_ref, kr_ref, wk_ref, wv_ref, k_ref, v_ref):
    _emit_kv(ckv_ref[...], kr_ref[...], wk_ref, wv_ref, k_ref, v_ref)


def _kv_cache_call(ckv, krope_pad, W):
    m = ckv.shape[0]
    const = lambda i: (0, 0)
    return pl.pallas_call(
        _kv_cache_kernel,
        grid=(1,),
        in_specs=[
            pl.BlockSpec((m, KV_LORA), const),
            pl.BlockSpec((m, LANES), const),
            pl.BlockSpec((KV_LORA, N_HEADS * LANES), const),
            pl.BlockSpec((KV_LORA, HEAD_PAIRS * LANES), const),
        ],
        out_specs=[
            pl.BlockSpec((N_HEADS, m, LANES), lambda i: (0, 0, 0)),
            pl.BlockSpec((HEAD_PAIRS, m, LANES), lambda i: (0, 0, 0)),
        ],
        out_shape=[
            jax.ShapeDtypeStruct((N_HEADS, m, LANES), BF16),
            jax.ShapeDtypeStruct((HEAD_PAIRS, m, LANES), BF16),
        ],
        compiler_params=_params(("arbitrary",)),
        name="kv_cache",
    )(ckv, krope_pad, W["wk"], W["wv"])


def _attn_kernel(q_ref, *refs, n_pairs):
    o_ref, kv = refs[-1], refs[:-1]
    parts = [(kv[i], kv[i + 1]) for i in range(0, len(kv), 2)]
    heads = [(p, e) for p in range(n_pairs) for e in range(2)]

    def scores(p, e):
        q = q_ref[2 * p + e]
        return [lax.dot_general(q, k_ref[2 * p + e], (((1,), (1,)), ((), ())), preferred_element_type=F32)
                for k_ref, _ in parts]

    outs = []
    s_next = scores(*heads[0])
    for i, (p, e) in enumerate(heads):
        s = s_next
        if i + 1 < len(heads):
            s_next = scores(*heads[i + 1])
        m = functools.reduce(jnp.maximum, [jnp.max(x, axis=-1, keepdims=True) for x in s])
        w = [jnp.exp2(x - m) for x in s]
        l = sum(jnp.sum(x, axis=-1, keepdims=True) for x in w)
        pv = sum(_dot(x.astype(BF16), v_ref[p]) for x, (_, v_ref) in zip(w, parts))
        outs.append(pv / l)
        if e == 1:
            lane = lax.broadcasted_iota(jnp.int32, pv.shape, 1)
            o_ref[:, p * LANES:(p + 1) * LANES] = jnp.where(lane < V_HEAD, outs[-2], outs[-1]).astype(BF16)


def _attn_call(q, kv_parts, q_tile, n_pairs):
    _, b, t, _ = q.shape
    nq = t // q_tile
    in_specs = [pl.BlockSpec((2 * n_pairs, None, q_tile, LANES), lambda bi, g, qi: (g, bi, qi, 0))]
    operands = [q]
    for k, v in kv_parts:
        s = k.shape[2]
        in_specs += [pl.BlockSpec((2 * n_pairs, None, s, LANES), lambda bi, g, qi: (g, bi, 0, 0)),
                     pl.BlockSpec((n_pairs, None, s, LANES), lambda bi, g, qi: (g, bi, 0, 0))]
        operands += [k, v]
    return pl.pallas_call(
        functools.partial(_attn_kernel, n_pairs=n_pairs),
        grid=(b, HEAD_PAIRS // n_pairs, nq),
        in_specs=in_specs,
        out_specs=pl.BlockSpec((q_tile, n_pairs * LANES), lambda bi, g, qi: (bi * nq + qi, g)),
        out_shape=jax.ShapeDtypeStruct((b * t, N_HEADS * V_HEAD), BF16),
        compiler_params=_params(("arbitrary", "arbitrary", "arbitrary"), ATTN_VMEM),
        name=f"attn_{len(kv_parts)}part",
    )(*operands)


def _pool_kernel(x_ref, mod_ref, nw_ref, w_ref, sc_ref, o_ref, band_scr, y_scr, *, seq):
    @pl.when(pl.program_id(0) == 0)
    def _():
        r = lax.broadcasted_iota(jnp.int32, (seq, seq), 0)
        c = lax.broadcasted_iota(jnp.int32, (seq, seq), 1)
        for g, win in enumerate(POOL_WINDOWS):
            inside = (c >= r - win // 2) & (c < r + win - win // 2)
            band_scr[g] = jnp.where(inside, 1.0, 0.0).astype(BF16)

    x = x_ref[...]
    h = _norm_mod(x, nw_ref[...], mod_ref[...], 0)
    h_hi = h.astype(BF16)
    h_lo = (h - h_hi.astype(F32)).astype(BF16)
    t = lax.broadcasted_iota(jnp.int32, (seq, 1), 0)
    for g, win in enumerate(POOL_WINDOWS):
        cnt = (jnp.minimum(t + (win - win // 2), seq) - jnp.maximum(t - win // 2, 0)).astype(F32)
        cols = slice(g * GROUP, (g + 1) * GROUP)
        wg = w_ref[g].astype(BF16)
        for s in range(x.shape[0] // seq):
            rows = slice(s * seq, (s + 1) * seq)
            tot = _dot(band_scr[g], h_hi[rows, cols]) + _dot(band_scr[g], h_lo[rows, cols])
            pooled = (tot / cnt - h[rows, cols]).astype(BF16)
            y_scr[rows, cols] = _dot(pooled, wg)
    o_ref[...] = x + mod_ref[2:3, :] * (y_scr[...] * sc_ref[...])


def _pool_call(x, mods, li, cond_of_row, seq, P):
    m = x.shape[0]
    lj = li // 4
    return pl.pallas_call(
        functools.partial(_pool_kernel, seq=seq),
        grid=(m // ROW_TILE,),
        in_specs=[
            pl.BlockSpec((ROW_TILE, D_MODEL), lambda i: (i, 0)),
            _mod_spec(li, cond_of_row, ROW_TILE, 1),
            pl.BlockSpec((None, 1, D_MODEL), lambda i: (li, 0, 0)),
            pl.BlockSpec((None, N_GROUPS, GROUP, GROUP), lambda i: (lj, 0, 0, 0)),
            pl.BlockSpec((None, 1, D_MODEL), lambda i: (lj, 0, 0)),
        ],
        out_specs=pl.BlockSpec((ROW_TILE, D_MODEL), lambda i: (i, 0)),
        out_shape=jax.ShapeDtypeStruct((m, D_MODEL), F32),
        scratch_shapes=[pltpu.VMEM((N_GROUPS, seq, seq), BF16), pltpu.VMEM((ROW_TILE, D_MODEL), F32)],
        compiler_params=_params(("arbitrary",)),
        name=f"pool_t{seq}",
    )(x, mods, P["norm1_w"], P["pool_w"], P["pool_scale"])


def _dft_tables(seq):
    def cs(n):
        k = np.arange(n, dtype=np.int64)
        ang = 2.0 * np.pi * ((k[:, None] * k[None, :]) % n) / n
        return np.cos(ang), np.sin(ang)
    ct, st = cs(seq)
    cc, sc = cs(GROUP)
    e_t = jnp.asarray(np.concatenate([ct, st], axis=0), dtype=F32)
    e_c = jnp.asarray(np.concatenate([cc, -sc], axis=0), dtype=F32)
    return e_t.astype(BF16), e_c.astype(BF16)


def _fourier_kernel(x_ref, mod_ref, nw_ref, et_ref, ec_ref, w_ref, o_ref, f_scr, *, seq):
    x = x_ref[...]
    h = _norm_mod(x, nw_ref[...], mod_ref[...], 0).astype(BF16)
    norm = np.float32(1.0 / math.sqrt(seq * GROUP))
    for s in range(x.shape[0] // seq):
        rows = slice(s * seq, (s + 1) * seq)
        for g in range(N_GROUPS):
            cols = slice(g * GROUP, (g + 1) * GROUP)
            y = _dot(et_ref[...], h[rows, cols]).astype(BF16)
            f = _dot(y[:seq], ec_ref[:GROUP, :]) + _dot(y[seq:], ec_ref[GROUP:, :])
            f_scr[rows, cols] = (f * norm).astype(BF16)
    o_ref[...] = x + mod_ref[2:3, :] * _dot(f_scr[...], w_ref[...].astype(BF16))


def _fourier_call(x, mods, li, cond_of_row, seq, P):
    m = x.shape[0]
    lj = li // 4
    e_t, e_c = _dft_tables(seq)
    return pl.pallas_call(
        functools.partial(_fourier_kernel, seq=seq),
        grid=(m // ROW_TILE,),
        in_specs=[
            pl.BlockSpec((ROW_TILE, D_MODEL), lambda i: (i, 0)),
            _mod_spec(li, cond_of_row, ROW_TILE, 1),
            pl.BlockSpec((None, 1, D_MODEL), lambda i: (li, 0, 0)),
            pl.BlockSpec((2 * seq, seq), lambda i: (0, 0)),
            pl.BlockSpec((2 * GROUP, GROUP), lambda i: (0, 0)),
            pl.BlockSpec((None, D_MODEL, D_MODEL), lambda i: (lj, 0, 0)),
        ],
        out_specs=pl.BlockSpec((ROW_TILE, D_MODEL), lambda i: (i, 0)),
        out_shape=jax.ShapeDtypeStruct((m, D_MODEL), F32),
        scratch_shapes=[pltpu.VMEM((ROW_TILE, D_MODEL), BF16)],
        compiler_params=_params(("arbitrary",)),
        name=f"fourier_t{seq}",
    )(x, mods, P["norm1_w"], e_t, e_c, P["fnet_w"])


def _rope_tables(rows):
    r, col = jnp.meshgrid(jnp.arange(rows), jnp.arange(GRID_W), indexing="ij")
    r = r.reshape(-1).astype(F32)
    col = col.reshape(-1).astype(F32)
    n_freq = QK_ROPE // 4
    inv = ROPE_THETA ** (-jnp.arange(n_freq, dtype=F32) / n_freq)
    ang = jnp.concatenate([r[:, None] * inv, col[:, None] * inv], axis=-1)
    cos = jnp.repeat(jnp.cos(ang), 2, axis=-1)
    sin = jnp.repeat(jnp.sin(ang), 2, axis=-1)
    t = cos.shape[0]
    pad = LANES - QK_NOPE - QK_ROPE
    cos_p = jnp.concatenate([jnp.ones((t, QK_NOPE), F32), cos, jnp.ones((t, pad), F32)], axis=-1)
    sin_p = jnp.concatenate([jnp.zeros((t, QK_NOPE), F32), sin, jnp.zeros((t, pad), F32)], axis=-1)
    return cos_p, sin_p


def _pair_swap(w):
    wr = w.reshape(w.shape[:-1] + (w.shape[-1] // 2, 2))
    return jnp.stack([-wr[..., 1], wr[..., 0]], axis=-1).reshape(w.shape)


def _mla_weights(P, j, rope, seq_rows):
    pad = LANES - QK_NOPE - QK_ROPE
    wuq = P["mla_wuq"][j].reshape(Q_LORA, N_HEADS, QK_NOPE + QK_ROPE)
    q_slots = jnp.pad(wuq, ((0, 0), (0, 0), (0, pad))).reshape(Q_LORA, N_HEADS * LANES)
    wdkv = P["mla_wdkv"][j]
    kr_slot = jnp.pad(wdkv[:, KV_LORA:], ((0, 0), (QK_NOPE, pad)))
    parts1 = [P["mla_wdq"][j], wdkv[:, :KV_LORA], kr_slot]
    parts2 = [q_slots]
    if rope:
        parts1.append(jnp.pad(_pair_swap(wdkv[:, KV_LORA:]), ((0, 0), (QK_NOPE, pad))))
        q_sw = jnp.pad(_pair_swap(wuq[:, :, QK_NOPE:]), ((0, 0), (0, 0), (QK_NOPE, pad)))
        parts2.append(q_sw.reshape(Q_LORA, N_HEADS * LANES))
        cos, sin = _rope_tables(seq_rows // GRID_W)
    else:
        cos = sin = jnp.zeros((PROJ_TILE, LANES), F32)
    wukv = P["mla_wukv"][j].reshape(KV_LORA, N_HEADS, QK_NOPE + V_HEAD)
    wk = jnp.pad(wukv[:, :, :QK_NOPE], ((0, 0), (0, 0), (0, LANES - QK_NOPE)))
    return dict(
        norm1_w=P["norm1_w"], q_norm=P["mla_q_norm"][j][None, :], kv_norm=P["mla_kv_norm"][j][None, :],
        w1=jnp.concatenate(parts1, axis=1), w2=jnp.concatenate(parts2, axis=1),
        wk=wk.reshape(KV_LORA, N_HEADS * LANES),
        wv=wukv[:, :, QK_NOPE:].reshape(KV_LORA, N_HEADS * V_HEAD), cos=cos, sin=sin)


def _mla_layer(x, mods, li, cond_of_row, batch, seq, P, cache):
    j = li // 4
    W = _mla_weights(P, j, cache is not None, seq)
    q, k, v, ckv, krope = _mla_proj_call(x, mods, li, cond_of_row, W, cache is not None)
    q = q.reshape(N_HEADS, batch, seq, LANES)
    k = k.reshape(N_HEADS, batch, seq, LANES)
    v = v.reshape(HEAD_PAIRS, batch, seq, LANES)
    if cache is None:
        o = _attn_call(q, [(k, v)], seq, HEAD_PAIRS)
    else:
        c_ckv, c_krope = cache
        past = c_ckv.shape[1]
        pad = LANES - QK_NOPE - QK_ROPE
        kr_pad = jnp.pad(c_krope.reshape(batch * past, QK_ROPE), ((0, 0), (QK_NOPE, pad)))
        kc, vc = _kv_cache_call(c_ckv.reshape(batch * past, KV_LORA), kr_pad, W)
        cached = (kc.reshape(N_HEADS, batch, past, LANES), vc.reshape(HEAD_PAIRS, batch, past, LANES))
        o = _attn_call(q, [cached, (k, v)], 512, 4)
    return o, ckv, krope


def _run_pass(x, mods, cond_of_row, batch, seq, P, cache):
    ckv = krope = None
    for li in range(DEPTH):
        kind = li % 4
        pre = None
        if kind == 0:
            o, ckv, krope = _mla_layer(x, mods, li, cond_of_row, batch, seq, P, cache)
            pre = (o, P["mla_wo_bf16"], li // 4)
        elif kind == 1:
            x = _pool_call(x, mods, li, cond_of_row, seq, P)
        elif kind == 2:
            x = _fourier_call(x, mods, li, cond_of_row, seq, P)
        else:
            x = _sconv_call(x, mods, li, cond_of_row, seq, P)
        x = _ffn_call(x, mods, li, cond_of_row, seq, P, pre, final_norm=(li == DEPTH - 1))
    return x, ckv, krope


def kernel(x_prompt, x_sample, cache_ckv, cache_krope, c, c_ctx, norm1_w, norm2_w, ada_w, ada_b, mla_wdq, mla_q_norm, mla_wuq, mla_wdkv, mla_kv_norm, mla_wukv, mla_wo, pool_w, pool_scale, fnet_w, sconv_win, sconv_conv, sconv_wout, ffn_up, ffn_conv_w, ffn_conv_b, ffn_down, final_norm_w):
    row = lambda w: w.reshape(w.shape[0], 1, w.shape[1])
    P = dict(norm1_w=row(norm1_w), norm2_w=row(norm2_w), mla_wdq=mla_wdq, mla_q_norm=mla_q_norm,
             mla_wuq=mla_wuq, mla_wdkv=mla_wdkv, mla_kv_norm=mla_kv_norm, mla_wukv=mla_wukv,
             mla_wo_bf16=mla_wo.astype(BF16),
             pool_w=pool_w, pool_scale=row(pool_scale), fnet_w=fnet_w, sconv_win=sconv_win,
             sconv_conv=sconv_conv, sconv_wout=sconv_wout, ffn_conv_w=ffn_conv_w,
             ffn_conv_b=ffn_conv_b, ffn_down=ffn_down, final_norm_w=final_norm_w,
             ffn_up_cm=_chunk_major(ffn_up, 2, "ffn_up_layout"),
             sconv_win_cm=_chunk_major(sconv_win, 3, "sconv_win_layout"),
             ffn_down_cm=_down_chunks(ffn_down), sconv_wout_cm=_down_chunks(sconv_wout))
    b_ctx, t_ctx, _ = x_prompt.shape
    b_lat, t_lat, _ = x_sample.shape
    assert ROW_TILE % t_ctx == 0 and t_lat % ROW_TILE == 0 and b_lat + 1 <= N_COND

    conds = jnp.concatenate([c_ctx[None, :], c, jnp.zeros((N_COND - 1 - b_lat, D_MODEL), F32)], axis=0)
    mods = _ada_call(conds, ada_w, ada_b).reshape(DEPTH, N_COND, 6, D_MODEL)

    xp, ckv, krope = _run_pass(x_prompt.reshape(b_ctx * t_ctx, D_MODEL), mods, lambda r: 0,
                               b_ctx, t_ctx, P, None)
    xs, _, _ = _run_pass(x_sample.reshape(b_lat * t_lat, D_MODEL), mods,
                         lambda r: 1 + r // t_lat, b_lat, t_lat, P,
                         (cache_ckv[:, 0], cache_krope[:, 0]))
    return (xp.reshape(b_ctx, t_ctx, D_MODEL), xs.reshape(b_lat, t_lat, D_MODEL),
            ckv.reshape(b_ctx, 1, t_ctx, KV_LORA), krope.reshape(b_ctx, 1, t_ctx, QK_ROPE))
```

```python
import functools
import math

import numpy as np
import jax
import jax.numpy as jnp
from jax import lax
from jax.experimental import pallas as pl
from jax.experimental.pallas import tpu as pltpu

F32 = jnp.float32
BF16 = jnp.bfloat16

D_MODEL = 1024
N_HEADS = 16
QK_NOPE = 64
QK_ROPE = 32
V_HEAD = 64
Q_LORA = 384
KV_LORA = 256
ROPE_THETA = 10000.0
GRID_W = 64
POOL_WINDOWS = (2, 4, 8, 16)
N_GROUPS = 4
GROUP = D_MODEL // N_GROUPS
D_FF = 2816
EPS = 1e-6
DEPTH = 4

LANES = 128
Q_SCALE = np.float32(math.log2(math.e) / math.sqrt(QK_NOPE + QK_ROPE))
HEAD_PAIRS = N_HEADS // 2
N_COND = 8
ROW_TILE = 1024
PROJ_TILE = 512
FF_TILE = 256
ROW_BLOCK = 256
HALO = 8
RELAYOUT_ROWS = 128
ADA_TILE = 1536
MIB = 1024 * 1024
VMEM_DEFAULT = 32 * MIB
VMEM_HEADROOM = 4 * MIB
ATTN_VMEM = 32 * MIB


def _params(sem, vmem_bytes=VMEM_DEFAULT):
    return pltpu.CompilerParams(dimension_semantics=sem, vmem_limit_bytes=int(vmem_bytes))


def _nbytes(a):
    return math.prod(a.shape) * jnp.dtype(a.dtype).itemsize


def _dot(a, b):
    return jnp.dot(a, b, preferred_element_type=F32)


def _rms(x, w):
    return x * lax.rsqrt(jnp.mean(x * x, axis=-1, keepdims=True) + EPS) * w


def _norm_mod(x, nw, mod, base):
    return _rms(x, nw) * (1.0 + mod[base + 1:base + 2, :]) + mod[base:base + 1, :]


def _gelu_exact(x):
    return 0.5 * x * (1.0 + lax.erf(x * np.float32(math.sqrt(0.5))))


def _ada_kernel(c_ref, w_ref, b_ref, o_ref):
    s = jax.nn.silu(c_ref[...]).astype(BF16)
    o_ref[...] = _dot(s, w_ref[...].astype(BF16)) + b_ref[...]


def _ada_call(conds, ada_w, ada_b):
    n_out = ada_w.shape[-1]
    return pl.pallas_call(
        _ada_kernel,
        grid=(DEPTH, n_out // ADA_TILE),
        in_specs=[
            pl.BlockSpec((N_COND, D_MODEL), lambda l, j: (0, 0)),
            pl.BlockSpec((None, D_MODEL, ADA_TILE), lambda l, j: (l, 0, j)),
            pl.BlockSpec((None, 1, ADA_TILE), lambda l, j: (l, 0, j)),
        ],
        out_specs=pl.BlockSpec((None, N_COND, ADA_TILE), lambda l, j: (l, 0, j)),
        out_shape=jax.ShapeDtypeStruct((DEPTH, N_COND, n_out), F32),
        compiler_params=_params(("arbitrary", "arbitrary"), 2 * D_MODEL * ADA_TILE * 4 + VMEM_HEADROOM),
        name="ada",
    )(conds, ada_w, ada_b.reshape(DEPTH, 1, n_out))


def _mod_spec(li, cond_of_row, tile, n_grid):
    if n_grid == 1:
        return pl.BlockSpec((None, None, 6, D_MODEL), lambda i: (li, cond_of_row(i * tile), 0, 0))
    return pl.BlockSpec((None, None, 6, D_MODEL), lambda i, j: (li, cond_of_row(i * tile), 0, 0))


def _conv_window(w, cw, t0, seq):
    n = w.shape[0]
    mid = slice(HALO, HALO + ROW_BLOCK)
    t = (t0 + lax.broadcasted_iota(jnp.int32, (ROW_BLOCK, 1), 0)) & (seq - 1)
    prev = jnp.where(t == 0, 0.0, pltpu.roll(w, 1, axis=0)[mid])
    nxt = jnp.where(t == seq - 1, 0.0, pltpu.roll(w, n - 1, axis=0)[mid])
    return prev * cw[0:1, :] + w[mid] * cw[1:2, :] + nxt * cw[2:3, :]


def _ffn_act(win, mid, cw, cb, t0, seq):
    return _gelu_exact(_conv_window(win(0), cw, t0, seq) + cb) * mid(1)


def _sconv_act(win, mid, cw, cb, t0, seq):
    return mid(0) * _conv_window(win(1) * win(2), cw, t0, seq)


def _ffn_kernel(x_ref, mod_ref, nw_ref, up_ref, cw_ref, cb_ref, down_ref, fnw_ref, o_ref, *scratch, **cfg):
    _glu_body(x_ref, mod_ref, nw_ref, up_ref, cw_ref, cb_ref, down_ref, fnw_ref, None, o_ref, *scratch,
              n_up=2, act=_ffn_act, mod_base=3, **cfg)


def _ffn_pre_kernel(x_ref, mod_ref, nw_ref, up_ref, cw_ref, cb_ref, down_ref, fnw_ref, a_ref, wa_ref, o_ref,
                    *scratch, **cfg):
    _glu_body(x_ref, mod_ref, nw_ref, up_ref, cw_ref, cb_ref, down_ref, fnw_ref, (a_ref, wa_ref), o_ref,
              *scratch, n_up=2, act=_ffn_act, mod_base=3, **cfg)


def _sconv_kernel(x_ref, mod_ref, nw_ref, up_ref, cw_ref, down_ref, o_ref, *scratch, **cfg):
    _glu_body(x_ref, mod_ref, nw_ref, up_ref, cw_ref, None, down_ref, None, None, o_ref, *scratch,
              n_up=3, act=_sconv_act, mod_base=0, final_norm=False, **cfg)


def _glu_body(x_ref, mod_ref, nw_ref, up_ref, cw_ref, cb_ref, down_ref, fnw_ref, pre, o_ref,
              h_scr, acc_scr, gu0, gu1, *, n_up, n_chunks, act, seq, mod_base, final_norm):
    gu = (gu0, gu1)
    n_blocks = ROW_TILE // ROW_BLOCK

    def up(c, slot, r):
        rows = slice(r * ROW_BLOCK, (r + 1) * ROW_BLOCK)
        res = _dot(h_scr[rows, :], up_ref[c])
        for p in range(n_up):
            gu[slot][p, HALO + r * ROW_BLOCK:HALO + (r + 1) * ROW_BLOCK, :] = res[:, p * FF_TILE:(p + 1) * FF_TILE]

    def activate(c, slot, r):
        win = lambda p: gu[slot][p, r * ROW_BLOCK:(r + 1) * ROW_BLOCK + 2 * HALO, :]
        mid = lambda p: gu[slot][p, HALO + r * ROW_BLOCK:HALO + (r + 1) * ROW_BLOCK, :]
        cb = None if cb_ref is None else cb_ref[c]
        return act(win, mid, cw_ref[c], cb, r * ROW_BLOCK, seq).astype(BF16)

    def down(a, c, r):
        rows = slice(r * ROW_BLOCK, (r + 1) * ROW_BLOCK)
        acc_scr[rows, :] += _dot(a, down_ref[c])

    def stage(s, slot):
        for r in range(n_blocks):
            a = activate(s - 1, 1 - slot, r)
            up(s, slot, r)
            down(a, s - 1, r)

    def block(r):
        return slice(r * ROW_BLOCK, (r + 1) * ROW_BLOCK)

    def mixer_proj(r):
        a_ref, wa_ref = pre
        return _dot(a_ref[block(r), :], wa_ref[...])

    for g in gu:
        g[:, :HALO, :] = jnp.zeros((n_up, HALO, FF_TILE), F32)
        g[:, HALO + ROW_TILE:, :] = jnp.zeros((n_up, HALO, FF_TILE), F32)

    proj = mixer_proj(0) if pre is not None else None
    for r in range(n_blocks):
        x = x_ref[block(r), :]
        if pre is not None:
            x = x + mod_ref[2:3, :] * proj
            o_ref[block(r), :] = x
            if r + 1 < n_blocks:
                proj = mixer_proj(r + 1)
        h_scr[block(r), :] = _norm_mod(x, nw_ref[...], mod_ref[...], mod_base).astype(BF16)
        acc_scr[block(r), :] = jnp.zeros((ROW_BLOCK, D_MODEL), F32)
        up(0, 0, r)

    n_pairs = (n_chunks - 1) // 2

    def pair(i, carry):
        s = 2 * i + 1
        stage(s, 1)
        stage(s + 1, 0)
        return carry

    lax.fori_loop(0, n_pairs, pair, 0)
    for s in range(2 * n_pairs + 1, n_chunks):
        stage(s, s % 2)
    last = n_chunks - 1
    res = x_ref if pre is None else o_ref
    for r in range(n_blocks):
        down(activate(last, last % 2, r), last, r)
        y = res[block(r), :] + mod_ref[mod_base + 2:mod_base + 3, :] * acc_scr[block(r), :]
        if final_norm:
            y = _rms(y, fnw_ref[...])
        o_ref[block(r), :] = y


def _chunked(w, n_chunks):
    return w.reshape(w.shape[0], n_chunks, FF_TILE).transpose(1, 0, 2)


def _glu_call(kernel_fn, name, x, mods, li, cond_of_row, n_up, n_chunks, norm_w, operands, specs, cfg):
    m = x.shape[0]
    whole = lambda a: pl.BlockSpec(a.shape, lambda i: (0,) * a.ndim)
    in_specs = [pl.BlockSpec((ROW_TILE, D_MODEL), lambda i: (i, 0)),
                _mod_spec(li, cond_of_row, ROW_TILE, 1),
                pl.BlockSpec((None, 1, D_MODEL), lambda i: (li, 0, 0))]
    layer_w = lambda a, l: pl.BlockSpec((None,) + a.shape[1:], lambda i: (l,) + (0,) * (a.ndim - 1),
                                        pipeline_mode=pl.Buffered(1))
    rows = lambda a: pl.BlockSpec((ROW_TILE, a.shape[1]), lambda i: (i, 0))
    spec_of = {None: whole, "rows": rows, "resident": lambda a: pl.BlockSpec(memory_space=pltpu.VMEM)}
    in_specs += [layer_w(a, s) if isinstance(s, int) else spec_of[s](a) for a, s in zip(operands, specs)]
    gu_shape = (n_up, ROW_TILE + 2 * HALO, FF_TILE)
    tile = ROW_TILE * D_MODEL
    window = {None: _nbytes, "rows": lambda a: 2 * ROW_TILE * a.shape[1] * a.dtype.itemsize,
              "resident": lambda a: 0}
    vmem = (2 * 2 * tile * 4 + tile * 2 + tile * 4 + 2 * math.prod(gu_shape) * 4 + VMEM_HEADROOM
            + sum(_nbytes(a) // a.shape[0] if isinstance(s, int) else window[s](a)
                  for a, s in zip(operands, specs)))
    return pl.pallas_call(
        functools.partial(kernel_fn, n_chunks=n_chunks, **cfg),
        grid=(m // ROW_TILE,),
        in_specs=in_specs,
        out_specs=pl.BlockSpec((ROW_TILE, D_MODEL), lambda i: (i, 0)),
        out_shape=jax.ShapeDtypeStruct((m, D_MODEL), F32),
        scratch_shapes=[
            pltpu.VMEM((ROW_TILE, D_MODEL), BF16),
            pltpu.VMEM((ROW_TILE, D_MODEL), F32),
            pltpu.VMEM(gu_shape, F32), pltpu.VMEM(gu_shape, F32),
        ],
        compiler_params=_params(("arbitrary",), vmem),
        name=name,
    )(x, mods, norm_w, *operands)


def _in_vmem(a):
    return pltpu.with_memory_space_constraint(a, pltpu.VMEM)


def _layer_weights_kernel(up_ref, dn_ref, uo_ref, do_ref, *, n_up, n_chunks):
    for c in range(n_chunks):
        for p in range(n_up):
            col = (p * n_chunks + c) * FF_TILE
            uo_ref[c, :, p * FF_TILE:(p + 1) * FF_TILE] = up_ref[:, col:col + FF_TILE].astype(BF16)
    do_ref[...] = dn_ref[...].astype(BF16)


def _layer_weights(up, down, layer, n_up, name):
    _, d, cols = up.shape
    _, hidden, d_out = down.shape
    n_chunks = hidden // FF_TILE
    steps = d // RELAYOUT_ROWS
    dn_rows = hidden // steps
    up_cm, dn = pl.pallas_call(
        functools.partial(_layer_weights_kernel, n_up=n_up, n_chunks=n_chunks),
        grid=(steps,),
        in_specs=[pl.BlockSpec((None, RELAYOUT_ROWS, cols), lambda k: (layer, k, 0)),
                  pl.BlockSpec((None, dn_rows, d_out), lambda k: (layer, k, 0))],
        out_specs=[pl.BlockSpec((n_chunks, RELAYOUT_ROWS, n_up * FF_TILE), lambda k: (0, k, 0)),
                   pl.BlockSpec((dn_rows, d_out), lambda k: (k, 0))],
        out_shape=[jax.ShapeDtypeStruct((n_chunks, d, n_up * FF_TILE), BF16),
                   jax.ShapeDtypeStruct((hidden, d_out), BF16)],
        compiler_params=_params(("arbitrary",),
                                2 * (RELAYOUT_ROWS * cols + dn_rows * d_out) * (4 + 2) + VMEM_HEADROOM),
        name=name,
    )(up, down)
    return _in_vmem(up_cm), _in_vmem(dn.reshape(n_chunks, FF_TILE, d_out))


def _ffn_call(x, mods, li, cond_of_row, seq, P, pre, final_norm):
    n = D_FF // FF_TILE
    up_cm, down_cm = P["ffn_w"](li)
    operands = [up_cm, _chunked(P["ffn_conv_w"][li], n), _chunked(P["ffn_conv_b"][li][None, :], n),
                down_cm, P["final_norm_w"].reshape(1, D_MODEL)]
    specs = ["resident", None, None, "resident", None]
    kernel_fn = _ffn_kernel
    if pre is not None:
        a, w_a, l_a = pre
        operands += [a, w_a]
        specs += ["rows", l_a]
        kernel_fn = _ffn_pre_kernel
    return _glu_call(kernel_fn, f"ffn{li}", x, mods, li, cond_of_row, 2, n, P["norm2_w"],
                     operands, specs, dict(seq=seq, final_norm=final_norm))


def _sconv_call(x, mods, li, cond_of_row, seq, P):
    lj = li // 4
    n = D_MODEL // FF_TILE
    up_cm, down_cm = P["sconv_w"](lj)
    operands = [up_cm, _chunked(P["sconv_conv"][lj], n), down_cm]
    return _glu_call(_sconv_kernel, f"sconv{li}", x, mods, li, cond_of_row, 3, n, P["norm1_w"],
                     operands, ["resident", None, "resident"], dict(seq=seq))


def _emit_kv(ckv, krope_pad, wk_ref, wv_ref, k_ref, v_ref):
    c = ckv.astype(BF16)
    kx = _dot(c, wk_ref[...].astype(BF16))
    vx = _dot(c, wv_ref[...].astype(BF16))
    for h in range(N_HEADS):
        k_ref[h] = (kx[:, h * LANES:(h + 1) * LANES] + krope_pad).astype(BF16)
    for p in range(HEAD_PAIRS):
        v_ref[p] = vx[:, p * LANES:(p + 1) * LANES].astype(BF16)


def _mla_proj_kernel(x_ref, mod_ref, nw_ref, w1_ref, qn_ref, kvn_ref, w2_ref, wk_ref, wv_ref, cos_ref,
                     sin_ref, q_ref, k_ref, v_ref, ckv_ref, kr_ref, *, rope):
    h = _norm_mod(x_ref[...], nw_ref[...], mod_ref[...], 0).astype(BF16)
    z = _dot(h, w1_ref[...].astype(BF16))
    cq = _rms(z[:, :Q_LORA], qn_ref[...]).astype(BF16)
    ckv = _rms(z[:, Q_LORA:Q_LORA + KV_LORA], kvn_ref[...])
    kr = z[:, Q_LORA + KV_LORA:Q_LORA + KV_LORA + LANES]
    if rope:
        cos, sin = cos_ref[...], sin_ref[...]
        kr = kr * cos + z[:, Q_LORA + KV_LORA + LANES:] * sin
    for p in range(HEAD_PAIRS):
        q2 = _dot(cq, w2_ref[:, 2 * p * LANES:(2 * p + 2) * LANES].astype(BF16))
        if rope:
            off = N_HEADS * LANES
            qs = _dot(cq, w2_ref[:, off + 2 * p * LANES:off + (2 * p + 2) * LANES].astype(BF16))
        for e in range(2):
            qh = q2[:, e * LANES:(e + 1) * LANES]
            if rope:
                qh = qh * cos + qs[:, e * LANES:(e + 1) * LANES] * sin
            q_ref[2 * p + e] = (qh * Q_SCALE).astype(BF16)
    ckv_ref[...] = ckv
    kr_ref[...] = kr[:, QK_NOPE:QK_NOPE + QK_ROPE]
    _emit_kv(ckv, kr, wk_ref, wv_ref, k_ref, v_ref)


def _mla_proj_call(x, mods, li, cond_of_row, W, rope):
    m = x.shape[0]
    n1 = W["w1"].shape[1]
    n2 = W["w2"].shape[1]
    const = lambda i: (0, 0)
    pos_tiles = W["cos"].shape[0] // PROJ_TILE
    pos = lambda i: (i % pos_tiles, 0)
    return pl.pallas_call(
        functools.partial(_mla_proj_kernel, rope=rope),
        grid=(m // PROJ_TILE,),
        in_specs=[
            pl.BlockSpec((PROJ_TILE, D_MODEL), lambda i: (i, 0)),
            _mod_spec(li, cond_of_row, PROJ_TILE, 1),
            pl.BlockSpec((None, 1, D_MODEL), lambda i: (li, 0, 0)),
            pl.BlockSpec((D_MODEL, n1), const),
            pl.BlockSpec((1, Q_LORA), const),
            pl.BlockSpec((1, KV_LORA), const),
            pl.BlockSpec((Q_LORA, n2), const),
            pl.BlockSpec((KV_LORA, N_HEADS * LANES), const),
            pl.BlockSpec((KV_LORA, HEAD_PAIRS * LANES), const),
            pl.BlockSpec((PROJ_TILE, LANES), pos),
            pl.BlockSpec((PROJ_TILE, LANES), pos),
        ],
        out_specs=[
            pl.BlockSpec((N_HEADS, PROJ_TILE, LANES), lambda i: (0, i, 0)),
            pl.BlockSpec((N_HEADS, PROJ_TILE, LANES), lambda i: (0, i, 0)),
            pl.BlockSpec((HEAD_PAIRS, PROJ_TILE, LANES), lambda i: (0, i, 0)),
            pl.BlockSpec((PROJ_TILE, KV_LORA), lambda i: (i, 0)),
            pl.BlockSpec((PROJ_TILE, QK_ROPE), lambda i: (i, 0)),
        ],
        out_shape=[
            jax.ShapeDtypeStruct((N_HEADS, m, LANES), BF16),
            jax.ShapeDtypeStruct((N_HEADS, m, LANES), BF16),
            jax.ShapeDtypeStruct((HEAD_PAIRS, m, LANES), BF16),
            jax.ShapeDtypeStruct((m, KV_LORA), F32),
            jax.ShapeDtypeStruct((m, QK_ROPE), F32),
        ],
        compiler_params=_params(("arbitrary",)),
        name="mla_proj_rope" if rope else "mla_proj",
    )(x, mods, W["norm1_w"], W["w1"], W["q_norm"], W["kv_norm"], W["w2"], W["wk"], W["wv"],
      W["cos"], W["sin"])


def _kv_cache_kernel(ckv_ref, kr_ref, wk_ref, wv_ref, k_ref, v_ref):
    _emit_kv(ckv_ref[...], kr_ref[...], wk_ref, wv_ref, k_ref, v_ref)


def _kv_cache_call(ckv, krope_pad, W):
    m = ckv.shape[0]
    const = lambda i: (0, 0)
    return pl.pallas_call(
        _kv_cache_kernel,
        grid=(1,),
        in_specs=[
            pl.BlockSpec((m, KV_LORA), const),
            pl.BlockSpec((m, LANES), const),
            pl.BlockSpec((KV_LORA, N_HEADS * LANES), const),
            pl.BlockSpec((KV_LORA, HEAD_PAIRS * LANES), const),
        ],
        out_specs=[
            pl.BlockSpec((N_HEADS, m, LANES), lambda i: (0, 0, 0)),
            pl.BlockSpec((HEAD_PAIRS, m, LANES), lambda i: (0, 0, 0)),
        ],
        out_shape=[
            jax.ShapeDtypeStruct((N_HEADS, m, LANES), BF16),
            jax.ShapeDtypeStruct((HEAD_PAIRS, m, LANES), BF16),
        ],
        compiler_params=_params(("arbitrary",)),
        name="kv_cache",
    )(ckv, krope_pad, W["wk"], W["wv"])


def _attn_kernel(q_ref, *refs, n_pairs):
    o_ref, kv = refs[-1], refs[:-1]
    parts = [(kv[i], kv[i + 1]) for i in range(0, len(kv), 2)]
    heads = [(p, e) for p in range(n_pairs) for e in range(2)]

    def scores(p, e):
        q = q_ref[2 * p + e]
        return [lax.dot_general(q, k_ref[2 * p + e], (((1,), (1,)), ((), ())), preferred_element_type=F32)
                for k_ref, _ in parts]

    outs = []
    s_next = scores(*heads[0])
    for i, (p, e) in enumerate(heads):
        s = s_next
        if i + 1 < len(heads):
            s_next = scores(*heads[i + 1])
        m = functools.reduce(jnp.maximum, [jnp.max(x, axis=-1, keepdims=True) for x in s])
        w = [jnp.exp2(x - m) for x in s]
        l = sum(jnp.sum(x, axis=-1, keepdims=True) for x in w)
        pv = sum(_dot(x.astype(BF16), v_ref[p]) for x, (_, v_ref) in zip(w, parts))
        outs.append(pv / l)
        if e == 1:
            lane = lax.broadcasted_iota(jnp.int32, pv.shape, 1)
            o_ref[:, p * LANES:(p + 1) * LANES] = jnp.where(lane < V_HEAD, outs[-2], outs[-1]).astype(BF16)


def _attn_call(q, kv_parts, q_tile, n_pairs):
    _, b, t, _ = q.shape
    nq = t // q_tile
    in_specs = [pl.BlockSpec((2 * n_pairs, None, q_tile, LANES), lambda bi, g, qi: (g, bi, qi, 0))]
    operands = [q]
    for k, v in kv_parts:
        s = k.shape[2]
        in_specs += [pl.BlockSpec((2 * n_pairs, None, s, LANES), lambda bi, g, qi: (g, bi, 0, 0)),
                     pl.BlockSpec((n_pairs, None, s, LANES), lambda bi, g, qi: (g, bi, 0, 0))]
        operands += [k, v]
    return pl.pallas_call(
        functools.partial(_attn_kernel, n_pairs=n_pairs),
        grid=(b, HEAD_PAIRS // n_pairs, nq),
        in_specs=in_specs,
        out_specs=pl.BlockSpec((q_tile, n_pairs * LANES), lambda bi, g, qi: (bi * nq + qi, g)),
        out_shape=jax.ShapeDtypeStruct((b * t, N_HEADS * V_HEAD), BF16),
        compiler_params=_params(("arbitrary", "arbitrary", "arbitrary"), ATTN_VMEM),
        name=f"attn_{len(kv_parts)}part",
    )(*operands)


def _pool_kernel(x_ref, mod_ref, nw_ref, w_ref, sc_ref, o_ref, band_scr, y_scr, *, seq):
    @pl.when(pl.program_id(0) == 0)
    def _():
        r = lax.broadcasted_iota(jnp.int32, (seq, seq), 0)
        c = lax.broadcasted_iota(jnp.int32, (seq, seq), 1)
        for g, win in enumerate(POOL_WINDOWS):
            inside = (c >= r - win // 2) & (c < r + win - win // 2)
            band_scr[g] = jnp.where(inside, 1.0, 0.0).astype(BF16)

    x = x_ref[...]
    h = _norm_mod(x, nw_ref[...], mod_ref[...], 0)
    h_hi = h.astype(BF16)
    h_lo = (h - h_hi.astype(F32)).astype(BF16)
    t = lax.broadcasted_iota(jnp.int32, (seq, 1), 0)
    for g, win in enumerate(POOL_WINDOWS):
        cnt = (jnp.minimum(t + (win - win // 2), seq) - jnp.maximum(t - win // 2, 0)).astype(F32)
        cols = slice(g * GROUP, (g + 1) * GROUP)
        wg = w_ref[g].astype(BF16)
        for s in range(x.shape[0] // seq):
            rows = slice(s * seq, (s + 1) * seq)
            tot = _dot(band_scr[g], h_hi[rows, cols]) + _dot(band_scr[g], h_lo[rows, cols])
            pooled = (tot / cnt - h[rows, cols]).astype(BF16)
            y_scr[rows, cols] = _dot(pooled, wg)
    o_ref[...] = x + mod_ref[2:3, :] * (y_scr[...] * sc_ref[...])


def _pool_call(x, mods, li, cond_of_row, seq, P):
    m = x.shape[0]
    lj = li // 4
    return pl.pallas_call(
        functools.partial(_pool_kernel, seq=seq),
        grid=(m // ROW_TILE,),
        in_specs=[
            pl.BlockSpec((ROW_TILE, D_MODEL), lambda i: (i, 0)),
            _mod_spec(li, cond_of_row, ROW_TILE, 1),
            pl.BlockSpec((None, 1, D_MODEL), lambda i: (li, 0, 0)),
            pl.BlockSpec((None, N_GROUPS, GROUP, GROUP), lambda i: (lj, 0, 0, 0)),
            pl.BlockSpec((None, 1, D_MODEL), lambda i: (lj, 0, 0)),
        ],
        out_specs=pl.BlockSpec((ROW_TILE, D_MODEL), lambda i: (i, 0)),
        out_shape=jax.ShapeDtypeStruct((m, D_MODEL), F32),
        scratch_shapes=[pltpu.VMEM((N_GROUPS, seq, seq), BF16), pltpu.VMEM((ROW_TILE, D_MODEL), F32)],
        compiler_params=_params(("arbitrary",)),
        name=f"pool_t{seq}",
    )(x, mods, P["norm1_w"], P["pool_w"], P["pool_scale"])


def _dft_tables(seq):
    def cs(n):
        k = np.arange(n, dtype=np.int64)
        ang = 2.0 * np.pi * ((k[:, None] * k[None, :]) % n) / n
        return np.cos(ang), np.sin(ang)
    ct, st = cs(seq)
    cc, sc = cs(GROUP)
    e_t = jnp.asarray(np.concatenate([ct, st], axis=0), dtype=F32)
    e_c = jnp.asarray(np.concatenate([cc, -sc], axis=0), dtype=F32)
    return e_t.astype(BF16), e_c.astype(BF16)


def _fourier_kernel(x_ref, mod_ref, nw_ref, et_ref, ec_ref, w_ref, o_ref, f_scr, *, seq):
    x = x_ref[...]
    h = _norm_mod(x, nw_ref[...], mod_ref[...], 0).astype(BF16)
    norm = np.float32(1.0 / math.sqrt(seq * GROUP))
    for s in range(x.shape[0] // seq):
        rows = slice(s * seq, (s + 1) * seq)
        for g in range(N_GROUPS):
            cols = slice(g * GROUP, (g + 1) * GROUP)
            y = _dot(et_ref[...], h[rows, cols]).astype(BF16)
            f = _dot(y[:seq], ec_ref[:GROUP, :]) + _dot(y[seq:], ec_ref[GROUP:, :])
            f_scr[rows, cols] = (f * norm).astype(BF16)
    o_ref[...] = x + mod_ref[2:3, :] * _dot(f_scr[...], w_ref[...].astype(BF16))


def _fourier_call(x, mods, li, cond_of_row, seq, P):
    m = x.shape[0]
    lj = li // 4
    e_t, e_c = _dft_tables(seq)
    return pl.pallas_call(
        functools.partial(_fourier_kernel, seq=seq),
        grid=(m // ROW_TILE,),
        in_specs=[
            pl.BlockSpec((ROW_TILE, D_MODEL), lambda i: (i, 0)),
            _mod_spec(li, cond_of_row, ROW_TILE, 1),
            pl.BlockSpec((None, 1, D_MODEL), lambda i: (li, 0, 0)),
            pl.BlockSpec((2 * seq, seq), lambda i: (0, 0)),
            pl.BlockSpec((2 * GROUP, GROUP), lambda i: (0, 0)),
            pl.BlockSpec((None, D_MODEL, D_MODEL), lambda i: (lj, 0, 0)),
        ],
        out_specs=pl.BlockSpec((ROW_TILE, D_MODEL), lambda i: (i, 0)),
        out_shape=jax.ShapeDtypeStruct((m, D_MODEL), F32),
        scratch_shapes=[pltpu.VMEM((ROW_TILE, D_MODEL), BF16)],
        compiler_params=_params(("arbitrary",)),
        name=f"fourier_t{seq}",
    )(x, mods, P["norm1_w"], e_t, e_c, P["fnet_w"])


def _rope_tables(rows):
    r, col = jnp.meshgrid(jnp.arange(rows), jnp.arange(GRID_W), indexing="ij")
    r = r.reshape(-1).astype(F32)
    col = col.reshape(-1).astype(F32)
    n_freq = QK_ROPE // 4
    inv = ROPE_THETA ** (-jnp.arange(n_freq, dtype=F32) / n_freq)
    ang = jnp.concatenate([r[:, None] * inv, col[:, None] * inv], axis=-1)
    cos = jnp.repeat(jnp.cos(ang), 2, axis=-1)
    sin = jnp.repeat(jnp.sin(ang), 2, axis=-1)
    t = cos.shape[0]
    pad = LANES - QK_NOPE - QK_ROPE
    cos_p = jnp.concatenate([jnp.ones((t, QK_NOPE), F32), cos, jnp.ones((t, pad), F32)], axis=-1)
    sin_p = jnp.concatenate([jnp.zeros((t, QK_NOPE), F32), sin, jnp.zeros((t, pad), F32)], axis=-1)
    return cos_p, sin_p


def _pair_swap(w):
    wr = w.reshape(w.shape[:-1] + (w.shape[-1] // 2, 2))
    return jnp.stack([-wr[..., 1], wr[..., 0]], axis=-1).reshape(w.shape)


def _mla_weights(P, j, rope, seq_rows):
    pad = LANES - QK_NOPE - QK_ROPE
    wuq = P["mla_wuq"][j].reshape(Q_LORA, N_HEADS, QK_NOPE + QK_ROPE)
    q_slots = jnp.pad(wuq, ((0, 0), (0, 0), (0, pad))).reshape(Q_LORA, N_HEADS * LANES)
    wdkv = P["mla_wdkv"][j]
    kr_slot = jnp.pad(wdkv[:, KV_LORA:], ((0, 0), (QK_NOPE, pad)))
    parts1 = [P["mla_wdq"][j], wdkv[:, :KV_LORA], kr_slot]
    parts2 = [q_slots]
    if rope:
        parts1.append(jnp.pad(_pair_swap(wdkv[:, KV_LORA:]), ((0, 0), (QK_NOPE, pad))))
        q_sw = jnp.pad(_pair_swap(wuq[:, :, QK_NOPE:]), ((0, 0), (0, 0), (QK_NOPE, pad)))
        parts2.append(q_sw.reshape(Q_LORA, N_HEADS * LANES))
        cos, sin = _rope_tables(seq_rows // GRID_W)
    else:
        cos = sin = jnp.zeros((PROJ_TILE, LANES), F32)
    wukv = P["mla_wukv"][j].reshape(KV_LORA, N_HEADS, QK_NOPE + V_HEAD)
    wk = jnp.pad(wukv[:, :, :QK_NOPE], ((0, 0), (0, 0), (0, LANES - QK_NOPE)))
    return dict(
        norm1_w=P["norm1_w"], q_norm=P["mla_q_norm"][j][None, :], kv_norm=P["mla_kv_norm"][j][None, :],
        w1=jnp.concatenate(parts1, axis=1), w2=jnp.concatenate(parts2, axis=1),
        wk=wk.reshape(KV_LORA, N_HEADS * LANES),
        wv=wukv[:, :, QK_NOPE:].reshape(KV_LORA, N_HEADS * V_HEAD), cos=cos, sin=sin)


def _mla_layer(x, mods, li, cond_of_row, batch, seq, P, cache):
    j = li // 4
    W = _mla_weights(P, j, cache is not None, seq)
    q, k, v, ckv, krope = _mla_proj_call(x, mods, li, cond_of_row, W, cache is not None)
    q = q.reshape(N_HEADS, batch, seq, LANES)
    k = k.reshape(N_HEADS, batch, seq, LANES)
    v = v.reshape(HEAD_PAIRS, batch, seq, LANES)
    if cache is None:
        o = _attn_call(q, [(k, v)], seq, HEAD_PAIRS)
    else:
        c_ckv, c_krope = cache
        past = c_ckv.shape[1]
        pad = LANES - QK_NOPE - QK_ROPE
        kr_pad = jnp.pad(c_krope.reshape(batch * past, QK_ROPE), ((0, 0), (QK_NOPE, pad)))
        kc, vc = _kv_cache_call(c_ckv.reshape(batch * past, KV_LORA), kr_pad, W)
        cached = (kc.reshape(N_HEADS, batch, past, LANES), vc.reshape(HEAD_PAIRS, batch, past, LANES))
        o = _attn_call(q, [cached, (k, v)], 512, 4)
    return o, ckv, krope


def _mixer(x, mods, li, cond_of_row, batch, seq, P, cache):
    kind = li % 4
    if kind == 0:
        o, ckv, krope = _mla_layer(x, mods, li, cond_of_row, batch, seq, P, cache)
        return x, (o, P["mla_wo_bf16"], li // 4), (ckv, krope)
    if kind == 1:
        return _pool_call(x, mods, li, cond_of_row, seq, P), None, None
    if kind == 2:
        return _fourier_call(x, mods, li, cond_of_row, seq, P), None, None
    return _sconv_call(x, mods, li, cond_of_row, seq, P), None, None


def _run_passes(passes, mods, P):
    new_cache = None
    for li in range(DEPTH):
        pending = []
        for ps in passes:
            ps["x"], pre, made = _mixer(ps["x"], mods, li, ps["cond_of_row"], ps["batch"], ps["seq"], P,
                                        ps["cache"])
            pending.append(pre)
            if made is not None and ps["cache"] is None:
                new_cache = made
        for ps, pre in zip(passes, pending):
            ps["x"] = _ffn_call(ps["x"], mods, li, ps["cond_of_row"], ps["seq"], P, pre,
                                final_norm=(li == DEPTH - 1))
    return new_cache


def kernel(x_prompt, x_sample, cache_ckv, cache_krope, c, c_ctx, norm1_w, norm2_w, ada_w, ada_b, mla_wdq, mla_q_norm, mla_wuq, mla_wdkv, mla_kv_norm, mla_wukv, mla_wo, pool_w, pool_scale, fnet_w, sconv_win, sconv_conv, sconv_wout, ffn_up, ffn_conv_w, ffn_conv_b, ffn_down, final_norm_w):
    row = lambda w: w.reshape(w.shape[0], 1, w.shape[1])
    P = dict(norm1_w=row(norm1_w), norm2_w=row(norm2_w), mla_wdq=mla_wdq, mla_q_norm=mla_q_norm,
             mla_wuq=mla_wuq, mla_wdkv=mla_wdkv, mla_kv_norm=mla_kv_norm, mla_wukv=mla_wukv,
             mla_wo_bf16=mla_wo.astype(BF16),
             pool_w=pool_w, pool_scale=row(pool_scale), fnet_w=fnet_w, sconv_win=sconv_win,
             sconv_conv=sconv_conv, sconv_wout=sconv_wout, ffn_conv_w=ffn_conv_w,
             ffn_conv_b=ffn_conv_b, ffn_down=ffn_down, final_norm_w=final_norm_w,
             ffn_w=functools.cache(lambda l: _layer_weights(ffn_up, ffn_down, l, 2, f"ffn{l}_weights")),
             sconv_w=functools.cache(lambda l: _layer_weights(sconv_win, sconv_wout, l, 3, f"sconv{l}_weights")))
    b_ctx, t_ctx, _ = x_prompt.shape
    b_lat, t_lat, _ = x_sample.shape
    assert ROW_TILE % t_ctx == 0 and t_lat % ROW_TILE == 0 and b_lat + 1 <= N_COND

    conds = jnp.concatenate([c_ctx[None, :], c, jnp.zeros((N_COND - 1 - b_lat, D_MODEL), F32)], axis=0)
    mods = _ada_call(conds, ada_w, ada_b).reshape(DEPTH, N_COND, 6, D_MODEL)

    ctx = dict(x=x_prompt.reshape(b_ctx * t_ctx, D_MODEL), cond_of_row=lambda r: 0, batch=b_ctx, seq=t_ctx,
               cache=None)
    lat = dict(x=x_sample.reshape(b_lat * t_lat, D_MODEL), cond_of_row=lambda r: 1 + r // t_lat, batch=b_lat,
               seq=t_lat, cache=(cache_ckv[:, 0], cache_krope[:, 0]))
    ckv, krope = _run_passes([ctx, lat], mods, P)
    return (ctx["x"].reshape(b_ctx, t_ctx, D_MODEL), lat["x"].reshape(b_lat, t_lat, D_MODEL),
            ckv.reshape(b_ctx, 1, t_ctx, KV_LORA), krope.reshape(b_ctx, 1, t_ctx, QK_ROPE))
```

```python
import functools
import math

import numpy as np
import jax
import jax.numpy as jnp
from jax import lax
from jax.experimental import pallas as pl
from jax.experimental.pallas import tpu as pltpu

F32 = jnp.float32
BF16 = jnp.bfloat16

D_MODEL = 1024
N_HEADS = 16
QK_NOPE = 64
QK_ROPE = 32
V_HEAD = 64
Q_LORA = 384
KV_LORA = 256
ROPE_THETA = 10000.0
GRID_W = 64
POOL_WINDOWS = (2, 4, 8, 16)
N_GROUPS = 4
GROUP = D_MODEL // N_GROUPS
D_FF = 2816
EPS = 1e-6
DEPTH = 4

LANES = 128
Q_SCALE = np.float32(math.log2(math.e) / math.sqrt(QK_NOPE + QK_ROPE))
HEAD_PAIRS = N_HEADS // 2
N_COND = 8
ROW_TILE = 1024
PROJ_TILE = 512
FF_TILE = 256
ROW_BLOCK = 256
HALO = 8
RELAYOUT_ROWS = 128
ADA_TILE = 1536
MIB = 1024 * 1024
VMEM_DEFAULT = 32 * MIB
VMEM_HEADROOM = 4 * MIB
ATTN_VMEM = 32 * MIB


def _params(sem, vmem_bytes=VMEM_DEFAULT):
    return pltpu.CompilerParams(dimension_semantics=sem, vmem_limit_bytes=int(vmem_bytes))


def _nbytes(a):
    return math.prod(a.shape) * jnp.dtype(a.dtype).itemsize


def _dot(a, b):
    return jnp.dot(a, b, preferred_element_type=F32)


def _rms(x, w):
    return x * lax.rsqrt(jnp.mean(x * x, axis=-1, keepdims=True) + EPS) * w


def _norm_mod(x, nw, mod, base):
    return _rms(x, nw) * (1.0 + mod[base + 1:base + 2, :]) + mod[base:base + 1, :]


def _gelu_exact(x):
    return 0.5 * x * (1.0 + lax.erf(x * np.float32(math.sqrt(0.5))))


def _ada_kernel(c_ref, w_ref, b_ref, o_ref):
    s = jax.nn.silu(c_ref[...]).astype(BF16)
    o_ref[...] = _dot(s, w_ref[...].astype(BF16)) + b_ref[...]


def _ada_call(conds, ada_w, ada_b):
    n_out = ada_w.shape[-1]
    return pl.pallas_call(
        _ada_kernel,
        grid=(DEPTH, n_out // ADA_TILE),
        in_specs=[
            pl.BlockSpec((N_COND, D_MODEL), lambda l, j: (0, 0)),
            pl.BlockSpec((None, D_MODEL, ADA_TILE), lambda l, j: (l, 0, j)),
            pl.BlockSpec((None, 1, ADA_TILE), lambda l, j: (l, 0, j)),
        ],
        out_specs=pl.BlockSpec((None, N_COND, ADA_TILE), lambda l, j: (l, 0, j)),
        out_shape=jax.ShapeDtypeStruct((DEPTH, N_COND, n_out), F32),
        compiler_params=_params(("arbitrary", "arbitrary"), 2 * D_MODEL * ADA_TILE * 4 + VMEM_HEADROOM),
        name="ada",
    )(conds, ada_w, ada_b.reshape(DEPTH, 1, n_out))


def _mod_spec(li, cond_of_row, tile, n_grid):
    if n_grid == 1:
        return pl.BlockSpec((None, None, 6, D_MODEL), lambda i: (li, cond_of_row(i * tile), 0, 0))
    return pl.BlockSpec((None, None, 6, D_MODEL), lambda i, j: (li, cond_of_row(i * tile), 0, 0))


def _conv_window(w, cw, t0, seq):
    n = w.shape[0]
    mid = slice(HALO, HALO + ROW_BLOCK)
    t = (t0 + lax.broadcasted_iota(jnp.int32, (ROW_BLOCK, 1), 0)) & (seq - 1)
    prev = jnp.where(t == 0, 0.0, pltpu.roll(w, 1, axis=0)[mid])
    nxt = jnp.where(t == seq - 1, 0.0, pltpu.roll(w, n - 1, axis=0)[mid])
    return prev * cw[0:1, :] + w[mid] * cw[1:2, :] + nxt * cw[2:3, :]


def _ffn_act(win, mid, cw, cb, t0, seq):
    return _gelu_exact(_conv_window(win(0), cw, t0, seq) + cb) * mid(1)


def _sconv_act(win, mid, cw, cb, t0, seq):
    return mid(0) * _conv_window(win(1) * win(2), cw, t0, seq)


def _ffn_kernel(x_ref, mod_ref, nw_ref, up_ref, cw_ref, cb_ref, down_ref, fnw_ref, o_ref, *scratch, **cfg):
    _glu_body(x_ref, mod_ref, nw_ref, up_ref, cw_ref, cb_ref, down_ref, fnw_ref, None, o_ref, *scratch,
              n_up=2, act=_ffn_act, mod_base=3, **cfg)


def _ffn_pre_kernel(x_ref, mod_ref, nw_ref, up_ref, cw_ref, cb_ref, down_ref, fnw_ref, a_ref, wa_ref, o_ref,
                    *scratch, **cfg):
    _glu_body(x_ref, mod_ref, nw_ref, up_ref, cw_ref, cb_ref, down_ref, fnw_ref, (a_ref, wa_ref), o_ref,
              *scratch, n_up=2, act=_ffn_act, mod_base=3, **cfg)


def _sconv_kernel(x_ref, mod_ref, nw_ref, up_ref, cw_ref, down_ref, o_ref, *scratch, **cfg):
    _glu_body(x_ref, mod_ref, nw_ref, up_ref, cw_ref, None, down_ref, None, None, o_ref, *scratch,
              n_up=3, act=_sconv_act, mod_base=0, final_norm=False, **cfg)


def _glu_body(x_ref, mod_ref, nw_ref, up_ref, cw_ref, cb_ref, down_ref, fnw_ref, pre, o_ref,
              h_scr, acc_scr, gu0, gu1, *, n_up, n_chunks, act, seq, mod_base, final_norm):
    gu = (gu0, gu1)
    n_blocks = ROW_TILE // ROW_BLOCK

    def up(c, slot, r):
        rows = slice(r * ROW_BLOCK, (r + 1) * ROW_BLOCK)
        res = _dot(h_scr[rows, :], up_ref[c])
        for p in range(n_up):
            gu[slot][p, HALO + r * ROW_BLOCK:HALO + (r + 1) * ROW_BLOCK, :] = res[:, p * FF_TILE:(p + 1) * FF_TILE]

    def activate(c, slot, r):
        win = lambda p: gu[slot][p, r * ROW_BLOCK:(r + 1) * ROW_BLOCK + 2 * HALO, :]
        mid = lambda p: gu[slot][p, HALO + r * ROW_BLOCK:HALO + (r + 1) * ROW_BLOCK, :]
        cb = None if cb_ref is None else cb_ref[c]
        return act(win, mid, cw_ref[c], cb, r * ROW_BLOCK, seq).astype(BF16)

    def down(a, c, r):
        rows = slice(r * ROW_BLOCK, (r + 1) * ROW_BLOCK)
        acc_scr[rows, :] += _dot(a, down_ref[c])

    def stage(s, slot):
        for r in range(n_blocks):
            a = activate(s - 1, 1 - slot, r)
            up(s, slot, r)
            down(a, s - 1, r)

    def block(r):
        return slice(r * ROW_BLOCK, (r + 1) * ROW_BLOCK)

    def mixer_proj(r):
        a_ref, wa_ref = pre
        return _dot(a_ref[block(r), :], wa_ref[...])

    for g in gu:
        g[:, :HALO, :] = jnp.zeros((n_up, HALO, FF_TILE), F32)
        g[:, HALO + ROW_TILE:, :] = jnp.zeros((n_up, HALO, FF_TILE), F32)

    proj = mixer_proj(0) if pre is not None else None
    for r in range(n_blocks):
        x = x_ref[block(r), :]
        if pre is not None:
            x = x + mod_ref[2:3, :] * proj
            o_ref[block(r), :] = x
            if r + 1 < n_blocks:
                proj = mixer_proj(r + 1)
        h_scr[block(r), :] = _norm_mod(x, nw_ref[...], mod_ref[...], mod_base).astype(BF16)
        acc_scr[block(r), :] = jnp.zeros((ROW_BLOCK, D_MODEL), F32)
        up(0, 0, r)

    n_pairs = (n_chunks - 1) // 2

    def pair(i, carry):
        s = 2 * i + 1
        stage(s, 1)
        stage(s + 1, 0)
        return carry

    lax.fori_loop(0, n_pairs, pair, 0)
    for s in range(2 * n_pairs + 1, n_chunks):
        stage(s, s % 2)
    last = n_chunks - 1
    res = x_ref if pre is None else o_ref
    for r in range(n_blocks):
        down(activate(last, last % 2, r), last, r)
        y = res[block(r), :] + mod_ref[mod_base + 2:mod_base + 3, :] * acc_scr[block(r), :]
        if final_norm:
            y = _rms(y, fnw_ref[...])
        o_ref[block(r), :] = y


def _chunked(w, n_chunks):
    return w.reshape(w.shape[0], n_chunks, FF_TILE).transpose(1, 0, 2)


def _glu_call(kernel_fn, name, x, mods, li, cond_of_row, n_up, n_chunks, norm_w, operands, specs, cfg):
    m = x.shape[0]
    whole = lambda a: pl.BlockSpec(a.shape, lambda i: (0,) * a.ndim)
    in_specs = [pl.BlockSpec((ROW_TILE, D_MODEL), lambda i: (i, 0)),
                _mod_spec(li, cond_of_row, ROW_TILE, 1),
                pl.BlockSpec((None, 1, D_MODEL), lambda i: (li, 0, 0))]
    layer_w = lambda a, l: pl.BlockSpec((None,) + a.shape[1:], lambda i: (l,) + (0,) * (a.ndim - 1),
                                        pipeline_mode=pl.Buffered(1))
    rows = lambda a: pl.BlockSpec((ROW_TILE, a.shape[1]), lambda i: (i, 0))
    spec_of = {None: whole, "rows": rows, "resident": lambda a: pl.BlockSpec(memory_space=pltpu.VMEM)}
    in_specs += [layer_w(a, s) if isinstance(s, int) else spec_of[s](a) for a, s in zip(operands, specs)]
    gu_shape = (n_up, ROW_TILE + 2 * HALO, FF_TILE)
    tile = ROW_TILE * D_MODEL
    window = {None: _nbytes, "rows": lambda a: 2 * ROW_TILE * a.shape[1] * a.dtype.itemsize,
              "resident": lambda a: 0}
    vmem = (2 * 2 * tile * 4 + tile * 2 + tile * 4 + 2 * math.prod(gu_shape) * 4 + VMEM_HEADROOM
            + sum(_nbytes(a) // a.shape[0] if isinstance(s, int) else window[s](a)
                  for a, s in zip(operands, specs)))
    return pl.pallas_call(
        functools.partial(kernel_fn, n_chunks=n_chunks, **cfg),
        grid=(m // ROW_TILE,),
        in_specs=in_specs,
        out_specs=pl.BlockSpec((ROW_TILE, D_MODEL), lambda i: (i, 0)),
        out_shape=jax.ShapeDtypeStruct((m, D_MODEL), F32),
        scratch_shapes=[
            pltpu.VMEM((ROW_TILE, D_MODEL), BF16),
            pltpu.VMEM((ROW_TILE, D_MODEL), F32),
            pltpu.VMEM(gu_shape, F32), pltpu.VMEM(gu_shape, F32),
        ],
        compiler_params=_params(("arbitrary",), vmem),
        name=name,
    )(x, mods, norm_w, *operands)


def _in_vmem(a):
    return pltpu.with_memory_space_constraint(a, pltpu.VMEM)


def _layer_weights_kernel(up_ref, dn_ref, uo_ref, do_ref, *, n_up, n_chunks):
    for c in range(n_chunks):
        for p in range(n_up):
            col = (p * n_chunks + c) * FF_TILE
            uo_ref[c, :, p * FF_TILE:(p + 1) * FF_TILE] = up_ref[:, col:col + FF_TILE].astype(BF16)
    do_ref[...] = dn_ref[...].astype(BF16)


def _layer_weights(up, down, layer, n_up, name):
    _, d, cols = up.shape
    _, hidden, d_out = down.shape
    n_chunks = hidden // FF_TILE
    steps = d // RELAYOUT_ROWS
    dn_rows = hidden // steps
    up_cm, dn = pl.pallas_call(
        functools.partial(_layer_weights_kernel, n_up=n_up, n_chunks=n_chunks),
        grid=(steps,),
        in_specs=[pl.BlockSpec((None, RELAYOUT_ROWS, cols), lambda k: (layer, k, 0)),
                  pl.BlockSpec((None, dn_rows, d_out), lambda k: (layer, k, 0))],
        out_specs=[pl.BlockSpec((n_chunks, RELAYOUT_ROWS, n_up * FF_TILE), lambda k: (0, k, 0)),
                   pl.BlockSpec((dn_rows, d_out), lambda k: (k, 0))],
        out_shape=[jax.ShapeDtypeStruct((n_chunks, d, n_up * FF_TILE), BF16),
                   jax.ShapeDtypeStruct((hidden, d_out), BF16)],
        compiler_params=_params(("arbitrary",),
                                2 * (RELAYOUT_ROWS * cols + dn_rows * d_out) * (4 + 2) + VMEM_HEADROOM),
        name=name,
    )(up, down)
    return _in_vmem(up_cm), _in_vmem(dn.reshape(n_chunks, FF_TILE, d_out))


def _ffn_call(x, mods, li, cond_of_row, seq, P, pre, final_norm):
    n = D_FF // FF_TILE
    up_cm, down_cm = P["ffn_w"](li)
    operands = [up_cm, _chunked(P["ffn_conv_w"][li], n), _chunked(P["ffn_conv_b"][li][None, :], n),
                down_cm, P["final_norm_w"].reshape(1, D_MODEL)]
    specs = ["resident", None, None, "resident", None]
    kernel_fn = _ffn_kernel
    if pre is not None:
        a, w_a, l_a = pre
        operands += [a, w_a]
        specs += ["rows", l_a]
        kernel_fn = _ffn_pre_kernel
    return _glu_call(kernel_fn, f"ffn{li}", x, mods, li, cond_of_row, 2, n, P["norm2_w"],
                     operands, specs, dict(seq=seq, final_norm=final_norm))


def _sconv_call(x, mods, li, cond_of_row, seq, P):
    lj = li // 4
    n = D_MODEL // FF_TILE
    up_cm, down_cm = P["sconv_w"](lj)
    operands = [up_cm, _chunked(P["sconv_conv"][lj], n), down_cm]
    return _glu_call(_sconv_kernel, f"sconv{li}", x, mods, li, cond_of_row, 3, n, P["norm1_w"],
                     operands, ["resident", None, "resident"], dict(seq=seq))


def _emit_kv(ckv, krope_pad, wk_ref, wv_ref, k_ref, v_ref):
    c = ckv.astype(BF16)
    kx = _dot(c, wk_ref[...].astype(BF16))
    vx = _dot(c, wv_ref[...].astype(BF16))
    for h in range(N_HEADS):
        k_ref[h] = (kx[:, h * LANES:(h + 1) * LANES] + krope_pad).astype(BF16)
    for p in range(HEAD_PAIRS):
        v_ref[p] = vx[:, p * LANES:(p + 1) * LANES].astype(BF16)


def _mla_proj_kernel(x_ref, mod_ref, nw_ref, w1_ref, qn_ref, kvn_ref, w2_ref, wk_ref, wv_ref, cos_ref,
                     sin_ref, q_ref, k_ref, v_ref, ckv_ref, kr_ref, *, rope):
    h = _norm_mod(x_ref[...], nw_ref[...], mod_ref[...], 0).astype(BF16)
    z = _dot(h, w1_ref[...].astype(BF16))
    cq = _rms(z[:, :Q_LORA], qn_ref[...]).astype(BF16)
    ckv = _rms(z[:, Q_LORA:Q_LORA + KV_LORA], kvn_ref[...])
    kr = z[:, Q_LORA + KV_LORA:Q_LORA + KV_LORA + LANES]
    if rope:
        cos, sin = cos_ref[...], sin_ref[...]
        kr = kr * cos + z[:, Q_LORA + KV_LORA + LANES:] * sin
    for p in range(HEAD_PAIRS):
        q2 = _dot(cq, w2_ref[:, 2 * p * LANES:(2 * p + 2) * LANES].astype(BF16))
        if rope:
            off = N_HEADS * LANES
            qs = _dot(cq, w2_ref[:, off + 2 * p * LANES:off + (2 * p + 2) * LANES].astype(BF16))
        for e in range(2):
            qh = q2[:, e * LANES:(e + 1) * LANES]
            if rope:
                qh = qh * cos + qs[:, e * LANES:(e + 1) * LANES] * sin
            q_ref[2 * p + e] = (qh * Q_SCALE).astype(BF16)
    ckv_ref[...] = ckv
    kr_ref[...] = kr[:, QK_NOPE:QK_NOPE + QK_ROPE]
    _emit_kv(ckv, kr, wk_ref, wv_ref, k_ref, v_ref)


def _mla_proj_call(x, mods, li, cond_of_row, W, rope):
    m = x.shape[0]
    n1 = W["w1"].shape[1]
    n2 = W["w2"].shape[1]
    const = lambda i: (0, 0)
    pos_tiles = W["cos"].shape[0] // PROJ_TILE
    pos = lambda i: (i % pos_tiles, 0)
    return pl.pallas_call(
        functools.partial(_mla_proj_kernel, rope=rope),
        grid=(m // PROJ_TILE,),
        in_specs=[
            pl.BlockSpec((PROJ_TILE, D_MODEL), lambda i: (i, 0)),
            _mod_spec(li, cond_of_row, PROJ_TILE, 1),
            pl.BlockSpec((None, 1, D_MODEL), lambda i: (li, 0, 0)),
            pl.BlockSpec((D_MODEL, n1), const),
            pl.BlockSpec((1, Q_LORA), const),
            pl.BlockSpec((1, KV_LORA), const),
            pl.BlockSpec((Q_LORA, n2), const),
            pl.BlockSpec((KV_LORA, N_HEADS * LANES), const),
            pl.BlockSpec((KV_LORA, HEAD_PAIRS * LANES), const),
            pl.BlockSpec((PROJ_TILE, LANES), pos),
            pl.BlockSpec((PROJ_TILE, LANES), pos),
        ],
        out_specs=[
            pl.BlockSpec((N_HEADS, PROJ_TILE, LANES), lambda i: (0, i, 0)),
            pl.BlockSpec((N_HEADS, PROJ_TILE, LANES), lambda i: (0, i, 0)),
            pl.BlockSpec((HEAD_PAIRS, PROJ_TILE, LANES), lambda i: (0, i, 0)),
            pl.BlockSpec((PROJ_TILE, KV_LORA), lambda i: (i, 0)),
            pl.BlockSpec((PROJ_TILE, QK_ROPE), lambda i: (i, 0)),
        ],
        out_shape=[
            jax.ShapeDtypeStruct((N_HEADS, m, LANES), BF16),
            jax.ShapeDtypeStruct((N_HEADS, m, LANES), BF16),
            jax.ShapeDtypeStruct((HEAD_PAIRS, m, LANES), BF16),
            jax.ShapeDtypeStruct((m, KV_LORA), F32),
            jax.ShapeDtypeStruct((m, QK_ROPE), F32),
        ],
        compiler_params=_params(("arbitrary",)),
        name="mla_proj_rope" if rope else "mla_proj",
    )(x, mods, W["norm1_w"], W["w1"], W["q_norm"], W["kv_norm"], W["w2"], W["wk"], W["wv"],
      W["cos"], W["sin"])


def _kv_cache_kernel(ckv_ref, kr_ref, wk_ref, wv_ref, k_ref, v_ref):
    _emit_kv(ckv_ref[...], kr_ref[...], wk_ref, wv_ref, k_ref, v_ref)


def _kv_cache_call(ckv, krope_pad, W):
    m = ckv.shape[0]
    const = lambda i: (0, 0)
    return pl.pallas_call(
        _kv_cache_kernel,
        grid=(1,),
        in_specs=[
            pl.BlockSpec((m, KV_LORA), const),
            pl.BlockSpec((m, LANES), const),
            pl.BlockSpec((KV_LORA, N_HEADS * LANES), const),
            pl.BlockSpec((KV_LORA, HEAD_PAIRS * LANES), const),
        ],
        out_specs=[
            pl.BlockSpec((N_HEADS, m, LANES), lambda i: (0, 0, 0)),
            pl.BlockSpec((HEAD_PAIRS, m, LANES), lambda i: (0, 0, 0)),
        ],
        out_shape=[
            jax.ShapeDtypeStruct((N_HEADS, m, LANES), BF16),
            jax.ShapeDtypeStruct((HEAD_PAIRS, m, LANES), BF16),
        ],
        compiler_params=_params(("arbitrary",)),
        name="kv_cache",
    )(ckv, krope_pad, W["wk"], W["wv"])


def _attn_kernel(q_ref, *refs, n_pairs):
    o_ref, kv = refs[-1], refs[:-1]
    parts = [(kv[i], kv[i + 1]) for i in range(0, len(kv), 2)]
    heads = [(p, e) for p in range(n_pairs) for e in range(2)]

    def scores(p, e):
        q = q_ref[2 * p + e]
        return [lax.dot_general(q, k_ref[2 * p + e], (((1,), (1,)), ((), ())), preferred_element_type=F32)
                for k_ref, _ in parts]

    outs = []
    s_next = scores(*heads[0])
    for i, (p, e) in enumerate(heads):
        s = s_next
        if i + 1 < len(heads):
            s_next = scores(*heads[i + 1])
        m = functools.reduce(jnp.maximum, [jnp.max(x, axis=-1, keepdims=True) for x in s])
        w = [jnp.exp2(x - m) for x in s]
        l = sum(jnp.sum(x, axis=-1, keepdims=True) for x in w)
        pv = sum(_dot(x.astype(BF16), v_ref[p]) for x, (_, v_ref) in zip(w, parts))
        outs.append(pv / l)
        if e == 1:
            lane = lax.broadcasted_iota(jnp.int32, pv.shape, 1)
            o_ref[:, p * LANES:(p + 1) * LANES] = jnp.where(lane < V_HEAD, outs[-2], outs[-1]).astype(BF16)


def _attn_call(q, kv_parts, q_tile, n_pairs):
    _, b, t, _ = q.shape
    nq = t // q_tile
    in_specs = [pl.BlockSpec((2 * n_pairs, None, q_tile, LANES), lambda bi, g, qi: (g, bi, qi, 0))]
    operands = [q]
    for k, v in kv_parts:
        s = k.shape[2]
        in_specs += [pl.BlockSpec((2 * n_pairs, None, s, LANES), lambda bi, g, qi: (g, bi, 0, 0)),
                     pl.BlockSpec((n_pairs, None, s, LANES), lambda bi, g, qi: (g, bi, 0, 0))]
        operands += [k, v]
    return pl.pallas_call(
        functools.partial(_attn_kernel, n_pairs=n_pairs),
        grid=(b, HEAD_PAIRS // n_pairs, nq),
        in_specs=in_specs,
        out_specs=pl.BlockSpec((q_tile, n_pairs * LANES), lambda bi, g, qi: (bi * nq + qi, g)),
        out_shape=jax.ShapeDtypeStruct((b * t, N_HEADS * V_HEAD), BF16),
        compiler_params=_params(("arbitrary", "arbitrary", "arbitrary"), ATTN_VMEM),
        name=f"attn_{len(kv_parts)}part",
    )(*operands)


def _pool_kernel(x_ref, mod_ref, nw_ref, w_ref, sc_ref, o_ref, band_scr, y_scr, *, seq):
    @pl.when(pl.program_id(0) == 0)
    def _():
        r = lax.broadcasted_iota(jnp.int32, (seq, seq), 0)
        c = lax.broadcasted_iota(jnp.int32, (seq, seq), 1)
        for g, win in enumerate(POOL_WINDOWS):
            inside = (c >= r - win // 2) & (c < r + win - win // 2)
            band_scr[g] = jnp.where(inside, 1.0, 0.0).astype(BF16)

    x = x_ref[...]
    h = _norm_mod(x, nw_ref[...], mod_ref[...], 0)
    h_hi = h.astype(BF16)
    h_lo = (h - h_hi.astype(F32)).astype(BF16)
    t = lax.broadcasted_iota(jnp.int32, (seq, 1), 0)
    for g, win in enumerate(POOL_WINDOWS):
        cnt = (jnp.minimum(t + (win - win // 2), seq) - jnp.maximum(t - win // 2, 0)).astype(F32)
        cols = slice(g * GROUP, (g + 1) * GROUP)
        wg = w_ref[g].astype(BF16)
        for s in range(x.shape[0] // seq):
            rows = slice(s * seq, (s + 1) * seq)
            tot = _dot(band_scr[g], h_hi[rows, cols]) + _dot(band_scr[g], h_lo[rows, cols])
            pooled = (tot / cnt - h[rows, cols]).astype(BF16)
            y_scr[rows, cols] = _dot(pooled, wg)
    o_ref[...] = x + mod_ref[2:3, :] * (y_scr[...] * sc_ref[...])


def _pool_call(x, mods, li, cond_of_row, seq, P):
    m = x.shape[0]
    lj = li // 4
    return pl.pallas_call(
        functools.partial(_pool_kernel, seq=seq),
        grid=(m // ROW_TILE,),
        in_specs=[
            pl.BlockSpec((ROW_TILE, D_MODEL), lambda i: (i, 0)),
            _mod_spec(li, cond_of_row, ROW_TILE, 1),
            pl.BlockSpec((None, 1, D_MODEL), lambda i: (li, 0, 0)),
            pl.BlockSpec((None, N_GROUPS, GROUP, GROUP), lambda i: (lj, 0, 0, 0)),
            pl.BlockSpec((None, 1, D_MODEL), lambda i: (lj, 0, 0)),
        ],
        out_specs=pl.BlockSpec((ROW_TILE, D_MODEL), lambda i: (i, 0)),
        out_shape=jax.ShapeDtypeStruct((m, D_MODEL), F32),
        scratch_shapes=[pltpu.VMEM((N_GROUPS, seq, seq), BF16), pltpu.VMEM((ROW_TILE, D_MODEL), F32)],
        compiler_params=_params(("arbitrary",)),
        name=f"pool_t{seq}",
    )(x, mods, P["norm1_w"], P["pool_w"], P["pool_scale"])


def _dft_tables(seq):
    def cs(n):
        k = np.arange(n, dtype=np.int64)
        ang = 2.0 * np.pi * ((k[:, None] * k[None, :]) % n) / n
        return np.cos(ang), np.sin(ang)
    ct, st = cs(seq)
    cc, sc = cs(GROUP)
    e_t = jnp.asarray(np.concatenate([ct, st], axis=0), dtype=F32)
    e_c = jnp.asarray(np.concatenate([cc, -sc], axis=0), dtype=F32)
    return e_t.astype(BF16), e_c.astype(BF16)


def _fourier_kernel(x_ref, mod_ref, nw_ref, et_ref, ec_ref, w_ref, o_ref, f_scr, *, seq):
    x = x_ref[...]
    h = _norm_mod(x, nw_ref[...], mod_ref[...], 0).astype(BF16)
    norm = np.float32(1.0 / math.sqrt(seq * GROUP))
    for s in range(x.shape[0] // seq):
        rows = slice(s * seq, (s + 1) * seq)
        for g in range(N_GROUPS):
            cols = slice(g * GROUP, (g + 1) * GROUP)
            y = _dot(et_ref[...], h[rows, cols]).astype(BF16)
            f = _dot(y[:seq], ec_ref[:GROUP, :]) + _dot(y[seq:], ec_ref[GROUP:, :])
            f_scr[rows, cols] = (f * norm).astype(BF16)
    o_ref[...] = x + mod_ref[2:3, :] * _dot(f_scr[...], w_ref[...].astype(BF16))


def _fourier_call(x, mods, li, cond_of_row, seq, P):
    m = x.shape[0]
    lj = li // 4
    e_t, e_c = _dft_tables(seq)
    return pl.pallas_call(
        functools.partial(_fourier_kernel, seq=seq),
        grid=(m // ROW_TILE,),
        in_specs=[
            pl.BlockSpec((ROW_TILE, D_MODEL), lambda i: (i, 0)),
            _mod_spec(li, cond_of_row, ROW_TILE, 1),
            pl.BlockSpec((None, 1, D_MODEL), lambda i: (li, 0, 0)),
            pl.BlockSpec((2 * seq, seq), lambda i: (0, 0)),
            pl.BlockSpec((2 * GROUP, GROUP), lambda i: (0, 0)),
            pl.BlockSpec((None, D_MODEL, D_MODEL), lambda i: (lj, 0, 0)),
        ],
        out_specs=pl.BlockSpec((ROW_TILE, D_MODEL), lambda i: (i, 0)),
        out_shape=jax.ShapeDtypeStruct((m, D_MODEL), F32),
        scratch_shapes=[pltpu.VMEM((ROW_TILE, D_MODEL), BF16)],
        compiler_params=_params(("arbitrary",)),
        name=f"fourier_t{seq}",
    )(x, mods, P["norm1_w"], e_t, e_c, P["fnet_w"])


def _rope_tables(rows):
    r, col = jnp.meshgrid(jnp.arange(rows), jnp.arange(GRID_W), indexing="ij")
    r = r.reshape(-1).astype(F32)
    col = col.reshape(-1).astype(F32)
    n_freq = QK_ROPE // 4
    inv = ROPE_THETA ** (-jnp.arange(n_freq, dtype=F32) / n_freq)
    ang = jnp.concatenate([r[:, None] * inv, col[:, None] * inv], axis=-1)
    cos = jnp.repeat(jnp.cos(ang), 2, axis=-1)
    sin = jnp.repeat(jnp.sin(ang), 2, axis=-1)
    t = cos.shape[0]
    pad = LANES - QK_NOPE - QK_ROPE
    cos_p = jnp.concatenate([jnp.ones((t, QK_NOPE), F32), cos, jnp.ones((t, pad), F32)], axis=-1)
    sin_p = jnp.concatenate([jnp.zeros((t, QK_NOPE), F32), sin, jnp.zeros((t, pad), F32)], axis=-1)
    return cos_p, sin_p


def _pair_swap(w):
    wr = w.reshape(w.shape[:-1] + (w.shape[-1] // 2, 2))
    return jnp.stack([-wr[..., 1], wr[..., 0]], axis=-1).reshape(w.shape)


def _mla_weights(P, j, rope, seq_rows):
    pad = LANES - QK_NOPE - QK_ROPE
    wuq = P["mla_wuq"][j].reshape(Q_LORA, N_HEADS, QK_NOPE + QK_ROPE)
    q_slots = jnp.pad(wuq, ((0, 0), (0, 0), (0, pad))).reshape(Q_LORA, N_HEADS * LANES)
    wdkv = P["mla_wdkv"][j]
    kr_slot = jnp.pad(wdkv[:, KV_LORA:], ((0, 0), (QK_NOPE, pad)))
    parts1 = [P["mla_wdq"][j], wdkv[:, :KV_LORA], kr_slot]
    parts2 = [q_slots]
    if rope:
        parts1.append(jnp.pad(_pair_swap(wdkv[:, KV_LORA:]), ((0, 0), (QK_NOPE, pad))))
        q_sw = jnp.pad(_pair_swap(wuq[:, :, QK_NOPE:]), ((0, 0), (0, 0), (QK_NOPE, pad)))
        parts2.append(q_sw.reshape(Q_LORA, N_HEADS * LANES))
        cos, sin = _rope_tables(seq_rows // GRID_W)
    else:
        cos = sin = jnp.zeros((PROJ_TILE, LANES), F32)
    wukv = P["mla_wukv"][j].reshape(KV_LORA, N_HEADS, QK_NOPE + V_HEAD)
    wk = jnp.pad(wukv[:, :, :QK_NOPE], ((0, 0), (0, 0), (0, LANES - QK_NOPE)))
    return dict(
        norm1_w=P["norm1_w"], q_norm=P["mla_q_norm"][j][None, :], kv_norm=P["mla_kv_norm"][j][None, :],
        w1=jnp.concatenate(parts1, axis=1), w2=jnp.concatenate(parts2, axis=1),
        wk=wk.reshape(KV_LORA, N_HEADS * LANES),
        wv=wukv[:, :, QK_NOPE:].reshape(KV_LORA, N_HEADS * V_HEAD), cos=cos, sin=sin)


def _mla_layer(x, mods, li, cond_of_row, batch, seq, P, cache):
    j = li // 4
    W = _mla_weights(P, j, cache is not None, seq)
    q, k, v, ckv, krope = _mla_proj_call(x, mods, li, cond_of_row, W, cache is not None)
    q = q.reshape(N_HEADS, batch, seq, LANES)
    k = k.reshape(N_HEADS, batch, seq, LANES)
    v = v.reshape(HEAD_PAIRS, batch, seq, LANES)
    if cache is None:
        o = _attn_call(q, [(k, v)], seq, HEAD_PAIRS)
    else:
        c_ckv, c_krope = cache
        past = c_ckv.shape[1]
        pad = LANES - QK_NOPE - QK_ROPE
        kr_pad = jnp.pad(c_krope.reshape(batch * past, QK_ROPE), ((0, 0), (QK_NOPE, pad)))
        kc, vc = _kv_cache_call(c_ckv.reshape(batch * past, KV_LORA), kr_pad, W)
        cached = (kc.reshape(N_HEADS, batch, past, LANES), vc.reshape(HEAD_PAIRS, batch, past, LANES))
        o = _attn_call(q, [cached, (k, v)], 512, 4)
    return o, ckv, krope


def _mixer(x, mods, li, cond_of_row, batch, seq, P, cache):
    kind = li % 4
    if kind == 0:
        o, ckv, krope = _mla_layer(x, mods, li, cond_of_row, batch, seq, P, cache)
        return x, (o, P["mla_wo_bf16"], li // 4), (ckv, krope)
    if kind == 1:
        return _pool_call(x, mods, li, cond_of_row, seq, P), None, None
    if kind == 2:
        return _fourier_call(x, mods, li, cond_of_row, seq, P), None, None
    return _sconv_call(x, mods, li, cond_of_row, seq, P), None, None


def _run_passes(passes, mods, P):
    new_cache = None
    for li in range(DEPTH):
        pending = []
        for ps in passes:
            ps["x"], pre, made = _mixer(ps["x"], mods, li, ps["cond_of_row"], ps["batch"], ps["seq"], P,
                                        ps["cache"])
            pending.append(pre)
            if made is not None and ps["cache"] is None:
                new_cache = made
        xs, outs = lax.optimization_barrier(([ps["x"] for ps in passes],
                                              [None if pre is None else pre[0] for pre in pending]))
        pending = [None if pre is None else (o,) + pre[1:] for pre, o in zip(pending, outs)]
        xs = [_ffn_call(x, mods, li, ps["cond_of_row"], ps["seq"], P, pre, final_norm=(li == DEPTH - 1))
              for x, ps, pre in zip(xs, passes, pending)]
        xs = lax.optimization_barrier(xs)
        for ps, x in zip(passes, xs):
            ps["x"] = x
    return new_cache


def kernel(x_prompt, x_sample, cache_ckv, cache_krope, c, c_ctx, norm1_w, norm2_w, ada_w, ada_b, mla_wdq, mla_q_norm, mla_wuq, mla_wdkv, mla_kv_norm, mla_wukv, mla_wo, pool_w, pool_scale, fnet_w, sconv_win, sconv_conv, sconv_wout, ffn_up, ffn_conv_w, ffn_conv_b, ffn_down, final_norm_w):
    row = lambda w: w.reshape(w.shape[0], 1, w.shape[1])
    P = dict(norm1_w=row(norm1_w), norm2_w=row(norm2_w), mla_wdq=mla_wdq, mla_q_norm=mla_q_norm,
             mla_wuq=mla_wuq, mla_wdkv=mla_wdkv, mla_kv_norm=mla_kv_norm, mla_wukv=mla_wukv,
             mla_wo_bf16=mla_wo.astype(BF16),
             pool_w=pool_w, pool_scale=row(pool_scale), fnet_w=fnet_w, sconv_win=sconv_win,
             sconv_conv=sconv_conv, sconv_wout=sconv_wout, ffn_conv_w=ffn_conv_w,
             ffn_conv_b=ffn_conv_b, ffn_down=ffn_down, final_norm_w=final_norm_w,
             ffn_w=functools.cache(lambda l: _layer_weights(ffn_up, ffn_down, l, 2, f"ffn{l}_weights")),
             sconv_w=functools.cache(lambda l: _layer_weights(sconv_win, sconv_wout, l, 3, f"sconv{l}_weights")))
    b_ctx, t_ctx, _ = x_prompt.shape
    b_lat, t_lat, _ = x_sample.shape
    assert ROW_TILE % t_ctx == 0 and t_lat % ROW_TILE == 0 and b_lat + 1 <= N_COND

    conds = jnp.concatenate([c_ctx[None, :], c, jnp.zeros((N_COND - 1 - b_lat, D_MODEL), F32)], axis=0)
    mods = _ada_call(conds, ada_w, ada_b).reshape(DEPTH, N_COND, 6, D_MODEL)

    ctx = dict(x=x_prompt.reshape(b_ctx * t_ctx, D_MODEL), cond_of_row=lambda r: 0, batch=b_ctx, seq=t_ctx,
               cache=None)
    lat = dict(x=x_sample.reshape(b_lat * t_lat, D_MODEL), cond_of_row=lambda r: 1 + r // t_lat, batch=b_lat,
               seq=t_lat, cache=(cache_ckv[:, 0], cache_krope[:, 0]))
    ckv, krope = _run_passes([ctx, lat], mods, P)
    return (ctx["x"].reshape(b_ctx, t_ctx, D_MODEL), lat["x"].reshape(b_lat, t_lat, D_MODEL),
            ckv.reshape(b_ctx, 1, t_ctx, KV_LORA), krope.reshape(b_ctx, 1, t_ctx, QK_ROPE))
```

```python
import functools
import math

import numpy as np
import jax
import jax.numpy as jnp
from jax import lax
from jax.experimental import pallas as pl
from jax.experimental.pallas import tpu as pltpu

F32 = jnp.float32
BF16 = jnp.bfloat16

D_MODEL = 1024
N_HEADS = 16
QK_NOPE = 64
QK_ROPE = 32
V_HEAD = 64
Q_LORA = 384
KV_LORA = 256
ROPE_THETA = 10000.0
GRID_W = 64
POOL_WINDOWS = (2, 4, 8, 16)
N_GROUPS = 4
GROUP = D_MODEL // N_GROUPS
D_FF = 2816
EPS = 1e-6
DEPTH = 4

LANES = 128
Q_SCALE = np.float32(math.log2(math.e) / math.sqrt(QK_NOPE + QK_ROPE))
HEAD_PAIRS = N_HEADS // 2
N_COND = 8
ROW_TILE = 1024
PROJ_TILE = 512
FF_TILE = 256
ROW_BLOCK = 256
HALO = 8
RELAYOUT_ROWS = 128
ADA_TILE = 1536
MIB = 1024 * 1024
VMEM_DEFAULT = 44 * MIB
VMEM_HEADROOM = 4 * MIB
ATTN_VMEM = 44 * MIB


def _params(sem, vmem_bytes=VMEM_DEFAULT):
    return pltpu.CompilerParams(dimension_semantics=sem, vmem_limit_bytes=int(vmem_bytes))


def _nbytes(a):
    return math.prod(a.shape) * jnp.dtype(a.dtype).itemsize


def _dot(a, b):
    return jnp.dot(a, b, preferred_element_type=F32)


def _rms(x, w):
    return x * lax.rsqrt(jnp.mean(x * x, axis=-1, keepdims=True) + EPS) * w


def _norm_mod(x, nw, mod, base):
    return _rms(x, nw) * (1.0 + mod[base + 1:base + 2, :]) + mod[base:base + 1, :]


def _gelu_exact(x):
    return 0.5 * x * (1.0 + lax.erf(x * np.float32(math.sqrt(0.5))))


def _ada_kernel(c_ref, w_ref, b_ref, o_ref):
    s = jax.nn.silu(c_ref[...]).astype(BF16)
    o_ref[...] = _dot(s, w_ref[...].astype(BF16)) + b_ref[...]


def _ada_call(conds, ada_w, ada_b):
    n_out = ada_w.shape[-1]
    return pl.pallas_call(
        _ada_kernel,
        grid=(DEPTH, n_out // ADA_TILE),
        in_specs=[
            pl.BlockSpec((N_COND, D_MODEL), lambda l, j: (0, 0)),
            pl.BlockSpec((None, D_MODEL, ADA_TILE), lambda l, j: (l, 0, j)),
            pl.BlockSpec((None, 1, ADA_TILE), lambda l, j: (l, 0, j)),
        ],
        out_specs=pl.BlockSpec((None, N_COND, ADA_TILE), lambda l, j: (l, 0, j)),
        out_shape=jax.ShapeDtypeStruct((DEPTH, N_COND, n_out), F32),
        compiler_params=_params(("arbitrary", "arbitrary"), 2 * D_MODEL * ADA_TILE * 4 + VMEM_HEADROOM),
        name="ada",
    )(conds, ada_w, ada_b.reshape(DEPTH, 1, n_out))


def _mod_spec(li, cond_of_row, tile, n_grid):
    if n_grid == 1:
        return pl.BlockSpec((None, None, 6, D_MODEL), lambda i: (li, cond_of_row(i * tile), 0, 0))
    return pl.BlockSpec((None, None, 6, D_MODEL), lambda i, j: (li, cond_of_row(i * tile), 0, 0))


def _conv_window(w, cw, t0, seq):
    n = w.shape[0]
    mid = slice(HALO, HALO + ROW_BLOCK)
    t = (t0 + lax.broadcasted_iota(jnp.int32, (ROW_BLOCK, 1), 0)) & (seq - 1)
    prev = jnp.where(t == 0, 0.0, pltpu.roll(w, 1, axis=0)[mid])
    nxt = jnp.where(t == seq - 1, 0.0, pltpu.roll(w, n - 1, axis=0)[mid])
    return prev * cw[0:1, :] + w[mid] * cw[1:2, :] + nxt * cw[2:3, :]


def _ffn_act(win, mid, cw, cb, t0, seq):
    return _gelu_exact(_conv_window(win(0), cw, t0, seq) + cb) * mid(1)


def _sconv_act(win, mid, cw, cb, t0, seq):
    return mid(0) * _conv_window(win(1) * win(2), cw, t0, seq)


def _ffn_kernel(x_ref, mod_ref, nw_ref, up_ref, cw_ref, cb_ref, down_ref, fnw_ref, o_ref, *scratch, **cfg):
    _glu_body(x_ref, mod_ref, nw_ref, up_ref, cw_ref, cb_ref, down_ref, fnw_ref, None, o_ref, *scratch,
              n_up=2, act=_ffn_act, mod_base=3, **cfg)


def _ffn_pre_kernel(x_ref, mod_ref, nw_ref, up_ref, cw_ref, cb_ref, down_ref, fnw_ref, a_ref, wa_ref, o_ref,
                    *scratch, **cfg):
    _glu_body(x_ref, mod_ref, nw_ref, up_ref, cw_ref, cb_ref, down_ref, fnw_ref, (a_ref, wa_ref), o_ref,
              *scratch, n_up=2, act=_ffn_act, mod_base=3, **cfg)


def _sconv_kernel(x_ref, mod_ref, nw_ref, up_ref, cw_ref, down_ref, o_ref, *scratch, **cfg):
    _glu_body(x_ref, mod_ref, nw_ref, up_ref, cw_ref, None, down_ref, None, None, o_ref, *scratch,
              n_up=3, act=_sconv_act, mod_base=0, final_norm=False, **cfg)


def _glu_body(x_ref, mod_ref, nw_ref, up_ref, cw_ref, cb_ref, down_ref, fnw_ref, pre, o_ref,
              h_scr, acc_scr, gu0, gu1, *, n_up, n_chunks, act, seq, mod_base, final_norm):
    gu = (gu0, gu1)
    n_blocks = ROW_TILE // ROW_BLOCK

    def up(c, slot, r):
        rows = slice(r * ROW_BLOCK, (r + 1) * ROW_BLOCK)
        res = _dot(h_scr[rows, :], up_ref[c])
        for p in range(n_up):
            gu[slot][p, HALO + r * ROW_BLOCK:HALO + (r + 1) * ROW_BLOCK, :] = res[:, p * FF_TILE:(p + 1) * FF_TILE]

    def activate(c, slot, r):
        win = lambda p: gu[slot][p, r * ROW_BLOCK:(r + 1) * ROW_BLOCK + 2 * HALO, :]
        mid = lambda p: gu[slot][p, HALO + r * ROW_BLOCK:HALO + (r + 1) * ROW_BLOCK, :]
        cb = None if cb_ref is None else cb_ref[c]
        return act(win, mid, cw_ref[c], cb, r * ROW_BLOCK, seq).astype(BF16)

    def down(a, c, r):
        rows = slice(r * ROW_BLOCK, (r + 1) * ROW_BLOCK)
        acc_scr[rows, :] += _dot(a, down_ref[c])

    def stage(s, slot):
        for r in range(n_blocks):
            a = activate(s - 1, 1 - slot, r)
            up(s, slot, r)
            down(a, s - 1, r)

    def block(r):
        return slice(r * ROW_BLOCK, (r + 1) * ROW_BLOCK)

    def mixer_proj(r):
        a_ref, wa_ref = pre
        return _dot(a_ref[block(r), :], wa_ref[...])

    for g in gu:
        g[:, :HALO, :] = jnp.zeros((n_up, HALO, FF_TILE), F32)
        g[:, HALO + ROW_TILE:, :] = jnp.zeros((n_up, HALO, FF_TILE), F32)

    proj = mixer_proj(0) if pre is not None else None
    for r in range(n_blocks):
        x = x_ref[block(r), :]
        if pre is not None:
            x = x + mod_ref[2:3, :] * proj
            o_ref[block(r), :] = x
            if r + 1 < n_blocks:
                proj = mixer_proj(r + 1)
        h_scr[block(r), :] = _norm_mod(x, nw_ref[...], mod_ref[...], mod_base).astype(BF16)
        acc_scr[block(r), :] = jnp.zeros((ROW_BLOCK, D_MODEL), F32)
        up(0, 0, r)

    n_pairs = (n_chunks - 1) // 2

    def pair(i, carry):
        s = 2 * i + 1
        stage(s, 1)
        stage(s + 1, 0)
        return carry

    lax.fori_loop(0, n_pairs, pair, 0)
    for s in range(2 * n_pairs + 1, n_chunks):
        stage(s, s % 2)
    last = n_chunks - 1
    res = x_ref if pre is None else o_ref
    for r in range(n_blocks):
        down(activate(last, last % 2, r), last, r)
        y = res[block(r), :] + mod_ref[mod_base + 2:mod_base + 3, :] * acc_scr[block(r), :]
        if final_norm:
            y = _rms(y, fnw_ref[...])
        o_ref[block(r), :] = y


def _chunked(w, n_chunks):
    return w.reshape(w.shape[0], n_chunks, FF_TILE).transpose(1, 0, 2)


def _glu_call(kernel_fn, name, x, mods, li, cond_of_row, n_up, n_chunks, norm_w, operands, specs, cfg):
    m = x.shape[0]
    whole = lambda a: pl.BlockSpec(a.shape, lambda i: (0,) * a.ndim)
    in_specs = [pl.BlockSpec((ROW_TILE, D_MODEL), lambda i: (i, 0)),
                _mod_spec(li, cond_of_row, ROW_TILE, 1),
                pl.BlockSpec((None, 1, D_MODEL), lambda i: (li, 0, 0))]
    layer_w = lambda a, l: pl.BlockSpec((None,) + a.shape[1:], lambda i: (l,) + (0,) * (a.ndim - 1),
                                        pipeline_mode=pl.Buffered(1))
    rows = lambda a: pl.BlockSpec((ROW_TILE, a.shape[1]), lambda i: (i, 0))
    spec_of = {None: whole, "rows": rows, "resident": lambda a: pl.BlockSpec(memory_space=pltpu.VMEM)}
    in_specs += [layer_w(a, s) if isinstance(s, int) else spec_of[s](a) for a, s in zip(operands, specs)]
    gu_shape = (n_up, ROW_TILE + 2 * HALO, FF_TILE)
    tile = ROW_TILE * D_MODEL
    window = {None: _nbytes, "rows": lambda a: 2 * ROW_TILE * a.shape[1] * a.dtype.itemsize,
              "resident": lambda a: 0}
    vmem = (2 * 2 * tile * 4 + tile * 2 + tile * 4 + 2 * math.prod(gu_shape) * 4 + VMEM_HEADROOM
            + sum(_nbytes(a) // a.shape[0] if isinstance(s, int) else window[s](a)
                  for a, s in zip(operands, specs)))
    return pl.pallas_call(
        functools.partial(kernel_fn, n_chunks=n_chunks, **cfg),
        grid=(m // ROW_TILE,),
        in_specs=in_specs,
        out_specs=pl.BlockSpec((ROW_TILE, D_MODEL), lambda i: (i, 0)),
        out_shape=jax.ShapeDtypeStruct((m, D_MODEL), F32),
        scratch_shapes=[
            pltpu.VMEM((ROW_TILE, D_MODEL), BF16),
            pltpu.VMEM((ROW_TILE, D_MODEL), F32),
            pltpu.VMEM(gu_shape, F32), pltpu.VMEM(gu_shape, F32),
        ],
        compiler_params=_params(("arbitrary",), vmem),
        name=name,
    )(x, mods, norm_w, *operands)


def _in_vmem(a):
    return pltpu.with_memory_space_constraint(a, pltpu.VMEM)


def _layer_weights_kernel(up_ref, dn_ref, uo_ref, do_ref, *, n_up, n_chunks):
    for c in range(n_chunks):
        for p in range(n_up):
            col = (p * n_chunks + c) * FF_TILE
            uo_ref[c, :, p * FF_TILE:(p + 1) * FF_TILE] = up_ref[:, col:col + FF_TILE].astype(BF16)
    do_ref[...] = dn_ref[...].astype(BF16)


def _layer_weights(up, down, layer, n_up, name):
    _, d, cols = up.shape
    _, hidden, d_out = down.shape
    n_chunks = hidden // FF_TILE
    steps = d // RELAYOUT_ROWS
    dn_rows = hidden // steps
    up_cm, dn = pl.pallas_call(
        functools.partial(_layer_weights_kernel, n_up=n_up, n_chunks=n_chunks),
        grid=(steps,),
        in_specs=[pl.BlockSpec((None, RELAYOUT_ROWS, cols), lambda k: (layer, k, 0)),
                  pl.BlockSpec((None, dn_rows, d_out), lambda k: (layer, k, 0))],
        out_specs=[pl.BlockSpec((n_chunks, RELAYOUT_ROWS, n_up * FF_TILE), lambda k: (0, k, 0)),
                   pl.BlockSpec((dn_rows, d_out), lambda k: (k, 0))],
        out_shape=[jax.ShapeDtypeStruct((n_chunks, d, n_up * FF_TILE), BF16),
                   jax.ShapeDtypeStruct((hidden, d_out), BF16)],
        compiler_params=_params(("arbitrary",),
                                2 * (RELAYOUT_ROWS * cols + dn_rows * d_out) * (4 + 2) + VMEM_HEADROOM),
        name=name,
    )(up, down)
    return _in_vmem(up_cm), _in_vmem(dn.reshape(n_chunks, FF_TILE, d_out))


def _ffn_call(x, mods, li, cond_of_row, seq, P, pre, final_norm):
    n = D_FF // FF_TILE
    up_cm, down_cm = P["ffn_w"]
    operands = [up_cm, _chunked(P["ffn_conv_w"][li], n), _chunked(P["ffn_conv_b"][li][None, :], n),
                down_cm, P["final_norm_w"].reshape(1, D_MODEL)]
    specs = ["resident", None, None, "resident", None]
    kernel_fn = _ffn_kernel
    if pre is not None:
        a, w_a, l_a = pre
        operands += [a, w_a]
        specs += ["rows", l_a]
        kernel_fn = _ffn_pre_kernel
    return _glu_call(kernel_fn, f"ffn{li}", x, mods, li, cond_of_row, 2, n, P["norm2_w"],
                     operands, specs, dict(seq=seq, final_norm=final_norm))


def _sconv_call(x, mods, li, cond_of_row, seq, P):
    lj = li // 4
    n = D_MODEL // FF_TILE
    up_cm, down_cm = P["sconv_w"]
    operands = [up_cm, _chunked(P["sconv_conv"][lj], n), down_cm]
    return _glu_call(_sconv_kernel, f"sconv{li}", x, mods, li, cond_of_row, 3, n, P["norm1_w"],
                     operands, ["resident", None, "resident"], dict(seq=seq))


def _emit_kv(ckv, krope_pad, wk_ref, wv_ref, k_ref, v_ref):
    c = ckv.astype(BF16)
    kx = _dot(c, wk_ref[...].astype(BF16))
    vx = _dot(c, wv_ref[...].astype(BF16))
    for h in range(N_HEADS):
        k_ref[h] = (kx[:, h * LANES:(h + 1) * LANES] + krope_pad).astype(BF16)
    for p in range(HEAD_PAIRS):
        v_ref[p] = vx[:, p * LANES:(p + 1) * LANES].astype(BF16)


def _mla_proj_kernel(x_ref, mod_ref, nw_ref, w1_ref, qn_ref, kvn_ref, w2_ref, wk_ref, wv_ref, cos_ref,
                     sin_ref, q_ref, k_ref, v_ref, ckv_ref, kr_ref, *, rope):
    h = _norm_mod(x_ref[...], nw_ref[...], mod_ref[...], 0).astype(BF16)
    z = _dot(h, w1_ref[...].astype(BF16))
    cq = _rms(z[:, :Q_LORA], qn_ref[...]).astype(BF16)
    ckv = _rms(z[:, Q_LORA:Q_LORA + KV_LORA], kvn_ref[...])
    kr = z[:, Q_LORA + KV_LORA:Q_LORA + KV_LORA + LANES]
    if rope:
        cos, sin = cos_ref[...], sin_ref[...]
        kr = kr * cos + z[:, Q_LORA + KV_LORA + LANES:] * sin
    for p in range(HEAD_PAIRS):
        q2 = _dot(cq, w2_ref[:, 2 * p * LANES:(2 * p + 2) * LANES].astype(BF16))
        if rope:
            off = N_HEADS * LANES
            qs = _dot(cq, w2_ref[:, off + 2 * p * LANES:off + (2 * p + 2) * LANES].astype(BF16))
        for e in range(2):
            qh = q2[:, e * LANES:(e + 1) * LANES]
            if rope:
                qh = qh * cos + qs[:, e * LANES:(e + 1) * LANES] * sin
            q_ref[2 * p + e] = (qh * Q_SCALE).astype(BF16)
    ckv_ref[...] = ckv
    kr_ref[...] = kr[:, QK_NOPE:QK_NOPE + QK_ROPE]
    _emit_kv(ckv, kr, wk_ref, wv_ref, k_ref, v_ref)


def _mla_proj_call(x, mods, li, cond_of_row, W, rope):
    m = x.shape[0]
    n1 = W["w1"].shape[1]
    n2 = W["w2"].shape[1]
    const = lambda i: (0, 0)
    pos_tiles = W["cos"].shape[0] // PROJ_TILE
    pos = lambda i: (i % pos_tiles, 0)
    return pl.pallas_call(
        functools.partial(_mla_proj_kernel, rope=rope),
        grid=(m // PROJ_TILE,),
        in_specs=[
            pl.BlockSpec((PROJ_TILE, D_MODEL), lambda i: (i, 0)),
            _mod_spec(li, cond_of_row, PROJ_TILE, 1),
            pl.BlockSpec((None, 1, D_MODEL), lambda i: (li, 0, 0)),
            pl.BlockSpec((D_MODEL, n1), const),
            pl.BlockSpec((1, Q_LORA), const),
            pl.BlockSpec((1, KV_LORA), const),
            pl.BlockSpec((Q_LORA, n2), const),
            pl.BlockSpec((KV_LORA, N_HEADS * LANES), const),
            pl.BlockSpec((KV_LORA, HEAD_PAIRS * LANES), const),
            pl.BlockSpec((PROJ_TILE, LANES), pos),
            pl.BlockSpec((PROJ_TILE, LANES), pos),
        ],
        out_specs=[
            pl.BlockSpec((N_HEADS, PROJ_TILE, LANES), lambda i: (0, i, 0)),
            pl.BlockSpec((N_HEADS, PROJ_TILE, LANES), lambda i: (0, i, 0)),
            pl.BlockSpec((HEAD_PAIRS, PROJ_TILE, LANES), lambda i: (0, i, 0)),
            pl.BlockSpec((PROJ_TILE, KV_LORA), lambda i: (i, 0)),
            pl.BlockSpec((PROJ_TILE, QK_ROPE), lambda i: (i, 0)),
        ],
        out_shape=[
            jax.ShapeDtypeStruct((N_HEADS, m, LANES), BF16),
            jax.ShapeDtypeStruct((N_HEADS, m, LANES), BF16),
            jax.ShapeDtypeStruct((HEAD_PAIRS, m, LANES), BF16),
            jax.ShapeDtypeStruct((m, KV_LORA), F32),
            jax.ShapeDtypeStruct((m, QK_ROPE), F32),
        ],
        compiler_params=_params(("arbitrary",)),
        name="mla_proj_rope" if rope else "mla_proj",
    )(x, mods, W["norm1_w"], W["w1"], W["q_norm"], W["kv_norm"], W["w2"], W["wk"], W["wv"],
      W["cos"], W["sin"])


def _kv_cache_kernel(ckv_ref, kr_ref, wk_ref, wv_ref, k_ref, v_ref):
    _emit_kv(ckv_ref[...], kr_ref[...], wk_ref, wv_ref, k_ref, v_ref)


def _kv_cache_call(ckv, krope_pad, W):
    m = ckv.shape[0]
    const = lambda i: (0, 0)
    return pl.pallas_call(
        _kv_cache_kernel,
        grid=(1,),
        in_specs=[
            pl.BlockSpec((m, KV_LORA), const),
            pl.BlockSpec((m, LANES), const),
            pl.BlockSpec((KV_LORA, N_HEADS * LANES), const),
            pl.BlockSpec((KV_LORA, HEAD_PAIRS * LANES), const),
        ],
        out_specs=[
            pl.BlockSpec((N_HEADS, m, LANES), lambda i: (0, 0, 0)),
            pl.BlockSpec((HEAD_PAIRS, m, LANES), lambda i: (0, 0, 0)),
        ],
        out_shape=[
            jax.ShapeDtypeStruct((N_HEADS, m, LANES), BF16),
            jax.ShapeDtypeStruct((HEAD_PAIRS, m, LANES), BF16),
        ],
        compiler_params=_params(("arbitrary",)),
        name="kv_cache",
    )(ckv, krope_pad, W["wk"], W["wv"])


def _attn_kernel(q_ref, *refs, n_pairs):
    o_ref, kv = refs[-1], refs[:-1]
    parts = [(kv[i], kv[i + 1]) for i in range(0, len(kv), 2)]
    heads = [(p, e) for p in range(n_pairs) for e in range(2)]

    def scores(p, e):
        q = q_ref[2 * p + e]
        return [lax.dot_general(q, k_ref[2 * p + e], (((1,), (1,)), ((), ())), preferred_element_type=F32)
                for k_ref, _ in parts]

    outs = []
    s_next = scores(*heads[0])
    for i, (p, e) in enumerate(heads):
        s = s_next
        if i + 1 < len(heads):
            s_next = scores(*heads[i + 1])
        m = functools.reduce(jnp.maximum, [jnp.max(x, axis=-1, keepdims=True) for x in s])
        w = [jnp.exp2(x - m) for x in s]
        l = sum(jnp.sum(x, axis=-1, keepdims=True) for x in w)
        pv = sum(_dot(x.astype(BF16), v_ref[p]) for x, (_, v_ref) in zip(w, parts))
        outs.append(pv / l)
        if e == 1:
            lane = lax.broadcasted_iota(jnp.int32, pv.shape, 1)
            o_ref[:, p * LANES:(p + 1) * LANES] = jnp.where(lane < V_HEAD, outs[-2], outs[-1]).astype(BF16)


def _attn_call(q, kv_parts, q_tile, n_pairs):
    _, b, t, _ = q.shape
    nq = t // q_tile
    in_specs = [pl.BlockSpec((2 * n_pairs, None, q_tile, LANES), lambda bi, g, qi: (g, bi, qi, 0))]
    operands = [q]
    for k, v in kv_parts:
        s = k.shape[2]
        in_specs += [pl.BlockSpec((2 * n_pairs, None, s, LANES), lambda bi, g, qi: (g, bi, 0, 0)),
                     pl.BlockSpec((n_pairs, None, s, LANES), lambda bi, g, qi: (g, bi, 0, 0))]
        operands += [k, v]
    return pl.pallas_call(
        functools.partial(_attn_kernel, n_pairs=n_pairs),
        grid=(b, HEAD_PAIRS // n_pairs, nq),
        in_specs=in_specs,
        out_specs=pl.BlockSpec((q_tile, n_pairs * LANES), lambda bi, g, qi: (bi * nq + qi, g)),
        out_shape=jax.ShapeDtypeStruct((b * t, N_HEADS * V_HEAD), BF16),
        compiler_params=_params(("arbitrary", "arbitrary", "arbitrary"), ATTN_VMEM),
        name=f"attn_{len(kv_parts)}part",
    )(*operands)


def _pool_kernel(x_ref, mod_ref, nw_ref, w_ref, sc_ref, o_ref, band_scr, y_scr, *, seq):
    @pl.when(pl.program_id(0) == 0)
    def _():
        r = lax.broadcasted_iota(jnp.int32, (seq, seq), 0)
        c = lax.broadcasted_iota(jnp.int32, (seq, seq), 1)
        for g, win in enumerate(POOL_WINDOWS):
            inside = (c >= r - win // 2) & (c < r + win - win // 2)
            band_scr[g] = jnp.where(inside, 1.0, 0.0).astype(BF16)

    x = x_ref[...]
    h = _norm_mod(x, nw_ref[...], mod_ref[...], 0)
    h_hi = h.astype(BF16)
    h_lo = (h - h_hi.astype(F32)).astype(BF16)
    t = lax.broadcasted_iota(jnp.int32, (seq, 1), 0)
    for g, win in enumerate(POOL_WINDOWS):
        cnt = (jnp.minimum(t + (win - win // 2), seq) - jnp.maximum(t - win // 2, 0)).astype(F32)
        cols = slice(g * GROUP, (g + 1) * GROUP)
        wg = w_ref[g].astype(BF16)
        for s in range(x.shape[0] // seq):
            rows = slice(s * seq, (s + 1) * seq)
            tot = _dot(band_scr[g], h_hi[rows, cols]) + _dot(band_scr[g], h_lo[rows, cols])
            pooled = (tot / cnt - h[rows, cols]).astype(BF16)
            y_scr[rows, cols] = _dot(pooled, wg)
    o_ref[...] = x + mod_ref[2:3, :] * (y_scr[...] * sc_ref[...])


def _pool_call(x, mods, li, cond_of_row, seq, P):
    m = x.shape[0]
    lj = li // 4
    return pl.pallas_call(
        functools.partial(_pool_kernel, seq=seq),
        grid=(m // ROW_TILE,),
        in_specs=[
            pl.BlockSpec((ROW_TILE, D_MODEL), lambda i: (i, 0)),
            _mod_spec(li, cond_of_row, ROW_TILE, 1),
            pl.BlockSpec((None, 1, D_MODEL), lambda i: (li, 0, 0)),
            pl.BlockSpec((None, N_GROUPS, GROUP, GROUP), lambda i: (lj, 0, 0, 0)),
            pl.BlockSpec((None, 1, D_MODEL), lambda i: (lj, 0, 0)),
        ],
        out_specs=pl.BlockSpec((ROW_TILE, D_MODEL), lambda i: (i, 0)),
        out_shape=jax.ShapeDtypeStruct((m, D_MODEL), F32),
        scratch_shapes=[pltpu.VMEM((N_GROUPS, seq, seq), BF16), pltpu.VMEM((ROW_TILE, D_MODEL), F32)],
        compiler_params=_params(("arbitrary",)),
        name=f"pool_t{seq}",
    )(x, mods, P["norm1_w"], P["pool_w"], P["pool_scale"])


def _dft_tables(seq):
    def cs(n):
        k = np.arange(n, dtype=np.int64)
        ang = 2.0 * np.pi * ((k[:, None] * k[None, :]) % n) / n
        return np.cos(ang), np.sin(ang)
    ct, st = cs(seq)
    cc, sc = cs(GROUP)
    e_t = jnp.asarray(np.concatenate([ct, st], axis=0), dtype=F32)
    e_c = jnp.asarray(np.concatenate([cc, -sc], axis=0), dtype=F32)
    return e_t.astype(BF16), e_c.astype(BF16)


def _fourier_kernel(x_ref, mod_ref, nw_ref, et_ref, ec_ref, w_ref, o_ref, f_scr, *, seq):
    x = x_ref[...]
    h = _norm_mod(x, nw_ref[...], mod_ref[...], 0).astype(BF16)
    norm = np.float32(1.0 / math.sqrt(seq * GROUP))
    for s in range(x.shape[0] // seq):
        rows = slice(s * seq, (s + 1) * seq)
        for g in range(N_GROUPS):
            cols = slice(g * GROUP, (g + 1) * GROUP)
            y = _dot(et_ref[...], h[rows, cols]).astype(BF16)
            f = _dot(y[:seq], ec_ref[:GROUP, :]) + _dot(y[seq:], ec_ref[GROUP:, :])
            f_scr[rows, cols] = (f * norm).astype(BF16)
    o_ref[...] = x + mod_ref[2:3, :] * _dot(f_scr[...], w_ref[...].astype(BF16))


def _fourier_call(x, mods, li, cond_of_row, seq, P):
    m = x.shape[0]
    lj = li // 4
    e_t, e_c = _dft_tables(seq)
    return pl.pallas_call(
        functools.partial(_fourier_kernel, seq=seq),
        grid=(m // ROW_TILE,),
        in_specs=[
            pl.BlockSpec((ROW_TILE, D_MODEL), lambda i: (i, 0)),
            _mod_spec(li, cond_of_row, ROW_TILE, 1),
            pl.BlockSpec((None, 1, D_MODEL), lambda i: (li, 0, 0)),
            pl.BlockSpec((2 * seq, seq), lambda i: (0, 0)),
            pl.BlockSpec((2 * GROUP, GROUP), lambda i: (0, 0)),
            pl.BlockSpec((None, D_MODEL, D_MODEL), lambda i: (lj, 0, 0)),
        ],
        out_specs=pl.BlockSpec((ROW_TILE, D_MODEL), lambda i: (i, 0)),
        out_shape=jax.ShapeDtypeStruct((m, D_MODEL), F32),
        scratch_shapes=[pltpu.VMEM((ROW_TILE, D_MODEL), BF16)],
        compiler_params=_params(("arbitrary",)),
        name=f"fourier_t{seq}",
    )(x, mods, P["norm1_w"], e_t, e_c, P["fnet_w"])


def _rope_tables(rows):
    r, col = jnp.meshgrid(jnp.arange(rows), jnp.arange(GRID_W), indexing="ij")
    r = r.reshape(-1).astype(F32)
    col = col.reshape(-1).astype(F32)
    n_freq = QK_ROPE // 4
    inv = ROPE_THETA ** (-jnp.arange(n_freq, dtype=F32) / n_freq)
    ang = jnp.concatenate([r[:, None] * inv, col[:, None] * inv], axis=-1)
    cos = jnp.repeat(jnp.cos(ang), 2, axis=-1)
    sin = jnp.repeat(jnp.sin(ang), 2, axis=-1)
    t = cos.shape[0]
    pad = LANES - QK_NOPE - QK_ROPE
    cos_p = jnp.concatenate([jnp.ones((t, QK_NOPE), F32), cos, jnp.ones((t, pad), F32)], axis=-1)
    sin_p = jnp.concatenate([jnp.zeros((t, QK_NOPE), F32), sin, jnp.zeros((t, pad), F32)], axis=-1)
    return cos_p, sin_p


def _pair_swap(w):
    wr = w.reshape(w.shape[:-1] + (w.shape[-1] // 2, 2))
    return jnp.stack([-wr[..., 1], wr[..., 0]], axis=-1).reshape(w.shape)


def _mla_weights(P, j, rope, seq_rows):
    pad = LANES - QK_NOPE - QK_ROPE
    wuq = P["mla_wuq"][j].reshape(Q_LORA, N_HEADS, QK_NOPE + QK_ROPE)
    q_slots = jnp.pad(wuq, ((0, 0), (0, 0), (0, pad))).reshape(Q_LORA, N_HEADS * LANES)
    wdkv = P["mla_wdkv"][j]
    kr_slot = jnp.pad(wdkv[:, KV_LORA:], ((0, 0), (QK_NOPE, pad)))
    parts1 = [P["mla_wdq"][j], wdkv[:, :KV_LORA], kr_slot]
    parts2 = [q_slots]
    if rope:
        parts1.append(jnp.pad(_pair_swap(wdkv[:, KV_LORA:]), ((0, 0), (QK_NOPE, pad))))
        q_sw = jnp.pad(_pair_swap(wuq[:, :, QK_NOPE:]), ((0, 0), (0, 0), (QK_NOPE, pad)))
        parts2.append(q_sw.reshape(Q_LORA, N_HEADS * LANES))
        cos, sin = _rope_tables(seq_rows // GRID_W)
    else:
        cos = sin = jnp.zeros((PROJ_TILE, LANES), F32)
    wukv = P["mla_wukv"][j].reshape(KV_LORA, N_HEADS, QK_NOPE + V_HEAD)
    wk = jnp.pad(wukv[:, :, :QK_NOPE], ((0, 0), (0, 0), (0, LANES - QK_NOPE)))
    return dict(
        norm1_w=P["norm1_w"], q_norm=P["mla_q_norm"][j][None, :], kv_norm=P["mla_kv_norm"][j][None, :],
        w1=jnp.concatenate(parts1, axis=1), w2=jnp.concatenate(parts2, axis=1),
        wk=wk.reshape(KV_LORA, N_HEADS * LANES),
        wv=wukv[:, :, QK_NOPE:].reshape(KV_LORA, N_HEADS * V_HEAD), cos=cos, sin=sin)


def _mla_layer(x, mods, li, cond_of_row, batch, seq, P, cache):
    j = li // 4
    W = _mla_weights(P, j, cache is not None, seq)
    q, k, v, ckv, krope = _mla_proj_call(x, mods, li, cond_of_row, W, cache is not None)
    q = q.reshape(N_HEADS, batch, seq, LANES)
    k = k.reshape(N_HEADS, batch, seq, LANES)
    v = v.reshape(HEAD_PAIRS, batch, seq, LANES)
    if cache is None:
        o = _attn_call(q, [(k, v)], seq, HEAD_PAIRS)
    else:
        c_ckv, c_krope = cache
        past = c_ckv.shape[1]
        pad = LANES - QK_NOPE - QK_ROPE
        kr_pad = jnp.pad(c_krope.reshape(batch * past, QK_ROPE), ((0, 0), (QK_NOPE, pad)))
        kc, vc = _kv_cache_call(c_ckv.reshape(batch * past, KV_LORA), kr_pad, W)
        cached = (kc.reshape(N_HEADS, batch, past, LANES), vc.reshape(HEAD_PAIRS, batch, past, LANES))
        o = _attn_call(q, [cached, (k, v)], 512, 4)
    return o, ckv, krope


def _mixer(x, mods, li, cond_of_row, batch, seq, P, cache):
    kind = li % 4
    if kind == 0:
        o, ckv, krope = _mla_layer(x, mods, li, cond_of_row, batch, seq, P, cache)
        return x, (o, P["mla_wo_bf16"], li // 4), (ckv, krope)
    if kind == 1:
        return _pool_call(x, mods, li, cond_of_row, seq, P), None, None
    if kind == 2:
        return _fourier_call(x, mods, li, cond_of_row, seq, P), None, None
    return _sconv_call(x, mods, li, cond_of_row, seq, P), None, None


def _run_passes(passes, mods, P):
    new_cache = None
    raw = (P["ffn_up"], P["ffn_down"], P["sconv_win"], P["sconv_wout"])
    for li in range(DEPTH):
        xs, raw = lax.optimization_barrier(([ps["x"] for ps in passes], raw))
        for ps, x in zip(passes, xs):
            ps["x"] = x
        P["ffn_w"] = _layer_weights(raw[0], raw[1], li, 2, f"ffn{li}_weights")
        if li % 4 == 3:
            P["sconv_w"] = _layer_weights(raw[2], raw[3], li // 4, 3, f"sconv{li}_weights")
        pending = []
        for ps in passes:
            ps["x"], pre, made = _mixer(ps["x"], mods, li, ps["cond_of_row"], ps["batch"], ps["seq"], P,
                                        ps["cache"])
            pending.append(pre)
            if made is not None and ps["cache"] is None:
                new_cache = made
        xs, outs = lax.optimization_barrier(([ps["x"] for ps in passes],
                                              [None if pre is None else pre[0] for pre in pending]))
        pending = [None if pre is None else (o,) + pre[1:] for pre, o in zip(pending, outs)]
        xs = [_ffn_call(x, mods, li, ps["cond_of_row"], ps["seq"], P, pre, final_norm=(li == DEPTH - 1))
              for x, ps, pre in zip(xs, passes, pending)]
        for ps, x in zip(passes, xs):
            ps["x"] = x
    return new_cache


def kernel(x_prompt, x_sample, cache_ckv, cache_krope, c, c_ctx, norm1_w, norm2_w, ada_w, ada_b, mla_wdq, mla_q_norm, mla_wuq, mla_wdkv, mla_kv_norm, mla_wukv, mla_wo, pool_w, pool_scale, fnet_w, sconv_win, sconv_conv, sconv_wout, ffn_up, ffn_conv_w, ffn_conv_b, ffn_down, final_norm_w):
    row = lambda w: w.reshape(w.shape[0], 1, w.shape[1])
    P = dict(norm1_w=row(norm1_w), norm2_w=row(norm2_w), mla_wdq=mla_wdq, mla_q_norm=mla_q_norm,
             mla_wuq=mla_wuq, mla_wdkv=mla_wdkv, mla_kv_norm=mla_kv_norm, mla_wukv=mla_wukv,
             mla_wo_bf16=mla_wo.astype(BF16),
             pool_w=pool_w, pool_scale=row(pool_scale), fnet_w=fnet_w, sconv_win=sconv_win,
             sconv_conv=sconv_conv, sconv_wout=sconv_wout, ffn_conv_w=ffn_conv_w,
             ffn_conv_b=ffn_conv_b, ffn_up=ffn_up, ffn_down=ffn_down, final_norm_w=final_norm_w)
    b_ctx, t_ctx, _ = x_prompt.shape
    b_lat, t_lat, _ = x_sample.shape
    assert ROW_TILE % t_ctx == 0 and t_lat % ROW_TILE == 0 and b_lat + 1 <= N_COND

    conds = jnp.concatenate([c_ctx[None, :], c, jnp.zeros((N_COND - 1 - b_lat, D_MODEL), F32)], axis=0)
    mods = _ada_call(conds, ada_w, ada_b).reshape(DEPTH, N_COND, 6, D_MODEL)

    ctx = dict(x=x_prompt.reshape(b_ctx * t_ctx, D_MODEL), cond_of_row=lambda r: 0, batch=b_ctx, seq=t_ctx,
               cache=None)
    lat = dict(x=x_sample.reshape(b_lat * t_lat, D_MODEL), cond_of_row=lambda r: 1 + r // t_lat, batch=b_lat,
               seq=t_lat, cache=(cache_ckv[:, 0], cache_krope[:, 0]))
    ckv, krope = _run_passes([ctx, lat], mods, P)
    return (ctx["x"].reshape(b_ctx, t_ctx, D_MODEL), lat["x"].reshape(b_lat, t_lat, D_MODEL),
            ckv.reshape(b_ctx, 1, t_ctx, KV_LORA), krope.reshape(b_ctx, 1, t_ctx, QK_ROPE))
```

```python
import functools
import math

import numpy as np
import jax
import jax.numpy as jnp
from jax import lax
from jax.experimental import pallas as pl
from jax.experimental.pallas import tpu as pltpu

F32 = jnp.float32
BF16 = jnp.bfloat16

D_MODEL = 1024
N_HEADS = 16
QK_NOPE = 64
QK_ROPE = 32
V_HEAD = 64
Q_LORA = 384
KV_LORA = 256
ROPE_THETA = 10000.0
GRID_W = 64
POOL_WINDOWS = (2, 4, 8, 16)
N_GROUPS = 4
GROUP = D_MODEL // N_GROUPS
D_FF = 2816
EPS = 1e-6
DEPTH = 4

LANES = 128
Q_SCALE = np.float32(math.log2(math.e) / math.sqrt(QK_NOPE + QK_ROPE))
HEAD_PAIRS = N_HEADS // 2
N_COND = 8
ROW_TILE = 1024
PROJ_TILE = 512
FF_TILE = 256
ROW_BLOCK = 256
HALO = 8
RELAYOUT_ROWS = 128
ADA_TILE = 1536
MIB = 1024 * 1024
VMEM_DEFAULT = 48 * MIB
VMEM_HEADROOM = 5 * MIB
ATTN_VMEM = 48 * MIB


def _params(sem, vmem_bytes=VMEM_DEFAULT):
    return pltpu.CompilerParams(dimension_semantics=sem, vmem_limit_bytes=int(vmem_bytes))


def _nbytes(a):
    return math.prod(a.shape) * jnp.dtype(a.dtype).itemsize


def _dot(a, b):
    return jnp.dot(a, b, preferred_element_type=F32)


def _rms(x, w):
    return x * lax.rsqrt(jnp.mean(x * x, axis=-1, keepdims=True) + EPS) * w


def _norm_mod(x, nw, mod, base):
    return _rms(x, nw) * (1.0 + mod[base + 1:base + 2, :]) + mod[base:base + 1, :]


def _gelu_exact(x):
    return 0.5 * x * (1.0 + lax.erf(x * np.float32(math.sqrt(0.5))))


def _ada_kernel(c_ref, w_ref, b_ref, o_ref):
    s = jax.nn.silu(c_ref[...]).astype(BF16)
    o_ref[...] = _dot(s, w_ref[...].astype(BF16)) + b_ref[...]


def _ada_call(conds, ada_w, ada_b):
    n_out = ada_w.shape[-1]
    return pl.pallas_call(
        _ada_kernel,
        grid=(DEPTH, n_out // ADA_TILE),
        in_specs=[
            pl.BlockSpec((N_COND, D_MODEL), lambda l, j: (0, 0)),
            pl.BlockSpec((None, D_MODEL, ADA_TILE), lambda l, j: (l, 0, j)),
            pl.BlockSpec((None, 1, ADA_TILE), lambda l, j: (l, 0, j)),
        ],
        out_specs=pl.BlockSpec((None, N_COND, ADA_TILE), lambda l, j: (l, 0, j)),
        out_shape=jax.ShapeDtypeStruct((DEPTH, N_COND, n_out), F32),
        compiler_params=_params(("arbitrary", "arbitrary"), 2 * D_MODEL * ADA_TILE * 4 + VMEM_HEADROOM),
        name="ada",
    )(conds, ada_w, ada_b.reshape(DEPTH, 1, n_out))


def _mod_spec(li, cond_of_row, tile, n_grid):
    if n_grid == 1:
        return pl.BlockSpec((None, None, 6, D_MODEL), lambda i: (li, cond_of_row(i * tile), 0, 0))
    return pl.BlockSpec((None, None, 6, D_MODEL), lambda i, j: (li, cond_of_row(i * tile), 0, 0))


def _conv_window(w, cw, t0, seq):
    n = w.shape[0]
    mid = slice(HALO, HALO + ROW_BLOCK)
    t = (t0 + lax.broadcasted_iota(jnp.int32, (ROW_BLOCK, 1), 0)) & (seq - 1)
    prev = jnp.where(t == 0, 0.0, pltpu.roll(w, 1, axis=0)[mid])
    nxt = jnp.where(t == seq - 1, 0.0, pltpu.roll(w, n - 1, axis=0)[mid])
    return prev * cw[0:1, :] + w[mid] * cw[1:2, :] + nxt * cw[2:3, :]


def _ffn_act(win, mid, cw, cb, t0, seq):
    return _gelu_exact(_conv_window(win(0), cw, t0, seq) + cb) * mid(1)


def _sconv_act(win, mid, cw, cb, t0, seq):
    return mid(0) * _conv_window(win(1) * win(2), cw, t0, seq)


def _ffn_kernel(x_ref, mod_ref, nw_ref, up_ref, cw_ref, cb_ref, down_ref, fnw_ref, o_ref, *scratch, **cfg):
    _glu_body(x_ref, mod_ref, nw_ref, up_ref, cw_ref, cb_ref, down_ref, fnw_ref, None, o_ref, *scratch,
              n_up=2, act=_ffn_act, mod_base=3, **cfg)


def _ffn_pre_kernel(x_ref, mod_ref, nw_ref, up_ref, cw_ref, cb_ref, down_ref, fnw_ref, a_ref, wa_ref, o_ref,
                    *scratch, **cfg):
    _glu_body(x_ref, mod_ref, nw_ref, up_ref, cw_ref, cb_ref, down_ref, fnw_ref, (a_ref, wa_ref), o_ref,
              *scratch, n_up=2, act=_ffn_act, mod_base=3, **cfg)


def _sconv_kernel(x_ref, mod_ref, nw_ref, up_ref, cw_ref, down_ref, o_ref, *scratch, **cfg):
    _glu_body(x_ref, mod_ref, nw_ref, up_ref, cw_ref, None, down_ref, None, None, o_ref, *scratch,
              n_up=3, act=_sconv_act, mod_base=0, final_norm=False, **cfg)


def _glu_body(x_ref, mod_ref, nw_ref, up_ref, cw_ref, cb_ref, down_ref, fnw_ref, pre, o_ref,
              h_scr, acc_scr, gu0, gu1, *, n_up, n_chunks, act, seq, mod_base, final_norm):
    gu = (gu0, gu1)
    n_blocks = ROW_TILE // ROW_BLOCK

    def up(c, slot, r):
        rows = slice(r * ROW_BLOCK, (r + 1) * ROW_BLOCK)
        k0 = c * D_MODEL if isinstance(c, int) else pl.multiple_of(c * D_MODEL, D_MODEL)
        res = _dot(h_scr[rows, :], up_ref[pl.ds(k0, D_MODEL), :])
        for p in range(n_up):
            gu[slot][p, HALO + r * ROW_BLOCK:HALO + (r + 1) * ROW_BLOCK, :] = res[:, p * FF_TILE:(p + 1) * FF_TILE]

    def activate(c, slot, r):
        win = lambda p: gu[slot][p, r * ROW_BLOCK:(r + 1) * ROW_BLOCK + 2 * HALO, :]
        mid = lambda p: gu[slot][p, HALO + r * ROW_BLOCK:HALO + (r + 1) * ROW_BLOCK, :]
        cb = None if cb_ref is None else cb_ref[c]
        return act(win, mid, cw_ref[c], cb, r * ROW_BLOCK, seq).astype(BF16)

    def down(a, c, r):
        rows = slice(r * ROW_BLOCK, (r + 1) * ROW_BLOCK)
        y = _dot(a, down_ref[c])
        if isinstance(c, int) and c == 0:
            acc_scr[rows, :] = y
        else:
            acc_scr[rows, :] += y

    def stage(s, slot):
        for r in range(n_blocks):
            a = activate(s - 1, 1 - slot, r)
            up(s, slot, r)
            down(a, s - 1, r)

    def block(r):
        return slice(r * ROW_BLOCK, (r + 1) * ROW_BLOCK)

    def mixer_proj(r):
        a_ref, wa_ref = pre
        return _dot(a_ref[block(r), :], wa_ref[...])

    for g in gu:
        g[:, :HALO, :] = jnp.zeros((n_up, HALO, FF_TILE), F32)
        g[:, HALO + ROW_TILE:, :] = jnp.zeros((n_up, HALO, FF_TILE), F32)

    proj = mixer_proj(0) if pre is not None else None
    for r in range(n_blocks):
        x = x_ref[block(r), :]
        if pre is not None:
            x = x + mod_ref[2:3, :] * proj
            o_ref[block(r), :] = x
            if r + 1 < n_blocks:
                proj = mixer_proj(r + 1)
        h_scr[block(r), :] = _norm_mod(x, nw_ref[...], mod_ref[...], mod_base).astype(BF16)
        up(0, 0, r)

    if n_chunks > 1:
        stage(1, 1)
    n_pairs = (n_chunks - 2) // 2

    def pair(i, carry):
        s = 2 * i + 2
        stage(s, 0)
        stage(s + 1, 1)
        return carry

    lax.fori_loop(0, n_pairs, pair, 0)
    for s in range(2 * n_pairs + 2, n_chunks):
        stage(s, s % 2)
    last = n_chunks - 1
    res = x_ref if pre is None else o_ref
    for r in range(n_blocks):
        down(activate(last, last % 2, r), last, r)
        y = res[block(r), :] + mod_ref[mod_base + 2:mod_base + 3, :] * acc_scr[block(r), :]
        if final_norm:
            y = _rms(y, fnw_ref[...])
        o_ref[block(r), :] = y


def _chunked(w, n_chunks):
    return w.reshape(w.shape[0], n_chunks, FF_TILE).transpose(1, 0, 2)


def _glu_call(kernel_fn, name, x, mods, li, cond_of_row, n_up, n_chunks, norm_w, operands, specs, cfg):
    m = x.shape[0]
    whole = lambda a: pl.BlockSpec(a.shape, lambda i: (0,) * a.ndim)
    in_specs = [pl.BlockSpec((ROW_TILE, D_MODEL), lambda i: (i, 0)),
                _mod_spec(li, cond_of_row, ROW_TILE, 1),
                pl.BlockSpec((None, 1, D_MODEL), lambda i: (li, 0, 0))]
    layer_w = lambda a, l: pl.BlockSpec((None,) + a.shape[1:], lambda i: (l,) + (0,) * (a.ndim - 1),
                                        pipeline_mode=pl.Buffered(1))
    rows = lambda a: pl.BlockSpec((ROW_TILE, a.shape[1]), lambda i: (i, 0))
    spec_of = {None: whole, "rows": rows, "resident": lambda a: pl.BlockSpec(memory_space=pltpu.VMEM)}
    in_specs += [layer_w(a, s) if isinstance(s, int) else spec_of[s](a) for a, s in zip(operands, specs)]
    gu_shape = (n_up, ROW_TILE + 2 * HALO, FF_TILE)
    tile = ROW_TILE * D_MODEL
    window = {None: _nbytes, "rows": lambda a: 2 * ROW_TILE * a.shape[1] * a.dtype.itemsize,
              "resident": lambda a: 0}
    vmem = (2 * 2 * tile * 4 + tile * 2 + tile * 4 + 2 * math.prod(gu_shape) * 4 + VMEM_HEADROOM
            + sum(_nbytes(a) // a.shape[0] if isinstance(s, int) else window[s](a)
                  for a, s in zip(operands, specs)))
    return pl.pallas_call(
        functools.partial(kernel_fn, n_chunks=n_chunks, **cfg),
        grid=(m // ROW_TILE,),
        in_specs=in_specs,
        out_specs=pl.BlockSpec((ROW_TILE, D_MODEL), lambda i: (i, 0)),
        out_shape=jax.ShapeDtypeStruct((m, D_MODEL), F32),
        scratch_shapes=[
            pltpu.VMEM((ROW_TILE, D_MODEL), BF16),
            pltpu.VMEM((ROW_TILE, D_MODEL), F32),
            pltpu.VMEM(gu_shape, F32), pltpu.VMEM(gu_shape, F32),
        ],
        compiler_params=_params(("arbitrary",), vmem),
        name=name,
    )(x, mods, norm_w, *operands)


def _in_vmem(a):
    return pltpu.with_memory_space_constraint(a, pltpu.VMEM)


def _layer_weights_kernel(up_ref, dn_ref, uo_ref, do_ref, *, n_up, n_chunks):
    for c in range(n_chunks):
        for p in range(n_up):
            col = (p * n_chunks + c) * FF_TILE
            uo_ref[c, :, p * FF_TILE:(p + 1) * FF_TILE] = up_ref[:, col:col + FF_TILE].astype(BF16)
    do_ref[...] = dn_ref[...].astype(BF16)


def _layer_weights(up, down, layer, n_up, name):
    _, d, cols = up.shape
    _, hidden, d_out = down.shape
    n_chunks = hidden // FF_TILE
    steps = d // RELAYOUT_ROWS
    dn_rows = hidden // steps
    up_cm, dn = pl.pallas_call(
        functools.partial(_layer_weights_kernel, n_up=n_up, n_chunks=n_chunks),
        grid=(steps,),
        in_specs=[pl.BlockSpec((None, RELAYOUT_ROWS, cols), lambda k: (layer, k, 0)),
                  pl.BlockSpec((None, dn_rows, d_out), lambda k: (layer, k, 0))],
        out_specs=[pl.BlockSpec((n_chunks, RELAYOUT_ROWS, n_up * FF_TILE), lambda k: (0, k, 0)),
                   pl.BlockSpec((dn_rows, d_out), lambda k: (k, 0))],
        out_shape=[jax.ShapeDtypeStruct((n_chunks, d, n_up * FF_TILE), BF16),
                   jax.ShapeDtypeStruct((hidden, d_out), BF16)],
        compiler_params=_params(("arbitrary",),
                                2 * (RELAYOUT_ROWS * cols + dn_rows * d_out) * (4 + 2) + VMEM_HEADROOM),
        name=name,
    )(up, down)
    return _in_vmem(up_cm.reshape(n_chunks * d, n_up * FF_TILE)), _in_vmem(dn.reshape(n_chunks, FF_TILE, d_out))


def _ffn_call(x, mods, li, cond_of_row, seq, P, pre, final_norm):
    n = D_FF // FF_TILE
    up_cm, down_cm = P["ffn_w"](li)
    operands = [up_cm, _chunked(P["ffn_conv_w"][li], n), _chunked(P["ffn_conv_b"][li][None, :], n),
                down_cm, P["final_norm_w"].reshape(1, D_MODEL)]
    specs = ["resident", None, None, "resident", None]
    kernel_fn = _ffn_kernel
    if pre is not None:
        a, w_a, l_a = pre
        operands += [a, w_a]
        specs += ["rows", l_a]
        kernel_fn = _ffn_pre_kernel
    return _glu_call(kernel_fn, f"ffn{li}", x, mods, li, cond_of_row, 2, n, P["norm2_w"],
                     operands, specs, dict(seq=seq, final_norm=final_norm))


def _sconv_call(x, mods, li, cond_of_row, seq, P):
    lj = li // 4
    n = D_MODEL // FF_TILE
    up_cm, down_cm = P["sconv_w"](lj)
    operands = [up_cm, _chunked(P["sconv_conv"][lj], n), down_cm]
    return _glu_call(_sconv_kernel, f"sconv{li}", x, mods, li, cond_of_row, 3, n, P["norm1_w"],
                     operands, ["resident", None, "resident"], dict(seq=seq))


def _emit_kv(ckv, krope_pad, wk_ref, wv_ref, k_ref, v_ref):
    c = ckv.astype(BF16)
    kx = _dot(c, wk_ref[...].astype(BF16))
    vx = _dot(c, wv_ref[...].astype(BF16))
    for h in range(N_HEADS):
        k_ref[h] = (kx[:, h * LANES:(h + 1) * LANES] + krope_pad).astype(BF16)
    for p in range(HEAD_PAIRS):
        v_ref[p] = vx[:, p * LANES:(p + 1) * LANES].astype(BF16)


def _mla_proj_kernel(x_ref, mod_ref, nw_ref, w1_ref, qn_ref, kvn_ref, w2_ref, wk_ref, wv_ref, cos_ref,
                     sin_ref, q_ref, k_ref, v_ref, ckv_ref, kr_ref, *, rope):
    h = _norm_mod(x_ref[...], nw_ref[...], mod_ref[...], 0).astype(BF16)
    z = _dot(h, w1_ref[...].astype(BF16))
    cq = _rms(z[:, :Q_LORA], qn_ref[...]).astype(BF16)
    ckv = _rms(z[:, Q_LORA:Q_LORA + KV_LORA], kvn_ref[...])
    kr = z[:, Q_LORA + KV_LORA:Q_LORA + KV_LORA + LANES]
    if rope:
        cos, sin = cos_ref[...], sin_ref[...]
        kr = kr * cos + z[:, Q_LORA + KV_LORA + LANES:] * sin
    for p in range(HEAD_PAIRS):
        q2 = _dot(cq, w2_ref[:, 2 * p * LANES:(2 * p + 2) * LANES].astype(BF16))
        if rope:
            off = N_HEADS * LANES
            qs = _dot(cq, w2_ref[:, off + 2 * p * LANES:off + (2 * p + 2) * LANES].astype(BF16))
        for e in range(2):
            qh = q2[:, e * LANES:(e + 1) * LANES]
            if rope:
                qh = qh * cos + qs[:, e * LANES:(e + 1) * LANES] * sin
            q_ref[2 * p + e] = (qh * Q_SCALE).astype(BF16)
    ckv_ref[...] = ckv
    kr_ref[...] = kr[:, QK_NOPE:QK_NOPE + QK_ROPE]
    _emit_kv(ckv, kr, wk_ref, wv_ref, k_ref, v_ref)


def _mla_proj_call(x, mods, li, cond_of_row, W, rope):
    m = x.shape[0]
    n1 = W["w1"].shape[1]
    n2 = W["w2"].shape[1]
    const = lambda i: (0, 0)
    pos_tiles = W["cos"].shape[0] // PROJ_TILE
    pos = lambda i: (i % pos_tiles, 0)
    return pl.pallas_call(
        functools.partial(_mla_proj_kernel, rope=rope),
        grid=(m // PROJ_TILE,),
        in_specs=[
            pl.BlockSpec((PROJ_TILE, D_MODEL), lambda i: (i, 0)),
            _mod_spec(li, cond_of_row, PROJ_TILE, 1),
            pl.BlockSpec((None, 1, D_MODEL), lambda i: (li, 0, 0)),
            pl.BlockSpec((D_MODEL, n1), const),
            pl.BlockSpec((1, Q_LORA), const),
            pl.BlockSpec((1, KV_LORA), const),
            pl.BlockSpec((Q_LORA, n2), const),
            pl.BlockSpec((KV_LORA, N_HEADS * LANES), const),
            pl.BlockSpec((KV_LORA, HEAD_PAIRS * LANES), const),
            pl.BlockSpec((PROJ_TILE, LANES), pos),
            pl.BlockSpec((PROJ_TILE, LANES), pos),
        ],
        out_specs=[
            pl.BlockSpec((N_HEADS, PROJ_TILE, LANES), lambda i: (0, i, 0)),
            pl.BlockSpec((N_HEADS, PROJ_TILE, LANES), lambda i: (0, i, 0)),
            pl.BlockSpec((HEAD_PAIRS, PROJ_TILE, LANES), lambda i: (0, i, 0)),
            pl.BlockSpec((PROJ_TILE, KV_LORA), lambda i: (i, 0)),
            pl.BlockSpec((PROJ_TILE, QK_ROPE), lambda i: (i, 0)),
        ],
        out_shape=[
            jax.ShapeDtypeStruct((N_HEADS, m, LANES), BF16),
            jax.ShapeDtypeStruct((N_HEADS, m, LANES), BF16),
            jax.ShapeDtypeStruct((HEAD_PAIRS, m, LANES), BF16),
            jax.ShapeDtypeStruct((m, KV_LORA), F32),
            jax.ShapeDtypeStruct((m, QK_ROPE), F32),
        ],
        compiler_params=_params(("arbitrary",)),
        name="mla_proj_rope" if rope else "mla_proj",
    )(x, mods, W["norm1_w"], W["w1"], W["q_norm"], W["kv_norm"], W["w2"], W["wk"], W["wv"],
      W["cos"], W["sin"])


def _kv_cache_kernel(ckv_ref, kr_ref, wk_ref, wv_ref, k_ref, v_ref):
    _emit_kv(ckv_ref[...], kr_ref[...], wk_ref, wv_ref, k_ref, v_ref)


def _kv_cache_call(ckv, krope_pad, W):
    m = ckv.shape[0]
    const = lambda i: (0, 0)
    return pl.pallas_call(
        _kv_cache_kernel,
        grid=(1,),
        in_specs=[
            pl.BlockSpec((m, KV_LORA), const),
            pl.BlockSpec((m, LANES), const),
            pl.BlockSpec((KV_LORA, N_HEADS * LANES), const),
            pl.BlockSpec((KV_LORA, HEAD_PAIRS * LANES), const),
        ],
        out_specs=[
            pl.BlockSpec((N_HEADS, m, LANES), lambda i: (0, 0, 0)),
            pl.BlockSpec((HEAD_PAIRS, m, LANES), lambda i: (0, 0, 0)),
        ],
        out_shape=[
            jax.ShapeDtypeStruct((N_HEADS, m, LANES), BF16),
            jax.ShapeDtypeStruct((HEAD_PAIRS, m, LANES), BF16),
        ],
        compiler_params=_params(("arbitrary",)),
        name="kv_cache",
    )(ckv, krope_pad, W["wk"], W["wv"])


def _attn_kernel(q_ref, *refs, n_pairs):
    o_ref, kv = refs[-1], refs[:-1]
    parts = [(kv[i], kv[i + 1]) for i in range(0, len(kv), 2)]
    heads = [(p, e) for p in range(n_pairs) for e in range(2)]

    def scores(p, e):
        q = q_ref[2 * p + e]
        return [lax.dot_general(q, k_ref[2 * p + e], (((1,), (1,)), ((), ())), preferred_element_type=F32)
                for k_ref, _ in parts]

    outs = []
    s_next = scores(*heads[0])
    for i, (p, e) in enumerate(heads):
        s = s_next
        if i + 1 < len(heads):
            s_next = scores(*heads[i + 1])
        m = functools.reduce(jnp.maximum, [jnp.max(x, axis=-1, keepdims=True) for x in s])
        w = [jnp.exp2(x - m) for x in s]
        l = sum(jnp.sum(x, axis=-1, keepdims=True) for x in w)
        pv = sum(_dot(x.astype(BF16), v_ref[p]) for x, (_, v_ref) in zip(w, parts))
        outs.append(pv / l)
        if e == 1:
            lane = lax.broadcasted_iota(jnp.int32, pv.shape, 1)
            o_ref[:, p * LANES:(p + 1) * LANES] = jnp.where(lane < V_HEAD, outs[-2], outs[-1]).astype(BF16)


def _attn_call(q, kv_parts, q_tile, n_pairs):
    _, b, t, _ = q.shape
    nq = t // q_tile
    in_specs = [pl.BlockSpec((2 * n_pairs, None, q_tile, LANES), lambda bi, g, qi: (g, bi, qi, 0))]
    operands = [q]
    for k, v in kv_parts:
        s = k.shape[2]
        in_specs += [pl.BlockSpec((2 * n_pairs, None, s, LANES), lambda bi, g, qi: (g, bi, 0, 0)),
                     pl.BlockSpec((n_pairs, None, s, LANES), lambda bi, g, qi: (g, bi, 0, 0))]
        operands += [k, v]
    return pl.pallas_call(
        functools.partial(_attn_kernel, n_pairs=n_pairs),
        grid=(b, HEAD_PAIRS // n_pairs, nq),
        in_specs=in_specs,
        out_specs=pl.BlockSpec((q_tile, n_pairs * LANES), lambda bi, g, qi: (bi * nq + qi, g)),
        out_shape=jax.ShapeDtypeStruct((b * t, N_HEADS * V_HEAD), BF16),
        compiler_params=_params(("arbitrary", "arbitrary", "arbitrary"), ATTN_VMEM),
        name=f"attn_{len(kv_parts)}part",
    )(*operands)


def _pool_kernel(x_ref, mod_ref, nw_ref, w_ref, sc_ref, o_ref, band_scr, y_scr, *, seq):
    @pl.when(pl.program_id(0) == 0)
    def _():
        r = lax.broadcasted_iota(jnp.int32, (seq, seq), 0)
        c = lax.broadcasted_iota(jnp.int32, (seq, seq), 1)
        for g, win in enumerate(POOL_WINDOWS):
            inside = (c >= r - win // 2) & (c < r + win - win // 2)
            band_scr[g] = jnp.where(inside, 1.0, 0.0).astype(BF16)

    x = x_ref[...]
    h = _norm_mod(x, nw_ref[...], mod_ref[...], 0)
    h_hi = h.astype(BF16)
    h_lo = (h - h_hi.astype(F32)).astype(BF16)
    t = lax.broadcasted_iota(jnp.int32, (seq, 1), 0)
    for g, win in enumerate(POOL_WINDOWS):
        cnt = (jnp.minimum(t + (win - win // 2), seq) - jnp.maximum(t - win // 2, 0)).astype(F32)
        cols = slice(g * GROUP, (g + 1) * GROUP)
        wg = w_ref[g].astype(BF16)
        for s in range(x.shape[0] // seq):
            rows = slice(s * seq, (s + 1) * seq)
            tot = _dot(band_scr[g], h_hi[rows, cols]) + _dot(band_scr[g], h_lo[rows, cols])
            pooled = (tot / cnt - h[rows, cols]).astype(BF16)
            y_scr[rows, cols] = _dot(pooled, wg)
    o_ref[...] = x + mod_ref[2:3, :] * (y_scr[...] * sc_ref[...])


def _pool_call(x, mods, li, cond_of_row, seq, P):
    m = x.shape[0]
    lj = li // 4
    return pl.pallas_call(
        functools.partial(_pool_kernel, seq=seq),
        grid=(m // ROW_TILE,),
        in_specs=[
            pl.BlockSpec((ROW_TILE, D_MODEL), lambda i: (i, 0)),
            _mod_spec(li, cond_of_row, ROW_TILE, 1),
            pl.BlockSpec((None, 1, D_MODEL), lambda i: (li, 0, 0)),
            pl.BlockSpec((None, N_GROUPS, GROUP, GROUP), lambda i: (lj, 0, 0, 0)),
            pl.BlockSpec((None, 1, D_MODEL), lambda i: (lj, 0, 0)),
        ],
        out_specs=pl.BlockSpec((ROW_TILE, D_MODEL), lambda i: (i, 0)),
        out_shape=jax.ShapeDtypeStruct((m, D_MODEL), F32),
        scratch_shapes=[pltpu.VMEM((N_GROUPS, seq, seq), BF16), pltpu.VMEM((ROW_TILE, D_MODEL), F32)],
        compiler_params=_params(("arbitrary",)),
        name=f"pool_t{seq}",
    )(x, mods, P["norm1_w"], P["pool_w"], P["pool_scale"])


def _dft_tables(seq):
    def cs(n):
        k = np.arange(n, dtype=np.int64)
        ang = 2.0 * np.pi * ((k[:, None] * k[None, :]) % n) / n
        return np.cos(ang), np.sin(ang)
    ct, st = cs(seq)
    cc, sc = cs(GROUP)
    e_t = jnp.asarray(np.concatenate([ct, st], axis=0), dtype=F32)
    e_c = jnp.asarray(np.concatenate([cc, -sc], axis=0), dtype=F32)
    return e_t.astype(BF16), e_c.astype(BF16)


def _fourier_kernel(x_ref, mod_ref, nw_ref, et_ref, ec_ref, w_ref, o_ref, f_scr, *, seq):
    x = x_ref[...]
    h = _norm_mod(x, nw_ref[...], mod_ref[...], 0).astype(BF16)
    norm = np.float32(1.0 / math.sqrt(seq * GROUP))
    for s in range(x.shape[0] // seq):
        rows = slice(s * seq, (s + 1) * seq)
        for g in range(N_GROUPS):
            cols = slice(g * GROUP, (g + 1) * GROUP)
            y = _dot(et_ref[...], h[rows, cols]).astype(BF16)
            f = _dot(y[:seq], ec_ref[:GROUP, :]) + _dot(y[seq:], ec_ref[GROUP:, :])
            f_scr[rows, cols] = (f * norm).astype(BF16)
    o_ref[...] = x + mod_ref[2:3, :] * _dot(f_scr[...], w_ref[...].astype(BF16))


def _fourier_call(x, mods, li, cond_of_row, seq, P):
    m = x.shape[0]
    lj = li // 4
    e_t, e_c = _dft_tables(seq)
    return pl.pallas_call(
        functools.partial(_fourier_kernel, seq=seq),
        grid=(m // ROW_TILE,),
        in_specs=[
            pl.BlockSpec((ROW_TILE, D_MODEL), lambda i: (i, 0)),
            _mod_spec(li, cond_of_row, ROW_TILE, 1),
            pl.BlockSpec((None, 1, D_MODEL), lambda i: (li, 0, 0)),
            pl.BlockSpec((2 * seq, seq), lambda i: (0, 0)),
            pl.BlockSpec((2 * GROUP, GROUP), lambda i: (0, 0)),
            pl.BlockSpec((None, D_MODEL, D_MODEL), lambda i: (lj, 0, 0)),
        ],
        out_specs=pl.BlockSpec((ROW_TILE, D_MODEL), lambda i: (i, 0)),
        out_shape=jax.ShapeDtypeStruct((m, D_MODEL), F32),
        scratch_shapes=[pltpu.VMEM((ROW_TILE, D_MODEL), BF16)],
        compiler_params=_params(("arbitrary",)),
        name=f"fourier_t{seq}",
    )(x, mods, P["norm1_w"], e_t, e_c, P["fnet_w"])


def _rope_tables(rows):
    r, col = jnp.meshgrid(jnp.arange(rows), jnp.arange(GRID_W), indexing="ij")
    r = r.reshape(-1).astype(F32)
    col = col.reshape(-1).astype(F32)
    n_freq = QK_ROPE // 4
    inv = ROPE_THETA ** (-jnp.arange(n_freq, dtype=F32) / n_freq)
    ang = jnp.concatenate([r[:, None] * inv, col[:, None] * inv], axis=-1)
    cos = jnp.repeat(jnp.cos(ang), 2, axis=-1)
    sin = jnp.repeat(jnp.sin(ang), 2, axis=-1)
    t = cos.shape[0]
    pad = LANES - QK_NOPE - QK_ROPE
    cos_p = jnp.concatenate([jnp.ones((t, QK_NOPE), F32), cos, jnp.ones((t, pad), F32)], axis=-1)
    sin_p = jnp.concatenate([jnp.zeros((t, QK_NOPE), F32), sin, jnp.zeros((t, pad), F32)], axis=-1)
    return cos_p, sin_p


def _pair_swap(w):
    wr = w.reshape(w.shape[:-1] + (w.shape[-1] // 2, 2))
    return jnp.stack([-wr[..., 1], wr[..., 0]], axis=-1).reshape(w.shape)


def _mla_weights(P, j, rope, seq_rows):
    pad = LANES - QK_NOPE - QK_ROPE
    wuq = P["mla_wuq"][j].reshape(Q_LORA, N_HEADS, QK_NOPE + QK_ROPE)
    q_slots = jnp.pad(wuq, ((0, 0), (0, 0), (0, pad))).reshape(Q_LORA, N_HEADS * LANES)
    wdkv = P["mla_wdkv"][j]
    kr_slot = jnp.pad(wdkv[:, KV_LORA:], ((0, 0), (QK_NOPE, pad)))
    parts1 = [P["mla_wdq"][j], wdkv[:, :KV_LORA], kr_slot]
    parts2 = [q_slots]
    if rope:
        parts1.append(jnp.pad(_pair_swap(wdkv[:, KV_LORA:]), ((0, 0), (QK_NOPE, pad))))
        q_sw = jnp.pad(_pair_swap(wuq[:, :, QK_NOPE:]), ((0, 0), (0, 0), (QK_NOPE, pad)))
        parts2.append(q_sw.reshape(Q_LORA, N_HEADS * LANES))
        cos, sin = _rope_tables(seq_rows // GRID_W)
    else:
        cos = sin = jnp.zeros((PROJ_TILE, LANES), F32)
    wukv = P["mla_wukv"][j].reshape(KV_LORA, N_HEADS, QK_NOPE + V_HEAD)
    wk = jnp.pad(wukv[:, :, :QK_NOPE], ((0, 0), (0, 0), (0, LANES - QK_NOPE)))
    return dict(
        norm1_w=P["norm1_w"], q_norm=P["mla_q_norm"][j][None, :], kv_norm=P["mla_kv_norm"][j][None, :],
        w1=jnp.concatenate(parts1, axis=1), w2=jnp.concatenate(parts2, axis=1),
        wk=wk.reshape(KV_LORA, N_HEADS * LANES),
        wv=wukv[:, :, QK_NOPE:].reshape(KV_LORA, N_HEADS * V_HEAD), cos=cos, sin=sin)


def _mla_layer(x, mods, li, cond_of_row, batch, seq, P, cache):
    j = li // 4
    W = _mla_weights(P, j, cache is not None, seq)
    q, k, v, ckv, krope = _mla_proj_call(x, mods, li, cond_of_row, W, cache is not None)
    q = q.reshape(N_HEADS, batch, seq, LANES)
    k = k.reshape(N_HEADS, batch, seq, LANES)
    v = v.reshape(HEAD_PAIRS, batch, seq, LANES)
    if cache is None:
        o = _attn_call(q, [(k, v)], seq, HEAD_PAIRS)
    else:
        c_ckv, c_krope = cache
        past = c_ckv.shape[1]
        pad = LANES - QK_NOPE - QK_ROPE
        kr_pad = jnp.pad(c_krope.reshape(batch * past, QK_ROPE), ((0, 0), (QK_NOPE, pad)))
        kc, vc = _kv_cache_call(c_ckv.reshape(batch * past, KV_LORA), kr_pad, W)
        cached = (kc.reshape(N_HEADS, batch, past, LANES), vc.reshape(HEAD_PAIRS, batch, past, LANES))
        o = _attn_call(q, [cached, (k, v)], 512, 4)
    return o, ckv, krope


def _mixer(x, mods, li, cond_of_row, batch, seq, P, cache):
    kind = li % 4
    if kind == 0:
        o, ckv, krope = _mla_layer(x, mods, li, cond_of_row, batch, seq, P, cache)
        return x, (o, P["mla_wo_bf16"], li // 4), (ckv, krope)
    if kind == 1:
        return _pool_call(x, mods, li, cond_of_row, seq, P), None, None
    if kind == 2:
        return _fourier_call(x, mods, li, cond_of_row, seq, P), None, None
    return _sconv_call(x, mods, li, cond_of_row, seq, P), None, None


def _run_passes(passes, mods, P):
    new_cache = None
    for li in range(DEPTH):
        pending = []
        for ps in passes:
            ps["x"], pre, made = _mixer(ps["x"], mods, li, ps["cond_of_row"], ps["batch"], ps["seq"], P,
                                        ps["cache"])
            pending.append(pre)
            if made is not None and ps["cache"] is None:
                new_cache = made
        xs, outs = lax.optimization_barrier(([ps["x"] for ps in passes],
                                              [None if pre is None else pre[0] for pre in pending]))
        pending = [None if pre is None else (o,) + pre[1:] for pre, o in zip(pending, outs)]
        xs = [_ffn_call(x, mods, li, ps["cond_of_row"], ps["seq"], P, pre, final_norm=(li == DEPTH - 1))
              for x, ps, pre in zip(xs, passes, pending)]
        xs = lax.optimization_barrier(xs)
        for ps, x in zip(passes, xs):
            ps["x"] = x
    return new_cache


def kernel(x_prompt, x_sample, cache_ckv, cache_krope, c, c_ctx, norm1_w, norm2_w, ada_w, ada_b, mla_wdq, mla_q_norm, mla_wuq, mla_wdkv, mla_kv_norm, mla_wukv, mla_wo, pool_w, pool_scale, fnet_w, sconv_win, sconv_conv, sconv_wout, ffn_up, ffn_conv_w, ffn_conv_b, ffn_down, final_norm_w):
    row = lambda w: w.reshape(w.shape[0], 1, w.shape[1])
    P = dict(norm1_w=row(norm1_w), norm2_w=row(norm2_w), mla_wdq=mla_wdq, mla_q_norm=mla_q_norm,
             mla_wuq=mla_wuq, mla_wdkv=mla_wdkv, mla_kv_norm=mla_kv_norm, mla_wukv=mla_wukv,
             mla_wo_bf16=mla_wo.astype(BF16),
             pool_w=pool_w, pool_scale=row(pool_scale), fnet_w=fnet_w, sconv_win=sconv_win,
             sconv_conv=sconv_conv, sconv_wout=sconv_wout, ffn_conv_w=ffn_conv_w,
             ffn_conv_b=ffn_conv_b, ffn_down=ffn_down, final_norm_w=final_norm_w,
             ffn_w=functools.cache(lambda l: _layer_weights(ffn_up, ffn_down, l, 2, f"ffn{l}_weights")),
             sconv_w=functools.cache(lambda l: _layer_weights(sconv_win, sconv_wout, l, 3, f"sconv{l}_weights")))
    b_ctx, t_ctx, _ = x_prompt.shape
    b_lat, t_lat, _ = x_sample.shape
    assert ROW_TILE % t_ctx == 0 and t_lat % ROW_TILE == 0 and b_lat + 1 <= N_COND

    conds = jnp.concatenate([c_ctx[None, :], c, jnp.zeros((N_COND - 1 - b_lat, D_MODEL), F32)], axis=0)
    mods = _ada_call(conds, ada_w, ada_b).reshape(DEPTH, N_COND, 6, D_MODEL)

    ctx = dict(x=x_prompt.reshape(b_ctx * t_ctx, D_MODEL), cond_of_row=lambda r: 0, batch=b_ctx, seq=t_ctx,
               cache=None)
    lat = dict(x=x_sample.reshape(b_lat * t_lat, D_MODEL), cond_of_row=lambda r: 1 + r // t_lat, batch=b_lat,
               seq=t_lat, cache=(cache_ckv[:, 0], cache_krope[:, 0]))
    ckv, krope = _run_passes([ctx, lat], mods, P)
    return (ctx["x"].reshape(b_ctx, t_ctx, D_MODEL), lat["x"].reshape(b_lat, t_lat, D_MODEL),
            ckv.reshape(b_ctx, 1, t_ctx, KV_LORA), krope.reshape(b_ctx, 1, t_ctx, QK_ROPE))
```

```python
import functools
import math

import numpy as np
import jax
import jax.numpy as jnp
from jax import lax
from jax.experimental import pallas as pl
from jax.experimental.pallas import tpu as pltpu

F32 = jnp.float32
BF16 = jnp.bfloat16

D_MODEL = 1024
N_HEADS = 16
QK_NOPE = 64
QK_ROPE = 32
V_HEAD = 64
Q_LORA = 384
KV_LORA = 256
ROPE_THETA = 10000.0
GRID_W = 64
POOL_WINDOWS = (2, 4, 8, 16)
N_GROUPS = 4
GROUP = D_MODEL // N_GROUPS
D_FF = 2816
EPS = 1e-6
DEPTH = 4

LANES = 128
Q_SCALE = np.float32(math.log2(math.e) / math.sqrt(QK_NOPE + QK_ROPE))
HEAD_PAIRS = N_HEADS // 2
N_COND = 8
ROW_TILE = 1024
PROJ_TILE = 512
FF_TILE = 256
ROW_BLOCK = 256
HALO = 8
RELAYOUT_ROWS = 128
ADA_TILE = 1536
MIB = 1024 * 1024
VMEM_DEFAULT = 48 * MIB
VMEM_HEADROOM = 4 * MIB + 512 * 1024
ATTN_VMEM = 48 * MIB


def _params(sem, vmem_bytes=VMEM_DEFAULT):
    return pltpu.CompilerParams(dimension_semantics=sem, vmem_limit_bytes=int(vmem_bytes))


def _nbytes(a):
    return math.prod(a.shape) * jnp.dtype(a.dtype).itemsize


def _dot(a, b):
    return jnp.dot(a, b, preferred_element_type=F32)


def _rms(x, w):
    return x * lax.rsqrt(jnp.mean(x * x, axis=-1, keepdims=True) + EPS) * w


def _norm_mod(x, nw, mod, base):
    return _rms(x, nw) * (1.0 + mod[base + 1:base + 2, :]) + mod[base:base + 1, :]


def _gelu_exact(x):
    return 0.5 * x * (1.0 + lax.erf(x * np.float32(math.sqrt(0.5))))


def _ada_kernel(c_ref, w_ref, b_ref, o_ref):
    s = jax.nn.silu(c_ref[...]).astype(BF16)
    o_ref[...] = _dot(s, w_ref[...].astype(BF16)) + b_ref[...]


def _ada_call(conds, ada_w, ada_b):
    n_out = ada_w.shape[-1]
    return pl.pallas_call(
        _ada_kernel,
        grid=(DEPTH, n_out // ADA_TILE),
        in_specs=[
            pl.BlockSpec((N_COND, D_MODEL), lambda l, j: (0, 0)),
            pl.BlockSpec((None, D_MODEL, ADA_TILE), lambda l, j: (l, 0, j)),
            pl.BlockSpec((None, 1, ADA_TILE), lambda l, j: (l, 0, j)),
        ],
        out_specs=pl.BlockSpec((None, N_COND, ADA_TILE), lambda l, j: (l, 0, j)),
        out_shape=jax.ShapeDtypeStruct((DEPTH, N_COND, n_out), F32),
        compiler_params=_params(("arbitrary", "arbitrary"), 2 * D_MODEL * ADA_TILE * 4 + VMEM_HEADROOM),
        name="ada",
    )(conds, ada_w, ada_b.reshape(DEPTH, 1, n_out))


def _mod_spec(li, cond_of_row, tile, n_grid):
    if n_grid == 1:
        return pl.BlockSpec((None, None, 6, D_MODEL), lambda i: (li, cond_of_row(i * tile), 0, 0))
    return pl.BlockSpec((None, None, 6, D_MODEL), lambda i, j: (li, cond_of_row(i * tile), 0, 0))


def _conv_window(w, cw, t0, seq):
    n = w.shape[0]
    mid = slice(HALO, HALO + ROW_BLOCK)
    t = (t0 + lax.broadcasted_iota(jnp.int32, (ROW_BLOCK, 1), 0)) & (seq - 1)
    prev = jnp.where(t == 0, 0.0, pltpu.roll(w, 1, axis=0)[mid])
    nxt = jnp.where(t == seq - 1, 0.0, pltpu.roll(w, n - 1, axis=0)[mid])
    return prev * cw[0:1, :] + w[mid] * cw[1:2, :] + nxt * cw[2:3, :]


def _ffn_act(win, mid, cw, cb, t0, seq):
    return _gelu_exact(_conv_window(win(0), cw, t0, seq) + cb) * mid(1)


def _sconv_act(win, mid, cw, cb, t0, seq):
    return mid(0) * _conv_window(win(1) * win(2), cw, t0, seq)


def _ffn_kernel(x_ref, mod_ref, nw_ref, up_ref, cw_ref, cb_ref, down_ref, fnw_ref, o_ref, *scratch, **cfg):
    _glu_body(x_ref, mod_ref, nw_ref, up_ref, cw_ref, cb_ref, down_ref, fnw_ref, None, o_ref, *scratch,
              n_up=2, act=_ffn_act, mod_base=3, **cfg)


def _ffn_pre_kernel(x_ref, mod_ref, nw_ref, up_ref, cw_ref, cb_ref, down_ref, fnw_ref, a_ref, wa_ref, o_ref,
                    *scratch, **cfg):
    _glu_body(x_ref, mod_ref, nw_ref, up_ref, cw_ref, cb_ref, down_ref, fnw_ref, (a_ref, wa_ref), o_ref,
              *scratch, n_up=2, act=_ffn_act, mod_base=3, **cfg)


def _sconv_kernel(x_ref, mod_ref, nw_ref, up_ref, cw_ref, down_ref, o_ref, *scratch, **cfg):
    _glu_body(x_ref, mod_ref, nw_ref, up_ref, cw_ref, None, down_ref, None, None, o_ref, *scratch,
              n_up=3, act=_sconv_act, mod_base=0, final_norm=False, **cfg)


def _glu_body(x_ref, mod_ref, nw_ref, up_ref, cw_ref, cb_ref, down_ref, fnw_ref, pre, o_ref,
              h_scr, acc_scr, gu0, gu1, *, n_up, n_chunks, act, seq, mod_base, final_norm):
    gu = (gu0, gu1)
    n_blocks = ROW_TILE // ROW_BLOCK

    def up(c, slot, r):
        rows = slice(r * ROW_BLOCK, (r + 1) * ROW_BLOCK)
        res = _dot(h_scr[rows, :], up_ref[c])
        for p in range(n_up):
            gu[slot][p, HALO + r * ROW_BLOCK:HALO + (r + 1) * ROW_BLOCK, :] = res[:, p * FF_TILE:(p + 1) * FF_TILE]

    def activate(c, slot, r):
        win = lambda p: gu[slot][p, r * ROW_BLOCK:(r + 1) * ROW_BLOCK + 2 * HALO, :]
        mid = lambda p: gu[slot][p, HALO + r * ROW_BLOCK:HALO + (r + 1) * ROW_BLOCK, :]
        cb = None if cb_ref is None else cb_ref[c]
        return act(win, mid, cw_ref[c], cb, r * ROW_BLOCK, seq).astype(BF16)

    def down(a, c, r):
        rows = slice(r * ROW_BLOCK, (r + 1) * ROW_BLOCK)
        y = _dot(a, down_ref[c])
        if isinstance(c, int) and c == 0:
            acc_scr[rows, :] = y
        else:
            acc_scr[rows, :] += y

    def stage(s, slot):
        for r in range(n_blocks):
            a = activate(s - 1, 1 - slot, r)
            up(s, slot, r)
            down(a, s - 1, r)

    def block(r):
        return slice(r * ROW_BLOCK, (r + 1) * ROW_BLOCK)

    def mixer_proj(r):
        a_ref, wa_ref = pre
        return _dot(a_ref[block(r), :], wa_ref[...])

    for g in gu:
        g[:, :HALO, :] = jnp.zeros((n_up, HALO, FF_TILE), F32)
        g[:, HALO + ROW_TILE:, :] = jnp.zeros((n_up, HALO, FF_TILE), F32)

    proj = mixer_proj(0) if pre is not None else None
    for r in range(n_blocks):
        x = x_ref[block(r), :]
        if pre is not None:
            x = x + mod_ref[2:3, :] * proj
            o_ref[block(r), :] = x
            if r + 1 < n_blocks:
                proj = mixer_proj(r + 1)
        h_scr[block(r), :] = _norm_mod(x, nw_ref[...], mod_ref[...], mod_base).astype(BF16)
        up(0, 0, r)

    if n_chunks > 1:
        stage(1, 1)
    n_pairs = (n_chunks - 2) // 2

    def pair(i, carry):
        s = 2 * i + 2
        stage(s, 0)
        stage(s + 1, 1)
        return carry

    lax.fori_loop(0, n_pairs, pair, 0)
    for s in range(2 * n_pairs + 2, n_chunks):
        stage(s, s % 2)
    last = n_chunks - 1
    res = x_ref if pre is None else o_ref
    for r in range(n_blocks):
        down(activate(last, last % 2, r), last, r)
        y = res[block(r), :] + mod_ref[mod_base + 2:mod_base + 3, :] * acc_scr[block(r), :]
        if final_norm:
            y = _rms(y, fnw_ref[...])
        o_ref[block(r), :] = y


def _chunked(w, n_chunks):
    return w.reshape(w.shape[0], n_chunks, FF_TILE).transpose(1, 0, 2)


def _glu_call(kernel_fn, name, x, mods, li, cond_of_row, n_up, n_chunks, norm_w, operands, specs, cfg):
    m = x.shape[0]
    whole = lambda a: pl.BlockSpec(a.shape, lambda i: (0,) * a.ndim)
    in_specs = [pl.BlockSpec((ROW_TILE, D_MODEL), lambda i: (i, 0)),
                _mod_spec(li, cond_of_row, ROW_TILE, 1),
                pl.BlockSpec((None, 1, D_MODEL), lambda i: (li, 0, 0))]
    layer_w = lambda a, l: pl.BlockSpec((None,) + a.shape[1:], lambda i: (l,) + (0,) * (a.ndim - 1),
                                        pipeline_mode=pl.Buffered(1))
    rows = lambda a: pl.BlockSpec((ROW_TILE, a.shape[1]), lambda i: (i, 0))
    spec_of = {None: whole, "rows": rows, "resident": lambda a: pl.BlockSpec(memory_space=pltpu.VMEM)}
    in_specs += [layer_w(a, s) if isinstance(s, int) else spec_of[s](a) for a, s in zip(operands, specs)]
    gu_shape = (n_up, ROW_TILE + 2 * HALO, FF_TILE)
    tile = ROW_TILE * D_MODEL
    window = {None: _nbytes, "rows": lambda a: 2 * ROW_TILE * a.shape[1] * a.dtype.itemsize,
              "resident": lambda a: 0}
    vmem = (2 * 2 * tile * 4 + tile * 2 + tile * 4 + 2 * math.prod(gu_shape) * 4 + VMEM_HEADROOM
            + sum(_nbytes(a) // a.shape[0] if isinstance(s, int) else window[s](a)
                  for a, s in zip(operands, specs)))
    return pl.pallas_call(
        functools.partial(kernel_fn, n_chunks=n_chunks, **cfg),
        grid=(m // ROW_TILE,),
        in_specs=in_specs,
        out_specs=pl.BlockSpec((ROW_TILE, D_MODEL), lambda i: (i, 0)),
        out_shape=jax.ShapeDtypeStruct((m, D_MODEL), F32),
        scratch_shapes=[
            pltpu.VMEM((ROW_TILE, D_MODEL), BF16),
            pltpu.VMEM((ROW_TILE, D_MODEL), F32),
            pltpu.VMEM(gu_shape, F32), pltpu.VMEM(gu_shape, F32),
        ],
        compiler_params=_params(("arbitrary",), vmem),
        name=name,
    )(x, mods, norm_w, *operands)


def _in_vmem(a):
    return pltpu.with_memory_space_constraint(a, pltpu.VMEM)


def _layer_weights_kernel(up_ref, dn_ref, uo_ref, do_ref, *, n_up, n_chunks):
    for c in range(n_chunks):
        for p in range(n_up):
            col = (p * n_chunks + c) * FF_TILE
            uo_ref[c, :, p * FF_TILE:(p + 1) * FF_TILE] = up_ref[:, col:col + FF_TILE].astype(BF16)
    do_ref[...] = dn_ref[...].astype(BF16)


def _layer_weights(up, down, layer, n_up, name):
    _, d, cols = up.shape
    _, hidden, d_out = down.shape
    n_chunks = hidden // FF_TILE
    steps = d // RELAYOUT_ROWS
    dn_rows = hidden // steps
    up_cm, dn = pl.pallas_call(
        functools.partial(_layer_weights_kernel, n_up=n_up, n_chunks=n_chunks),
        grid=(steps,),
        in_specs=[pl.BlockSpec((None, RELAYOUT_ROWS, cols), lambda k: (layer, k, 0)),
                  pl.BlockSpec((None, dn_rows, d_out), lambda k: (layer, k, 0))],
        out_specs=[pl.BlockSpec((n_chunks, RELAYOUT_ROWS, n_up * FF_TILE), lambda k: (0, k, 0)),
                   pl.BlockSpec((dn_rows, d_out), lambda k: (k, 0))],
        out_shape=[jax.ShapeDtypeStruct((n_chunks, d, n_up * FF_TILE), BF16),
                   jax.ShapeDtypeStruct((hidden, d_out), BF16)],
        compiler_params=_params(("arbitrary",),
                                2 * (RELAYOUT_ROWS * cols + dn_rows * d_out) * (4 + 2) + VMEM_HEADROOM),
        name=name,
    )(up, down)
    return _in_vmem(up_cm), _in_vmem(dn.reshape(n_chunks, FF_TILE, d_out))


def _ffn_call(x, mods, li, cond_of_row, seq, P, pre, final_norm):
    n = D_FF // FF_TILE
    up_cm, down_cm = P["ffn_w"](li)
    operands = [up_cm, _chunked(P["ffn_conv_w"][li], n), _chunked(P["ffn_conv_b"][li][None, :], n),
                down_cm, P["final_norm_w"].reshape(1, D_MODEL)]
    specs = ["resident", None, None, "resident", None]
    kernel_fn = _ffn_kernel
    if pre is not None:
        a, w_a, l_a = pre
        operands += [a, w_a]
        specs += ["rows", l_a]
        kernel_fn = _ffn_pre_kernel
    return _glu_call(kernel_fn, f"ffn{li}", x, mods, li, cond_of_row, 2, n, P["norm2_w"],
                     operands, specs, dict(seq=seq, final_norm=final_norm))


def _sconv_call(x, mods, li, cond_of_row, seq, P):
    lj = li // 4
    n = D_MODEL // FF_TILE
    up_cm, down_cm = P["sconv_w"](lj)
    operands = [up_cm, _chunked(P["sconv_conv"][lj], n), down_cm]
    return _glu_call(_sconv_kernel, f"sconv{li}", x, mods, li, cond_of_row, 3, n, P["norm1_w"],
                     operands, ["resident", None, "resident"], dict(seq=seq))


def _emit_kv(ckv, krope_pad, wk_ref, wv_ref, k_ref, v_ref):
    c = ckv.astype(BF16)
    kx = _dot(c, wk_ref[...].astype(BF16))
    vx = _dot(c, wv_ref[...].astype(BF16))
    for h in range(N_HEADS):
        k_ref[h] = (kx[:, h * LANES:(h + 1) * LANES] + krope_pad).astype(BF16)
    for p in range(HEAD_PAIRS):
        v_ref[p] = vx[:, p * LANES:(p + 1) * LANES].astype(BF16)


def _mla_proj_kernel(x_ref, mod_ref, nw_ref, w1_ref, qn_ref, kvn_ref, w2_ref, wk_ref, wv_ref, cos_ref,
                     sin_ref, q_ref, k_ref, v_ref, ckv_ref, kr_ref, *, rope):
    h = _norm_mod(x_ref[...], nw_ref[...], mod_ref[...], 0).astype(BF16)
    z = _dot(h, w1_ref[...].astype(BF16))
    cq = _rms(z[:, :Q_LORA], qn_ref[...]).astype(BF16)
    ckv = _rms(z[:, Q_LORA:Q_LORA + KV_LORA], kvn_ref[...])
    kr = z[:, Q_LORA + KV_LORA:Q_LORA + KV_LORA + LANES]
    if rope:
        cos, sin = cos_ref[...], sin_ref[...]
        kr = kr * cos + z[:, Q_LORA + KV_LORA + LANES:] * sin
    for p in range(HEAD_PAIRS):
        q2 = _dot(cq, w2_ref[:, 2 * p * LANES:(2 * p + 2) * LANES].astype(BF16))
        if rope:
            off = N_HEADS * LANES
            qs = _dot(cq, w2_ref[:, off + 2 * p * LANES:off + (2 * p + 2) * LANES].astype(BF16))
        for e in range(2):
            qh = q2[:, e * LANES:(e + 1) * LANES]
            if rope:
                qh = qh * cos + qs[:, e * LANES:(e + 1) * LANES] * sin
            q_ref[2 * p + e] = (qh * Q_SCALE).astype(BF16)
    ckv_ref[...] = ckv
    kr_ref[...] = kr[:, QK_NOPE:QK_NOPE + QK_ROPE]
    _emit_kv(ckv, kr, wk_ref, wv_ref, k_ref, v_ref)


def _mla_proj_call(x, mods, li, cond_of_row, W, rope):
    m = x.shape[0]
    n1 = W["w1"].shape[1]
    n2 = W["w2"].shape[1]
    const = lambda i: (0, 0)
    pos_tiles = W["cos"].shape[0] // PROJ_TILE
    pos = lambda i: (i % pos_tiles, 0)
    return pl.pallas_call(
        functools.partial(_mla_proj_kernel, rope=rope),
        grid=(m // PROJ_TILE,),
        in_specs=[
            pl.BlockSpec((PROJ_TILE, D_MODEL), lambda i: (i, 0)),
            _mod_spec(li, cond_of_row, PROJ_TILE, 1),
            pl.BlockSpec((None, 1, D_MODEL), lambda i: (li, 0, 0)),
            pl.BlockSpec((D_MODEL, n1), const),
            pl.BlockSpec((1, Q_LORA), const),
            pl.BlockSpec((1, KV_LORA), const),
            pl.BlockSpec((Q_LORA, n2), const),
            pl.BlockSpec((KV_LORA, N_HEADS * LANES), const),
            pl.BlockSpec((KV_LORA, HEAD_PAIRS * LANES), const),
            pl.BlockSpec((PROJ_TILE, LANES), pos),
            pl.BlockSpec((PROJ_TILE, LANES), pos),
        ],
        out_specs=[
            pl.BlockSpec((N_HEADS, PROJ_TILE, LANES), lambda i: (0, i, 0)),
            pl.BlockSpec((N_HEADS, PROJ_TILE, LANES), lambda i: (0, i, 0)),
            pl.BlockSpec((HEAD_PAIRS, PROJ_TILE, LANES), lambda i: (0, i, 0)),
            pl.BlockSpec((PROJ_TILE, KV_LORA), lambda i: (i, 0)),
            pl.BlockSpec((PROJ_TILE, QK_ROPE), lambda i: (i, 0)),
        ],
        out_shape=[
            jax.ShapeDtypeStruct((N_HEADS, m, LANES), BF16),
            jax.ShapeDtypeStruct((N_HEADS, m, LANES), BF16),
            jax.ShapeDtypeStruct((HEAD_PAIRS, m, LANES), BF16),
            jax.ShapeDtypeStruct((m, KV_LORA), F32),
            jax.ShapeDtypeStruct((m, QK_ROPE), F32),
        ],
        compiler_params=_params(("arbitrary",)),
        name="mla_proj_rope" if rope else "mla_proj",
    )(x, mods, W["norm1_w"], W["w1"], W["q_norm"], W["kv_norm"], W["w2"], W["wk"], W["wv"],
      W["cos"], W["sin"])


def _kv_cache_kernel(ckv_ref, kr_ref, wk_ref, wv_ref, k_ref, v_ref):
    _emit_kv(ckv_ref[...], kr_ref[...], wk_ref, wv_ref, k_ref, v_ref)


def _kv_cache_call(ckv, krope_pad, W):
    m = ckv.shape[0]
    const = lambda i: (0, 0)
    return pl.pallas_call(
        _kv_cache_kernel,
        grid=(1,),
        in_specs=[
            pl.BlockSpec((m, KV_LORA), const),
            pl.BlockSpec((m, LANES), const),
            pl.BlockSpec((KV_LORA, N_HEADS * LANES), const),
            pl.BlockSpec((KV_LORA, HEAD_PAIRS * LANES), const),
        ],
        out_specs=[
            pl.BlockSpec((N_HEADS, m, LANES), lambda i: (0, 0, 0)),
            pl.BlockSpec((HEAD_PAIRS, m, LANES), lambda i: (0, 0, 0)),
        ],
        out_shape=[
            jax.ShapeDtypeStruct((N_HEADS, m, LANES), BF16),
            jax.ShapeDtypeStruct((HEAD_PAIRS, m, LANES), BF16),
        ],
        compiler_params=_params(("arbitrary",)),
        name="kv_cache",
    )(ckv, krope_pad, W["wk"], W["wv"])


def _attn_kernel(q_ref, *refs, n_pairs):
    o_ref, kv = refs[-1], refs[:-1]
    parts = [(kv[i], kv[i + 1]) for i in range(0, len(kv), 2)]
    heads = [(p, e) for p in range(n_pairs) for e in range(2)]

    def scores(p, e):
        q = q_ref[2 * p + e]
        return [lax.dot_general(q, k_ref[2 * p + e], (((1,), (1,)), ((), ())), preferred_element_type=F32)
                for k_ref, _ in parts]

    outs = []
    s_next = scores(*heads[0])
    for i, (p, e) in enumerate(heads):
        s = s_next
        if i + 1 < len(heads):
            s_next = scores(*heads[i + 1])
        m = functools.reduce(jnp.maximum, [jnp.max(x, axis=-1, keepdims=True) for x in s])
        w = [jnp.exp2(x - m) for x in s]
        l = sum(jnp.sum(x, axis=-1, keepdims=True) for x in w)
        pv = sum(_dot(x.astype(BF16), v_ref[p]) for x, (_, v_ref) in zip(w, parts))
        outs.append(pv / l)
        if e == 1:
            lane = lax.broadcasted_iota(jnp.int32, pv.shape, 1)
            o_ref[:, p * LANES:(p + 1) * LANES] = jnp.where(lane < V_HEAD, outs[-2], outs[-1]).astype(BF16)


def _attn_call(q, kv_parts, q_tile, n_pairs):
    _, b, t, _ = q.shape
    nq = t // q_tile
    in_specs = [pl.BlockSpec((2 * n_pairs, None, q_tile, LANES), lambda bi, g, qi: (g, bi, qi, 0))]
    operands = [q]
    for k, v in kv_parts:
        s = k.shape[2]
        in_specs += [pl.BlockSpec((2 * n_pairs, None, s, LANES), lambda bi, g, qi: (g, bi, 0, 0)),
                     pl.BlockSpec((n_pairs, None, s, LANES), lambda bi, g, qi: (g, bi, 0, 0))]
        operands += [k, v]
    return pl.pallas_call(
        functools.partial(_attn_kernel, n_pairs=n_pairs),
        grid=(b, HEAD_PAIRS // n_pairs, nq),
        in_specs=in_specs,
        out_specs=pl.BlockSpec((q_tile, n_pairs * LANES), lambda bi, g, qi: (bi * nq + qi, g)),
        out_shape=jax.ShapeDtypeStruct((b * t, N_HEADS * V_HEAD), BF16),
        compiler_params=_params(("arbitrary", "arbitrary", "arbitrary"), ATTN_VMEM),
        name=f"attn_{len(kv_parts)}part",
    )(*operands)


def _pool_kernel(x_ref, mod_ref, nw_ref, w_ref, sc_ref, o_ref, band_scr, y_scr, *, seq):
    @pl.when(pl.program_id(0) == 0)
    def _():
        r = lax.broadcasted_iota(jnp.int32, (seq, seq), 0)
        c = lax.broadcasted_iota(jnp.int32, (seq, seq), 1)
        for g, win in enumerate(POOL_WINDOWS):
            inside = (c >= r - win // 2) & (c < r + win - win // 2)
            band_scr[g] = jnp.where(inside, 1.0, 0.0).astype(BF16)

    x = x_ref[...]
    h = _norm_mod(x, nw_ref[...], mod_ref[...], 0)
    h_hi = h.astype(BF16)
    h_lo = (h - h_hi.astype(F32)).astype(BF16)
    t = lax.broadcasted_iota(jnp.int32, (seq, 1), 0)
    for g, win in enumerate(POOL_WINDOWS):
        cnt = (jnp.minimum(t + (win - win // 2), seq) - jnp.maximum(t - win // 2, 0)).astype(F32)
        cols = slice(g * GROUP, (g + 1) * GROUP)
        wg = w_ref[g].astype(BF16)
        for s in range(x.shape[0] // seq):
            rows = slice(s * seq, (s + 1) * seq)
            tot = _dot(band_scr[g], h_hi[rows, cols]) + _dot(band_scr[g], h_lo[rows, cols])
            pooled = (tot / cnt - h[rows, cols]).astype(BF16)
            y_scr[rows, cols] = _dot(pooled, wg)
    o_ref[...] = x + mod_ref[2:3, :] * (y_scr[...] * sc_ref[...])


def _pool_call(x, mods, li, cond_of_row, seq, P):
    m = x.shape[0]
    lj = li // 4
    return pl.pallas_call(
        functools.partial(_pool_kernel, seq=seq),
        grid=(m // ROW_TILE,),
        in_specs=[
            pl.BlockSpec((ROW_TILE, D_MODEL), lambda i: (i, 0)),
            _mod_spec(li, cond_of_row, ROW_TILE, 1),
            pl.BlockSpec((None, 1, D_MODEL), lambda i: (li, 0, 0)),
            pl.BlockSpec((None, N_GROUPS, GROUP, GROUP), lambda i: (lj, 0, 0, 0)),
            pl.BlockSpec((None, 1, D_MODEL), lambda i: (lj, 0, 0)),
        ],
        out_specs=pl.BlockSpec((ROW_TILE, D_MODEL), lambda i: (i, 0)),
        out_shape=jax.ShapeDtypeStruct((m, D_MODEL), F32),
        scratch_shapes=[pltpu.VMEM((N_GROUPS, seq, seq), BF16), pltpu.VMEM((ROW_TILE, D_MODEL), F32)],
        compiler_params=_params(("arbitrary",)),
        name=f"pool_t{seq}",
    )(x, mods, P["norm1_w"], P["pool_w"], P["pool_scale"])


def _dft_tables(seq):
    def cs(n):
        k = np.arange(n, dtype=np.int64)
        ang = 2.0 * np.pi * ((k[:, None] * k[None, :]) % n) / n
        return np.cos(ang), np.sin(ang)
    ct, st = cs(seq)
    cc, sc = cs(GROUP)
    e_t = jnp.asarray(np.concatenate([ct, st], axis=0), dtype=F32)
    e_c = jnp.asarray(np.concatenate([cc, -sc], axis=0), dtype=F32)
    return e_t.astype(BF16), e_c.astype(BF16)


def _fourier_kernel(x_ref, mod_ref, nw_ref, et_ref, ec_ref, w_ref, o_ref, f_scr, *, seq):
    x = x_ref[...]
    h = _norm_mod(x, nw_ref[...], mod_ref[...], 0).astype(BF16)
    norm = np.float32(1.0 / math.sqrt(seq * GROUP))
    for s in range(x.shape[0] // seq):
        rows = slice(s * seq, (s + 1) * seq)
        for g in range(N_GROUPS):
            cols = slice(g * GROUP, (g + 1) * GROUP)
            y = _dot(et_ref[...], h[rows, cols]).astype(BF16)
            f = _dot(y[:seq], ec_ref[:GROUP, :]) + _dot(y[seq:], ec_ref[GROUP:, :])
            f_scr[rows, cols] = (f * norm).astype(BF16)
    o_ref[...] = x + mod_ref[2:3, :] * _dot(f_scr[...], w_ref[...].astype(BF16))


def _fourier_call(x, mods, li, cond_of_row, seq, P):
    m = x.shape[0]
    lj = li // 4
    e_t, e_c = _dft_tables(seq)
    return pl.pallas_call(
        functools.partial(_fourier_kernel, seq=seq),
        grid=(m // ROW_TILE,),
        in_specs=[
            pl.BlockSpec((ROW_TILE, D_MODEL), lambda i: (i, 0)),
            _mod_spec(li, cond_of_row, ROW_TILE, 1),
            pl.BlockSpec((None, 1, D_MODEL), lambda i: (li, 0, 0)),
            pl.BlockSpec((2 * seq, seq), lambda i: (0, 0)),
            pl.BlockSpec((2 * GROUP, GROUP), lambda i: (0, 0)),
            pl.BlockSpec((None, D_MODEL, D_MODEL), lambda i: (lj, 0, 0)),
        ],
        out_specs=pl.BlockSpec((ROW_TILE, D_MODEL), lambda i: (i, 0)),
        out_shape=jax.ShapeDtypeStruct((m, D_MODEL), F32),
        scratch_shapes=[pltpu.VMEM((ROW_TILE, D_MODEL), BF16)],
        compiler_params=_params(("arbitrary",)),
        name=f"fourier_t{seq}",
    )(x, mods, P["norm1_w"], e_t, e_c, P["fnet_w"])


def _rope_tables(rows):
    r, col = jnp.meshgrid(jnp.arange(rows), jnp.arange(GRID_W), indexing="ij")
    r = r.reshape(-1).astype(F32)
    col = col.reshape(-1).astype(F32)
    n_freq = QK_ROPE // 4
    inv = ROPE_THETA ** (-jnp.arange(n_freq, dtype=F32) / n_freq)
    ang = jnp.concatenate([r[:, None] * inv, col[:, None] * inv], axis=-1)
    cos = jnp.repeat(jnp.cos(ang), 2, axis=-1)
    sin = jnp.repeat(jnp.sin(ang), 2, axis=-1)
    t = cos.shape[0]
    pad = LANES - QK_NOPE - QK_ROPE
    cos_p = jnp.concatenate([jnp.ones((t, QK_NOPE), F32), cos, jnp.ones((t, pad), F32)], axis=-1)
    sin_p = jnp.concatenate([jnp.zeros((t, QK_NOPE), F32), sin, jnp.zeros((t, pad), F32)], axis=-1)
    return cos_p, sin_p


def _pair_swap(w):
    wr = w.reshape(w.shape[:-1] + (w.shape[-1] // 2, 2))
    return jnp.stack([-wr[..., 1], wr[..., 0]], axis=-1).reshape(w.shape)


def _mla_weights(P, j, rope, seq_rows):
    pad = LANES - QK_NOPE - QK_ROPE
    wuq = P["mla_wuq"][j].reshape(Q_LORA, N_HEADS, QK_NOPE + QK_ROPE)
    q_slots = jnp.pad(wuq, ((0, 0), (0, 0), (0, pad))).reshape(Q_LORA, N_HEADS * LANES)
    wdkv = P["mla_wdkv"][j]
    kr_slot = jnp.pad(wdkv[:, KV_LORA:], ((0, 0), (QK_NOPE, pad)))
    parts1 = [P["mla_wdq"][j], wdkv[:, :KV_LORA], kr_slot]
    parts2 = [q_slots]
    if rope:
        parts1.append(jnp.pad(_pair_swap(wdkv[:, KV_LORA:]), ((0, 0), (QK_NOPE, pad))))
        q_sw = jnp.pad(_pair_swap(wuq[:, :, QK_NOPE:]), ((0, 0), (0, 0), (QK_NOPE, pad)))
        parts2.append(q_sw.reshape(Q_LORA, N_HEADS * LANES))
        cos, sin = _rope_tables(seq_rows // GRID_W)
    else:
        cos = sin = jnp.zeros((PROJ_TILE, LANES), F32)
    wukv = P["mla_wukv"][j].reshape(KV_LORA, N_HEADS, QK_NOPE + V_HEAD)
    wk = jnp.pad(wukv[:, :, :QK_NOPE], ((0, 0), (0, 0), (0, LANES - QK_NOPE)))
    return dict(
        norm1_w=P["norm1_w"], q_norm=P["mla_q_norm"][j][None, :], kv_norm=P["mla_kv_norm"][j][None, :],
        w1=jnp.concatenate(parts1, axis=1), w2=jnp.concatenate(parts2, axis=1),
        wk=wk.reshape(KV_LORA, N_HEADS * LANES),
        wv=wukv[:, :, QK_NOPE:].reshape(KV_LORA, N_HEADS * V_HEAD), cos=cos, sin=sin)


def _mla_layer(x, mods, li, cond_of_row, batch, seq, P, cache):
    j = li // 4
    W = _mla_weights(P, j, cache is not None, seq)
    q, k, v, ckv, krope = _mla_proj_call(x, mods, li, cond_of_row, W, cache is not None)
    q = q.reshape(N_HEADS, batch, seq, LANES)
    k = k.reshape(N_HEADS, batch, seq, LANES)
    v = v.reshape(HEAD_PAIRS, batch, seq, LANES)
    if cache is None:
        o = _attn_call(q, [(k, v)], seq, HEAD_PAIRS)
    else:
        c_ckv, c_krope = cache
        past = c_ckv.shape[1]
        pad = LANES - QK_NOPE - QK_ROPE
        kr_pad = jnp.pad(c_krope.reshape(batch * past, QK_ROPE), ((0, 0), (QK_NOPE, pad)))
        kc, vc = _kv_cache_call(c_ckv.reshape(batch * past, KV_LORA), kr_pad, W)
        cached = (kc.reshape(N_HEADS, batch, past, LANES), vc.reshape(HEAD_PAIRS, batch, past, LANES))
        o = _attn_call(q, [cached, (k, v)], 512, 4)
    return o, ckv, krope


def _mixer(x, mods, li, cond_of_row, batch, seq, P, cache):
    kind = li % 4
    if kind == 0:
        o, ckv, krope = _mla_layer(x, mods, li, cond_of_row, batch, seq, P, cache)
        return x, (o, P["mla_wo_bf16"], li // 4), (ckv, krope)
    if kind == 1:
        return _pool_call(x, mods, li, cond_of_row, seq, P), None, None
    if kind == 2:
        return _fourier_call(x, mods, li, cond_of_row, seq, P), None, None
    return _sconv_call(x, mods, li, cond_of_row, seq, P), None, None


def _run_passes(passes, mods, P):
    new_cache = None
    for li in range(DEPTH):
        pending = []
        for ps in passes:
            ps["x"], pre, made = _mixer(ps["x"], mods, li, ps["cond_of_row"], ps["batch"], ps["seq"], P,
                                        ps["cache"])
            pending.append(pre)
            if made is not None and ps["cache"] is None:
                new_cache = made
        xs, outs = lax.optimization_barrier(([ps["x"] for ps in passes],
                                              [None if pre is None else pre[0] for pre in pending]))
        pending = [None if pre is None else (o,) + pre[1:] for pre, o in zip(pending, outs)]
        xs = [_ffn_call(x, mods, li, ps["cond_of_row"], ps["seq"], P, pre, final_norm=(li == DEPTH - 1))
              for x, ps, pre in zip(xs, passes, pending)]
        xs = lax.optimization_barrier(xs)
        for ps, x in zip(passes, xs):
            ps["x"] = x
    return new_cache


def kernel(x_prompt, x_sample, cache_ckv, cache_krope, c, c_ctx, norm1_w, norm2_w, ada_w, ada_b, mla_wdq, mla_q_norm, mla_wuq, mla_wdkv, mla_kv_norm, mla_wukv, mla_wo, pool_w, pool_scale, fnet_w, sconv_win, sconv_conv, sconv_wout, ffn_up, ffn_conv_w, ffn_conv_b, ffn_down, final_norm_w):
    row = lambda w: w.reshape(w.shape[0], 1, w.shape[1])
    P = dict(norm1_w=row(norm1_w), norm2_w=row(norm2_w), mla_wdq=mla_wdq, mla_q_norm=mla_q_norm,
             mla_wuq=mla_wuq, mla_wdkv=mla_wdkv, mla_kv_norm=mla_kv_norm, mla_wukv=mla_wukv,
             mla_wo_bf16=mla_wo.astype(BF16),
             pool_w=pool_w, pool_scale=row(pool_scale), fnet_w=fnet_w, sconv_win=sconv_win,
             sconv_conv=sconv_conv, sconv_wout=sconv_wout, ffn_conv_w=ffn_conv_w,
             ffn_conv_b=ffn_conv_b, ffn_down=ffn_down, final_norm_w=final_norm_w,
             ffn_w=functools.cache(lambda l: _layer_weights(ffn_up, ffn_down, l, 2, f"ffn{l}_weights")),
             sconv_w=functools.cache(lambda l: _layer_weights(sconv_win, sconv_wout, l, 3, f"sconv{l}_weights")))
    b_ctx, t_ctx, _ = x_prompt.shape
    b_lat, t_lat, _ = x_sample.shape
    assert ROW_TILE % t_ctx == 0 and t_lat % ROW_TILE == 0 and b_lat + 1 <= N_COND

    conds = jnp.concatenate([c_ctx[None, :], c, jnp.zeros((N_COND - 1 - b_lat, D_MODEL), F32)], axis=0)
    mods = _ada_call(conds, ada_w, ada_b).reshape(DEPTH, N_COND, 6, D_MODEL)

    ctx = dict(x=x_prompt.reshape(b_ctx * t_ctx, D_MODEL), cond_of_row=lambda r: 0, batch=b_ctx, seq=t_ctx,
               cache=None)
    lat = dict(x=x_sample.reshape(b_lat * t_lat, D_MODEL), cond_of_row=lambda r: 1 + r // t_lat, batch=b_lat,
               seq=t_lat, cache=(cache_ckv[:, 0], cache_krope[:, 0]))
    ckv, krope = _run_passes([ctx, lat], mods, P)
    return (ctx["x"].reshape(b_ctx, t_ctx, D_MODEL), lat["x"].reshape(b_lat, t_lat, D_MODEL),
            ckv.reshape(b_ctx, 1, t_ctx, KV_LORA), krope.reshape(b_ctx, 1, t_ctx, QK_ROPE))
```

```python
import functools
import math

import numpy as np
import jax
import jax.numpy as jnp
from jax import lax
from jax.experimental import pallas as pl
from jax.experimental.pallas import tpu as pltpu

F32 = jnp.float32
BF16 = jnp.bfloat16

D_MODEL = 1024
N_HEADS = 16
QK_NOPE = 64
QK_ROPE = 32
V_HEAD = 64
Q_LORA = 384
KV_LORA = 256
ROPE_THETA = 10000.0
GRID_W = 64
POOL_WINDOWS = (2, 4, 8, 16)
N_GROUPS = 4
GROUP = D_MODEL // N_GROUPS
D_FF = 2816
EPS = 1e-6
DEPTH = 4

LANES = 128
Q_SCALE = np.float32(math.log2(math.e) / math.sqrt(QK_NOPE + QK_ROPE))
HEAD_PAIRS = N_HEADS // 2
N_COND = 8
ROW_TILE = 1024
PROJ_TILE = 512
FF_TILE = 256
ROW_BLOCK = 256
HALO = 8
RELAYOUT_ROWS = 128
ADA_TILE = 1536
MIB = 1024 * 1024
VMEM_DEFAULT = 48 * MIB
VMEM_HEADROOM = 4 * MIB + 512 * 1024
ATTN_VMEM = 48 * MIB


def _params(sem, vmem_bytes=VMEM_DEFAULT):
    return pltpu.CompilerParams(dimension_semantics=sem, vmem_limit_bytes=int(vmem_bytes))


def _nbytes(a):
    return math.prod(a.shape) * jnp.dtype(a.dtype).itemsize


def _dot(a, b):
    return jnp.dot(a, b, preferred_element_type=F32)


def _rms(x, w):
    return x * lax.rsqrt(jnp.mean(x * x, axis=-1, keepdims=True) + EPS) * w


def _norm_mod(x, nw, mod, base):
    return _rms(x, nw) * (1.0 + mod[base + 1:base + 2, :]) + mod[base:base + 1, :]


def _gelu_exact(x):
    return 0.5 * x * (1.0 + lax.erf(x * np.float32(math.sqrt(0.5))))


def _ada_kernel(c_ref, w_ref, b_ref, o_ref):
    s = jax.nn.silu(c_ref[...]).astype(BF16)
    o_ref[...] = _dot(s, w_ref[...].astype(BF16)) + b_ref[...]


def _ada_call(conds, ada_w, ada_b):
    n_out = ada_w.shape[-1]
    return pl.pallas_call(
        _ada_kernel,
        grid=(DEPTH, n_out // ADA_TILE),
        in_specs=[
            pl.BlockSpec((N_COND, D_MODEL), lambda l, j: (0, 0)),
            pl.BlockSpec((None, D_MODEL, ADA_TILE), lambda l, j: (l, 0, j)),
            pl.BlockSpec((None, 1, ADA_TILE), lambda l, j: (l, 0, j)),
        ],
        out_specs=pl.BlockSpec((None, N_COND, ADA_TILE), lambda l, j: (l, 0, j)),
        out_shape=jax.ShapeDtypeStruct((DEPTH, N_COND, n_out), F32),
        compiler_params=_params(("arbitrary", "arbitrary"), 2 * D_MODEL * ADA_TILE * 4 + VMEM_HEADROOM),
        name="ada",
    )(conds, ada_w, ada_b.reshape(DEPTH, 1, n_out))


def _mod_spec(li, cond_of_row, tile, n_grid):
    if n_grid == 1:
        return pl.BlockSpec((None, None, 6, D_MODEL), lambda i: (li, cond_of_row(i * tile), 0, 0))
    return pl.BlockSpec((None, None, 6, D_MODEL), lambda i, j: (li, cond_of_row(i * tile), 0, 0))


def _conv_window(w, cw, t0, seq):
    n = w.shape[0]
    mid = slice(HALO, HALO + ROW_BLOCK)
    t = (t0 + lax.broadcasted_iota(jnp.int32, (ROW_BLOCK, 1), 0)) & (seq - 1)
    prev = jnp.where(t == 0, 0.0, pltpu.roll(w, 1, axis=0)[mid])
    nxt = jnp.where(t == seq - 1, 0.0, pltpu.roll(w, n - 1, axis=0)[mid])
    return prev * cw[0:1, :] + w[mid] * cw[1:2, :] + nxt * cw[2:3, :]


def _ffn_act(win, mid, cw, cb, t0, seq):
    return _gelu_exact(_conv_window(win(0), cw, t0, seq) + cb) * mid(1)


def _sconv_act(win, mid, cw, cb, t0, seq):
    return mid(0) * _conv_window(win(1) * win(2), cw, t0, seq)


def _ffn_kernel(x_ref, mod_ref, nw_ref, up_ref, cw_ref, cb_ref, down_ref, fnw_ref, o_ref, *scratch, **cfg):
    _glu_body(x_ref, mod_ref, nw_ref, up_ref, cw_ref, cb_ref, down_ref, fnw_ref, None, o_ref, *scratch,
              n_up=2, act=_ffn_act, mod_base=3, **cfg)


def _ffn_pre_kernel(x_ref, mod_ref, nw_ref, up_ref, cw_ref, cb_ref, down_ref, fnw_ref, a_ref, wa_ref, o_ref,
                    *scratch, **cfg):
    _glu_body(x_ref, mod_ref, nw_ref, up_ref, cw_ref, cb_ref, down_ref, fnw_ref, (a_ref, wa_ref), o_ref,
              *scratch, n_up=2, act=_ffn_act, mod_base=3, **cfg)


def _sconv_kernel(x_ref, mod_ref, nw_ref, up_ref, cw_ref, down_ref, o_ref, *scratch, **cfg):
    _glu_body(x_ref, mod_ref, nw_ref, up_ref, cw_ref, None, down_ref, None, None, o_ref, *scratch,
              n_up=3, act=_sconv_act, mod_base=0, final_norm=False, **cfg)


def _glu_body(x_ref, mod_ref, nw_ref, up_ref, cw_ref, cb_ref, down_ref, fnw_ref, pre, o_ref,
              h_scr, acc_scr, gu0, gu1, *, n_up, n_chunks, act, seq, mod_base, final_norm):
    gu = (gu0, gu1)
    n_blocks = ROW_TILE // ROW_BLOCK

    def up(c, slot, r):
        rows = slice(r * ROW_BLOCK, (r + 1) * ROW_BLOCK)
        res = _dot(h_scr[rows, :], up_ref[c])
        for p in range(n_up):
            gu[slot][p, HALO + r * ROW_BLOCK:HALO + (r + 1) * ROW_BLOCK, :] = res[:, p * FF_TILE:(p + 1) * FF_TILE]

    def activate(c, slot, r):
        win = lambda p: gu[slot][p, r * ROW_BLOCK:(r + 1) * ROW_BLOCK + 2 * HALO, :]
        mid = lambda p: gu[slot][p, HALO + r * ROW_BLOCK:HALO + (r + 1) * ROW_BLOCK, :]
        cb = None if cb_ref is None else cb_ref[c]
        return act(win, mid, cw_ref[c], cb, r * ROW_BLOCK, seq).astype(BF16)

    def down(a, c, r):
        rows = slice(r * ROW_BLOCK, (r + 1) * ROW_BLOCK)
        y = _dot(a, down_ref[c])
        if isinstance(c, int) and c == 0:
            acc_scr[rows, :] = y
        else:
            acc_scr[rows, :] += y

    def stage(s, slot):
        for r in range(n_blocks):
            a = activate(s - 1, 1 - slot, r)
            up(s, slot, r)
            down(a, s - 1, r)

    def block(r):
        return slice(r * ROW_BLOCK, (r + 1) * ROW_BLOCK)

    def mixer_proj(r):
        a_ref, wa_ref = pre
        return _dot(a_ref[block(r), :], wa_ref[...])

    for g in gu:
        g[:, :HALO, :] = jnp.zeros((n_up, HALO, FF_TILE), F32)
        g[:, HALO + ROW_TILE:, :] = jnp.zeros((n_up, HALO, FF_TILE), F32)

    proj = mixer_proj(0) if pre is not None else None
    for r in range(n_blocks):
        x = x_ref[block(r), :]
        if pre is not None:
            x = x + mod_ref[2:3, :] * proj
            o_ref[block(r), :] = x
            if r + 1 < n_blocks:
                proj = mixer_proj(r + 1)
        h_scr[block(r), :] = _norm_mod(x, nw_ref[...], mod_ref[...], mod_base).astype(BF16)
        up(0, 0, r)

    if n_chunks > 1:
        stage(1, 1)
    n_pairs = (n_chunks - 2) // 2

    def pair(i, carry):
        s = 2 * i + 2
        stage(s, 0)
        stage(s + 1, 1)
        return carry

    lax.fori_loop(0, n_pairs, pair, 0)
    for s in range(2 * n_pairs + 2, n_chunks):
        stage(s, s % 2)
    last = n_chunks - 1
    res = x_ref if pre is None else o_ref
    for r in range(n_blocks):
        down(activate(last, last % 2, r), last, r)
        y = res[block(r), :] + mod_ref[mod_base + 2:mod_base + 3, :] * acc_scr[block(r), :]
        if final_norm:
            y = _rms(y, fnw_ref[...])
        o_ref[block(r), :] = y


def _chunked(w, n_chunks):
    return w.reshape(w.shape[0], n_chunks, FF_TILE).transpose(1, 0, 2)


def _glu_call(kernel_fn, name, x, mods, li, cond_of_row, n_up, n_chunks, norm_w, operands, specs, cfg):
    m = x.shape[0]
    whole = lambda a: pl.BlockSpec(a.shape, lambda i: (0,) * a.ndim)
    in_specs = [pl.BlockSpec((ROW_TILE, D_MODEL), lambda i: (i, 0)),
                _mod_spec(li, cond_of_row, ROW_TILE, 1),
                pl.BlockSpec((None, 1, D_MODEL), lambda i: (li, 0, 0))]
    layer_w = lambda a, l: pl.BlockSpec((None,) + a.shape[1:], lambda i: (l,) + (0,) * (a.ndim - 1),
                                        pipeline_mode=pl.Buffered(1))
    rows = lambda a: pl.BlockSpec((ROW_TILE, a.shape[1]), lambda i: (i, 0))
    spec_of = {None: whole, "rows": rows, "resident": lambda a: pl.BlockSpec(memory_space=pltpu.VMEM)}
    in_specs += [layer_w(a, s) if isinstance(s, int) else spec_of[s](a) for a, s in zip(operands, specs)]
    gu_shape = (n_up, ROW_TILE + 2 * HALO, FF_TILE)
    tile = ROW_TILE * D_MODEL
    window = {None: _nbytes, "rows": lambda a: 2 * ROW_TILE * a.shape[1] * a.dtype.itemsize,
              "resident": lambda a: 0}
    vmem = (2 * 2 * tile * 4 + tile * 2 + tile * 4 + 2 * math.prod(gu_shape) * 4 + VMEM_HEADROOM
            + sum(_nbytes(a) // a.shape[0] if isinstance(s, int) else window[s](a)
                  for a, s in zip(operands, specs)))
    return pl.pallas_call(
        functools.partial(kernel_fn, n_chunks=n_chunks, **cfg),
        grid=(m // ROW_TILE,),
        in_specs=in_specs,
        out_specs=pl.BlockSpec((ROW_TILE, D_MODEL), lambda i: (i, 0)),
        out_shape=jax.ShapeDtypeStruct((m, D_MODEL), F32),
        scratch_shapes=[
            pltpu.VMEM((ROW_TILE, D_MODEL), BF16),
            pltpu.VMEM((ROW_TILE, D_MODEL), F32),
            pltpu.VMEM(gu_shape, F32), pltpu.VMEM(gu_shape, F32),
        ],
        compiler_params=_params(("arbitrary",), vmem),
        name=name,
    )(x, mods, norm_w, *operands)


def _in_vmem(a):
    return pltpu.with_memory_space_constraint(a, pltpu.VMEM)


def _layer_weights_kernel(up_ref, dn_ref, uo_ref, do_ref, *, n_up, n_chunks):
    for c in range(n_chunks):
        for p in range(n_up):
            col = (p * n_chunks + c) * FF_TILE
            uo_ref[c, :, p * FF_TILE:(p + 1) * FF_TILE] = up_ref[:, col:col + FF_TILE].astype(BF16)
    do_ref[...] = dn_ref[...].astype(BF16)


def _layer_weights(up, down, layer, n_up, name):
    _, d, cols = up.shape
    _, hidden, d_out = down.shape
    n_chunks = hidden // FF_TILE
    steps = d // RELAYOUT_ROWS
    dn_rows = hidden // steps
    up_cm, dn = pl.pallas_call(
        functools.partial(_layer_weights_kernel, n_up=n_up, n_chunks=n_chunks),
        grid=(steps,),
        in_specs=[pl.BlockSpec((None, RELAYOUT_ROWS, cols), lambda k: (layer, k, 0)),
                  pl.BlockSpec((None, dn_rows, d_out), lambda k: (layer, k, 0))],
        out_specs=[pl.BlockSpec((n_chunks, RELAYOUT_ROWS, n_up * FF_TILE), lambda k: (0, k, 0)),
                   pl.BlockSpec((dn_rows, d_out), lambda k: (k, 0))],
        out_shape=[jax.ShapeDtypeStruct((n_chunks, d, n_up * FF_TILE), BF16),
                   jax.ShapeDtypeStruct((hidden, d_out), BF16)],
        compiler_params=_params(("arbitrary",),
                                2 * (RELAYOUT_ROWS * cols + dn_rows * d_out) * (4 + 2) + VMEM_HEADROOM),
        name=name,
    )(up, down)
    return _in_vmem(up_cm), _in_vmem(dn.reshape(n_chunks, FF_TILE, d_out))


def _ffn_call(x, mods, li, cond_of_row, seq, P, pre, final_norm):
    n = D_FF // FF_TILE
    up_cm, down_cm = P["ffn_w"](li)
    operands = [up_cm, _chunked(P["ffn_conv_w"][li], n), _chunked(P["ffn_conv_b"][li][None, :], n),
                down_cm, P["final_norm_w"].reshape(1, D_MODEL)]
    specs = ["resident", None, None, "resident", None]
    kernel_fn = _ffn_kernel
    if pre is not None:
        a, w_a, l_a = pre
        operands += [a, w_a]
        specs += ["rows", l_a]
        kernel_fn = _ffn_pre_kernel
    return _glu_call(kernel_fn, f"ffn{li}", x, mods, li, cond_of_row, 2, n, P["norm2_w"],
                     operands, specs, dict(seq=seq, final_norm=final_norm))


def _sconv_call(x, mods, li, cond_of_row, seq, P):
    lj = li // 4
    n = D_MODEL // FF_TILE
    up_cm, down_cm = P["sconv_w"](lj)
    operands = [up_cm, _chunked(P["sconv_conv"][lj], n), down_cm]
    return _glu_call(_sconv_kernel, f"sconv{li}", x, mods, li, cond_of_row, 3, n, P["norm1_w"],
                     operands, ["resident", None, "resident"], dict(seq=seq))


def _emit_kv(ckv, krope_pad, wk_ref, wv_ref, k_ref, v_ref):
    c = ckv.astype(BF16)
    kx = _dot(c, wk_ref[...].astype(BF16))
    vx = _dot(c, wv_ref[...].astype(BF16))
    for h in range(N_HEADS):
        k_ref[h] = (kx[:, h * LANES:(h + 1) * LANES] + krope_pad).astype(BF16)
    for p in range(HEAD_PAIRS):
        v_ref[p] = vx[:, p * LANES:(p + 1) * LANES].astype(BF16)


def _mla_proj_kernel(x_ref, mod_ref, nw_ref, w1_ref, qn_ref, kvn_ref, w2_ref, wk_ref, wv_ref, cos_ref,
                     sin_ref, q_ref, k_ref, v_ref, ckv_ref, kr_ref, *, rope):
    h = _norm_mod(x_ref[...], nw_ref[...], mod_ref[...], 0).astype(BF16)
    z = _dot(h, w1_ref[...].astype(BF16))
    cq = _rms(z[:, :Q_LORA], qn_ref[...]).astype(BF16)
    ckv = _rms(z[:, Q_LORA:Q_LORA + KV_LORA], kvn_ref[...])
    kr = z[:, Q_LORA + KV_LORA:Q_LORA + KV_LORA + LANES]
    if rope:
        cos, sin = cos_ref[...], sin_ref[...]
        kr = kr * cos + z[:, Q_LORA + KV_LORA + LANES:] * sin
    for p in range(HEAD_PAIRS):
        q2 = _dot(cq, w2_ref[:, 2 * p * LANES:(2 * p + 2) * LANES].astype(BF16))
        if rope:
            off = N_HEADS * LANES
            qs = _dot(cq, w2_ref[:, off + 2 * p * LANES:off + (2 * p + 2) * LANES].astype(BF16))
        for e in range(2):
            qh = q2[:, e * LANES:(e + 1) * LANES]
            if rope:
                qh = qh * cos + qs[:, e * LANES:(e + 1) * LANES] * sin
            q_ref[2 * p + e] = (qh * Q_SCALE).astype(BF16)
    ckv_ref[...] = ckv
    kr_ref[...] = kr[:, QK_NOPE:QK_NOPE + QK_ROPE]
    _emit_kv(ckv, kr, wk_ref, wv_ref, k_ref, v_ref)


def _mla_proj_call(x, mods, li, cond_of_row, W, rope):
    m = x.shape[0]
    n1 = W["w1"].shape[1]
    n2 = W["w2"].shape[1]
    const = lambda i: (0, 0)
    pos_tiles = W["cos"].shape[0] // PROJ_TILE
    pos = lambda i: (i % pos_tiles, 0)
    return pl.pallas_call(
        functools.partial(_mla_proj_kernel, rope=rope),
        grid=(m // PROJ_TILE,),
        in_specs=[
            pl.BlockSpec((PROJ_TILE, D_MODEL), lambda i: (i, 0)),
            _mod_spec(li, cond_of_row, PROJ_TILE, 1),
            pl.BlockSpec((None, 1, D_MODEL), lambda i: (li, 0, 0)),
            pl.BlockSpec((D_MODEL, n1), const),
            pl.BlockSpec((1, Q_LORA), const),
            pl.BlockSpec((1, KV_LORA), const),
            pl.BlockSpec((Q_LORA, n2), const),
            pl.BlockSpec((KV_LORA, N_HEADS * LANES), const),
            pl.BlockSpec((KV_LORA, HEAD_PAIRS * LANES), const),
            pl.BlockSpec((PROJ_TILE, LANES), pos),
            pl.BlockSpec((PROJ_TILE, LANES), pos),
        ],
        out_specs=[
            pl.BlockSpec((N_HEADS, PROJ_TILE, LANES), lambda i: (0, i, 0)),
            pl.BlockSpec((N_HEADS, PROJ_TILE, LANES), lambda i: (0, i, 0)),
            pl.BlockSpec((HEAD_PAIRS, PROJ_TILE, LANES), lambda i: (0, i, 0)),
            pl.BlockSpec((PROJ_TILE, KV_LORA), lambda i: (i, 0)),
            pl.BlockSpec((PROJ_TILE, QK_ROPE), lambda i: (i, 0)),
        ],
        out_shape=[
            jax.ShapeDtypeStruct((N_HEADS, m, LANES), BF16),
            jax.ShapeDtypeStruct((N_HEADS, m, LANES), BF16),
            jax.ShapeDtypeStruct((HEAD_PAIRS, m, LANES), BF16),
            jax.ShapeDtypeStruct((m, KV_LORA), F32),
            jax.ShapeDtypeStruct((m, QK_ROPE), F32),
        ],
        compiler_params=_params(("arbitrary",)),
        name="mla_proj_rope" if rope else "mla_proj",
    )(x, mods, W["norm1_w"], W["w1"], W["q_norm"], W["kv_norm"], W["w2"], W["wk"], W["wv"],
      W["cos"], W["sin"])


def _kv_cache_kernel(ckv_ref, kr_ref, wk_ref, wv_ref, k_ref, v_ref):
    _emit_kv(ckv_ref[...], kr_ref[...], wk_ref, wv_ref, k_ref, v_ref)


def _kv_cache_call(ckv, krope_pad, W):
    m = ckv.shape[0]
    const = lambda i: (0, 0)
    return pl.pallas_call(
        _kv_cache_kernel,
        grid=(1,),
        in_specs=[
            pl.BlockSpec((m, KV_LORA), const),
            pl.BlockSpec((m, LANES), const),
            pl.BlockSpec((KV_LORA, N_HEADS * LANES), const),
            pl.BlockSpec((KV_LORA, HEAD_PAIRS * LANES), const),
        ],
        out_specs=[
            pl.BlockSpec((N_HEADS, m, LANES), lambda i: (0, 0, 0)),
            pl.BlockSpec((HEAD_PAIRS, m, LANES), lambda i: (0, 0, 0)),
        ],
        out_shape=[
            jax.ShapeDtypeStruct((N_HEADS, m, LANES), BF16),
            jax.ShapeDtypeStruct((HEAD_PAIRS, m, LANES), BF16),
        ],
        compiler_params=_params(("arbitrary",)),
        name="kv_cache",
    )(ckv, krope_pad, W["wk"], W["wv"])


def _attn_kernel(q_ref, *refs, n_pairs):
    o_ref, kv = refs[-1], refs[:-1]
    parts = [(kv[i], kv[i + 1]) for i in range(0, len(kv), 2)]
    heads = [(p, e) for p in range(n_pairs) for e in range(2)]

    def scores(p, e):
        q = q_ref[2 * p + e]
        return [lax.dot_general(q, k_ref[2 * p + e], (((1,), (1,)), ((), ())), preferred_element_type=F32)
                for k_ref, _ in parts]

    outs = []
    s_next = scores(*heads[0])
    for i, (p, e) in enumerate(heads):
        s = s_next
        if i + 1 < len(heads):
            s_next = scores(*heads[i + 1])
        m = functools.reduce(jnp.maximum, [jnp.max(x, axis=-1, keepdims=True) for x in s])
        w = [jnp.exp2(x - m) for x in s]
        l = sum(jnp.sum(x, axis=-1, keepdims=True) for x in w)
        pv = sum(_dot(x.astype(BF16), v_ref[p]) for x, (_, v_ref) in zip(w, parts))
        outs.append(pv / l)
        if e == 1:
            lane = lax.broadcasted_iota(jnp.int32, pv.shape, 1)
            o_ref[:, p * LANES:(p + 1) * LANES] = jnp.where(lane < V_HEAD, outs[-2], outs[-1]).astype(BF16)


def _attn_call(q, kv_parts, q_tile, n_pairs):
    _, b, t, _ = q.shape
    nq = t // q_tile
    in_specs = [pl.BlockSpec((2 * n_pairs, None, q_tile, LANES), lambda bi, g, qi: (g, bi, qi, 0))]
    operands = [q]
    for k, v in kv_parts:
        s = k.shape[2]
        in_specs += [pl.BlockSpec((2 * n_pairs, None, s, LANES), lambda bi, g, qi: (g, bi, 0, 0)),
                     pl.BlockSpec((n_pairs, None, s, LANES), lambda bi, g, qi: (g, bi, 0, 0))]
        operands += [k, v]
    return pl.pallas_call(
        functools.partial(_attn_kernel, n_pairs=n_pairs),
        grid=(b, HEAD_PAIRS // n_pairs, nq),
        in_specs=in_specs,
        out_specs=pl.BlockSpec((q_tile, n_pairs * LANES), lambda bi, g, qi: (bi * nq + qi, g)),
        out_shape=jax.ShapeDtypeStruct((b * t, N_HEADS * V_HEAD), BF16),
        compiler_params=_params(("arbitrary", "arbitrary", "arbitrary"), ATTN_VMEM),
        name=f"attn_{len(kv_parts)}part",
    )(*operands)


def _pool_kernel(x_ref, mod_ref, nw_ref, w_ref, sc_ref, o_ref, band_scr, h_scr, hi_scr, lo_scr, y_scr, *, seq):
    @pl.when(pl.program_id(0) == 0)
    def _():
        r = lax.broadcasted_iota(jnp.int32, (seq, seq), 0)
        c = lax.broadcasted_iota(jnp.int32, (seq, seq), 1)
        for g, win in enumerate(POOL_WINDOWS):
            inside = (c >= r - win // 2) & (c < r + win - win // 2)
            band_scr[g] = jnp.where(inside, 1.0, 0.0).astype(BF16)

    items = [(s, g) for s in range(ROW_TILE // seq) for g in range(N_GROUPS)]
    rows = lambda s: slice(s * seq, (s + 1) * seq)
    cols = lambda g: slice(g * GROUP, (g + 1) * GROUP)
    t = lax.broadcasted_iota(jnp.int32, (seq, 1), 0)
    counts = [(jnp.minimum(t + (win - win // 2), seq) - jnp.maximum(t - win // 2, 0)).astype(F32)
              for win in POOL_WINDOWS]

    def norm_seq(s):
        h = _norm_mod(x_ref[rows(s), :], nw_ref[...], mod_ref[...], 0)
        hi = h.astype(BF16)
        h_scr[rows(s), :] = h
        hi_scr[rows(s), :] = hi
        lo_scr[rows(s), :] = (h - hi.astype(F32)).astype(BF16)

    def window_sum(s, g):
        return _dot(band_scr[g], hi_scr[rows(s), cols(g)]) + _dot(band_scr[g], lo_scr[rows(s), cols(g)])

    norm_seq(0)
    tot = window_sum(*items[0])
    for i, (s, g) in enumerate(items):
        tot_next = None
        if i + 1 < len(items):
            s2, g2 = items[i + 1]
            if s2 != s:
                norm_seq(s2)
            tot_next = window_sum(s2, g2)
        pooled = (tot / counts[g] - h_scr[rows(s), cols(g)]).astype(BF16)
        y_scr[rows(s), cols(g)] = _dot(pooled, w_ref[g].astype(BF16))
        tot = tot_next
    for r in range(ROW_TILE // ROW_BLOCK):
        blk = slice(r * ROW_BLOCK, (r + 1) * ROW_BLOCK)
        o_ref[blk, :] = x_ref[blk, :] + mod_ref[2:3, :] * (y_scr[blk, :] * sc_ref[...])


def _pool_call(x, mods, li, cond_of_row, seq, P):
    m = x.shape[0]
    lj = li // 4
    return pl.pallas_call(
        functools.partial(_pool_kernel, seq=seq),
        grid=(m // ROW_TILE,),
        in_specs=[
            pl.BlockSpec((ROW_TILE, D_MODEL), lambda i: (i, 0)),
            _mod_spec(li, cond_of_row, ROW_TILE, 1),
            pl.BlockSpec((None, 1, D_MODEL), lambda i: (li, 0, 0)),
            pl.BlockSpec((None, N_GROUPS, GROUP, GROUP), lambda i: (lj, 0, 0, 0)),
            pl.BlockSpec((None, 1, D_MODEL), lambda i: (lj, 0, 0)),
        ],
        out_specs=pl.BlockSpec((ROW_TILE, D_MODEL), lambda i: (i, 0)),
        out_shape=jax.ShapeDtypeStruct((m, D_MODEL), F32),
        scratch_shapes=[pltpu.VMEM((N_GROUPS, seq, seq), BF16), pltpu.VMEM((ROW_TILE, D_MODEL), F32),
                        pltpu.VMEM((ROW_TILE, D_MODEL), BF16), pltpu.VMEM((ROW_TILE, D_MODEL), BF16),
                        pltpu.VMEM((ROW_TILE, D_MODEL), F32)],
        compiler_params=_params(("arbitrary",)),
        name=f"pool_t{seq}",
    )(x, mods, P["norm1_w"], P["pool_w"], P["pool_scale"])


def _dft_tables(seq):
    def cs(n):
        k = np.arange(n, dtype=np.int64)
        ang = 2.0 * np.pi * ((k[:, None] * k[None, :]) % n) / n
        return np.cos(ang), np.sin(ang)
    ct, st = cs(seq)
    cc, sc = cs(GROUP)
    e_t = jnp.asarray(np.concatenate([ct, st], axis=0), dtype=F32)
    e_c = jnp.asarray(np.concatenate([cc, -sc], axis=0), dtype=F32)
    return e_t.astype(BF16), e_c.astype(BF16)


def _fourier_kernel(x_ref, mod_ref, nw_ref, et_ref, ec_ref, w_ref, o_ref, h_scr, f_scr, wb_scr, *, seq):
    norm = np.float32(1.0 / math.sqrt(seq * GROUP))
    items = [(s, g) for s in range(ROW_TILE // seq) for g in range(N_GROUPS)]
    rows = lambda s: slice(s * seq, (s + 1) * seq)
    cols = lambda g: slice(g * GROUP, (g + 1) * GROUP)

    def norm_seq(s):
        h_scr[rows(s), :] = _norm_mod(x_ref[rows(s), :], nw_ref[...], mod_ref[...], 0).astype(BF16)

    def time_dft(s, g):
        return _dot(et_ref[...], h_scr[rows(s), cols(g)]).astype(BF16)

    norm_seq(0)
    wb_scr[...] = w_ref[...].astype(BF16)
    y = time_dft(*items[0])
    for i, (s, g) in enumerate(items):
        y_next = None
        if i + 1 < len(items):
            s2, g2 = items[i + 1]
            if s2 != s:
                norm_seq(s2)
            y_next = time_dft(s2, g2)
        f = _dot(y[:seq], ec_ref[:GROUP, :]) + _dot(y[seq:], ec_ref[GROUP:, :])
        f_scr[rows(s), cols(g)] = (f * norm).astype(BF16)
        y = y_next
    for r in range(ROW_TILE // ROW_BLOCK):
        blk = slice(r * ROW_BLOCK, (r + 1) * ROW_BLOCK)
        o_ref[blk, :] = x_ref[blk, :] + mod_ref[2:3, :] * _dot(f_scr[blk, :], wb_scr[...])


def _fourier_call(x, mods, li, cond_of_row, seq, P):
    m = x.shape[0]
    lj = li // 4
    e_t, e_c = _dft_tables(seq)
    return pl.pallas_call(
        functools.partial(_fourier_kernel, seq=seq),
        grid=(m // ROW_TILE,),
        in_specs=[
            pl.BlockSpec((ROW_TILE, D_MODEL), lambda i: (i, 0)),
            _mod_spec(li, cond_of_row, ROW_TILE, 1),
            pl.BlockSpec((None, 1, D_MODEL), lambda i: (li, 0, 0)),
            pl.BlockSpec((2 * seq, seq), lambda i: (0, 0)),
            pl.BlockSpec((2 * GROUP, GROUP), lambda i: (0, 0)),
            pl.BlockSpec((None, D_MODEL, D_MODEL), lambda i: (lj, 0, 0)),
        ],
        out_specs=pl.BlockSpec((ROW_TILE, D_MODEL), lambda i: (i, 0)),
        out_shape=jax.ShapeDtypeStruct((m, D_MODEL), F32),
        scratch_shapes=[pltpu.VMEM((ROW_TILE, D_MODEL), BF16), pltpu.VMEM((ROW_TILE, D_MODEL), BF16),
                        pltpu.VMEM((D_MODEL, D_MODEL), BF16)],
        compiler_params=_params(("arbitrary",)),
        name=f"fourier_t{seq}",
    )(x, mods, P["norm1_w"], e_t, e_c, P["fnet_w"])


def _rope_tables(rows):
    r, col = jnp.meshgrid(jnp.arange(rows), jnp.arange(GRID_W), indexing="ij")
    r = r.reshape(-1).astype(F32)
    col = col.reshape(-1).astype(F32)
    n_freq = QK_ROPE // 4
    inv = ROPE_THETA ** (-jnp.arange(n_freq, dtype=F32) / n_freq)
    ang = jnp.concatenate([r[:, None] * inv, col[:, None] * inv], axis=-1)
    cos = jnp.repeat(jnp.cos(ang), 2, axis=-1)
    sin = jnp.repeat(jnp.sin(ang), 2, axis=-1)
    t = cos.shape[0]
    pad = LANES - QK_NOPE - QK_ROPE
    cos_p = jnp.concatenate([jnp.ones((t, QK_NOPE), F32), cos, jnp.ones((t, pad), F32)], axis=-1)
    sin_p = jnp.concatenate([jnp.zeros((t, QK_NOPE), F32), sin, jnp.zeros((t, pad), F32)], axis=-1)
    return cos_p, sin_p


def _pair_swap(w):
    wr = w.reshape(w.shape[:-1] + (w.shape[-1] // 2, 2))
    return jnp.stack([-wr[..., 1], wr[..., 0]], axis=-1).reshape(w.shape)


def _mla_weights(P, j, rope, seq_rows):
    pad = LANES - QK_NOPE - QK_ROPE
    wuq = P["mla_wuq"][j].reshape(Q_LORA, N_HEADS, QK_NOPE + QK_ROPE)
    q_slots = jnp.pad(wuq, ((0, 0), (0, 0), (0, pad))).reshape(Q_LORA, N_HEADS * LANES)
    wdkv = P["mla_wdkv"][j]
    kr_slot = jnp.pad(wdkv[:, KV_LORA:], ((0, 0), (QK_NOPE, pad)))
    parts1 = [P["mla_wdq"][j], wdkv[:, :KV_LORA], kr_slot]
    parts2 = [q_slots]
    if rope:
        parts1.append(jnp.pad(_pair_swap(wdkv[:, KV_LORA:]), ((0, 0), (QK_NOPE, pad))))
        q_sw = jnp.pad(_pair_swap(wuq[:, :, QK_NOPE:]), ((0, 0), (0, 0), (QK_NOPE, pad)))
        parts2.append(q_sw.reshape(Q_LORA, N_HEADS * LANES))
        cos, sin = _rope_tables(seq_rows // GRID_W)
    else:
        cos = sin = jnp.zeros((PROJ_TILE, LANES), F32)
    wukv = P["mla_wukv"][j].reshape(KV_LORA, N_HEADS, QK_NOPE + V_HEAD)
    wk = jnp.pad(wukv[:, :, :QK_NOPE], ((0, 0), (0, 0), (0, LANES - QK_NOPE)))
    return dict(
        norm1_w=P["norm1_w"], q_norm=P["mla_q_norm"][j][None, :], kv_norm=P["mla_kv_norm"][j][None, :],
        w1=jnp.concatenate(parts1, axis=1), w2=jnp.concatenate(parts2, axis=1),
        wk=wk.reshape(KV_LORA, N_HEADS * LANES),
        wv=wukv[:, :, QK_NOPE:].reshape(KV_LORA, N_HEADS * V_HEAD), cos=cos, sin=sin)


def _mla_layer(x, mods, li, cond_of_row, batch, seq, P, cache):
    j = li // 4
    W = _mla_weights(P, j, cache is not None, seq)
    q, k, v, ckv, krope = _mla_proj_call(x, mods, li, cond_of_row, W, cache is not None)
    q = q.reshape(N_HEADS, batch, seq, LANES)
    k = k.reshape(N_HEADS, batch, seq, LANES)
    v = v.reshape(HEAD_PAIRS, batch, seq, LANES)
    if cache is None:
        o = _attn_call(q, [(k, v)], seq, HEAD_PAIRS)
    else:
        c_ckv, c_krope = cache
        past = c_ckv.shape[1]
        pad = LANES - QK_NOPE - QK_ROPE
        kr_pad = jnp.pad(c_krope.reshape(batch * past, QK_ROPE), ((0, 0), (QK_NOPE, pad)))
        kc, vc = _kv_cache_call(c_ckv.reshape(batch * past, KV_LORA), kr_pad, W)
        cached = (kc.reshape(N_HEADS, batch, past, LANES), vc.reshape(HEAD_PAIRS, batch, past, LANES))
        o = _attn_call(q, [cached, (k, v)], 512, 4)
    return o, ckv, krope


def _mixer(x, mods, li, cond_of_row, batch, seq, P, cache):
    kind = li % 4
    if kind == 0:
        o, ckv, krope = _mla_layer(x, mods, li, cond_of_row, batch, seq, P, cache)
        return x, (o, P["mla_wo_bf16"], li // 4), (ckv, krope)
    if kind == 1:
        return _pool_call(x, mods, li, cond_of_row, seq, P), None, None
    if kind == 2:
        return _fourier_call(x, mods, li, cond_of_row, seq, P), None, None
    return _sconv_call(x, mods, li, cond_of_row, seq, P), None, None


def _run_passes(passes, mods, P):
    new_cache = None
    for li in range(DEPTH):
        pending = []
        for ps in passes:
            ps["x"], pre, made = _mixer(ps["x"], mods, li, ps["cond_of_row"], ps["batch"], ps["seq"], P,
                                        ps["cache"])
            pending.append(pre)
            if made is not None and ps["cache"] is None:
                new_cache = made
        xs, outs = lax.optimization_barrier(([ps["x"] for ps in passes],
                                              [None if pre is None else pre[0] for pre in pending]))
        pending = [None if pre is None else (o,) + pre[1:] for pre, o in zip(pending, outs)]
        xs = [_ffn_call(x, mods, li, ps["cond_of_row"], ps["seq"], P, pre, final_norm=(li == DEPTH - 1))
              for x, ps, pre in zip(xs, passes, pending)]
        xs = lax.optimization_barrier(xs)
        for ps, x in zip(passes, xs):
            ps["x"] = x
    return new_cache


def kernel(x_prompt, x_sample, cache_ckv, cache_krope, c, c_ctx, norm1_w, norm2_w, ada_w, ada_b, mla_wdq, mla_q_norm, mla_wuq, mla_wdkv, mla_kv_norm, mla_wukv, mla_wo, pool_w, pool_scale, fnet_w, sconv_win, sconv_conv, sconv_wout, ffn_up, ffn_conv_w, ffn_conv_b, ffn_down, final_norm_w):
    row = lambda w: w.reshape(w.shape[0], 1, w.shape[1])
    P = dict(norm1_w=row(norm1_w), norm2_w=row(norm2_w), mla_wdq=mla_wdq, mla_q_norm=mla_q_norm,
             mla_wuq=mla_wuq, mla_wdkv=mla_wdkv, mla_kv_norm=mla_kv_norm, mla_wukv=mla_wukv,
             mla_wo_bf16=mla_wo.astype(BF16),
             pool_w=pool_w, pool_scale=row(pool_scale), fnet_w=fnet_w, sconv_win=sconv_win,
             sconv_conv=sconv_conv, sconv_wout=sconv_wout, ffn_conv_w=ffn_conv_w,
             ffn_conv_b=ffn_conv_b, ffn_down=ffn_down, final_norm_w=final_norm_w,
             ffn_w=functools.cache(lambda l: _layer_weights(ffn_up, ffn_down, l, 2, f"ffn{l}_weights")),
             sconv_w=functools.cache(lambda l: _layer_weights(sconv_win, sconv_wout, l, 3, f"sconv{l}_weights")))
    b_ctx, t_ctx, _ = x_prompt.shape
    b_lat, t_lat, _ = x_sample.shape
    assert ROW_TILE % t_ctx == 0 and t_lat % ROW_TILE == 0 and b_lat + 1 <= N_COND

    conds = jnp.concatenate([c_ctx[None, :], c, jnp.zeros((N_COND - 1 - b_lat, D_MODEL), F32)], axis=0)
    mods = _ada_call(conds, ada_w, ada_b).reshape(DEPTH, N_COND, 6, D_MODEL)

    ctx = dict(x=x_prompt.reshape(b_ctx * t_ctx, D_MODEL), cond_of_row=lambda r: 0, batch=b_ctx, seq=t_ctx,
               cache=None)
    lat = dict(x=x_sample.reshape(b_lat * t_lat, D_MODEL), cond_of_row=lambda r: 1 + r // t_lat, batch=b_lat,
               seq=t_lat, cache=(cache_ckv[:, 0], cache_krope[:, 0]))
    ckv, krope = _run_passes([ctx, lat], mods, P)
    return (ctx["x"].reshape(b_ctx, t_ctx, D_MODEL), lat["x"].reshape(b_lat, t_lat, D_MODEL),
            ckv.reshape(b_ctx, 1, t_ctx, KV_LORA), krope.reshape(b_ctx, 1, t_ctx, QK_ROPE))
```

```python
import functools
import math

import numpy as np
import jax
import jax.numpy as jnp
from jax import lax
from jax.experimental import pallas as pl
from jax.experimental.pallas import tpu as pltpu

F32 = jnp.float32
BF16 = jnp.bfloat16

D_MODEL = 1024
N_HEADS = 16
QK_NOPE = 64
QK_ROPE = 32
V_HEAD = 64
Q_LORA = 384
KV_LORA = 256
ROPE_THETA = 10000.0
GRID_W = 64
POOL_WINDOWS = (2, 4, 8, 16)
N_GROUPS = 4
GROUP = D_MODEL // N_GROUPS
D_FF = 2816
EPS = 1e-6
DEPTH = 4

LANES = 128
Q_SCALE = np.float32(math.log2(math.e) / math.sqrt(QK_NOPE + QK_ROPE))
HEAD_PAIRS = N_HEADS // 2
N_COND = 8
ROW_TILE = 1024
PROJ_TILE = 512
FF_TILE = 256
ROW_BLOCK = 256
HALO = 8
RELAYOUT_ROWS = 128
ADA_TILE = 1536
MIB = 1024 * 1024
VMEM_DEFAULT = 48 * MIB
VMEM_HEADROOM = 4 * MIB + 512 * 1024
ATTN_VMEM = 48 * MIB


def _params(sem, vmem_bytes=VMEM_DEFAULT):
    return pltpu.CompilerParams(dimension_semantics=sem, vmem_limit_bytes=int(vmem_bytes))


def _nbytes(a):
    return math.prod(a.shape) * jnp.dtype(a.dtype).itemsize


def _dot(a, b):
    return jnp.dot(a, b, preferred_element_type=F32)


def _rms(x, w):
    return x * lax.rsqrt(jnp.mean(x * x, axis=-1, keepdims=True) + EPS) * w


def _norm_mod(x, nw, mod, base):
    return _rms(x, nw) * (1.0 + mod[base + 1:base + 2, :]) + mod[base:base + 1, :]


def _gelu_exact(x):
    return 0.5 * x * (1.0 + lax.erf(x * np.float32(math.sqrt(0.5))))


def _ada_kernel(c_ref, w_ref, b_ref, o_ref):
    s = jax.nn.silu(c_ref[...]).astype(BF16)
    o_ref[...] = _dot(s, w_ref[...].astype(BF16)) + b_ref[...]


def _ada_call(conds, ada_w, ada_b):
    n_out = ada_w.shape[-1]
    return pl.pallas_call(
        _ada_kernel,
        grid=(DEPTH, n_out // ADA_TILE),
        in_specs=[
            pl.BlockSpec((N_COND, D_MODEL), lambda l, j: (0, 0)),
            pl.BlockSpec((None, D_MODEL, ADA_TILE), lambda l, j: (l, 0, j)),
            pl.BlockSpec((None, 1, ADA_TILE), lambda l, j: (l, 0, j)),
        ],
        out_specs=pl.BlockSpec((None, N_COND, ADA_TILE), lambda l, j: (l, 0, j)),
        out_shape=jax.ShapeDtypeStruct((DEPTH, N_COND, n_out), F32),
        compiler_params=_params(("arbitrary", "arbitrary"), 2 * D_MODEL * ADA_TILE * 4 + VMEM_HEADROOM),
        name="ada",
    )(conds, ada_w, ada_b.reshape(DEPTH, 1, n_out))


def _mod_spec(li, cond_of_row, tile, n_grid):
    if n_grid == 1:
        return pl.BlockSpec((None, None, 6, D_MODEL), lambda i: (li, cond_of_row(i * tile), 0, 0))
    return pl.BlockSpec((None, None, 6, D_MODEL), lambda i, j: (li, cond_of_row(i * tile), 0, 0))


def _conv_window(w, cw, t0, seq):
    n = w.shape[0]
    mid = slice(HALO, HALO + ROW_BLOCK)
    t = (t0 + lax.broadcasted_iota(jnp.int32, (ROW_BLOCK, 1), 0)) & (seq - 1)
    prev = jnp.where(t == 0, 0.0, pltpu.roll(w, 1, axis=0)[mid])
    nxt = jnp.where(t == seq - 1, 0.0, pltpu.roll(w, n - 1, axis=0)[mid])
    return prev * cw[0:1, :] + w[mid] * cw[1:2, :] + nxt * cw[2:3, :]


def _ffn_act(win, mid, cw, cb, t0, seq):
    return _gelu_exact(_conv_window(win(0), cw, t0, seq) + cb) * mid(1)


def _sconv_act(win, mid, cw, cb, t0, seq):
    return mid(0) * _conv_window(win(1) * win(2), cw, t0, seq)


def _ffn_kernel(x_ref, mod_ref, nw_ref, up_ref, cw_ref, cb_ref, down_ref, fnw_ref, o_ref, *scratch, **cfg):
    _glu_body(x_ref, mod_ref, nw_ref, up_ref, cw_ref, cb_ref, down_ref, fnw_ref, None, o_ref, *scratch,
              n_up=2, act=_ffn_act, mod_base=3, **cfg)


def _ffn_pre_kernel(x_ref, mod_ref, nw_ref, up_ref, cw_ref, cb_ref, down_ref, fnw_ref, a_ref, wa_ref, o_ref,
                    *scratch, **cfg):
    _glu_body(x_ref, mod_ref, nw_ref, up_ref, cw_ref, cb_ref, down_ref, fnw_ref, (a_ref, wa_ref), o_ref,
              *scratch, n_up=2, act=_ffn_act, mod_base=3, **cfg)


def _sconv_kernel(x_ref, mod_ref, nw_ref, up_ref, cw_ref, down_ref, o_ref, *scratch, **cfg):
    _glu_body(x_ref, mod_ref, nw_ref, up_ref, cw_ref, None, down_ref, None, None, o_ref, *scratch,
              n_up=3, act=_sconv_act, mod_base=0, final_norm=False, **cfg)


def _glu_body(x_ref, mod_ref, nw_ref, up_ref, cw_ref, cb_ref, down_ref, fnw_ref, pre, o_ref,
              h_scr, acc_scr, gu0, gu1, *, n_up, n_chunks, act, seq, mod_base, final_norm):
    gu = (gu0, gu1)
    n_blocks = ROW_TILE // ROW_BLOCK

    def up(c, slot, r):
        rows = slice(r * ROW_BLOCK, (r + 1) * ROW_BLOCK)
        res = _dot(h_scr[rows, :], up_ref[c])
        for p in range(n_up):
            gu[slot][p, HALO + r * ROW_BLOCK:HALO + (r + 1) * ROW_BLOCK, :] = res[:, p * FF_TILE:(p + 1) * FF_TILE]

    def activate(c, slot, r):
        win = lambda p: gu[slot][p, r * ROW_BLOCK:(r + 1) * ROW_BLOCK + 2 * HALO, :]
        mid = lambda p: gu[slot][p, HALO + r * ROW_BLOCK:HALO + (r + 1) * ROW_BLOCK, :]
        cb = None if cb_ref is None else cb_ref[c]
        return act(win, mid, cw_ref[c], cb, r * ROW_BLOCK, seq).astype(BF16)

    def down(a, c, r):
        rows = slice(r * ROW_BLOCK, (r + 1) * ROW_BLOCK)
        y = _dot(a, down_ref[c])
        if isinstance(c, int) and c == 0:
            acc_scr[rows, :] = y
        else:
            acc_scr[rows, :] += y

    def stage(s, slot):
        for r in range(n_blocks):
            a = activate(s - 1, 1 - slot, r)
            up(s, slot, r)
            down(a, s - 1, r)

    def block(r):
        return slice(r * ROW_BLOCK, (r + 1) * ROW_BLOCK)

    def mixer_proj(r):
        a_ref, wa_ref = pre
        return _dot(a_ref[block(r), :], wa_ref[...])

    for g in gu:
        g[:, :HALO, :] = jnp.zeros((n_up, HALO, FF_TILE), F32)
        g[:, HALO + ROW_TILE:, :] = jnp.zeros((n_up, HALO, FF_TILE), F32)

    proj = mixer_proj(0) if pre is not None else None
    for r in range(n_blocks):
        x = x_ref[block(r), :]
        if pre is not None:
            x = x + mod_ref[2:3, :] * proj
            o_ref[block(r), :] = x
            if r + 1 < n_blocks:
                proj = mixer_proj(r + 1)
        h_scr[block(r), :] = _norm_mod(x, nw_ref[...], mod_ref[...], mod_base).astype(BF16)
        up(0, 0, r)

    if n_chunks > 1:
        stage(1, 1)
    n_pairs = (n_chunks - 2) // 2

    def pair(i, carry):
        s = 2 * i + 2
        stage(s, 0)
        stage(s + 1, 1)
        return carry

    lax.fori_loop(0, n_pairs, pair, 0)
    for s in range(2 * n_pairs + 2, n_chunks):
        stage(s, s % 2)
    last = n_chunks - 1
    res = x_ref if pre is None else o_ref
    for r in range(n_blocks):
        down(activate(last, last % 2, r), last, r)
        y = res[block(r), :] + mod_ref[mod_base + 2:mod_base + 3, :] * acc_scr[block(r), :]
        if final_norm:
            y = _rms(y, fnw_ref[...])
        o_ref[block(r), :] = y


def _chunked(w, n_chunks):
    return w.reshape(w.shape[0], n_chunks, FF_TILE).transpose(1, 0, 2)


def _glu_call(kernel_fn, name, x, mods, li, cond_of_row, n_up, n_chunks, norm_w, operands, specs, cfg):
    m = x.shape[0]
    whole = lambda a: pl.BlockSpec(a.shape, lambda i: (0,) * a.ndim)
    in_specs = [pl.BlockSpec((ROW_TILE, D_MODEL), lambda i: (i, 0)),
                _mod_spec(li, cond_of_row, ROW_TILE, 1),
                pl.BlockSpec((None, 1, D_MODEL), lambda i: (li, 0, 0))]
    layer_w = lambda a, l: pl.BlockSpec((None,) + a.shape[1:], lambda i: (l,) + (0,) * (a.ndim - 1),
                                        pipeline_mode=pl.Buffered(1))
    rows = lambda a: pl.BlockSpec((ROW_TILE, a.shape[1]), lambda i: (i, 0))
    spec_of = {None: whole, "rows": rows, "resident": lambda a: pl.BlockSpec(memory_space=pltpu.VMEM)}
    in_specs += [layer_w(a, s) if isinstance(s, int) else spec_of[s](a) for a, s in zip(operands, specs)]
    gu_shape = (n_up, ROW_TILE + 2 * HALO, FF_TILE)
    tile = ROW_TILE * D_MODEL
    window = {None: _nbytes, "rows": lambda a: 2 * ROW_TILE * a.shape[1] * a.dtype.itemsize,
              "resident": lambda a: 0}
    vmem = (2 * 2 * tile * 4 + tile * 2 + tile * 4 + 2 * math.prod(gu_shape) * 4 + VMEM_HEADROOM
            + sum(_nbytes(a) // a.shape[0] if isinstance(s, int) else window[s](a)
                  for a, s in zip(operands, specs)))
    return pl.pallas_call(
        functools.partial(kernel_fn, n_chunks=n_chunks, **cfg),
        grid=(m // ROW_TILE,),
        in_specs=in_specs,
        out_specs=pl.BlockSpec((ROW_TILE, D_MODEL), lambda i: (i, 0)),
        out_shape=jax.ShapeDtypeStruct((m, D_MODEL), F32),
        scratch_shapes=[
            pltpu.VMEM((ROW_TILE, D_MODEL), BF16),
            pltpu.VMEM((ROW_TILE, D_MODEL), F32),
            pltpu.VMEM(gu_shape, F32), pltpu.VMEM(gu_shape, F32),
        ],
        compiler_params=_params(("arbitrary",), vmem),
        name=name,
    )(x, mods, norm_w, *operands)


def _in_vmem(a):
    return pltpu.with_memory_space_constraint(a, pltpu.VMEM)


def _layer_weights_kernel(up_ref, dn_ref, uo_ref, do_ref, *, n_up, n_chunks):
    for c in range(n_chunks):
        for p in range(n_up):
            col = (p * n_chunks + c) * FF_TILE
            uo_ref[c, :, p * FF_TILE:(p + 1) * FF_TILE] = up_ref[:, col:col + FF_TILE].astype(BF16)
    do_ref[...] = dn_ref[...].astype(BF16)


def _layer_weights(up, down, layer, n_up, name):
    _, d, cols = up.shape
    _, hidden, d_out = down.shape
    n_chunks = hidden // FF_TILE
    steps = d // RELAYOUT_ROWS
    dn_rows = hidden // steps
    up_cm, dn = pl.pallas_call(
        functools.partial(_layer_weights_kernel, n_up=n_up, n_chunks=n_chunks),
        grid=(steps,),
        in_specs=[pl.BlockSpec((None, RELAYOUT_ROWS, cols), lambda k: (layer, k, 0)),
                  pl.BlockSpec((None, dn_rows, d_out), lambda k: (layer, k, 0))],
        out_specs=[pl.BlockSpec((n_chunks, RELAYOUT_ROWS, n_up * FF_TILE), lambda k: (0, k, 0)),
                   pl.BlockSpec((dn_rows, d_out), lambda k: (k, 0))],
        out_shape=[jax.ShapeDtypeStruct((n_chunks, d, n_up * FF_TILE), BF16),
                   jax.ShapeDtypeStruct((hidden, d_out), BF16)],
        compiler_params=_params(("arbitrary",),
                                2 * (RELAYOUT_ROWS * cols + dn_rows * d_out) * (4 + 2) + VMEM_HEADROOM),
        name=name,
    )(up, down)
    return _in_vmem(up_cm), _in_vmem(dn.reshape(n_chunks, FF_TILE, d_out))


def _ffn_call(x, mods, li, cond_of_row, seq, P, pre, final_norm):
    n = D_FF // FF_TILE
    up_cm, down_cm = P["ffn_w"](li)
    operands = [up_cm, _chunked(P["ffn_conv_w"][li], n), _chunked(P["ffn_conv_b"][li][None, :], n),
                down_cm, P["final_norm_w"].reshape(1, D_MODEL)]
    specs = ["resident", None, None, "resident", None]
    kernel_fn = _ffn_kernel
    if pre is not None:
        a, w_a, l_a = pre
        operands += [a, w_a]
        specs += ["rows", l_a]
        kernel_fn = _ffn_pre_kernel
    return _glu_call(kernel_fn, f"ffn{li}", x, mods, li, cond_of_row, 2, n, P["norm2_w"],
                     operands, specs, dict(seq=seq, final_norm=final_norm))


def _sconv_call(x, mods, li, cond_of_row, seq, P):
    lj = li // 4
    n = D_MODEL // FF_TILE
    up_cm, down_cm = P["sconv_w"](lj)
    operands = [up_cm, _chunked(P["sconv_conv"][lj], n), down_cm]
    return _glu_call(_sconv_kernel, f"sconv{li}", x, mods, li, cond_of_row, 3, n, P["norm1_w"],
                     operands, ["resident", None, "resident"], dict(seq=seq))


def _emit_kv(ckv, krope_pad, wk_ref, wv_ref, k_ref, v_ref):
    c = ckv.astype(BF16)
    kx = _dot(c, wk_ref[...])
    vx = _dot(c, wv_ref[...])
    for h in range(N_HEADS):
        k_ref[h] = (kx[:, h * LANES:(h + 1) * LANES] + krope_pad).astype(BF16)
    for p in range(HEAD_PAIRS):
        v_ref[p] = vx[:, p * LANES:(p + 1) * LANES].astype(BF16)


def _mla_proj_kernel(x_ref, mod_ref, nw_ref, w1_ref, qn_ref, kvn_ref, w2_ref, wk_ref, wv_ref, cos_ref,
                     sin_ref, q_ref, k_ref, v_ref, ckv_ref, kr_ref, *, rope):
    h = _norm_mod(x_ref[...], nw_ref[...], mod_ref[...], 0).astype(BF16)
    z = _dot(h, w1_ref[...])
    cq = _rms(z[:, :Q_LORA], qn_ref[...]).astype(BF16)
    ckv = _rms(z[:, Q_LORA:Q_LORA + KV_LORA], kvn_ref[...])
    kr = z[:, Q_LORA + KV_LORA:Q_LORA + KV_LORA + LANES]
    if rope:
        cos, sin = cos_ref[...], sin_ref[...]
        kr = kr * cos + z[:, Q_LORA + KV_LORA + LANES:] * sin
    ckv_ref[...] = ckv
    kr_ref[...] = kr[:, QK_NOPE:QK_NOPE + QK_ROPE]
    c = ckv.astype(BF16)
    off = N_HEADS * LANES

    def pair_matmuls(p):
        cols = slice(2 * p * LANES, (2 * p + 2) * LANES)
        q2 = _dot(cq, w2_ref[:, cols])
        qs = _dot(cq, w2_ref[:, off + 2 * p * LANES:off + (2 * p + 2) * LANES]) if rope else None
        k2 = _dot(c, wk_ref[:, cols])
        v2 = _dot(c, wv_ref[:, p * LANES:(p + 2) * LANES]) if p % 2 == 0 else None
        return q2, qs, k2, v2

    nxt = pair_matmuls(0)
    for p in range(HEAD_PAIRS):
        q2, qs, k2, v2 = nxt
        if p + 1 < HEAD_PAIRS:
            nxt = pair_matmuls(p + 1)
        for e in range(2):
            lanes = slice(e * LANES, (e + 1) * LANES)
            qh = q2[:, lanes]
            if rope:
                qh = qh * cos + qs[:, lanes] * sin
            q_ref[2 * p + e] = (qh * Q_SCALE).astype(BF16)
            k_ref[2 * p + e] = (k2[:, lanes] + kr).astype(BF16)
        if v2 is not None:
            v_ref[p] = v2[:, :LANES].astype(BF16)
            v_ref[p + 1] = v2[:, LANES:].astype(BF16)


def _mla_proj_call(x, mods, li, cond_of_row, W, rope):
    m = x.shape[0]
    n1 = W["w1"].shape[1]
    n2 = W["w2"].shape[1]
    const = lambda i: (0, 0)
    pos_tiles = W["cos"].shape[0] // PROJ_TILE
    pos = lambda i: (i % pos_tiles, 0)
    return pl.pallas_call(
        functools.partial(_mla_proj_kernel, rope=rope),
        grid=(m // PROJ_TILE,),
        in_specs=[
            pl.BlockSpec((PROJ_TILE, D_MODEL), lambda i: (i, 0)),
            _mod_spec(li, cond_of_row, PROJ_TILE, 1),
            pl.BlockSpec((None, 1, D_MODEL), lambda i: (li, 0, 0)),
            pl.BlockSpec((D_MODEL, n1), const),
            pl.BlockSpec((1, Q_LORA), const),
            pl.BlockSpec((1, KV_LORA), const),
            pl.BlockSpec((Q_LORA, n2), const),
            pl.BlockSpec((KV_LORA, N_HEADS * LANES), const),
            pl.BlockSpec((KV_LORA, HEAD_PAIRS * LANES), const),
            pl.BlockSpec((PROJ_TILE, LANES), pos),
            pl.BlockSpec((PROJ_TILE, LANES), pos),
        ],
        out_specs=[
            pl.BlockSpec((N_HEADS, PROJ_TILE, LANES), lambda i: (0, i, 0)),
            pl.BlockSpec((N_HEADS, PROJ_TILE, LANES), lambda i: (0, i, 0)),
            pl.BlockSpec((HEAD_PAIRS, PROJ_TILE, LANES), lambda i: (0, i, 0)),
            pl.BlockSpec((PROJ_TILE, KV_LORA), lambda i: (i, 0)),
            pl.BlockSpec((PROJ_TILE, QK_ROPE), lambda i: (i, 0)),
        ],
        out_shape=[
            jax.ShapeDtypeStruct((N_HEADS, m, LANES), BF16),
            jax.ShapeDtypeStruct((N_HEADS, m, LANES), BF16),
            jax.ShapeDtypeStruct((HEAD_PAIRS, m, LANES), BF16),
            jax.ShapeDtypeStruct((m, KV_LORA), F32),
            jax.ShapeDtypeStruct((m, QK_ROPE), F32),
        ],
        compiler_params=_params(("arbitrary",)),
        name="mla_proj_rope" if rope else "mla_proj",
    )(x, mods, W["norm1_w"], W["w1"], W["q_norm"], W["kv_norm"], W["w2"], W["wk"], W["wv"],
      W["cos"], W["sin"])


def _kv_cache_kernel(ckv_ref, kr_ref, wk_ref, wv_ref, k_ref, v_ref):
    _emit_kv(ckv_ref[...], kr_ref[...], wk_ref, wv_ref, k_ref, v_ref)


def _kv_cache_call(ckv, krope_pad, W):
    m = ckv.shape[0]
    const = lambda i: (0, 0)
    return pl.pallas_call(
        _kv_cache_kernel,
        grid=(1,),
        in_specs=[
            pl.BlockSpec((m, KV_LORA), const),
            pl.BlockSpec((m, LANES), const),
            pl.BlockSpec((KV_LORA, N_HEADS * LANES), const),
            pl.BlockSpec((KV_LORA, HEAD_PAIRS * LANES), const),
        ],
        out_specs=[
            pl.BlockSpec((N_HEADS, m, LANES), lambda i: (0, 0, 0)),
            pl.BlockSpec((HEAD_PAIRS, m, LANES), lambda i: (0, 0, 0)),
        ],
        out_shape=[
            jax.ShapeDtypeStruct((N_HEADS, m, LANES), BF16),
            jax.ShapeDtypeStruct((HEAD_PAIRS, m, LANES), BF16),
        ],
        compiler_params=_params(("arbitrary",)),
        name="kv_cache",
    )(ckv, krope_pad, W["wk"], W["wv"])


def _attn_kernel(q_ref, *refs, n_pairs):
    o_ref, kv = refs[-1], refs[:-1]
    parts = [(kv[i], kv[i + 1]) for i in range(0, len(kv), 2)]
    heads = [(p, e) for p in range(n_pairs) for e in range(2)]

    def scores(p, e):
        q = q_ref[2 * p + e]
        return [lax.dot_general(q, k_ref[2 * p + e], (((1,), (1,)), ((), ())), preferred_element_type=F32)
                for k_ref, _ in parts]

    outs = []
    s_next = scores(*heads[0])
    for i, (p, e) in enumerate(heads):
        s = s_next
        if i + 1 < len(heads):
            s_next = scores(*heads[i + 1])
        m = functools.reduce(jnp.maximum, [jnp.max(x, axis=-1, keepdims=True) for x in s])
        w = [jnp.exp2(x - m) for x in s]
        l = sum(jnp.sum(x, axis=-1, keepdims=True) for x in w)
        pv = sum(_dot(x.astype(BF16), v_ref[p]) for x, (_, v_ref) in zip(w, parts))
        outs.append(pv / l)
        if e == 1:
            lane = lax.broadcasted_iota(jnp.int32, pv.shape, 1)
            o_ref[:, p * LANES:(p + 1) * LANES] = jnp.where(lane < V_HEAD, outs[-2], outs[-1]).astype(BF16)


def _attn_call(q, kv_parts, q_tile, n_pairs):
    _, b, t, _ = q.shape
    nq = t // q_tile
    in_specs = [pl.BlockSpec((2 * n_pairs, None, q_tile, LANES), lambda bi, g, qi: (g, bi, qi, 0))]
    operands = [q]
    for k, v in kv_parts:
        s = k.shape[2]
        in_specs += [pl.BlockSpec((2 * n_pairs, None, s, LANES), lambda bi, g, qi: (g, bi, 0, 0)),
                     pl.BlockSpec((n_pairs, None, s, LANES), lambda bi, g, qi: (g, bi, 0, 0))]
        operands += [k, v]
    return pl.pallas_call(
        functools.partial(_attn_kernel, n_pairs=n_pairs),
        grid=(b, HEAD_PAIRS // n_pairs, nq),
        in_specs=in_specs,
        out_specs=pl.BlockSpec((q_tile, n_pairs * LANES), lambda bi, g, qi: (bi * nq + qi, g)),
        out_shape=jax.ShapeDtypeStruct((b * t, N_HEADS * V_HEAD), BF16),
        compiler_params=_params(("arbitrary", "arbitrary", "arbitrary"), ATTN_VMEM),
        name=f"attn_{len(kv_parts)}part",
    )(*operands)


def _pool_kernel(x_ref, mod_ref, nw_ref, w_ref, sc_ref, o_ref, band_scr, h_scr, hi_scr, lo_scr, y_scr, *, seq):
    @pl.when(pl.program_id(0) == 0)
    def _():
        r = lax.broadcasted_iota(jnp.int32, (seq, seq), 0)
        c = lax.broadcasted_iota(jnp.int32, (seq, seq), 1)
        for g, win in enumerate(POOL_WINDOWS):
            inside = (c >= r - win // 2) & (c < r + win - win // 2)
            band_scr[g] = jnp.where(inside, 1.0, 0.0).astype(BF16)

    items = [(s, g) for s in range(ROW_TILE // seq) for g in range(N_GROUPS)]
    rows = lambda s: slice(s * seq, (s + 1) * seq)
    cols = lambda g: slice(g * GROUP, (g + 1) * GROUP)
    t = lax.broadcasted_iota(jnp.int32, (seq, 1), 0)
    counts = [(jnp.minimum(t + (win - win // 2), seq) - jnp.maximum(t - win // 2, 0)).astype(F32)
              for win in POOL_WINDOWS]

    def norm_seq(s):
        h = _norm_mod(x_ref[rows(s), :], nw_ref[...], mod_ref[...], 0)
        hi = h.astype(BF16)
        h_scr[rows(s), :] = h
        hi_scr[rows(s), :] = hi
        lo_scr[rows(s), :] = (h - hi.astype(F32)).astype(BF16)

    def window_sum(s, g):
        return _dot(band_scr[g], hi_scr[rows(s), cols(g)]) + _dot(band_scr[g], lo_scr[rows(s), cols(g)])

    norm_seq(0)
    tot = window_sum(*items[0])
    for i, (s, g) in enumerate(items):
        tot_next = None
        if i + 1 < len(items):
            s2, g2 = items[i + 1]
            if s2 != s:
                norm_seq(s2)
            tot_next = window_sum(s2, g2)
        pooled = (tot / counts[g] - h_scr[rows(s), cols(g)]).astype(BF16)
        y_scr[rows(s), cols(g)] = _dot(pooled, w_ref[g].astype(BF16))
        tot = tot_next
    for r in range(ROW_TILE // ROW_BLOCK):
        blk = slice(r * ROW_BLOCK, (r + 1) * ROW_BLOCK)
        o_ref[blk, :] = x_ref[blk, :] + mod_ref[2:3, :] * (y_scr[blk, :] * sc_ref[...])


def _pool_call(x, mods, li, cond_of_row, seq, P):
    m = x.shape[0]
    lj = li // 4
    return pl.pallas_call(
        functools.partial(_pool_kernel, seq=seq),
        grid=(m // ROW_TILE,),
        in_specs=[
            pl.BlockSpec((ROW_TILE, D_MODEL), lambda i: (i, 0)),
            _mod_spec(li, cond_of_row, ROW_TILE, 1),
            pl.BlockSpec((None, 1, D_MODEL), lambda i: (li, 0, 0)),
            pl.BlockSpec((None, N_GROUPS, GROUP, GROUP), lambda i: (lj, 0, 0, 0)),
            pl.BlockSpec((None, 1, D_MODEL), lambda i: (lj, 0, 0)),
        ],
        out_specs=pl.BlockSpec((ROW_TILE, D_MODEL), lambda i: (i, 0)),
        out_shape=jax.ShapeDtypeStruct((m, D_MODEL), F32),
        scratch_shapes=[pltpu.VMEM((N_GROUPS, seq, seq), BF16), pltpu.VMEM((ROW_TILE, D_MODEL), F32),
                        pltpu.VMEM((ROW_TILE, D_MODEL), BF16), pltpu.VMEM((ROW_TILE, D_MODEL), BF16),
                        pltpu.VMEM((ROW_TILE, D_MODEL), F32)],
        compiler_params=_params(("arbitrary",)),
        name=f"pool_t{seq}",
    )(x, mods, P["norm1_w"], P["pool_w"], P["pool_scale"])


def _dft_tables(seq):
    def cs(n):
        k = np.arange(n, dtype=np.int64)
        ang = 2.0 * np.pi * ((k[:, None] * k[None, :]) % n) / n
        return np.cos(ang), np.sin(ang)
    ct, st = cs(seq)
    cc, sc = cs(GROUP)
    e_t = jnp.asarray(np.concatenate([ct, st], axis=0), dtype=F32)
    e_c = jnp.asarray(np.concatenate([cc, -sc], axis=0), dtype=F32)
    return e_t.astype(BF16), e_c.astype(BF16)


def _fourier_kernel(x_ref, mod_ref, nw_ref, et_ref, ec_ref, w_ref, o_ref, h_scr, f_scr, wb_scr, *, seq):
    norm = np.float32(1.0 / math.sqrt(seq * GROUP))
    items = [(s, g) for s in range(ROW_TILE // seq) for g in range(N_GROUPS)]
    rows = lambda s: slice(s * seq, (s + 1) * seq)
    cols = lambda g: slice(g * GROUP, (g + 1) * GROUP)

    def norm_seq(s):
        h_scr[rows(s), :] = _norm_mod(x_ref[rows(s), :], nw_ref[...], mod_ref[...], 0).astype(BF16)

    def time_dft(s, g):
        return _dot(et_ref[...], h_scr[rows(s), cols(g)]).astype(BF16)

    norm_seq(0)
    wb_scr[...] = w_ref[...].astype(BF16)
    y = time_dft(*items[0])
    for i, (s, g) in enumerate(items):
        y_next = None
        if i + 1 < len(items):
            s2, g2 = items[i + 1]
            if s2 != s:
                norm_seq(s2)
            y_next = time_dft(s2, g2)
        f = _dot(y[:seq], ec_ref[:GROUP, :]) + _dot(y[seq:], ec_ref[GROUP:, :])
        f_scr[rows(s), cols(g)] = (f * norm).astype(BF16)
        y = y_next
    for r in range(ROW_TILE // ROW_BLOCK):
        blk = slice(r * ROW_BLOCK, (r + 1) * ROW_BLOCK)
        o_ref[blk, :] = x_ref[blk, :] + mod_ref[2:3, :] * _dot(f_scr[blk, :], wb_scr[...])


def _fourier_call(x, mods, li, cond_of_row, seq, P):
    m = x.shape[0]
    lj = li // 4
    e_t, e_c = _dft_tables(seq)
    return pl.pallas_call(
        functools.partial(_fourier_kernel, seq=seq),
        grid=(m // ROW_TILE,),
        in_specs=[
            pl.BlockSpec((ROW_TILE, D_MODEL), lambda i: (i, 0)),
            _mod_spec(li, cond_of_row, ROW_TILE, 1),
            pl.BlockSpec((None, 1, D_MODEL), lambda i: (li, 0, 0)),
            pl.BlockSpec((2 * seq, seq), lambda i: (0, 0)),
            pl.BlockSpec((2 * GROUP, GROUP), lambda i: (0, 0)),
            pl.BlockSpec((None, D_MODEL, D_MODEL), lambda i: (lj, 0, 0)),
        ],
        out_specs=pl.BlockSpec((ROW_TILE, D_MODEL), lambda i: (i, 0)),
        out_shape=jax.ShapeDtypeStruct((m, D_MODEL), F32),
        scratch_shapes=[pltpu.VMEM((ROW_TILE, D_MODEL), BF16), pltpu.VMEM((ROW_TILE, D_MODEL), BF16),
                        pltpu.VMEM((D_MODEL, D_MODEL), BF16)],
        compiler_params=_params(("arbitrary",)),
        name=f"fourier_t{seq}",
    )(x, mods, P["norm1_w"], e_t, e_c, P["fnet_w"])


def _rope_tables(rows):
    r, col = jnp.meshgrid(jnp.arange(rows), jnp.arange(GRID_W), indexing="ij")
    r = r.reshape(-1).astype(F32)
    col = col.reshape(-1).astype(F32)
    n_freq = QK_ROPE // 4
    inv = ROPE_THETA ** (-jnp.arange(n_freq, dtype=F32) / n_freq)
    ang = jnp.concatenate([r[:, None] * inv, col[:, None] * inv], axis=-1)
    cos = jnp.repeat(jnp.cos(ang), 2, axis=-1)
    sin = jnp.repeat(jnp.sin(ang), 2, axis=-1)
    t = cos.shape[0]
    pad = LANES - QK_NOPE - QK_ROPE
    cos_p = jnp.concatenate([jnp.ones((t, QK_NOPE), F32), cos, jnp.ones((t, pad), F32)], axis=-1)
    sin_p = jnp.concatenate([jnp.zeros((t, QK_NOPE), F32), sin, jnp.zeros((t, pad), F32)], axis=-1)
    return cos_p, sin_p


def _pair_swap(w):
    wr = w.reshape(w.shape[:-1] + (w.shape[-1] // 2, 2))
    return jnp.stack([-wr[..., 1], wr[..., 0]], axis=-1).reshape(w.shape)


def _mla_weights(P, j, rope, seq_rows):
    pad = LANES - QK_NOPE - QK_ROPE
    wuq = P["mla_wuq"][j].reshape(Q_LORA, N_HEADS, QK_NOPE + QK_ROPE)
    q_slots = jnp.pad(wuq, ((0, 0), (0, 0), (0, pad))).reshape(Q_LORA, N_HEADS * LANES)
    wdkv = P["mla_wdkv"][j]
    kr_slot = jnp.pad(wdkv[:, KV_LORA:], ((0, 0), (QK_NOPE, pad)))
    parts1 = [P["mla_wdq"][j], wdkv[:, :KV_LORA], kr_slot]
    parts2 = [q_slots]
    if rope:
        parts1.append(jnp.pad(_pair_swap(wdkv[:, KV_LORA:]), ((0, 0), (QK_NOPE, pad))))
        q_sw = jnp.pad(_pair_swap(wuq[:, :, QK_NOPE:]), ((0, 0), (0, 0), (QK_NOPE, pad)))
        parts2.append(q_sw.reshape(Q_LORA, N_HEADS * LANES))
        cos, sin = _rope_tables(seq_rows // GRID_W)
    else:
        cos = sin = jnp.zeros((PROJ_TILE, LANES), F32)
    wukv = P["mla_wukv"][j].reshape(KV_LORA, N_HEADS, QK_NOPE + V_HEAD)
    wk = jnp.pad(wukv[:, :, :QK_NOPE], ((0, 0), (0, 0), (0, LANES - QK_NOPE)))
    return dict(
        norm1_w=P["norm1_w"], q_norm=P["mla_q_norm"][j][None, :], kv_norm=P["mla_kv_norm"][j][None, :],
        w1=jnp.concatenate(parts1, axis=1).astype(BF16), w2=jnp.concatenate(parts2, axis=1).astype(BF16),
        wk=wk.reshape(KV_LORA, N_HEADS * LANES).astype(BF16),
        wv=wukv[:, :, QK_NOPE:].reshape(KV_LORA, N_HEADS * V_HEAD).astype(BF16), cos=cos, sin=sin)


def _mla_layer(x, mods, li, cond_of_row, batch, seq, P, cache):
    j = li // 4
    W = _mla_weights(P, j, cache is not None, seq)
    q, k, v, ckv, krope = _mla_proj_call(x, mods, li, cond_of_row, W, cache is not None)
    q = q.reshape(N_HEADS, batch, seq, LANES)
    k = k.reshape(N_HEADS, batch, seq, LANES)
    v = v.reshape(HEAD_PAIRS, batch, seq, LANES)
    if cache is None:
        o = _attn_call(q, [(k, v)], seq, HEAD_PAIRS)
    else:
        c_ckv, c_krope = cache
        past = c_ckv.shape[1]
        pad = LANES - QK_NOPE - QK_ROPE
        kr_pad = jnp.pad(c_krope.reshape(batch * past, QK_ROPE), ((0, 0), (QK_NOPE, pad)))
        kc, vc = _kv_cache_call(c_ckv.reshape(batch * past, KV_LORA), kr_pad, W)
        cached = (kc.reshape(N_HEADS, batch, past, LANES), vc.reshape(HEAD_PAIRS, batch, past, LANES))
        o = _attn_call(q, [cached, (k, v)], 512, 4)
    return o, ckv, krope


def _mixer(x, mods, li, cond_of_row, batch, seq, P, cache):
    kind = li % 4
    if kind == 0:
        o, ckv, krope = _mla_layer(x, mods, li, cond_of_row, batch, seq, P, cache)
        return x, (o, P["mla_wo_bf16"], li // 4), (ckv, krope)
    if kind == 1:
        return _pool_call(x, mods, li, cond_of_row, seq, P), None, None
    if kind == 2:
        return _fourier_call(x, mods, li, cond_of_row, seq, P), None, None
    return _sconv_call(x, mods, li, cond_of_row, seq, P), None, None


def _run_passes(passes, mods, P):
    new_cache = None
    for li in range(DEPTH):
        pending = []
        for ps in passes:
            ps["x"], pre, made = _mixer(ps["x"], mods, li, ps["cond_of_row"], ps["batch"], ps["seq"], P,
                                        ps["cache"])
            pending.append(pre)
            if made is not None and ps["cache"] is None:
                new_cache = made
        xs, outs = lax.optimization_barrier(([ps["x"] for ps in passes],
                                              [None if pre is None else pre[0] for pre in pending]))
        pending = [None if pre is None else (o,) + pre[1:] for pre, o in zip(pending, outs)]
        xs = [_ffn_call(x, mods, li, ps["cond_of_row"], ps["seq"], P, pre, final_norm=(li == DEPTH - 1))
              for x, ps, pre in zip(xs, passes, pending)]
        xs = lax.optimization_barrier(xs)
        for ps, x in zip(passes, xs):
            ps["x"] = x
    return new_cache


def kernel(x_prompt, x_sample, cache_ckv, cache_krope, c, c_ctx, norm1_w, norm2_w, ada_w, ada_b, mla_wdq, mla_q_norm, mla_wuq, mla_wdkv, mla_kv_norm, mla_wukv, mla_wo, pool_w, pool_scale, fnet_w, sconv_win, sconv_conv, sconv_wout, ffn_up, ffn_conv_w, ffn_conv_b, ffn_down, final_norm_w):
    row = lambda w: w.reshape(w.shape[0], 1, w.shape[1])
    P = dict(norm1_w=row(norm1_w), norm2_w=row(norm2_w), mla_wdq=mla_wdq, mla_q_norm=mla_q_norm,
             mla_wuq=mla_wuq, mla_wdkv=mla_wdkv, mla_kv_norm=mla_kv_norm, mla_wukv=mla_wukv,
             mla_wo_bf16=mla_wo.astype(BF16),
             pool_w=pool_w, pool_scale=row(pool_scale), fnet_w=fnet_w, sconv_win=sconv_win,
             sconv_conv=sconv_conv, sconv_wout=sconv_wout, ffn_conv_w=ffn_conv_w,
             ffn_conv_b=ffn_conv_b, ffn_down=ffn_down, final_norm_w=final_norm_w,
             ffn_w=functools.cache(lambda l: _layer_weights(ffn_up, ffn_down, l, 2, f"ffn{l}_weights")),
             sconv_w=functools.cache(lambda l: _layer_weights(sconv_win, sconv_wout, l, 3, f"sconv{l}_weights")))
    b_ctx, t_ctx, _ = x_prompt.shape
    b_lat, t_lat, _ = x_sample.shape
    assert ROW_TILE % t_ctx == 0 and t_lat % ROW_TILE == 0 and b_lat + 1 <= N_COND

    conds = jnp.concatenate([c_ctx[None, :], c, jnp.zeros((N_COND - 1 - b_lat, D_MODEL), F32)], axis=0)
    mods = _ada_call(conds, ada_w, ada_b).reshape(DEPTH, N_COND, 6, D_MODEL)

    ctx = dict(x=x_prompt.reshape(b_ctx * t_ctx, D_MODEL), cond_of_row=lambda r: 0, batch=b_ctx, seq=t_ctx,
               cache=None)
    lat = dict(x=x_sample.reshape(b_lat * t_lat, D_MODEL), cond_of_row=lambda r: 1 + r // t_lat, batch=b_lat,
               seq=t_lat, cache=(cache_ckv[:, 0], cache_krope[:, 0]))
    ckv, krope = _run_passes([ctx, lat], mods, P)
    return (ctx["x"].reshape(b_ctx, t_ctx, D_MODEL), lat["x"].reshape(b_lat, t_lat, D_MODEL),
            ckv.reshape(b_ctx, 1, t_ctx, KV_LORA), krope.reshape(b_ctx, 1, t_ctx, QK_ROPE))
```

```python
import functools
import math

import numpy as np
import jax
import jax.numpy as jnp
from jax import lax
from jax.experimental import pallas as pl
from jax.experimental.pallas import tpu as pltpu

F32 = jnp.float32
BF16 = jnp.bfloat16

D_MODEL = 1024
N_HEADS = 16
QK_NOPE = 64
QK_ROPE = 32
V_HEAD = 64
Q_LORA = 384
KV_LORA = 256
ROPE_THETA = 10000.0
GRID_W = 64
POOL_WINDOWS = (2, 4, 8, 16)
N_GROUPS = 4
GROUP = D_MODEL // N_GROUPS
D_FF = 2816
EPS = 1e-6
DEPTH = 4

LANES = 128
Q_SCALE = np.float32(math.log2(math.e) / math.sqrt(QK_NOPE + QK_ROPE))
HEAD_PAIRS = N_HEADS // 2
N_COND = 8
ROW_TILE = 1024
PROJ_TILE = 512
FF_TILE = 256
ROW_BLOCK = 256
HALO = 8
RELAYOUT_ROWS = 128
CHUNK_SLOTS = 16
ADA_TILE = 1536
MIB = 1024 * 1024
VMEM_DEFAULT = 48 * MIB
VMEM_HEADROOM = 4 * MIB + 512 * 1024
ATTN_VMEM = 48 * MIB


def _params(sem, vmem_bytes=VMEM_DEFAULT):
    return pltpu.CompilerParams(dimension_semantics=sem, vmem_limit_bytes=int(vmem_bytes))


def _nbytes(a):
    return math.prod(a.shape) * jnp.dtype(a.dtype).itemsize


def _dot(a, b):
    return jnp.dot(a, b, preferred_element_type=F32)


def _rms(x, w):
    return x * lax.rsqrt(jnp.mean(x * x, axis=-1, keepdims=True) + EPS) * w


def _norm_mod(x, nw, mod, base):
    return _rms(x, nw) * (1.0 + mod[base + 1:base + 2, :]) + mod[base:base + 1, :]


def _gelu_exact(x):
    return 0.5 * x * (1.0 + lax.erf(x * np.float32(math.sqrt(0.5))))


def _ada_kernel(c_ref, w_ref, b_ref, o_ref):
    s = jax.nn.silu(c_ref[...]).astype(BF16)
    o_ref[...] = _dot(s, w_ref[...].astype(BF16)) + b_ref[...]


def _ada_call(conds, ada_w, ada_b):
    n_out = ada_w.shape[-1]
    return pl.pallas_call(
        _ada_kernel,
        grid=(DEPTH, n_out // ADA_TILE),
        in_specs=[
            pl.BlockSpec((N_COND, D_MODEL), lambda l, j: (0, 0)),
            pl.BlockSpec((None, D_MODEL, ADA_TILE), lambda l, j: (l, 0, j)),
            pl.BlockSpec((None, 1, ADA_TILE), lambda l, j: (l, 0, j)),
        ],
        out_specs=pl.BlockSpec((None, N_COND, ADA_TILE), lambda l, j: (l, 0, j)),
        out_shape=jax.ShapeDtypeStruct((DEPTH, N_COND, n_out), F32),
        compiler_params=_params(("arbitrary", "arbitrary"), 2 * D_MODEL * ADA_TILE * 4 + VMEM_HEADROOM),
        name="ada",
    )(conds, ada_w, ada_b.reshape(DEPTH, 1, n_out))


def _mod_spec(li, cond_of_row, tile, n_grid):
    if n_grid == 1:
        return pl.BlockSpec((None, None, 6, D_MODEL), lambda i: (li, cond_of_row(i * tile), 0, 0))
    return pl.BlockSpec((None, None, 6, D_MODEL), lambda i, j: (li, cond_of_row(i * tile), 0, 0))


def _conv_window(w, cw, t0, seq):
    n = w.shape[0]
    mid = slice(HALO, HALO + ROW_BLOCK)
    t = (t0 + lax.broadcasted_iota(jnp.int32, (ROW_BLOCK, 1), 0)) & (seq - 1)
    prev = jnp.where(t == 0, 0.0, pltpu.roll(w, 1, axis=0)[mid])
    nxt = jnp.where(t == seq - 1, 0.0, pltpu.roll(w, n - 1, axis=0)[mid])
    return prev * cw[0:1, :] + w[mid] * cw[1:2, :] + nxt * cw[2:3, :]


def _ffn_act(win, mid, cw, cb, t0, seq):
    return _gelu_exact(_conv_window(win(0), cw, t0, seq) + cb[0:1, :]) * mid(1)


def _sconv_act(win, mid, cw, cb, t0, seq):
    return mid(0) * _conv_window(win(1) * win(2), cw, t0, seq)


def _ffn_kernel(x_ref, mod_ref, nw_ref, up_ref, cw_ref, cb_ref, down_ref, fnw_ref, o_ref, *scratch, **cfg):
    _glu_body(x_ref, mod_ref, nw_ref, up_ref, cw_ref, cb_ref, down_ref, fnw_ref, None, o_ref, *scratch,
              n_up=2, act=_ffn_act, mod_base=3, **cfg)


def _ffn_pre_kernel(x_ref, mod_ref, nw_ref, up_ref, cw_ref, cb_ref, down_ref, fnw_ref, a_ref, wa_ref, o_ref,
                    *scratch, **cfg):
    _glu_body(x_ref, mod_ref, nw_ref, up_ref, cw_ref, cb_ref, down_ref, fnw_ref, (a_ref, wa_ref), o_ref,
              *scratch, n_up=2, act=_ffn_act, mod_base=3, **cfg)


def _sconv_kernel(x_ref, mod_ref, nw_ref, up_ref, cw_ref, down_ref, o_ref, *scratch, **cfg):
    _glu_body(x_ref, mod_ref, nw_ref, up_ref, cw_ref, None, down_ref, None, None, o_ref, *scratch,
              n_up=3, act=_sconv_act, mod_base=0, final_norm=False, **cfg)


def _glu_body(x_ref, mod_ref, nw_ref, up_ref, cw_ref, cb_ref, down_ref, fnw_ref, pre, o_ref,
              h_scr, acc_scr, gu0, gu1, *, n_up, n_chunks, act, seq, mod_base, final_norm):
    gu = (gu0, gu1)
    n_blocks = ROW_TILE // ROW_BLOCK

    def up(c, slot, r):
        rows = slice(r * ROW_BLOCK, (r + 1) * ROW_BLOCK)
        res = _dot(h_scr[rows, :], up_ref[c])
        for p in range(n_up):
            gu[slot][p, HALO + r * ROW_BLOCK:HALO + (r + 1) * ROW_BLOCK, :] = res[:, p * FF_TILE:(p + 1) * FF_TILE]

    def activate(c, slot, r):
        win = lambda p: gu[slot][p, r * ROW_BLOCK:(r + 1) * ROW_BLOCK + 2 * HALO, :]
        mid = lambda p: gu[slot][p, HALO + r * ROW_BLOCK:HALO + (r + 1) * ROW_BLOCK, :]
        cb = None if cb_ref is None else cb_ref[c]
        return act(win, mid, cw_ref[c], cb, r * ROW_BLOCK, seq).astype(BF16)

    def down(a, c, r):
        rows = slice(r * ROW_BLOCK, (r + 1) * ROW_BLOCK)
        y = _dot(a, down_ref[c])
        if isinstance(c, int) and c == 0:
            acc_scr[rows, :] = y
        else:
            acc_scr[rows, :] += y

    def stage(s, slot):
        for r in range(n_blocks):
            a = activate(s - 1, 1 - slot, r)
            up(s, slot, r)
            down(a, s - 1, r)

    def block(r):
        return slice(r * ROW_BLOCK, (r + 1) * ROW_BLOCK)

    def mixer_proj(r):
        a_ref, wa_ref = pre
        return _dot(a_ref[block(r), :], wa_ref[...])

    for g in gu:
        g[:, :HALO, :] = jnp.zeros((n_up, HALO, FF_TILE), F32)
        g[:, HALO + ROW_TILE:, :] = jnp.zeros((n_up, HALO, FF_TILE), F32)

    proj = mixer_proj(0) if pre is not None else None
    for r in range(n_blocks):
        x = x_ref[block(r), :]
        if pre is not None:
            x = x + mod_ref[2:3, :] * proj
            o_ref[block(r), :] = x
            if r + 1 < n_blocks:
                proj = mixer_proj(r + 1)
        h_scr[block(r), :] = _norm_mod(x, nw_ref[...], mod_ref[...], mod_base).astype(BF16)
        up(0, 0, r)

    if n_chunks > 1:
        stage(1, 1)
    n_pairs = (n_chunks - 2) // 2

    def pair(i, carry):
        s = 2 * i + 2
        stage(s, 0)
        stage(s + 1, 1)
        return carry

    lax.fori_loop(0, n_pairs, pair, 0)
    for s in range(2 * n_pairs + 2, n_chunks):
        stage(s, s % 2)
    last = n_chunks - 1
    res = x_ref if pre is None else o_ref
    for r in range(n_blocks):
        down(activate(last, last % 2, r), last, r)
        y = res[block(r), :] + mod_ref[mod_base + 2:mod_base + 3, :] * acc_scr[block(r), :]
        if final_norm:
            y = _rms(y, fnw_ref[0:1, :])
        o_ref[block(r), :] = y


def _chunked(w, n_chunks):
    w = w.reshape(w.shape[0], n_chunks, FF_TILE).transpose(1, 0, 2)
    return jnp.pad(w, ((0, CHUNK_SLOTS - n_chunks), (0, 8 - w.shape[1]), (0, 0)))


def _glu_call(kernel_fn, name, x, mods, li, cond_of_row, n_up, n_chunks, norm_w, operands, specs, cfg):
    m = x.shape[0]
    whole = lambda a: pl.BlockSpec(a.shape, lambda i: (0,) * a.ndim)
    in_specs = [pl.BlockSpec((ROW_TILE, D_MODEL), lambda i: (i, 0)),
                _mod_spec(li, cond_of_row, ROW_TILE, 1),
                pl.BlockSpec((None, 1, D_MODEL), lambda i: (li, 0, 0))]
    layer_w = lambda a, l: pl.BlockSpec((None,) + a.shape[1:], lambda i: (l,) + (0,) * (a.ndim - 1),
                                        pipeline_mode=pl.Buffered(1))
    rows = lambda a: pl.BlockSpec((ROW_TILE, a.shape[1]), lambda i: (i, 0))
    spec_of = {None: whole, "rows": rows, "resident": lambda a: pl.BlockSpec(memory_space=pltpu.VMEM)}
    in_specs += [layer_w(a, s) if isinstance(s, int) else spec_of[s](a) for a, s in zip(operands, specs)]
    gu_shape = (n_up, ROW_TILE + 2 * HALO, FF_TILE)
    tile = ROW_TILE * D_MODEL
    window = {None: _nbytes, "rows": lambda a: 2 * ROW_TILE * a.shape[1] * a.dtype.itemsize,
              "resident": lambda a: 0}
    vmem = (2 * 2 * tile * 4 + tile * 2 + tile * 4 + 2 * math.prod(gu_shape) * 4 + VMEM_HEADROOM
            + sum(_nbytes(a) // a.shape[0] if isinstance(s, int) else window[s](a)
                  for a, s in zip(operands, specs)))
    return pl.pallas_call(
        functools.partial(kernel_fn, n_chunks=n_chunks, **cfg),
        grid=(m // ROW_TILE,),
        in_specs=in_specs,
        out_specs=pl.BlockSpec((ROW_TILE, D_MODEL), lambda i: (i, 0)),
        out_shape=jax.ShapeDtypeStruct((m, D_MODEL), F32),
        scratch_shapes=[
            pltpu.VMEM((ROW_TILE, D_MODEL), BF16),
            pltpu.VMEM((ROW_TILE, D_MODEL), F32),
            pltpu.VMEM(gu_shape, F32), pltpu.VMEM(gu_shape, F32),
        ],
        compiler_params=_params(("arbitrary",), vmem),
        name=name,
    )(x, mods, norm_w, *operands)


def _in_vmem(a):
    return pltpu.with_memory_space_constraint(a, pltpu.VMEM)


def _layer_weights_kernel(up_ref, dn_ref, uo_ref, do_ref, *, n_up, n_chunks):
    for c in range(n_chunks):
        for p in range(n_up):
            col = (p * n_chunks + c) * FF_TILE
            uo_ref[c, :, p * FF_TILE:(p + 1) * FF_TILE] = up_ref[:, col:col + FF_TILE].astype(BF16)
    do_ref[...] = dn_ref[...].astype(BF16)


def _layer_weights(up, down, layer, n_up, name):
    _, d, cols = up.shape
    _, hidden, d_out = down.shape
    n_chunks = hidden // FF_TILE
    steps = d // RELAYOUT_ROWS
    dn_rows = hidden // steps
    up_cm, dn = pl.pallas_call(
        functools.partial(_layer_weights_kernel, n_up=n_up, n_chunks=n_chunks),
        grid=(steps,),
        in_specs=[pl.BlockSpec((None, RELAYOUT_ROWS, cols), lambda k: (layer, k, 0)),
                  pl.BlockSpec((None, dn_rows, d_out), lambda k: (layer, k, 0))],
        out_specs=[pl.BlockSpec((n_chunks, RELAYOUT_ROWS, n_up * FF_TILE), lambda k: (0, k, 0)),
                   pl.BlockSpec((dn_rows, d_out), lambda k: (k, 0))],
        out_shape=[jax.ShapeDtypeStruct((n_chunks, d, n_up * FF_TILE), BF16),
                   jax.ShapeDtypeStruct((hidden, d_out), BF16)],
        compiler_params=_params(("arbitrary",),
                                2 * (RELAYOUT_ROWS * cols + dn_rows * d_out) * (4 + 2) + VMEM_HEADROOM),
        name=name,
    )(up, down)
    return _in_vmem(up_cm), _in_vmem(dn.reshape(n_chunks, FF_TILE, d_out))


def _ffn_call(x, mods, li, cond_of_row, seq, P, pre, final_norm):
    n = D_FF // FF_TILE
    up_cm, down_cm = P["ffn_w"](li)
    operands = [up_cm, _chunked(P["ffn_conv_w"][li], n), _chunked(P["ffn_conv_b"][li][None, :], n),
                down_cm, jnp.pad(P["final_norm_w"].reshape(1, D_MODEL), ((0, 7), (0, 0)))]
    specs = ["resident", None, None, "resident", None]
    kernel_fn = _ffn_kernel
    if pre is not None:
        a, w_a, l_a = pre
        operands += [a, w_a]
        specs += ["rows", l_a]
        kernel_fn = _ffn_pre_kernel
    return _glu_call(kernel_fn, f"ffn{li}", x, mods, li, cond_of_row, 2, n, P["norm2_w"],
                     operands, specs, dict(seq=seq, final_norm=final_norm))


def _sconv_call(x, mods, li, cond_of_row, seq, P):
    lj = li // 4
    n = D_MODEL // FF_TILE
    up_cm, down_cm = P["sconv_w"](lj)
    operands = [up_cm, _chunked(P["sconv_conv"][lj], n), down_cm]
    return _glu_call(_sconv_kernel, f"sconv{li}", x, mods, li, cond_of_row, 3, n, P["norm1_w"],
                     operands, ["resident", None, "resident"], dict(seq=seq))


def _emit_kv(ckv, krope_pad, wk_ref, wv_ref, k_ref, v_ref):
    c = ckv.astype(BF16)
    kx = _dot(c, wk_ref[...])
    vx = _dot(c, wv_ref[...])
    for h in range(N_HEADS):
        k_ref[h] = (kx[:, h * LANES:(h + 1) * LANES] + krope_pad).astype(BF16)
    for p in range(HEAD_PAIRS):
        v_ref[p] = vx[:, p * LANES:(p + 1) * LANES].astype(BF16)


def _mla_proj_kernel(x_ref, mod_ref, nw_ref, w1_ref, qn_ref, kvn_ref, w2_ref, wk_ref, wv_ref, cos_ref,
                     sin_ref, q_ref, k_ref, v_ref, ckv_ref, kr_ref, *, rope):
    h = _norm_mod(x_ref[...], nw_ref[...], mod_ref[...], 0).astype(BF16)
    z = _dot(h, w1_ref[...])
    cq = _rms(z[:, :Q_LORA], qn_ref[...]).astype(BF16)
    ckv = _rms(z[:, Q_LORA:Q_LORA + KV_LORA], kvn_ref[...])
    kr = z[:, Q_LORA + KV_LORA:Q_LORA + KV_LORA + LANES]
    if rope:
        cos, sin = cos_ref[...], sin_ref[...]
        kr = kr * cos + z[:, Q_LORA + KV_LORA + LANES:] * sin
    ckv_ref[...] = ckv
    kr_ref[...] = kr[:, QK_NOPE:QK_NOPE + QK_ROPE]
    c = ckv.astype(BF16)
    off = N_HEADS * LANES

    def pair_matmuls(p):
        cols = slice(2 * p * LANES, (2 * p + 2) * LANES)
        q2 = _dot(cq, w2_ref[:, cols])
        qs = _dot(cq, w2_ref[:, off + 2 * p * LANES:off + (2 * p + 2) * LANES]) if rope else None
        k2 = _dot(c, wk_ref[:, cols])
        v2 = _dot(c, wv_ref[:, p * LANES:(p + 2) * LANES]) if p % 2 == 0 else None
        return q2, qs, k2, v2

    nxt = pair_matmuls(0)
    for p in range(HEAD_PAIRS):
        q2, qs, k2, v2 = nxt
        if p + 1 < HEAD_PAIRS:
            nxt = pair_matmuls(p + 1)
        for e in range(2):
            lanes = slice(e * LANES, (e + 1) * LANES)
            qh = q2[:, lanes]
            if rope:
                qh = qh * cos + qs[:, lanes] * sin
            q_ref[2 * p + e] = (qh * Q_SCALE).astype(BF16)
            k_ref[2 * p + e] = (k2[:, lanes] + kr).astype(BF16)
        if v2 is not None:
            v_ref[p] = v2[:, :LANES].astype(BF16)
            v_ref[p + 1] = v2[:, LANES:].astype(BF16)


def _mla_proj_call(x, mods, li, cond_of_row, W, rope):
    m = x.shape[0]
    n1 = W["w1"].shape[1]
    n2 = W["w2"].shape[1]
    const = lambda i: (0, 0)
    pos_tiles = W["cos"].shape[0] // PROJ_TILE
    pos = lambda i: (i % pos_tiles, 0)
    return pl.pallas_call(
        functools.partial(_mla_proj_kernel, rope=rope),
        grid=(m // PROJ_TILE,),
        in_specs=[
            pl.BlockSpec((PROJ_TILE, D_MODEL), lambda i: (i, 0)),
            _mod_spec(li, cond_of_row, PROJ_TILE, 1),
            pl.BlockSpec((None, 1, D_MODEL), lambda i: (li, 0, 0)),
            pl.BlockSpec((D_MODEL, n1), const),
            pl.BlockSpec((1, Q_LORA), const),
            pl.BlockSpec((1, KV_LORA), const),
            pl.BlockSpec((Q_LORA, n2), const),
            pl.BlockSpec((KV_LORA, N_HEADS * LANES), const),
            pl.BlockSpec((KV_LORA, HEAD_PAIRS * LANES), const),
            pl.BlockSpec((PROJ_TILE, LANES), pos),
            pl.BlockSpec((PROJ_TILE, LANES), pos),
        ],
        out_specs=[
            pl.BlockSpec((N_HEADS, PROJ_TILE, LANES), lambda i: (0, i, 0)),
            pl.BlockSpec((N_HEADS, PROJ_TILE, LANES), lambda i: (0, i, 0)),
            pl.BlockSpec((HEAD_PAIRS, PROJ_TILE, LANES), lambda i: (0, i, 0)),
            pl.BlockSpec((PROJ_TILE, KV_LORA), lambda i: (i, 0)),
            pl.BlockSpec((PROJ_TILE, QK_ROPE), lambda i: (i, 0)),
        ],
        out_shape=[
            jax.ShapeDtypeStruct((N_HEADS, m, LANES), BF16),
            jax.ShapeDtypeStruct((N_HEADS, m, LANES), BF16),
            jax.ShapeDtypeStruct((HEAD_PAIRS, m, LANES), BF16),
            jax.ShapeDtypeStruct((m, KV_LORA), F32),
            jax.ShapeDtypeStruct((m, QK_ROPE), F32),
        ],
        compiler_params=_params(("arbitrary",)),
        name="mla_proj_rope" if rope else "mla_proj",
    )(x, mods, W["norm1_w"], W["w1"], W["q_norm"], W["kv_norm"], W["w2"], W["wk"], W["wv"],
      W["cos"], W["sin"])


def _kv_cache_kernel(ckv_ref, kr_ref, wk_ref, wv_ref, k_ref, v_ref):
    _emit_kv(ckv_ref[...], kr_ref[...], wk_ref, wv_ref, k_ref, v_ref)


def _kv_cache_call(ckv, krope_pad, W):
    m = ckv.shape[0]
    const = lambda i: (0, 0)
    return pl.pallas_call(
        _kv_cache_kernel,
        grid=(1,),
        in_specs=[
            pl.BlockSpec((m, KV_LORA), const),
            pl.BlockSpec((m, LANES), const),
            pl.BlockSpec((KV_LORA, N_HEADS * LANES), const),
            pl.BlockSpec((KV_LORA, HEAD_PAIRS * LANES), const),
        ],
        out_specs=[
            pl.BlockSpec((N_HEADS, m, LANES), lambda i: (0, 0, 0)),
            pl.BlockSpec((HEAD_PAIRS, m, LANES), lambda i: (0, 0, 0)),
        ],
        out_shape=[
            jax.ShapeDtypeStruct((N_HEADS, m, LANES), BF16),
            jax.ShapeDtypeStruct((HEAD_PAIRS, m, LANES), BF16),
        ],
        compiler_params=_params(("arbitrary",)),
        name="kv_cache",
    )(ckv, krope_pad, W["wk"], W["wv"])


def _attn_kernel(q_ref, *refs, n_pairs):
    o_ref, kv = refs[-1], refs[:-1]
    parts = [(kv[i], kv[i + 1]) for i in range(0, len(kv), 2)]
    heads = [(p, e) for p in range(n_pairs) for e in range(2)]

    def scores(p, e):
        q = q_ref[2 * p + e]
        return [lax.dot_general(q, k_ref[2 * p + e], (((1,), (1,)), ((), ())), preferred_element_type=F32)
                for k_ref, _ in parts]

    outs = []
    s_next = scores(*heads[0])
    for i, (p, e) in enumerate(heads):
        s = s_next
        if i + 1 < len(heads):
            s_next = scores(*heads[i + 1])
        m = functools.reduce(jnp.maximum, [jnp.max(x, axis=-1, keepdims=True) for x in s])
        w = [jnp.exp2(x - m) for x in s]
        l = sum(jnp.sum(x, axis=-1, keepdims=True) for x in w)
        pv = sum(_dot(x.astype(BF16), v_ref[p]) for x, (_, v_ref) in zip(w, parts))
        outs.append(pv / l)
        if e == 1:
            lane = lax.broadcasted_iota(jnp.int32, pv.shape, 1)
            o_ref[:, p * LANES:(p + 1) * LANES] = jnp.where(lane < V_HEAD, outs[-2], outs[-1]).astype(BF16)


def _attn_call(q, kv_parts, q_tile, n_pairs):
    _, b, t, _ = q.shape
    nq = t // q_tile
    in_specs = [pl.BlockSpec((2 * n_pairs, None, q_tile, LANES), lambda bi, g, qi: (g, bi, qi, 0))]
    operands = [q]
    for k, v in kv_parts:
        s = k.shape[2]
        in_specs += [pl.BlockSpec((2 * n_pairs, None, s, LANES), lambda bi, g, qi: (g, bi, 0, 0)),
                     pl.BlockSpec((n_pairs, None, s, LANES), lambda bi, g, qi: (g, bi, 0, 0))]
        operands += [k, v]
    return pl.pallas_call(
        functools.partial(_attn_kernel, n_pairs=n_pairs),
        grid=(b, HEAD_PAIRS // n_pairs, nq),
        in_specs=in_specs,
        out_specs=pl.BlockSpec((q_tile, n_pairs * LANES), lambda bi, g, qi: (bi * nq + qi, g)),
        out_shape=jax.ShapeDtypeStruct((b * t, N_HEADS * V_HEAD), BF16),
        compiler_params=_params(("arbitrary", "arbitrary", "arbitrary"), ATTN_VMEM),
        name=f"attn_{len(kv_parts)}part",
    )(*operands)


def _pool_kernel(x_ref, mod_ref, nw_ref, w_ref, sc_ref, o_ref, band_scr, h_scr, hi_scr, lo_scr, y_scr, *, seq):
    @pl.when(pl.program_id(0) == 0)
    def _():
        r = lax.broadcasted_iota(jnp.int32, (seq, seq), 0)
        c = lax.broadcasted_iota(jnp.int32, (seq, seq), 1)
        for g, win in enumerate(POOL_WINDOWS):
            inside = (c >= r - win // 2) & (c < r + win - win // 2)
            band_scr[g] = jnp.where(inside, 1.0, 0.0).astype(BF16)

    items = [(s, g) for s in range(ROW_TILE // seq) for g in range(N_GROUPS)]
    rows = lambda s: slice(s * seq, (s + 1) * seq)
    cols = lambda g: slice(g * GROUP, (g + 1) * GROUP)
    t = lax.broadcasted_iota(jnp.int32, (seq, 1), 0)
    counts = [(jnp.minimum(t + (win - win // 2), seq) - jnp.maximum(t - win // 2, 0)).astype(F32)
              for win in POOL_WINDOWS]

    def norm_seq(s):
        h = _norm_mod(x_ref[rows(s), :], nw_ref[...], mod_ref[...], 0)
        hi = h.astype(BF16)
        h_scr[rows(s), :] = h
        hi_scr[rows(s), :] = hi
        lo_scr[rows(s), :] = (h - hi.astype(F32)).astype(BF16)

    def window_sum(s, g):
        return _dot(band_scr[g], hi_scr[rows(s), cols(g)]) + _dot(band_scr[g], lo_scr[rows(s), cols(g)])

    norm_seq(0)
    tot = window_sum(*items[0])
    for i, (s, g) in enumerate(items):
        tot_next = None
        if i + 1 < len(items):
            s2, g2 = items[i + 1]
            if s2 != s:
                norm_seq(s2)
            tot_next = window_sum(s2, g2)
        pooled = (tot / counts[g] - h_scr[rows(s), cols(g)]).astype(BF16)
        y_scr[rows(s), cols(g)] = _dot(pooled, w_ref[g].astype(BF16))
        tot = tot_next
    for r in range(ROW_TILE // ROW_BLOCK):
        blk = slice(r * ROW_BLOCK, (r + 1) * ROW_BLOCK)
        o_ref[blk, :] = x_ref[blk, :] + mod_ref[2:3, :] * (y_scr[blk, :] * sc_ref[...])


def _pool_call(x, mods, li, cond_of_row, seq, P):
    m = x.shape[0]
    lj = li // 4
    return pl.pallas_call(
        functools.partial(_pool_kernel, seq=seq),
        grid=(m // ROW_TILE,),
        in_specs=[
            pl.BlockSpec((ROW_TILE, D_MODEL), lambda i: (i, 0)),
            _mod_spec(li, cond_of_row, ROW_TILE, 1),
            pl.BlockSpec((None, 1, D_MODEL), lambda i: (li, 0, 0)),
            pl.BlockSpec((None, N_GROUPS, GROUP, GROUP), lambda i: (lj, 0, 0, 0)),
            pl.BlockSpec((None, 1, D_MODEL), lambda i: (lj, 0, 0)),
        ],
        out_specs=pl.BlockSpec((ROW_TILE, D_MODEL), lambda i: (i, 0)),
        out_shape=jax.ShapeDtypeStruct((m, D_MODEL), F32),
        scratch_shapes=[pltpu.VMEM((N_GROUPS, seq, seq), BF16), pltpu.VMEM((ROW_TILE, D_MODEL), F32),
                        pltpu.VMEM((ROW_TILE, D_MODEL), BF16), pltpu.VMEM((ROW_TILE, D_MODEL), BF16),
                        pltpu.VMEM((ROW_TILE, D_MODEL), F32)],
        compiler_params=_params(("arbitrary",)),
        name=f"pool_t{seq}",
    )(x, mods, P["norm1_w"], P["pool_w"], P["pool_scale"])


def _dft_tables(seq):
    def cs(n):
        k = np.arange(n, dtype=np.int64)
        ang = 2.0 * np.pi * ((k[:, None] * k[None, :]) % n) / n
        return np.cos(ang), np.sin(ang)
    ct, st = cs(seq)
    cc, sc = cs(GROUP)
    e_t = jnp.asarray(np.concatenate([ct, st], axis=0), dtype=F32)
    e_c = jnp.asarray(np.concatenate([cc, -sc], axis=0), dtype=F32)
    return e_t.astype(BF16), e_c.astype(BF16)


def _fourier_kernel(x_ref, mod_ref, nw_ref, et_ref, ec_ref, w_ref, o_ref, h_scr, f_scr, wb_scr, *, seq):
    norm = np.float32(1.0 / math.sqrt(seq * GROUP))
    items = [(s, g) for s in range(ROW_TILE // seq) for g in range(N_GROUPS)]
    rows = lambda s: slice(s * seq, (s + 1) * seq)
    cols = lambda g: slice(g * GROUP, (g + 1) * GROUP)

    def norm_seq(s):
        h_scr[rows(s), :] = _norm_mod(x_ref[rows(s), :], nw_ref[...], mod_ref[...], 0).astype(BF16)

    def time_dft(s, g):
        return _dot(et_ref[...], h_scr[rows(s), cols(g)]).astype(BF16)

    norm_seq(0)
    wb_scr[...] = w_ref[...].astype(BF16)
    y = time_dft(*items[0])
    for i, (s, g) in enumerate(items):
        y_next = None
        if i + 1 < len(items):
            s2, g2 = items[i + 1]
            if s2 != s:
                norm_seq(s2)
            y_next = time_dft(s2, g2)
        f = _dot(y[:seq], ec_ref[:GROUP, :]) + _dot(y[seq:], ec_ref[GROUP:, :])
        f_scr[rows(s), cols(g)] = (f * norm).astype(BF16)
        y = y_next
    for r in range(ROW_TILE // ROW_BLOCK):
        blk = slice(r * ROW_BLOCK, (r + 1) * ROW_BLOCK)
        o_ref[blk, :] = x_ref[blk, :] + mod_ref[2:3, :] * _dot(f_scr[blk, :], wb_scr[...])


def _fourier_call(x, mods, li, cond_of_row, seq, P):
    m = x.shape[0]
    lj = li // 4
    e_t, e_c = _dft_tables(seq)
    return pl.pallas_call(
        functools.partial(_fourier_kernel, seq=seq),
        grid=(m // ROW_TILE,),
        in_specs=[
            pl.BlockSpec((ROW_TILE, D_MODEL), lambda i: (i, 0)),
            _mod_spec(li, cond_of_row, ROW_TILE, 1),
            pl.BlockSpec((None, 1, D_MODEL), lambda i: (li, 0, 0)),
            pl.BlockSpec((2 * seq, seq), lambda i: (0, 0)),
            pl.BlockSpec((2 * GROUP, GROUP), lambda i: (0, 0)),
            pl.BlockSpec((None, D_MODEL, D_MODEL), lambda i: (lj, 0, 0)),
        ],
        out_specs=pl.BlockSpec((ROW_TILE, D_MODEL), lambda i: (i, 0)),
        out_shape=jax.ShapeDtypeStruct((m, D_MODEL), F32),
        scratch_shapes=[pltpu.VMEM((ROW_TILE, D_MODEL), BF16), pltpu.VMEM((ROW_TILE, D_MODEL), BF16),
                        pltpu.VMEM((D_MODEL, D_MODEL), BF16)],
        compiler_params=_params(("arbitrary",)),
        name=f"fourier_t{seq}",
    )(x, mods, P["norm1_w"], e_t, e_c, P["fnet_w"])


def _rope_tables(rows):
    r, col = jnp.meshgrid(jnp.arange(rows), jnp.arange(GRID_W), indexing="ij")
    r = r.reshape(-1).astype(F32)
    col = col.reshape(-1).astype(F32)
    n_freq = QK_ROPE // 4
    inv = ROPE_THETA ** (-jnp.arange(n_freq, dtype=F32) / n_freq)
    ang = jnp.concatenate([r[:, None] * inv, col[:, None] * inv], axis=-1)
    cos = jnp.repeat(jnp.cos(ang), 2, axis=-1)
    sin = jnp.repeat(jnp.sin(ang), 2, axis=-1)
    t = cos.shape[0]
    pad = LANES - QK_NOPE - QK_ROPE
    cos_p = jnp.concatenate([jnp.ones((t, QK_NOPE), F32), cos, jnp.ones((t, pad), F32)], axis=-1)
    sin_p = jnp.concatenate([jnp.zeros((t, QK_NOPE), F32), sin, jnp.zeros((t, pad), F32)], axis=-1)
    return cos_p, sin_p


def _pair_swap(w):
    wr = w.reshape(w.shape[:-1] + (w.shape[-1] // 2, 2))
    return jnp.stack([-wr[..., 1], wr[..., 0]], axis=-1).reshape(w.shape)


def _mla_weights(P, j, rope, seq_rows):
    pad = LANES - QK_NOPE - QK_ROPE
    wuq = P["mla_wuq"][j].reshape(Q_LORA, N_HEADS, QK_NOPE + QK_ROPE)
    q_slots = jnp.pad(wuq, ((0, 0), (0, 0), (0, pad))).reshape(Q_LORA, N_HEADS * LANES)
    wdkv = P["mla_wdkv"][j]
    kr_slot = jnp.pad(wdkv[:, KV_LORA:], ((0, 0), (QK_NOPE, pad)))
    parts1 = [P["mla_wdq"][j], wdkv[:, :KV_LORA], kr_slot]
    parts2 = [q_slots]
    if rope:
        parts1.append(jnp.pad(_pair_swap(wdkv[:, KV_LORA:]), ((0, 0), (QK_NOPE, pad))))
        q_sw = jnp.pad(_pair_swap(wuq[:, :, QK_NOPE:]), ((0, 0), (0, 0), (QK_NOPE, pad)))
        parts2.append(q_sw.reshape(Q_LORA, N_HEADS * LANES))
        cos, sin = _rope_tables(seq_rows // GRID_W)
    else:
        cos = sin = jnp.zeros((PROJ_TILE, LANES), F32)
    wukv = P["mla_wukv"][j].reshape(KV_LORA, N_HEADS, QK_NOPE + V_HEAD)
    wk = jnp.pad(wukv[:, :, :QK_NOPE], ((0, 0), (0, 0), (0, LANES - QK_NOPE)))
    return dict(
        norm1_w=P["norm1_w"], q_norm=P["mla_q_norm"][j][None, :], kv_norm=P["mla_kv_norm"][j][None, :],
        w1=jnp.concatenate(parts1, axis=1).astype(BF16), w2=jnp.concatenate(parts2, axis=1).astype(BF16),
        wk=wk.reshape(KV_LORA, N_HEADS * LANES).astype(BF16),
        wv=wukv[:, :, QK_NOPE:].reshape(KV_LORA, N_HEADS * V_HEAD).astype(BF16), cos=cos, sin=sin)


def _mla_layer(x, mods, li, cond_of_row, batch, seq, P, cache):
    j = li // 4
    W = _mla_weights(P, j, cache is not None, seq)
    q, k, v, ckv, krope = _mla_proj_call(x, mods, li, cond_of_row, W, cache is not None)
    q = q.reshape(N_HEADS, batch, seq, LANES)
    k = k.reshape(N_HEADS, batch, seq, LANES)
    v = v.reshape(HEAD_PAIRS, batch, seq, LANES)
    if cache is None:
        o = _attn_call(q, [(k, v)], seq, HEAD_PAIRS)
    else:
        c_ckv, c_krope = cache
        past = c_ckv.shape[1]
        pad = LANES - QK_NOPE - QK_ROPE
        kr_pad = jnp.pad(c_krope.reshape(batch * past, QK_ROPE), ((0, 0), (QK_NOPE, pad)))
        kc, vc = _kv_cache_call(c_ckv.reshape(batch * past, KV_LORA), kr_pad, W)
        cached = (kc.reshape(N_HEADS, batch, past, LANES), vc.reshape(HEAD_PAIRS, batch, past, LANES))
        o = _attn_call(q, [cached, (k, v)], 512, 4)
    return o, ckv, krope


def _mixer(x, mods, li, cond_of_row, batch, seq, P, cache):
    kind = li % 4
    if kind == 0:
        o, ckv, krope = _mla_layer(x, mods, li, cond_of_row, batch, seq, P, cache)
        return x, (o, P["mla_wo_bf16"], li // 4), (ckv, krope)
    if kind == 1:
        return _pool_call(x, mods, li, cond_of_row, seq, P), None, None
    if kind == 2:
        return _fourier_call(x, mods, li, cond_of_row, seq, P), None, None
    return _sconv_call(x, mods, li, cond_of_row, seq, P), None, None


def _run_passes(passes, mods, P):
    new_cache = None
    for li in range(DEPTH):
        pending = []
        for ps in passes:
            ps["x"], pre, made = _mixer(ps["x"], mods, li, ps["cond_of_row"], ps["batch"], ps["seq"], P,
                                        ps["cache"])
            pending.append(pre)
            if made is not None and ps["cache"] is None:
                new_cache = made
        xs, outs = lax.optimization_barrier(([ps["x"] for ps in passes],
                                              [None if pre is None else pre[0] for pre in pending]))
        pending = [None if pre is None else (o,) + pre[1:] for pre, o in zip(pending, outs)]
        xs = [_ffn_call(x, mods, li, ps["cond_of_row"], ps["seq"], P, pre, final_norm=(li == DEPTH - 1))
              for x, ps, pre in zip(xs, passes, pending)]
        xs = lax.optimization_barrier(xs)
        for ps, x in zip(passes, xs):
            ps["x"] = x
    return new_cache


def kernel(x_prompt, x_sample, cache_ckv, cache_krope, c, c_ctx, norm1_w, norm2_w, ada_w, ada_b, mla_wdq, mla_q_norm, mla_wuq, mla_wdkv, mla_kv_norm, mla_wukv, mla_wo, pool_w, pool_scale, fnet_w, sconv_win, sconv_conv, sconv_wout, ffn_up, ffn_conv_w, ffn_conv_b, ffn_down, final_norm_w):
    row = lambda w: w.reshape(w.shape[0], 1, w.shape[1])
    P = dict(norm1_w=row(norm1_w), norm2_w=row(norm2_w), mla_wdq=mla_wdq, mla_q_norm=mla_q_norm,
             mla_wuq=mla_wuq, mla_wdkv=mla_wdkv, mla_kv_norm=mla_kv_norm, mla_wukv=mla_wukv,
             mla_wo_bf16=mla_wo.astype(BF16),
             pool_w=pool_w, pool_scale=row(pool_scale), fnet_w=fnet_w, sconv_win=sconv_win,
             sconv_conv=sconv_conv, sconv_wout=sconv_wout, ffn_conv_w=ffn_conv_w,
             ffn_conv_b=ffn_conv_b, ffn_down=ffn_down, final_norm_w=final_norm_w,
             ffn_w=functools.cache(lambda l: _layer_weights(ffn_up, ffn_down, l, 2, f"ffn{l}_weights")),
             sconv_w=functools.cache(lambda l: _layer_weights(sconv_win, sconv_wout, l, 3, f"sconv{l}_weights")))
    b_ctx, t_ctx, _ = x_prompt.shape
    b_lat, t_lat, _ = x_sample.shape
    assert ROW_TILE % t_ctx == 0 and t_lat % ROW_TILE == 0 and b_lat + 1 <= N_COND

    conds = jnp.concatenate([c_ctx[None, :], c, jnp.zeros((N_COND - 1 - b_lat, D_MODEL), F32)], axis=0)
    mods = _ada_call(conds, ada_w, ada_b).reshape(DEPTH, N_COND, 6, D_MODEL)

    ctx = dict(x=x_prompt.reshape(b_ctx * t_ctx, D_MODEL), cond_of_row=lambda r: 0, batch=b_ctx, seq=t_ctx,
               cache=None)
    lat = dict(x=x_sample.reshape(b_lat * t_lat, D_MODEL), cond_of_row=lambda r: 1 + r // t_lat, batch=b_lat,
               seq=t_lat, cache=(cache_ckv[:, 0], cache_krope[:, 0]))
    ckv, krope = _run_passes([ctx, lat], mods, P)
    return (ctx["x"].reshape(b_ctx, t_ctx, D_MODEL), lat["x"].reshape(b_lat, t_lat, D_MODEL),
            ckv.reshape(b_ctx, 1, t_ctx, KV_LORA), krope.reshape(b_ctx, 1, t_ctx, QK_ROPE))
```

```python
import functools
import math

import numpy as np
import jax
import jax.numpy as jnp
from jax import lax
from jax.experimental import pallas as pl
from jax.experimental.pallas import tpu as pltpu

F32 = jnp.float32
BF16 = jnp.bfloat16

D_MODEL = 1024
N_HEADS = 16
QK_NOPE = 64
QK_ROPE = 32
V_HEAD = 64
Q_LORA = 384
KV_LORA = 256
ROPE_THETA = 10000.0
GRID_W = 64
POOL_WINDOWS = (2, 4, 8, 16)
N_GROUPS = 4
GROUP = D_MODEL // N_GROUPS
D_FF = 2816
EPS = 1e-6
DEPTH = 4

LANES = 128
Q_SCALE = np.float32(math.log2(math.e) / math.sqrt(QK_NOPE + QK_ROPE))
HEAD_PAIRS = N_HEADS // 2
N_COND = 8
ROW_TILE = 1024
PROJ_TILE = 512
FF_TILE = 256
ROW_BLOCK = 256
HALO = 8
RELAYOUT_ROWS = 128
CHUNK_SLOTS = 16
ADA_TILE = 1536
MIB = 1024 * 1024
VMEM_DEFAULT = 48 * MIB
VMEM_HEADROOM = 4 * MIB + 512 * 1024
ATTN_VMEM = 48 * MIB


def _params(sem, vmem_bytes=VMEM_DEFAULT):
    return pltpu.CompilerParams(dimension_semantics=sem, vmem_limit_bytes=int(vmem_bytes))


def _nbytes(a):
    return math.prod(a.shape) * jnp.dtype(a.dtype).itemsize


def _dot(a, b):
    return jnp.dot(a, b, preferred_element_type=F32)


def _rms(x, w):
    return x * lax.rsqrt(jnp.mean(x * x, axis=-1, keepdims=True) + EPS) * w


def _norm_mod(x, nw, mod, base):
    return _rms(x, nw) * (1.0 + mod[base + 1:base + 2, :]) + mod[base:base + 1, :]


def _gelu_exact(x):
    return 0.5 * x * (1.0 + lax.erf(x * np.float32(math.sqrt(0.5))))


def _ada_kernel(c_ref, w_ref, b_ref, o_ref):
    s = jax.nn.silu(c_ref[...]).astype(BF16)
    o_ref[...] = _dot(s, w_ref[...].astype(BF16)) + b_ref[...]


def _ada_call(conds, ada_w, ada_b):
    n_out = ada_w.shape[-1]
    return pl.pallas_call(
        _ada_kernel,
        grid=(DEPTH, n_out // ADA_TILE),
        in_specs=[
            pl.BlockSpec((N_COND, D_MODEL), lambda l, j: (0, 0)),
            pl.BlockSpec((None, D_MODEL, ADA_TILE), lambda l, j: (l, 0, j)),
            pl.BlockSpec((None, 1, ADA_TILE), lambda l, j: (l, 0, j)),
        ],
        out_specs=pl.BlockSpec((None, N_COND, ADA_TILE), lambda l, j: (l, 0, j)),
        out_shape=jax.ShapeDtypeStruct((DEPTH, N_COND, n_out), F32),
        compiler_params=_params(("arbitrary", "arbitrary"), 2 * D_MODEL * ADA_TILE * 4 + VMEM_HEADROOM),
        name="ada",
    )(conds, ada_w, ada_b.reshape(DEPTH, 1, n_out))


def _mod_spec(li, cond_of_row, tile, n_grid):
    if n_grid == 1:
        return pl.BlockSpec((None, None, 6, D_MODEL), lambda i: (li, cond_of_row(i * tile), 0, 0))
    return pl.BlockSpec((None, None, 6, D_MODEL), lambda i, j: (li, cond_of_row(i * tile), 0, 0))


def _conv_window(w, cw, t0, seq):
    n = w.shape[0]
    mid = slice(HALO, HALO + ROW_BLOCK)
    t = (t0 + lax.broadcasted_iota(jnp.int32, (ROW_BLOCK, 1), 0)) & (seq - 1)
    prev = jnp.where(t == 0, 0.0, pltpu.roll(w, 1, axis=0)[mid])
    nxt = jnp.where(t == seq - 1, 0.0, pltpu.roll(w, n - 1, axis=0)[mid])
    return prev * cw[0:1, :] + w[mid] * cw[1:2, :] + nxt * cw[2:3, :]


def _ffn_act(win, mid, cw, cb, t0, seq):
    return _gelu_exact(_conv_window(win(0), cw, t0, seq) + cb[0:1, :]) * mid(1)


def _sconv_act(win, mid, cw, cb, t0, seq):
    return mid(0) * _conv_window(win(1) * win(2), cw, t0, seq)


def _ffn_kernel(x_ref, mod_ref, nw_ref, up_ref, cw_ref, cb_ref, down_ref, fnw_ref, o_ref, *scratch, **cfg):
    _glu_body(x_ref, mod_ref, nw_ref, up_ref, cw_ref, cb_ref, down_ref, fnw_ref, None, o_ref, *scratch,
              n_up=2, act=_ffn_act, mod_base=3, **cfg)


def _ffn_pre_kernel(x_ref, mod_ref, nw_ref, up_ref, cw_ref, cb_ref, down_ref, fnw_ref, a_ref, wa_ref, o_ref,
                    *scratch, **cfg):
    _glu_body(x_ref, mod_ref, nw_ref, up_ref, cw_ref, cb_ref, down_ref, fnw_ref, (a_ref, wa_ref), o_ref,
              *scratch, n_up=2, act=_ffn_act, mod_base=3, **cfg)


def _sconv_kernel(x_ref, mod_ref, nw_ref, up_ref, cw_ref, down_ref, o_ref, *scratch, **cfg):
    _glu_body(x_ref, mod_ref, nw_ref, up_ref, cw_ref, None, down_ref, None, None, o_ref, *scratch,
              n_up=3, act=_sconv_act, mod_base=0, final_norm=False, **cfg)


def _glu_body(x_ref, mod_ref, nw_ref, up_ref, cw_ref, cb_ref, down_ref, fnw_ref, pre, o_ref,
              h_scr, acc_scr, gu0, gu1, *, n_up, n_chunks, act, seq, mod_base, final_norm):
    gu = (gu0, gu1)
    n_blocks = ROW_TILE // ROW_BLOCK

    def up(c, slot, r):
        rows = slice(r * ROW_BLOCK, (r + 1) * ROW_BLOCK)
        res = _dot(h_scr[rows, :], up_ref[c])
        for p in range(n_up):
            gu[slot][p, HALO + r * ROW_BLOCK:HALO + (r + 1) * ROW_BLOCK, :] = res[:, p * FF_TILE:(p + 1) * FF_TILE]

    def activate(c, slot, r):
        win = lambda p: gu[slot][p, r * ROW_BLOCK:(r + 1) * ROW_BLOCK + 2 * HALO, :]
        mid = lambda p: gu[slot][p, HALO + r * ROW_BLOCK:HALO + (r + 1) * ROW_BLOCK, :]
        cb = None if cb_ref is None else cb_ref[c]
        return act(win, mid, cw_ref[c], cb, r * ROW_BLOCK, seq).astype(BF16)

    def down(a, c, r):
        rows = slice(r * ROW_BLOCK, (r + 1) * ROW_BLOCK)
        y = _dot(a, down_ref[c])
        if isinstance(c, int) and c == 0:
            acc_scr[rows, :] = y
        else:
            acc_scr[rows, :] += y

    def stage(s, slot):
        for r in range(n_blocks):
            a = activate(s - 1, 1 - slot, r)
            up(s, slot, r)
            down(a, s - 1, r)

    def block(r):
        return slice(r * ROW_BLOCK, (r + 1) * ROW_BLOCK)

    def mixer_proj(r):
        a_ref, wa_ref = pre
        return _dot(a_ref[block(r), :], wa_ref[...])

    for g in gu:
        g[:, :HALO, :] = jnp.zeros((n_up, HALO, FF_TILE), F32)
        g[:, HALO + ROW_TILE:, :] = jnp.zeros((n_up, HALO, FF_TILE), F32)

    proj = mixer_proj(0) if pre is not None else None
    for r in range(n_blocks):
        x = x_ref[block(r), :]
        if pre is not None:
            x = x + mod_ref[2:3, :] * proj
            o_ref[block(r), :] = x
            if r + 1 < n_blocks:
                proj = mixer_proj(r + 1)
        h_scr[block(r), :] = _norm_mod(x, nw_ref[0:1, :], mod_ref[...], mod_base).astype(BF16)
        up(0, 0, r)

    if n_chunks > 1:
        stage(1, 1)
    n_pairs = (n_chunks - 2) // 2

    def pair(i, carry):
        s = 2 * i + 2
        stage(s, 0)
        stage(s + 1, 1)
        return carry

    lax.fori_loop(0, n_pairs, pair, 0)
    for s in range(2 * n_pairs + 2, n_chunks):
        stage(s, s % 2)
    last = n_chunks - 1
    res = x_ref if pre is None else o_ref
    for r in range(n_blocks):
        down(activate(last, last % 2, r), last, r)
        y = res[block(r), :] + mod_ref[mod_base + 2:mod_base + 3, :] * acc_scr[block(r), :]
        if final_norm:
            y = _rms(y, fnw_ref[0:1, :])
        o_ref[block(r), :] = y


def _chunked(w, n_chunks):
    w = w.reshape(w.shape[0], n_chunks, FF_TILE).transpose(1, 0, 2)
    return jnp.pad(w, ((0, CHUNK_SLOTS - n_chunks), (0, 8 - w.shape[1]), (0, 0)))


def _glu_call(kernel_fn, name, x, mods, li, cond_of_row, n_up, n_chunks, norm_w, operands, specs, cfg):
    m = x.shape[0]
    whole = lambda a: pl.BlockSpec(a.shape, lambda i: (0,) * a.ndim)
    in_specs = [pl.BlockSpec((ROW_TILE, D_MODEL), lambda i: (i, 0)),
                _mod_spec(li, cond_of_row, ROW_TILE, 1),
                pl.BlockSpec((None, 8, D_MODEL), lambda i: (li, 0, 0))]
    layer_w = lambda a, l: pl.BlockSpec((None,) + a.shape[1:], lambda i: (l,) + (0,) * (a.ndim - 1),
                                        pipeline_mode=pl.Buffered(1))
    rows = lambda a: pl.BlockSpec((ROW_TILE, a.shape[1]), lambda i: (i, 0))
    spec_of = {None: whole, "rows": rows, "resident": lambda a: pl.BlockSpec(memory_space=pltpu.VMEM)}
    in_specs += [layer_w(a, s) if isinstance(s, int) else spec_of[s](a) for a, s in zip(operands, specs)]
    gu_shape = (n_up, ROW_TILE + 2 * HALO, FF_TILE)
    tile = ROW_TILE * D_MODEL
    window = {None: _nbytes, "rows": lambda a: 2 * ROW_TILE * a.shape[1] * a.dtype.itemsize,
              "resident": lambda a: 0}
    vmem = (2 * 2 * tile * 4 + tile * 2 + tile * 4 + 2 * math.prod(gu_shape) * 4 + VMEM_HEADROOM
            + sum(_nbytes(a) // a.shape[0] if isinstance(s, int) else window[s](a)
                  for a, s in zip(operands, specs)))
    return pl.pallas_call(
        functools.partial(kernel_fn, n_chunks=n_chunks, **cfg),
        grid=(m // ROW_TILE,),
        in_specs=in_specs,
        out_specs=pl.BlockSpec((ROW_TILE, D_MODEL), lambda i: (i, 0)),
        out_shape=jax.ShapeDtypeStruct((m, D_MODEL), F32),
        scratch_shapes=[
            pltpu.VMEM((ROW_TILE, D_MODEL), BF16),
            pltpu.VMEM((ROW_TILE, D_MODEL), F32),
            pltpu.VMEM(gu_shape, F32), pltpu.VMEM(gu_shape, F32),
        ],
        compiler_params=_params(("arbitrary",), vmem),
        name=name,
    )(x, mods, norm_w, *operands)


def _in_vmem(a):
    return pltpu.with_memory_space_constraint(a, pltpu.VMEM)


def _layer_weights_kernel(up_ref, dn_ref, uo_ref, do_ref, *, n_up, n_chunks):
    for c in range(n_chunks):
        for p in range(n_up):
            col = (p * n_chunks + c) * FF_TILE
            uo_ref[c, :, p * FF_TILE:(p + 1) * FF_TILE] = up_ref[:, col:col + FF_TILE].astype(BF16)
    do_ref[...] = dn_ref[...].astype(BF16)


def _layer_weights(up, down, layer, n_up, name):
    _, d, cols = up.shape
    _, hidden, d_out = down.shape
    n_chunks = hidden // FF_TILE
    steps = d // RELAYOUT_ROWS
    dn_rows = hidden // steps
    up_cm, dn = pl.pallas_call(
        functools.partial(_layer_weights_kernel, n_up=n_up, n_chunks=n_chunks),
        grid=(steps,),
        in_specs=[pl.BlockSpec((None, RELAYOUT_ROWS, cols), lambda k: (layer, k, 0)),
                  pl.BlockSpec((None, dn_rows, d_out), lambda k: (layer, k, 0))],
        out_specs=[pl.BlockSpec((n_chunks, RELAYOUT_ROWS, n_up * FF_TILE), lambda k: (0, k, 0)),
                   pl.BlockSpec((dn_rows, d_out), lambda k: (k, 0))],
        out_shape=[jax.ShapeDtypeStruct((n_chunks, d, n_up * FF_TILE), BF16),
                   jax.ShapeDtypeStruct((hidden, d_out), BF16)],
        compiler_params=_params(("arbitrary",),
                                2 * (RELAYOUT_ROWS * cols + dn_rows * d_out) * (4 + 2) + VMEM_HEADROOM),
        name=name,
    )(up, down)
    return _in_vmem(up_cm), _in_vmem(dn.reshape(n_chunks, FF_TILE, d_out))


def _ffn_call(x, mods, li, cond_of_row, seq, P, pre, final_norm):
    n = D_FF // FF_TILE
    up_cm, down_cm = P["ffn_w"](li)
    operands = [up_cm, _chunked(P["ffn_conv_w"][li], n), _chunked(P["ffn_conv_b"][li][None, :], n),
                down_cm, jnp.pad(P["final_norm_w"].reshape(1, D_MODEL), ((0, 7), (0, 0)))]
    specs = ["resident", None, None, "resident", None]
    kernel_fn = _ffn_kernel
    if pre is not None:
        a, w_a, l_a = pre
        operands += [a, w_a]
        specs += ["rows", l_a]
        kernel_fn = _ffn_pre_kernel
    return _glu_call(kernel_fn, f"ffn{li}", x, mods, li, cond_of_row, 2, n, P["norm2_w"],
                     operands, specs, dict(seq=seq, final_norm=final_norm))


def _sconv_call(x, mods, li, cond_of_row, seq, P):
    lj = li // 4
    n = D_MODEL // FF_TILE
    up_cm, down_cm = P["sconv_w"](lj)
    operands = [up_cm, _chunked(P["sconv_conv"][lj], n), down_cm]
    return _glu_call(_sconv_kernel, f"sconv{li}", x, mods, li, cond_of_row, 3, n, P["norm1_w"],
                     operands, ["resident", None, "resident"], dict(seq=seq))


def _emit_kv(ckv, krope_pad, wk_ref, wv_ref, k_ref, v_ref):
    c = ckv.astype(BF16)
    kx = _dot(c, wk_ref[...])
    vx = _dot(c, wv_ref[...])
    for h in range(N_HEADS):
        k_ref[h] = (kx[:, h * LANES:(h + 1) * LANES] + krope_pad).astype(BF16)
    for p in range(HEAD_PAIRS):
        v_ref[p] = vx[:, p * LANES:(p + 1) * LANES].astype(BF16)


def _mla_proj_kernel(x_ref, mod_ref, nw_ref, w1_ref, qn_ref, kvn_ref, w2_ref, wk_ref, wv_ref, cos_ref,
                     sin_ref, q_ref, k_ref, v_ref, ckv_ref, kr_ref, *, rope):
    h = _norm_mod(x_ref[...], nw_ref[0:1, :], mod_ref[...], 0).astype(BF16)
    z = _dot(h, w1_ref[...])
    cq = _rms(z[:, :Q_LORA], qn_ref[...]).astype(BF16)
    ckv = _rms(z[:, Q_LORA:Q_LORA + KV_LORA], kvn_ref[...])
    kr = z[:, Q_LORA + KV_LORA:Q_LORA + KV_LORA + LANES]
    if rope:
        cos, sin = cos_ref[...], sin_ref[...]
        kr = kr * cos + z[:, Q_LORA + KV_LORA + LANES:] * sin
    ckv_ref[...] = ckv
    kr_ref[...] = kr[:, QK_NOPE:QK_NOPE + QK_ROPE]
    c = ckv.astype(BF16)
    off = N_HEADS * LANES

    def pair_matmuls(p):
        cols = slice(2 * p * LANES, (2 * p + 2) * LANES)
        q2 = _dot(cq, w2_ref[:, cols])
        qs = _dot(cq, w2_ref[:, off + 2 * p * LANES:off + (2 * p + 2) * LANES]) if rope else None
        k2 = _dot(c, wk_ref[:, cols])
        v2 = _dot(c, wv_ref[:, p * LANES:(p + 2) * LANES]) if p % 2 == 0 else None
        return q2, qs, k2, v2

    nxt = pair_matmuls(0)
    for p in range(HEAD_PAIRS):
        q2, qs, k2, v2 = nxt
        if p + 1 < HEAD_PAIRS:
            nxt = pair_matmuls(p + 1)
        for e in range(2):
            lanes = slice(e * LANES, (e + 1) * LANES)
            qh = q2[:, lanes]
            if rope:
                qh = qh * cos + qs[:, lanes] * sin
            q_ref[2 * p + e] = (qh * Q_SCALE).astype(BF16)
            k_ref[2 * p + e] = (k2[:, lanes] + kr).astype(BF16)
        if v2 is not None:
            v_ref[p] = v2[:, :LANES].astype(BF16)
            v_ref[p + 1] = v2[:, LANES:].astype(BF16)


def _mla_proj_call(x, mods, li, cond_of_row, W, rope):
    m = x.shape[0]
    n1 = W["w1"].shape[1]
    n2 = W["w2"].shape[1]
    const = lambda i: (0, 0)
    pos_tiles = W["cos"].shape[0] // PROJ_TILE
    pos = lambda i: (i % pos_tiles, 0)
    return pl.pallas_call(
        functools.partial(_mla_proj_kernel, rope=rope),
        grid=(m // PROJ_TILE,),
        in_specs=[
            pl.BlockSpec((PROJ_TILE, D_MODEL), lambda i: (i, 0)),
            _mod_spec(li, cond_of_row, PROJ_TILE, 1),
            pl.BlockSpec((None, 8, D_MODEL), lambda i: (li, 0, 0)),
            pl.BlockSpec((D_MODEL, n1), const),
            pl.BlockSpec((1, Q_LORA), const),
            pl.BlockSpec((1, KV_LORA), const),
            pl.BlockSpec((Q_LORA, n2), const),
            pl.BlockSpec((KV_LORA, N_HEADS * LANES), const),
            pl.BlockSpec((KV_LORA, HEAD_PAIRS * LANES), const),
            pl.BlockSpec((PROJ_TILE, LANES), pos),
            pl.BlockSpec((PROJ_TILE, LANES), pos),
        ],
        out_specs=[
            pl.BlockSpec((N_HEADS, PROJ_TILE, LANES), lambda i: (0, i, 0)),
            pl.BlockSpec((N_HEADS, PROJ_TILE, LANES), lambda i: (0, i, 0)),
            pl.BlockSpec((HEAD_PAIRS, PROJ_TILE, LANES), lambda i: (0, i, 0)),
            pl.BlockSpec((PROJ_TILE, KV_LORA), lambda i: (i, 0)),
            pl.BlockSpec((PROJ_TILE, QK_ROPE), lambda i: (i, 0)),
        ],
        out_shape=[
            jax.ShapeDtypeStruct((N_HEADS, m, LANES), BF16),
            jax.ShapeDtypeStruct((N_HEADS, m, LANES), BF16),
            jax.ShapeDtypeStruct((HEAD_PAIRS, m, LANES), BF16),
            jax.ShapeDtypeStruct((m, KV_LORA), F32),
            jax.ShapeDtypeStruct((m, QK_ROPE), F32),
        ],
        compiler_params=_params(("arbitrary",)),
        name="mla_proj_rope" if rope else "mla_proj",
    )(x, mods, W["norm1_w"], W["w1"], W["q_norm"], W["kv_norm"], W["w2"], W["wk"], W["wv"],
      W["cos"], W["sin"])


def _kv_cache_kernel(ckv_ref, kr_ref, wk_ref, wv_ref, k_ref, v_ref):
    _emit_kv(ckv_ref[...], kr_ref[...], wk_ref, wv_ref, k_ref, v_ref)


def _kv_cache_call(ckv, krope_pad, W):
    m = ckv.shape[0]
    const = lambda i: (0, 0)
    return pl.pallas_call(
        _kv_cache_kernel,
        grid=(1,),
        in_specs=[
            pl.BlockSpec((m, KV_LORA), const),
            pl.BlockSpec((m, LANES), const),
            pl.BlockSpec((KV_LORA, N_HEADS * LANES), const),
            pl.BlockSpec((KV_LORA, HEAD_PAIRS * LANES), const),
        ],
        out_specs=[
            pl.BlockSpec((N_HEADS, m, LANES), lambda i: (0, 0, 0)),
            pl.BlockSpec((HEAD_PAIRS, m, LANES), lambda i: (0, 0, 0)),
        ],
        out_shape=[
            jax.ShapeDtypeStruct((N_HEADS, m, LANES), BF16),
            jax.ShapeDtypeStruct((HEAD_PAIRS, m, LANES), BF16),
        ],
        compiler_params=_params(("arbitrary",)),
        name="kv_cache",
    )(ckv, krope_pad, W["wk"], W["wv"])


def _attn_kernel(q_ref, *refs, n_pairs):
    o_ref, kv = refs[-1], refs[:-1]
    parts = [(kv[i], kv[i + 1]) for i in range(0, len(kv), 2)]
    heads = [(p, e) for p in range(n_pairs) for e in range(2)]

    def scores(p, e):
        q = q_ref[2 * p + e]
        return [lax.dot_general(q, k_ref[2 * p + e], (((1,), (1,)), ((), ())), preferred_element_type=F32)
                for k_ref, _ in parts]

    outs = []
    s_next = scores(*heads[0])
    for i, (p, e) in enumerate(heads):
        s = s_next
        if i + 1 < len(heads):
            s_next = scores(*heads[i + 1])
        m = functools.reduce(jnp.maximum, [jnp.max(x, axis=-1, keepdims=True) for x in s])
        w = [jnp.exp2(x - m) for x in s]
        l = sum(jnp.sum(x, axis=-1, keepdims=True) for x in w)
        pv = sum(_dot(x.astype(BF16), v_ref[p]) for x, (_, v_ref) in zip(w, parts))
        outs.append(pv / l)
        if e == 1:
            lane = lax.broadcasted_iota(jnp.int32, pv.shape, 1)
            o_ref[:, p * LANES:(p + 1) * LANES] = jnp.where(lane < V_HEAD, outs[-2], outs[-1]).astype(BF16)


def _attn_call(q, kv_parts, q_tile, n_pairs):
    _, b, t, _ = q.shape
    nq = t // q_tile
    in_specs = [pl.BlockSpec((2 * n_pairs, None, q_tile, LANES), lambda bi, g, qi: (g, bi, qi, 0))]
    operands = [q]
    for k, v in kv_parts:
        s = k.shape[2]
        in_specs += [pl.BlockSpec((2 * n_pairs, None, s, LANES), lambda bi, g, qi: (g, bi, 0, 0)),
                     pl.BlockSpec((n_pairs, None, s, LANES), lambda bi, g, qi: (g, bi, 0, 0))]
        operands += [k, v]
    return pl.pallas_call(
        functools.partial(_attn_kernel, n_pairs=n_pairs),
        grid=(b, HEAD_PAIRS // n_pairs, nq),
        in_specs=in_specs,
        out_specs=pl.BlockSpec((q_tile, n_pairs * LANES), lambda bi, g, qi: (bi * nq + qi, g)),
        out_shape=jax.ShapeDtypeStruct((b * t, N_HEADS * V_HEAD), BF16),
        compiler_params=_params(("arbitrary", "arbitrary", "arbitrary"), ATTN_VMEM),
        name=f"attn_{len(kv_parts)}part",
    )(*operands)


def _pool_kernel(x_ref, mod_ref, nw_ref, w_ref, sc_ref, o_ref, band_scr, h_scr, hi_scr, lo_scr, y_scr, *, seq):
    @pl.when(pl.program_id(0) == 0)
    def _():
        r = lax.broadcasted_iota(jnp.int32, (seq, seq), 0)
        c = lax.broadcasted_iota(jnp.int32, (seq, seq), 1)
        for g, win in enumerate(POOL_WINDOWS):
            inside = (c >= r - win // 2) & (c < r + win - win // 2)
            band_scr[g] = jnp.where(inside, 1.0, 0.0).astype(BF16)

    items = [(s, g) for s in range(ROW_TILE // seq) for g in range(N_GROUPS)]
    rows = lambda s: slice(s * seq, (s + 1) * seq)
    cols = lambda g: slice(g * GROUP, (g + 1) * GROUP)
    t = lax.broadcasted_iota(jnp.int32, (seq, 1), 0)
    counts = [(jnp.minimum(t + (win - win // 2), seq) - jnp.maximum(t - win // 2, 0)).astype(F32)
              for win in POOL_WINDOWS]

    def norm_seq(s):
        h = _norm_mod(x_ref[rows(s), :], nw_ref[0:1, :], mod_ref[...], 0)
        hi = h.astype(BF16)
        h_scr[rows(s), :] = h
        hi_scr[rows(s), :] = hi
        lo_scr[rows(s), :] = (h - hi.astype(F32)).astype(BF16)

    def window_sum(s, g):
        return _dot(band_scr[g], hi_scr[rows(s), cols(g)]) + _dot(band_scr[g], lo_scr[rows(s), cols(g)])

    norm_seq(0)
    tot = window_sum(*items[0])
    for i, (s, g) in enumerate(items):
        tot_next = None
        if i + 1 < len(items):
            s2, g2 = items[i + 1]
            if s2 != s:
                norm_seq(s2)
            tot_next = window_sum(s2, g2)
        pooled = (tot / counts[g] - h_scr[rows(s), cols(g)]).astype(BF16)
        y_scr[rows(s), cols(g)] = _dot(pooled, w_ref[g].astype(BF16))
        tot = tot_next
    for r in range(ROW_TILE // ROW_BLOCK):
        blk = slice(r * ROW_BLOCK, (r + 1) * ROW_BLOCK)
        o_ref[blk, :] = x_ref[blk, :] + mod_ref[2:3, :] * (y_scr[blk, :] * sc_ref[0:1, :])


def _pool_call(x, mods, li, cond_of_row, seq, P):
    m = x.shape[0]
    lj = li // 4
    return pl.pallas_call(
        functools.partial(_pool_kernel, seq=seq),
        grid=(m // ROW_TILE,),
        in_specs=[
            pl.BlockSpec((ROW_TILE, D_MODEL), lambda i: (i, 0)),
            _mod_spec(li, cond_of_row, ROW_TILE, 1),
            pl.BlockSpec((None, 8, D_MODEL), lambda i: (li, 0, 0)),
            pl.BlockSpec((None, N_GROUPS, GROUP, GROUP), lambda i: (lj, 0, 0, 0)),
            pl.BlockSpec((None, 8, D_MODEL), lambda i: (lj, 0, 0)),
        ],
        out_specs=pl.BlockSpec((ROW_TILE, D_MODEL), lambda i: (i, 0)),
        out_shape=jax.ShapeDtypeStruct((m, D_MODEL), F32),
        scratch_shapes=[pltpu.VMEM((N_GROUPS, seq, seq), BF16), pltpu.VMEM((ROW_TILE, D_MODEL), F32),
                        pltpu.VMEM((ROW_TILE, D_MODEL), BF16), pltpu.VMEM((ROW_TILE, D_MODEL), BF16),
                        pltpu.VMEM((ROW_TILE, D_MODEL), F32)],
        compiler_params=_params(("arbitrary",)),
        name=f"pool_t{seq}",
    )(x, mods, P["norm1_w"], P["pool_w"], P["pool_scale"])


def _dft_tables(seq):
    def cs(n):
        k = np.arange(n, dtype=np.int64)
        ang = 2.0 * np.pi * ((k[:, None] * k[None, :]) % n) / n
        return np.cos(ang), np.sin(ang)
    ct, st = cs(seq)
    cc, sc = cs(GROUP)
    e_t = jnp.asarray(np.concatenate([ct, st], axis=0), dtype=F32)
    e_c = jnp.asarray(np.concatenate([cc, -sc], axis=0), dtype=F32)
    return e_t.astype(BF16), e_c.astype(BF16)


def _fourier_kernel(x_ref, mod_ref, nw_ref, et_ref, ec_ref, w_ref, o_ref, h_scr, f_scr, wb_scr, *, seq):
    norm = np.float32(1.0 / math.sqrt(seq * GROUP))
    items = [(s, g) for s in range(ROW_TILE // seq) for g in range(N_GROUPS)]
    rows = lambda s: slice(s * seq, (s + 1) * seq)
    cols = lambda g: slice(g * GROUP, (g + 1) * GROUP)

    def norm_seq(s):
        h_scr[rows(s), :] = _norm_mod(x_ref[rows(s), :], nw_ref[0:1, :], mod_ref[...], 0).astype(BF16)

    def time_dft(s, g):
        return _dot(et_ref[...], h_scr[rows(s), cols(g)]).astype(BF16)

    norm_seq(0)
    wb_scr[...] = w_ref[...].astype(BF16)
    y = time_dft(*items[0])
    for i, (s, g) in enumerate(items):
        y_next = None
        if i + 1 < len(items):
            s2, g2 = items[i + 1]
            if s2 != s:
                norm_seq(s2)
            y_next = time_dft(s2, g2)
        f = _dot(y[:seq], ec_ref[:GROUP, :]) + _dot(y[seq:], ec_ref[GROUP:, :])
        f_scr[rows(s), cols(g)] = (f * norm).astype(BF16)
        y = y_next
    for r in range(ROW_TILE // ROW_BLOCK):
        blk = slice(r * ROW_BLOCK, (r + 1) * ROW_BLOCK)
        o_ref[blk, :] = x_ref[blk, :] + mod_ref[2:3, :] * _dot(f_scr[blk, :], wb_scr[...])


def _fourier_call(x, mods, li, cond_of_row, seq, P):
    m = x.shape[0]
    lj = li // 4
    e_t, e_c = _dft_tables(seq)
    return pl.pallas_call(
        functools.partial(_fourier_kernel, seq=seq),
        grid=(m // ROW_TILE,),
        in_specs=[
            pl.BlockSpec((ROW_TILE, D_MODEL), lambda i: (i, 0)),
            _mod_spec(li, cond_of_row, ROW_TILE, 1),
            pl.BlockSpec((None, 8, D_MODEL), lambda i: (li, 0, 0)),
            pl.BlockSpec((2 * seq, seq), lambda i: (0, 0)),
            pl.BlockSpec((2 * GROUP, GROUP), lambda i: (0, 0)),
            pl.BlockSpec((None, D_MODEL, D_MODEL), lambda i: (lj, 0, 0)),
        ],
        out_specs=pl.BlockSpec((ROW_TILE, D_MODEL), lambda i: (i, 0)),
        out_shape=jax.ShapeDtypeStruct((m, D_MODEL), F32),
        scratch_shapes=[pltpu.VMEM((ROW_TILE, D_MODEL), BF16), pltpu.VMEM((ROW_TILE, D_MODEL), BF16),
                        pltpu.VMEM((D_MODEL, D_MODEL), BF16)],
        compiler_params=_params(("arbitrary",)),
        name=f"fourier_t{seq}",
    )(x, mods, P["norm1_w"], e_t, e_c, P["fnet_w"])


def _rope_tables(rows):
    r, col = jnp.meshgrid(jnp.arange(rows), jnp.arange(GRID_W), indexing="ij")
    r = r.reshape(-1).astype(F32)
    col = col.reshape(-1).astype(F32)
    n_freq = QK_ROPE // 4
    inv = ROPE_THETA ** (-jnp.arange(n_freq, dtype=F32) / n_freq)
    ang = jnp.concatenate([r[:, None] * inv, col[:, None] * inv], axis=-1)
    cos = jnp.repeat(jnp.cos(ang), 2, axis=-1)
    sin = jnp.repeat(jnp.sin(ang), 2, axis=-1)
    t = cos.shape[0]
    pad = LANES - QK_NOPE - QK_ROPE
    cos_p = jnp.concatenate([jnp.ones((t, QK_NOPE), F32), cos, jnp.ones((t, pad), F32)], axis=-1)
    sin_p = jnp.concatenate([jnp.zeros((t, QK_NOPE), F32), sin, jnp.zeros((t, pad), F32)], axis=-1)
    return cos_p, sin_p


def _pair_swap(w):
    wr = w.reshape(w.shape[:-1] + (w.shape[-1] // 2, 2))
    return jnp.stack([-wr[..., 1], wr[..., 0]], axis=-1).reshape(w.shape)


def _mla_weights(P, j, rope, seq_rows):
    pad = LANES - QK_NOPE - QK_ROPE
    wuq = P["mla_wuq"][j].reshape(Q_LORA, N_HEADS, QK_NOPE + QK_ROPE)
    q_slots = jnp.pad(wuq, ((0, 0), (0, 0), (0, pad))).reshape(Q_LORA, N_HEADS * LANES)
    wdkv = P["mla_wdkv"][j]
    kr_slot = jnp.pad(wdkv[:, KV_LORA:], ((0, 0), (QK_NOPE, pad)))
    parts1 = [P["mla_wdq"][j], wdkv[:, :KV_LORA], kr_slot]
    parts2 = [q_slots]
    if rope:
        parts1.append(jnp.pad(_pair_swap(wdkv[:, KV_LORA:]), ((0, 0), (QK_NOPE, pad))))
        q_sw = jnp.pad(_pair_swap(wuq[:, :, QK_NOPE:]), ((0, 0), (0, 0), (QK_NOPE, pad)))
        parts2.append(q_sw.reshape(Q_LORA, N_HEADS * LANES))
        cos, sin = _rope_tables(seq_rows // GRID_W)
    else:
        cos = sin = jnp.zeros((PROJ_TILE, LANES), F32)
    wukv = P["mla_wukv"][j].reshape(KV_LORA, N_HEADS, QK_NOPE + V_HEAD)
    wk = jnp.pad(wukv[:, :, :QK_NOPE], ((0, 0), (0, 0), (0, LANES - QK_NOPE)))
    return dict(
        norm1_w=P["norm1_w"], q_norm=P["mla_q_norm"][j][None, :], kv_norm=P["mla_kv_norm"][j][None, :],
        w1=jnp.concatenate(parts1, axis=1).astype(BF16), w2=jnp.concatenate(parts2, axis=1).astype(BF16),
        wk=wk.reshape(KV_LORA, N_HEADS * LANES).astype(BF16),
        wv=wukv[:, :, QK_NOPE:].reshape(KV_LORA, N_HEADS * V_HEAD).astype(BF16), cos=cos, sin=sin)


def _mla_layer(x, mods, li, cond_of_row, batch, seq, P, cache):
    j = li // 4
    W = _mla_weights(P, j, cache is not None, seq)
    q, k, v, ckv, krope = _mla_proj_call(x, mods, li, cond_of_row, W, cache is not None)
    q = q.reshape(N_HEADS, batch, seq, LANES)
    k = k.reshape(N_HEADS, batch, seq, LANES)
    v = v.reshape(HEAD_PAIRS, batch, seq, LANES)
    if cache is None:
        o = _attn_call(q, [(k, v)], seq, HEAD_PAIRS)
    else:
        c_ckv, c_krope = cache
        past = c_ckv.shape[1]
        pad = LANES - QK_NOPE - QK_ROPE
        kr_pad = jnp.pad(c_krope.reshape(batch * past, QK_ROPE), ((0, 0), (QK_NOPE, pad)))
        kc, vc = _kv_cache_call(c_ckv.reshape(batch * past, KV_LORA), kr_pad, W)
        cached = (kc.reshape(N_HEADS, batch, past, LANES), vc.reshape(HEAD_PAIRS, batch, past, LANES))
        o = _attn_call(q, [cached, (k, v)], 512, 4)
    return o, ckv, krope


def _mixer(x, mods, li, cond_of_row, batch, seq, P, cache):
    kind = li % 4
    if kind == 0:
        o, ckv, krope = _mla_layer(x, mods, li, cond_of_row, batch, seq, P, cache)
        return x, (o, P["mla_wo_bf16"], li // 4), (ckv, krope)
    if kind == 1:
        return _pool_call(x, mods, li, cond_of_row, seq, P), None, None
    if kind == 2:
        return _fourier_call(x, mods, li, cond_of_row, seq, P), None, None
    return _sconv_call(x, mods, li, cond_of_row, seq, P), None, None


def _run_passes(passes, mods, P):
    new_cache = None
    for li in range(DEPTH):
        pending = []
        for ps in passes:
            ps["x"], pre, made = _mixer(ps["x"], mods, li, ps["cond_of_row"], ps["batch"], ps["seq"], P,
                                        ps["cache"])
            pending.append(pre)
            if made is not None and ps["cache"] is None:
                new_cache = made
        xs, outs = lax.optimization_barrier(([ps["x"] for ps in passes],
                                              [None if pre is None else pre[0] for pre in pending]))
        pending = [None if pre is None else (o,) + pre[1:] for pre, o in zip(pending, outs)]
        xs = [_ffn_call(x, mods, li, ps["cond_of_row"], ps["seq"], P, pre, final_norm=(li == DEPTH - 1))
              for x, ps, pre in zip(xs, passes, pending)]
        xs = lax.optimization_barrier(xs)
        for ps, x in zip(passes, xs):
            ps["x"] = x
    return new_cache


def kernel(x_prompt, x_sample, cache_ckv, cache_krope, c, c_ctx, norm1_w, norm2_w, ada_w, ada_b, mla_wdq, mla_q_norm, mla_wuq, mla_wdkv, mla_kv_norm, mla_wukv, mla_wo, pool_w, pool_scale, fnet_w, sconv_win, sconv_conv, sconv_wout, ffn_up, ffn_conv_w, ffn_conv_b, ffn_down, final_norm_w):
    row = lambda w: jnp.pad(w[:, None, :], ((0, 0), (0, 7), (0, 0)))
    P = dict(norm1_w=row(norm1_w), norm2_w=row(norm2_w), mla_wdq=mla_wdq, mla_q_norm=mla_q_norm,
             mla_wuq=mla_wuq, mla_wdkv=mla_wdkv, mla_kv_norm=mla_kv_norm, mla_wukv=mla_wukv,
             mla_wo_bf16=mla_wo.astype(BF16),
             pool_w=pool_w, pool_scale=row(pool_scale), fnet_w=fnet_w, sconv_win=sconv_win,
             sconv_conv=sconv_conv, sconv_wout=sconv_wout, ffn_conv_w=ffn_conv_w,
             ffn_conv_b=ffn_conv_b, ffn_down=ffn_down, final_norm_w=final_norm_w,
             ffn_w=functools.cache(lambda l: _layer_weights(ffn_up, ffn_down, l, 2, f"ffn{l}_weights")),
             sconv_w=functools.cache(lambda l: _layer_weights(sconv_win, sconv_wout, l, 3, f"sconv{l}_weights")))
    b_ctx, t_ctx, _ = x_prompt.shape
    b_lat, t_lat, _ = x_sample.shape
    assert ROW_TILE % t_ctx == 0 and t_lat % ROW_TILE == 0 and b_lat + 1 <= N_COND

    conds = jnp.concatenate([c_ctx[None, :], c, jnp.zeros((N_COND - 1 - b_lat, D_MODEL), F32)], axis=0)
    mods = _ada_call(conds, ada_w, ada_b).reshape(DEPTH, N_COND, 6, D_MODEL)

    ctx = dict(x=x_prompt.reshape(b_ctx * t_ctx, D_MODEL), cond_of_row=lambda r: 0, batch=b_ctx, seq=t_ctx,
               cache=None)
    lat = dict(x=x_sample.reshape(b_lat * t_lat, D_MODEL), cond_of_row=lambda r: 1 + r // t_lat, batch=b_lat,
               seq=t_lat, cache=(cache_ckv[:, 0], cache_krope[:, 0]))
    ckv, krope = _run_passes([ctx, lat], mods, P)
    return (ctx["x"].reshape(b_ctx, t_ctx, D_MODEL), lat["x"].reshape(b_lat, t_lat, D_MODEL),
            ckv.reshape(b_ctx, 1, t_ctx, KV_LORA), krope.reshape(b_ctx, 1, t_ctx, QK_ROPE))
```

```python
import functools
import math

import numpy as np
import jax
import jax.numpy as jnp
from jax import lax
from jax.experimental import pallas as pl
from jax.experimental.pallas import tpu as pltpu

F32 = jnp.float32
BF16 = jnp.bfloat16

D_MODEL = 1024
N_HEADS = 16
QK_NOPE = 64
QK_ROPE = 32
V_HEAD = 64
Q_LORA = 384
KV_LORA = 256
ROPE_THETA = 10000.0
GRID_W = 64
POOL_WINDOWS = (2, 4, 8, 16)
N_GROUPS = 4
GROUP = D_MODEL // N_GROUPS
D_FF = 2816
EPS = 1e-6
DEPTH = 4

LANES = 128
Q_SCALE = np.float32(math.log2(math.e) / math.sqrt(QK_NOPE + QK_ROPE))
HEAD_PAIRS = N_HEADS // 2
N_COND = 8
ROW_TILE = 1024
PROJ_TILE = 512
FF_TILE = 256
ROW_BLOCK = 256
HALO = 8
RELAYOUT_ROWS = 128
CHUNK_SLOTS = 16
ADA_TILE = 1536
MIB = 1024 * 1024
VMEM_DEFAULT = 48 * MIB
VMEM_HEADROOM = 4 * MIB + 512 * 1024
ATTN_VMEM = 48 * MIB


def _params(sem, vmem_bytes=VMEM_DEFAULT):
    return pltpu.CompilerParams(dimension_semantics=sem, vmem_limit_bytes=int(vmem_bytes))


def _nbytes(a):
    return math.prod(a.shape) * jnp.dtype(a.dtype).itemsize


def _dot(a, b):
    return jnp.dot(a, b, preferred_element_type=F32)


def _rms(x, w):
    return x * lax.rsqrt(jnp.mean(x * x, axis=-1, keepdims=True) + EPS) * w


def _norm_mod(x, nw, mod, base):
    return _rms(x, nw) * (1.0 + mod[base + 1:base + 2, :]) + mod[base:base + 1, :]


def _gelu_exact(x):
    return 0.5 * x * (1.0 + lax.erf(x * np.float32(math.sqrt(0.5))))


def _ada_kernel(c_ref, w_ref, b_ref, o_ref):
    s = jax.nn.silu(c_ref[...]).astype(BF16)
    o_ref[...] = _dot(s, w_ref[...].astype(BF16)) + b_ref[...]


def _ada_call(conds, ada_w, ada_b):
    n_out = ada_w.shape[-1]
    return pl.pallas_call(
        _ada_kernel,
        grid=(DEPTH, n_out // ADA_TILE),
        in_specs=[
            pl.BlockSpec((N_COND, D_MODEL), lambda l, j: (0, 0)),
            pl.BlockSpec((None, D_MODEL, ADA_TILE), lambda l, j: (l, 0, j)),
            pl.BlockSpec((None, 1, ADA_TILE), lambda l, j: (l, 0, j)),
        ],
        out_specs=pl.BlockSpec((None, N_COND, ADA_TILE), lambda l, j: (l, 0, j)),
        out_shape=jax.ShapeDtypeStruct((DEPTH, N_COND, n_out), F32),
        compiler_params=_params(("arbitrary", "arbitrary"), 2 * D_MODEL * ADA_TILE * 4 + VMEM_HEADROOM),
        name="ada",
    )(conds, ada_w, ada_b.reshape(DEPTH, 1, n_out))


def _mod_spec(li, cond_of_row, tile, n_grid):
    if n_grid == 1:
        return pl.BlockSpec((None, None, 6, D_MODEL), lambda i: (li, cond_of_row(i * tile), 0, 0))
    return pl.BlockSpec((None, None, 6, D_MODEL), lambda i, j: (li, cond_of_row(i * tile), 0, 0))


def _conv_window(w, cw, t0, seq):
    n = w.shape[0]
    mid = slice(HALO, HALO + ROW_BLOCK)
    t = (t0 + lax.broadcasted_iota(jnp.int32, (ROW_BLOCK, 1), 0)) & (seq - 1)
    prev = jnp.where(t == 0, 0.0, pltpu.roll(w, 1, axis=0)[mid])
    nxt = jnp.where(t == seq - 1, 0.0, pltpu.roll(w, n - 1, axis=0)[mid])
    return prev * cw[0:1, :] + w[mid] * cw[1:2, :] + nxt * cw[2:3, :]


def _ffn_act(win, mid, cw, cb, t0, seq):
    return _gelu_exact(_conv_window(win(0), cw, t0, seq) + cb[0:1, :]) * mid(1)


def _sconv_act(win, mid, cw, cb, t0, seq):
    return mid(0) * _conv_window(win(1) * win(2), cw, t0, seq)


def _ffn_kernel(x_ref, mod_ref, nw_ref, up_ref, cw_ref, cb_ref, down_ref, fnw_ref, o_ref, *scratch, **cfg):
    _glu_body(x_ref, mod_ref, nw_ref, up_ref, cw_ref, cb_ref, down_ref, fnw_ref, None, o_ref, *scratch,
              n_up=2, act=_ffn_act, mod_base=3, **cfg)


def _ffn_pre_kernel(x_ref, mod_ref, nw_ref, up_ref, cw_ref, cb_ref, down_ref, fnw_ref, a_ref, wa_ref, o_ref,
                    *scratch, **cfg):
    _glu_body(x_ref, mod_ref, nw_ref, up_ref, cw_ref, cb_ref, down_ref, fnw_ref, (a_ref, wa_ref), o_ref,
              *scratch, n_up=2, act=_ffn_act, mod_base=3, **cfg)


def _sconv_kernel(x_ref, mod_ref, nw_ref, up_ref, cw_ref, down_ref, o_ref, *scratch, **cfg):
    _glu_body(x_ref, mod_ref, nw_ref, up_ref, cw_ref, None, down_ref, None, None, o_ref, *scratch,
              n_up=3, act=_sconv_act, mod_base=0, final_norm=False, **cfg)


def _glu_body(x_ref, mod_ref, nw_ref, up_ref, cw_ref, cb_ref, down_ref, fnw_ref, pre, o_ref,
              h_scr, acc_scr, gu0, gu1, *, n_up, n_chunks, act, seq, mod_base, final_norm):
    gu = (gu0, gu1)
    n_blocks = ROW_TILE // ROW_BLOCK

    def up(c, slot, r):
        rows = slice(r * ROW_BLOCK, (r + 1) * ROW_BLOCK)
        res = _dot(h_scr[rows, :], up_ref[c])
        for p in range(n_up):
            gu[slot][p, HALO + r * ROW_BLOCK:HALO + (r + 1) * ROW_BLOCK, :] = res[:, p * FF_TILE:(p + 1) * FF_TILE]

    def activate(c, slot, r):
        win = lambda p: gu[slot][p, r * ROW_BLOCK:(r + 1) * ROW_BLOCK + 2 * HALO, :]
        mid = lambda p: gu[slot][p, HALO + r * ROW_BLOCK:HALO + (r + 1) * ROW_BLOCK, :]
        cb = None if cb_ref is None else cb_ref[c]
        return act(win, mid, cw_ref[c], cb, r * ROW_BLOCK, seq).astype(BF16)

    def down(a, c, r):
        rows = slice(r * ROW_BLOCK, (r + 1) * ROW_BLOCK)
        y = _dot(a, down_ref[c])
        if isinstance(c, int) and c == 0:
            acc_scr[rows, :] = y
        else:
            acc_scr[rows, :] += y

    def stage(s, slot):
        for r in range(n_blocks):
            a = activate(s - 1, 1 - slot, r)
            up(s, slot, r)
            down(a, s - 1, r)

    def block(r):
        return slice(r * ROW_BLOCK, (r + 1) * ROW_BLOCK)

    def mixer_proj(r):
        a_ref, wa_ref = pre
        return _dot(a_ref[block(r), :], wa_ref[...])

    for g in gu:
        g[:, :HALO, :] = jnp.zeros((n_up, HALO, FF_TILE), F32)
        g[:, HALO + ROW_TILE:, :] = jnp.zeros((n_up, HALO, FF_TILE), F32)

    proj = mixer_proj(0) if pre is not None else None
    for r in range(n_blocks):
        x = x_ref[block(r), :]
        if pre is not None:
            x = x + mod_ref[2:3, :] * proj
            o_ref[block(r), :] = x
            if r + 1 < n_blocks:
                proj = mixer_proj(r + 1)
        h_scr[block(r), :] = _norm_mod(x, nw_ref[...], mod_ref[...], mod_base).astype(BF16)
        up(0, 0, r)

    if n_chunks > 1:
        stage(1, 1)
    n_pairs = (n_chunks - 2) // 2

    def pair(i, carry):
        s = 2 * i + 2
        stage(s, 0)
        stage(s + 1, 1)
        return carry

    lax.fori_loop(0, n_pairs, pair, 0)
    for s in range(2 * n_pairs + 2, n_chunks):
        stage(s, s % 2)
    last = n_chunks - 1
    res = x_ref if pre is None else o_ref
    for r in range(n_blocks):
        down(activate(last, last % 2, r), last, r)
        y = res[block(r), :] + mod_ref[mod_base + 2:mod_base + 3, :] * acc_scr[block(r), :]
        if final_norm:
            y = _rms(y, fnw_ref[0:1, :])
        o_ref[block(r), :] = y


def _chunked(w, n_chunks):
    w = w.reshape(w.shape[0], n_chunks, FF_TILE).transpose(1, 0, 2)
    return jnp.pad(w, ((0, CHUNK_SLOTS - n_chunks), (0, 8 - w.shape[1]), (0, 0)))


def _glu_call(kernel_fn, name, x, mods, li, cond_of_row, n_up, n_chunks, norm_w, operands, specs, cfg):
    m = x.shape[0]
    whole = lambda a: pl.BlockSpec(a.shape, lambda i: (0,) * a.ndim)
    in_specs = [pl.BlockSpec((ROW_TILE, D_MODEL), lambda i: (i, 0)),
                _mod_spec(li, cond_of_row, ROW_TILE, 1),
                pl.BlockSpec((None, 1, D_MODEL), lambda i: (li, 0, 0))]
    layer_w = lambda a, l: pl.BlockSpec((None,) + a.shape[1:], lambda i: (l,) + (0,) * (a.ndim - 1),
                                        pipeline_mode=pl.Buffered(1))
    rows = lambda a: pl.BlockSpec((ROW_TILE, a.shape[1]), lambda i: (i, 0))
    spec_of = {None: whole, "rows": rows, "resident": lambda a: pl.BlockSpec(memory_space=pltpu.VMEM)}
    in_specs += [layer_w(a, s) if isinstance(s, int) else spec_of[s](a) for a, s in zip(operands, specs)]
    gu_shape = (n_up, ROW_TILE + 2 * HALO, FF_TILE)
    tile = ROW_TILE * D_MODEL
    window = {None: _nbytes, "rows": lambda a: 2 * ROW_TILE * a.shape[1] * a.dtype.itemsize,
              "resident": lambda a: 0}
    vmem = (2 * 2 * tile * 4 + tile * 2 + tile * 4 + 2 * math.prod(gu_shape) * 4 + VMEM_HEADROOM
            + sum(_nbytes(a) // a.shape[0] if isinstance(s, int) else window[s](a)
                  for a, s in zip(operands, specs)))
    return pl.pallas_call(
        functools.partial(kernel_fn, n_chunks=n_chunks, **cfg),
        grid=(m // ROW_TILE,),
        in_specs=in_specs,
        out_specs=pl.BlockSpec((ROW_TILE, D_MODEL), lambda i: (i, 0)),
        out_shape=jax.ShapeDtypeStruct((m, D_MODEL), F32),
        scratch_shapes=[
            pltpu.VMEM((ROW_TILE, D_MODEL), BF16),
            pltpu.VMEM((ROW_TILE, D_MODEL), F32),
            pltpu.VMEM(gu_shape, F32), pltpu.VMEM(gu_shape, F32),
        ],
        compiler_params=_params(("arbitrary",), vmem),
        name=name,
    )(x, mods, norm_w, *operands)


def _in_vmem(a):
    return pltpu.with_memory_space_constraint(a, pltpu.VMEM)


def _layer_weights_kernel(up_ref, dn_ref, uo_ref, do_ref, *, n_up, n_chunks):
    for c in range(n_chunks):
        for p in range(n_up):
            col = (p * n_chunks + c) * FF_TILE
            uo_ref[c, :, p * FF_TILE:(p + 1) * FF_TILE] = up_ref[:, col:col + FF_TILE].astype(BF16)
    do_ref[...] = dn_ref[...].astype(BF16)


def _layer_weights(up, down, layer, n_up, name):
    _, d, cols = up.shape
    _, hidden, d_out = down.shape
    n_chunks = hidden // FF_TILE
    steps = d // RELAYOUT_ROWS
    dn_rows = hidden // steps
    up_cm, dn = pl.pallas_call(
        functools.partial(_layer_weights_kernel, n_up=n_up, n_chunks=n_chunks),
        grid=(steps,),
        in_specs=[pl.BlockSpec((None, RELAYOUT_ROWS, cols), lambda k: (layer, k, 0)),
                  pl.BlockSpec((None, dn_rows, d_out), lambda k: (layer, k, 0))],
        out_specs=[pl.BlockSpec((n_chunks, RELAYOUT_ROWS, n_up * FF_TILE), lambda k: (0, k, 0)),
                   pl.BlockSpec((dn_rows, d_out), lambda k: (k, 0))],
        out_shape=[jax.ShapeDtypeStruct((n_chunks, d, n_up * FF_TILE), BF16),
                   jax.ShapeDtypeStruct((hidden, d_out), BF16)],
        compiler_params=_params(("arbitrary",),
                                2 * (RELAYOUT_ROWS * cols + dn_rows * d_out) * (4 + 2) + VMEM_HEADROOM),
        name=name,
    )(up, down)
    return _in_vmem(up_cm), _in_vmem(dn.reshape(n_chunks, FF_TILE, d_out))


def _ffn_call(x, mods, li, cond_of_row, seq, P, pre, final_norm):
    n = D_FF // FF_TILE
    up_cm, down_cm = P["ffn_w"](li)
    operands = [up_cm, _chunked(P["ffn_conv_w"][li], n), _chunked(P["ffn_conv_b"][li][None, :], n),
                down_cm, jnp.pad(P["final_norm_w"].reshape(1, D_MODEL), ((0, 7), (0, 0)))]
    specs = ["resident", None, None, "resident", None]
    kernel_fn = _ffn_kernel
    if pre is not None:
        a, w_a, l_a = pre
        operands += [a, w_a]
        specs += ["rows", l_a]
        kernel_fn = _ffn_pre_kernel
    return _glu_call(kernel_fn, f"ffn{li}", x, mods, li, cond_of_row, 2, n, P["norm2_w"],
                     operands, specs, dict(seq=seq, final_norm=final_norm))


def _sconv_call(x, mods, li, cond_of_row, seq, P):
    lj = li // 4
    n = D_MODEL // FF_TILE
    up_cm, down_cm = P["sconv_w"](lj)
    operands = [up_cm, _chunked(P["sconv_conv"][lj], n), down_cm]
    return _glu_call(_sconv_kernel, f"sconv{li}", x, mods, li, cond_of_row, 3, n, P["norm1_w"],
                     operands, ["resident", None, "resident"], dict(seq=seq))


def _emit_kv(ckv, krope_pad, wk_ref, wv_ref, k_ref, v_ref):
    c = ckv.astype(BF16)
    kx = _dot(c, wk_ref[...])
    vx = _dot(c, wv_ref[...])
    for h in range(N_HEADS):
        k_ref[h] = (kx[:, h * LANES:(h + 1) * LANES] + krope_pad).astype(BF16)
    for p in range(HEAD_PAIRS):
        v_ref[p] = vx[:, p * LANES:(p + 1) * LANES].astype(BF16)


def _mla_proj_kernel(x_ref, mod_ref, nw_ref, w1_ref, qn_ref, kvn_ref, w2_ref, wk_ref, wv_ref, cos_ref,
                     sin_ref, q_ref, k_ref, v_ref, ckv_ref, kr_ref, *, rope):
    h = _norm_mod(x_ref[...], nw_ref[...], mod_ref[...], 0).astype(BF16)
    z = _dot(h, w1_ref[...])
    cq = _rms(z[:, :Q_LORA], qn_ref[...]).astype(BF16)
    ckv = _rms(z[:, Q_LORA:Q_LORA + KV_LORA], kvn_ref[...])
    kr = z[:, Q_LORA + KV_LORA:Q_LORA + KV_LORA + LANES]
    if rope:
        cos, sin = cos_ref[...], sin_ref[...]
        kr = kr * cos + z[:, Q_LORA + KV_LORA + LANES:] * sin
    ckv_ref[...] = ckv
    kr_ref[...] = kr[:, QK_NOPE:QK_NOPE + QK_ROPE]
    c = ckv.astype(BF16)
    off = N_HEADS * LANES

    def pair_matmuls(p):
        cols = slice(2 * p * LANES, (2 * p + 2) * LANES)
        q2 = _dot(cq, w2_ref[:, cols])
        qs = _dot(cq, w2_ref[:, off + 2 * p * LANES:off + (2 * p + 2) * LANES]) if rope else None
        k2 = _dot(c, wk_ref[:, cols])
        v2 = _dot(c, wv_ref[:, p * LANES:(p + 2) * LANES]) if p % 2 == 0 else None
        return q2, qs, k2, v2

    nxt = pair_matmuls(0)
    for p in range(HEAD_PAIRS):
        q2, qs, k2, v2 = nxt
        if p + 1 < HEAD_PAIRS:
            nxt = pair_matmuls(p + 1)
        for e in range(2):
            lanes = slice(e * LANES, (e + 1) * LANES)
            qh = q2[:, lanes]
            if rope:
                qh = qh * cos + qs[:, lanes] * sin
            q_ref[2 * p + e] = (qh * Q_SCALE).astype(BF16)
            k_ref[2 * p + e] = (k2[:, lanes] + kr).astype(BF16)
        if v2 is not None:
            v_ref[p] = v2[:, :LANES].astype(BF16)
            v_ref[p + 1] = v2[:, LANES:].astype(BF16)


def _mla_proj_call(x, mods, li, cond_of_row, W, rope):
    m = x.shape[0]
    n1 = W["w1"].shape[1]
    n2 = W["w2"].shape[1]
    const = lambda i: (0, 0)
    pos_tiles = W["cos"].shape[0] // PROJ_TILE
    pos = lambda i: (i % pos_tiles, 0)
    return pl.pallas_call(
        functools.partial(_mla_proj_kernel, rope=rope),
        grid=(m // PROJ_TILE,),
        in_specs=[
            pl.BlockSpec((PROJ_TILE, D_MODEL), lambda i: (i, 0)),
            _mod_spec(li, cond_of_row, PROJ_TILE, 1),
            pl.BlockSpec((None, 1, D_MODEL), lambda i: (li, 0, 0)),
            pl.BlockSpec((D_MODEL, n1), const),
            pl.BlockSpec((1, Q_LORA), const),
            pl.BlockSpec((1, KV_LORA), const),
            pl.BlockSpec((Q_LORA, n2), const),
            pl.BlockSpec((KV_LORA, N_HEADS * LANES), const),
            pl.BlockSpec((KV_LORA, HEAD_PAIRS * LANES), const),
            pl.BlockSpec((PROJ_TILE, LANES), pos),
            pl.BlockSpec((PROJ_TILE, LANES), pos),
        ],
        out_specs=[
            pl.BlockSpec((N_HEADS, PROJ_TILE, LANES), lambda i: (0, i, 0)),
            pl.BlockSpec((N_HEADS, PROJ_TILE, LANES), lambda i: (0, i, 0)),
            pl.BlockSpec((HEAD_PAIRS, PROJ_TILE, LANES), lambda i: (0, i, 0)),
            pl.BlockSpec((PROJ_TILE, KV_LORA), lambda i: (i, 0)),
            pl.BlockSpec((PROJ_TILE, QK_ROPE), lambda i: (i, 0)),
        ],
        out_shape=[
            jax.ShapeDtypeStruct((N_HEADS, m, LANES), BF16),
            jax.ShapeDtypeStruct((N_HEADS, m, LANES), BF16),
            jax.ShapeDtypeStruct((HEAD_PAIRS, m, LANES), BF16),
            jax.ShapeDtypeStruct((m, KV_LORA), F32),
            jax.ShapeDtypeStruct((m, QK_ROPE), F32),
        ],
        compiler_params=_params(("arbitrary",)),
        name="mla_proj_rope" if rope else "mla_proj",
    )(x, mods, W["norm1_w"], W["w1"], W["q_norm"], W["kv_norm"], W["w2"], W["wk"], W["wv"],
      W["cos"], W["sin"])


def _kv_cache_kernel(ckv_ref, kr_ref, wk_ref, wv_ref, k_ref, v_ref):
    _emit_kv(ckv_ref[...], kr_ref[...], wk_ref, wv_ref, k_ref, v_ref)


def _kv_cache_call(ckv, krope_pad, W):
    m = ckv.shape[0]
    const = lambda i: (0, 0)
    return pl.pallas_call(
        _kv_cache_kernel,
        grid=(1,),
        in_specs=[
            pl.BlockSpec((m, KV_LORA), const),
            pl.BlockSpec((m, LANES), const),
            pl.BlockSpec((KV_LORA, N_HEADS * LANES), const),
            pl.BlockSpec((KV_LORA, HEAD_PAIRS * LANES), const),
        ],
        out_specs=[
            pl.BlockSpec((N_HEADS, m, LANES), lambda i: (0, 0, 0)),
            pl.BlockSpec((HEAD_PAIRS, m, LANES), lambda i: (0, 0, 0)),
        ],
        out_shape=[
            jax.ShapeDtypeStruct((N_HEADS, m, LANES), BF16),
            jax.ShapeDtypeStruct((HEAD_PAIRS, m, LANES), BF16),
        ],
        compiler_params=_params(("arbitrary",)),
        name="kv_cache",
    )(ckv, krope_pad, W["wk"], W["wv"])


def _attn_kernel(q_ref, *refs, n_pairs):
    o_ref, kv = refs[-1], refs[:-1]
    parts = [(kv[i], kv[i + 1]) for i in range(0, len(kv), 2)]
    heads = [(p, e) for p in range(n_pairs) for e in range(2)]

    def scores(p, e):
        q = q_ref[2 * p + e]
        return [lax.dot_general(q, k_ref[2 * p + e], (((1,), (1,)), ((), ())), preferred_element_type=F32)
                for k_ref, _ in parts]

    outs = []
    s_next = scores(*heads[0])
    for i, (p, e) in enumerate(heads):
        s = s_next
        if i + 1 < len(heads):
            s_next = scores(*heads[i + 1])
        m = functools.reduce(jnp.maximum, [jnp.max(x, axis=-1, keepdims=True) for x in s])
        w = [jnp.exp2(x - m) for x in s]
        l = sum(jnp.sum(x, axis=-1, keepdims=True) for x in w)
        pv = sum(_dot(x.astype(BF16), v_ref[p]) for x, (_, v_ref) in zip(w, parts))
        outs.append(pv / l)
        if e == 1:
            lane = lax.broadcasted_iota(jnp.int32, pv.shape, 1)
            o_ref[:, p * LANES:(p + 1) * LANES] = jnp.where(lane < V_HEAD, outs[-2], outs[-1]).astype(BF16)


def _attn_call(q, kv_parts, q_tile, n_pairs):
    _, b, t, _ = q.shape
    nq = t // q_tile
    in_specs = [pl.BlockSpec((2 * n_pairs, None, q_tile, LANES), lambda bi, g, qi: (g, bi, qi, 0))]
    operands = [q]
    for k, v in kv_parts:
        s = k.shape[2]
        in_specs += [pl.BlockSpec((2 * n_pairs, None, s, LANES), lambda bi, g, qi: (g, bi, 0, 0)),
                     pl.BlockSpec((n_pairs, None, s, LANES), lambda bi, g, qi: (g, bi, 0, 0))]
        operands += [k, v]
    return pl.pallas_call(
        functools.partial(_attn_kernel, n_pairs=n_pairs),
        grid=(b, HEAD_PAIRS // n_pairs, nq),
        in_specs=in_specs,
        out_specs=pl.BlockSpec((q_tile, n_pairs * LANES), lambda bi, g, qi: (bi * nq + qi, g)),
        out_shape=jax.ShapeDtypeStruct((b * t, N_HEADS * V_HEAD), BF16),
        compiler_params=_params(("arbitrary", "arbitrary", "arbitrary"), ATTN_VMEM),
        name=f"attn_{len(kv_parts)}part",
    )(*operands)


def _pool_kernel(x_ref, mod_ref, nw_ref, w_ref, sc_ref, o_ref, band_scr, h_scr, hi_scr, lo_scr, y_scr, *, seq):
    @pl.when(pl.program_id(0) == 0)
    def _():
        r = lax.broadcasted_iota(jnp.int32, (seq, seq), 0)
        c = lax.broadcasted_iota(jnp.int32, (seq, seq), 1)
        for g, win in enumerate(POOL_WINDOWS):
            inside = (c >= r - win // 2) & (c < r + win - win // 2)
            band_scr[g] = jnp.where(inside, 1.0, 0.0).astype(BF16)

    items = [(s, g) for s in range(ROW_TILE // seq) for g in range(N_GROUPS)]
    rows = lambda s: slice(s * seq, (s + 1) * seq)
    cols = lambda g: slice(g * GROUP, (g + 1) * GROUP)
    t = lax.broadcasted_iota(jnp.int32, (seq, 1), 0)
    counts = [(jnp.minimum(t + (win - win // 2), seq) - jnp.maximum(t - win // 2, 0)).astype(F32)
              for win in POOL_WINDOWS]

    def norm_seq(s):
        h = _norm_mod(x_ref[rows(s), :], nw_ref[...], mod_ref[...], 0)
        hi = h.astype(BF16)
        h_scr[rows(s), :] = h
        hi_scr[rows(s), :] = hi
        lo_scr[rows(s), :] = (h - hi.astype(F32)).astype(BF16)

    def window_sum(s, g):
        return _dot(band_scr[g], hi_scr[rows(s), cols(g)]) + _dot(band_scr[g], lo_scr[rows(s), cols(g)])

    norm_seq(0)
    tot = window_sum(*items[0])
    for i, (s, g) in enumerate(items):
        tot_next = None
        if i + 1 < len(items):
            s2, g2 = items[i + 1]
            if s2 != s:
                norm_seq(s2)
            tot_next = window_sum(s2, g2)
        pooled = (tot / counts[g] - h_scr[rows(s), cols(g)]).astype(BF16)
        y_scr[rows(s), cols(g)] = _dot(pooled, w_ref[g].astype(BF16))
        tot = tot_next
    for r in range(ROW_TILE // ROW_BLOCK):
        blk = slice(r * ROW_BLOCK, (r + 1) * ROW_BLOCK)
        o_ref[blk, :] = x_ref[blk, :] + mod_ref[2:3, :] * (y_scr[blk, :] * sc_ref[...])


def _pool_call(x, mods, li, cond_of_row, seq, P):
    m = x.shape[0]
    lj = li // 4
    return pl.pallas_call(
        functools.partial(_pool_kernel, seq=seq),
        grid=(m // ROW_TILE,),
        in_specs=[
            pl.BlockSpec((ROW_TILE, D_MODEL), lambda i: (i, 0)),
            _mod_spec(li, cond_of_row, ROW_TILE, 1),
            pl.BlockSpec((None, 1, D_MODEL), lambda i: (li, 0, 0)),
            pl.BlockSpec((None, N_GROUPS, GROUP, GROUP), lambda i: (lj, 0, 0, 0)),
            pl.BlockSpec((None, 1, D_MODEL), lambda i: (lj, 0, 0)),
        ],
        out_specs=pl.BlockSpec((ROW_TILE, D_MODEL), lambda i: (i, 0)),
        out_shape=jax.ShapeDtypeStruct((m, D_MODEL), F32),
        scratch_shapes=[pltpu.VMEM((N_GROUPS, seq, seq), BF16), pltpu.VMEM((ROW_TILE, D_MODEL), F32),
                        pltpu.VMEM((ROW_TILE, D_MODEL), BF16), pltpu.VMEM((ROW_TILE, D_MODEL), BF16),
                        pltpu.VMEM((ROW_TILE, D_MODEL), F32)],
        compiler_params=_params(("arbitrary",)),
        name=f"pool_t{seq}",
    )(x, mods, P["norm1_w"], P["pool_w"], P["pool_scale"])


def _dft_tables(seq):
    def cs(n):
        k = np.arange(n, dtype=np.int64)
        ang = 2.0 * np.pi * ((k[:, None] * k[None, :]) % n) / n
        return np.cos(ang), np.sin(ang)
    ct, st = cs(seq)
    cc, sc = cs(GROUP)
    e_t = jnp.asarray(np.concatenate([ct, st], axis=0), dtype=F32)
    e_c = jnp.asarray(np.concatenate([cc, -sc], axis=0), dtype=F32)
    return e_t.astype(BF16), e_c.astype(BF16)


def _fourier_kernel(x_ref, mod_ref, nw_ref, et_ref, ec_ref, w_ref, o_ref, h_scr, f_scr, wb_scr, *, seq):
    norm = np.float32(1.0 / math.sqrt(seq * GROUP))
    items = [(s, g) for s in range(ROW_TILE // seq) for g in range(N_GROUPS)]
    rows = lambda s: slice(s * seq, (s + 1) * seq)
    cols = lambda g: slice(g * GROUP, (g + 1) * GROUP)

    def norm_seq(s):
        h_scr[rows(s), :] = _norm_mod(x_ref[rows(s), :], nw_ref[...], mod_ref[...], 0).astype(BF16)

    def time_dft(s, g):
        return _dot(et_ref[...], h_scr[rows(s), cols(g)]).astype(BF16)

    norm_seq(0)
    wb_scr[...] = w_ref[...].astype(BF16)
    y = time_dft(*items[0])
    for i, (s, g) in enumerate(items):
        y_next = None
        if i + 1 < len(items):
            s2, g2 = items[i + 1]
            if s2 != s:
                norm_seq(s2)
            y_next = time_dft(s2, g2)
        f = _dot(y[:seq], ec_ref[:GROUP, :]) + _dot(y[seq:], ec_ref[GROUP:, :])
        f_scr[rows(s), cols(g)] = (f * norm).astype(BF16)
        y = y_next
    for r in range(ROW_TILE // ROW_BLOCK):
        blk = slice(r * ROW_BLOCK, (r + 1) * ROW_BLOCK)
        o_ref[blk, :] = x_ref[blk, :] + mod_ref[2:3, :] * _dot(f_scr[blk, :], wb_scr[...])


def _fourier_call(x, mods, li, cond_of_row, seq, P):
    m = x.shape[0]
    lj = li // 4
    e_t, e_c = _dft_tables(seq)
    return pl.pallas_call(
        functools.partial(_fourier_kernel, seq=seq),
        grid=(m // ROW_TILE,),
        in_specs=[
            pl.BlockSpec((ROW_TILE, D_MODEL), lambda i: (i, 0)),
            _mod_spec(li, cond_of_row, ROW_TILE, 1),
            pl.BlockSpec((None, 1, D_MODEL), lambda i: (li, 0, 0)),
            pl.BlockSpec((2 * seq, seq), lambda i: (0, 0)),
            pl.BlockSpec((2 * GROUP, GROUP), lambda i: (0, 0)),
            pl.BlockSpec((None, D_MODEL, D_MODEL), lambda i: (lj, 0, 0)),
        ],
        out_specs=pl.BlockSpec((ROW_TILE, D_MODEL), lambda i: (i, 0)),
        out_shape=jax.ShapeDtypeStruct((m, D_MODEL), F32),
        scratch_shapes=[pltpu.VMEM((ROW_TILE, D_MODEL), BF16), pltpu.VMEM((ROW_TILE, D_MODEL), BF16),
                        pltpu.VMEM((D_MODEL, D_MODEL), BF16)],
        compiler_params=_params(("arbitrary",)),
        name=f"fourier_t{seq}",
    )(x, mods, P["norm1_w"], e_t, e_c, P["fnet_w"])


def _rope_tables(rows):
    r, col = jnp.meshgrid(jnp.arange(rows), jnp.arange(GRID_W), indexing="ij")
    r = r.reshape(-1).astype(F32)
    col = col.reshape(-1).astype(F32)
    n_freq = QK_ROPE // 4
    inv = ROPE_THETA ** (-jnp.arange(n_freq, dtype=F32) / n_freq)
    ang = jnp.concatenate([r[:, None] * inv, col[:, None] * inv], axis=-1)
    cos = jnp.repeat(jnp.cos(ang), 2, axis=-1)
    sin = jnp.repeat(jnp.sin(ang), 2, axis=-1)
    t = cos.shape[0]
    pad = LANES - QK_NOPE - QK_ROPE
    cos_p = jnp.concatenate([jnp.ones((t, QK_NOPE), F32), cos, jnp.ones((t, pad), F32)], axis=-1)
    sin_p = jnp.concatenate([jnp.zeros((t, QK_NOPE), F32), sin, jnp.zeros((t, pad), F32)], axis=-1)
    return cos_p, sin_p


def _pair_swap(w):
    wr = w.reshape(w.shape[:-1] + (w.shape[-1] // 2, 2))
    return jnp.stack([-wr[..., 1], wr[..., 0]], axis=-1).reshape(w.shape)


def _mla_weights(P, j, rope, seq_rows):
    pad = LANES - QK_NOPE - QK_ROPE
    wuq = P["mla_wuq"][j].reshape(Q_LORA, N_HEADS, QK_NOPE + QK_ROPE)
    q_slots = jnp.pad(wuq, ((0, 0), (0, 0), (0, pad))).reshape(Q_LORA, N_HEADS * LANES)
    wdkv = P["mla_wdkv"][j]
    kr_slot = jnp.pad(wdkv[:, KV_LORA:], ((0, 0), (QK_NOPE, pad)))
    parts1 = [P["mla_wdq"][j], wdkv[:, :KV_LORA], kr_slot]
    parts2 = [q_slots]
    if rope:
        parts1.append(jnp.pad(_pair_swap(wdkv[:, KV_LORA:]), ((0, 0), (QK_NOPE, pad))))
        q_sw = jnp.pad(_pair_swap(wuq[:, :, QK_NOPE:]), ((0, 0), (0, 0), (QK_NOPE, pad)))
        parts2.append(q_sw.reshape(Q_LORA, N_HEADS * LANES))
        cos, sin = _rope_tables(seq_rows // GRID_W)
    else:
        cos = sin = jnp.zeros((PROJ_TILE, LANES), F32)
    wukv = P["mla_wukv"][j].reshape(KV_LORA, N_HEADS, QK_NOPE + V_HEAD)
    wk = jnp.pad(wukv[:, :, :QK_NOPE], ((0, 0), (0, 0), (0, LANES - QK_NOPE)))
    return dict(
        norm1_w=P["norm1_w"], q_norm=P["mla_q_norm"][j][None, :], kv_norm=P["mla_kv_norm"][j][None, :],
        w1=jnp.concatenate(parts1, axis=1).astype(BF16), w2=jnp.concatenate(parts2, axis=1).astype(BF16),
        wk=wk.reshape(KV_LORA, N_HEADS * LANES).astype(BF16),
        wv=wukv[:, :, QK_NOPE:].reshape(KV_LORA, N_HEADS * V_HEAD).astype(BF16), cos=cos, sin=sin)


def _mla_layer(x, mods, li, cond_of_row, batch, seq, P, cache):
    j = li // 4
    W = _mla_weights(P, j, cache is not None, seq)
    q, k, v, ckv, krope = _mla_proj_call(x, mods, li, cond_of_row, W, cache is not None)
    q = q.reshape(N_HEADS, batch, seq, LANES)
    k = k.reshape(N_HEADS, batch, seq, LANES)
    v = v.reshape(HEAD_PAIRS, batch, seq, LANES)
    if cache is None:
        o = _attn_call(q, [(k, v)], seq, HEAD_PAIRS)
    else:
        c_ckv, c_krope = cache
        past = c_ckv.shape[1]
        pad = LANES - QK_NOPE - QK_ROPE
        kr_pad = jnp.pad(c_krope.reshape(batch * past, QK_ROPE), ((0, 0), (QK_NOPE, pad)))
        kc, vc = _kv_cache_call(c_ckv.reshape(batch * past, KV_LORA), kr_pad, W)
        cached = (kc.reshape(N_HEADS, batch, past, LANES), vc.reshape(HEAD_PAIRS, batch, past, LANES))
        o = _attn_call(q, [cached, (k, v)], 512, 4)
    return o, ckv, krope


def _mixer(x, mods, li, cond_of_row, batch, seq, P, cache):
    kind = li % 4
    if kind == 0:
        o, ckv, krope = _mla_layer(x, mods, li, cond_of_row, batch, seq, P, cache)
        return x, (o, P["mla_wo_bf16"], li // 4), (ckv, krope)
    if kind == 1:
        return _pool_call(x, mods, li, cond_of_row, seq, P), None, None
    if kind == 2:
        return _fourier_call(x, mods, li, cond_of_row, seq, P), None, None
    return _sconv_call(x, mods, li, cond_of_row, seq, P), None, None


def _run_passes(passes, mods, P):
    new_cache = None
    for li in range(DEPTH):
        pending = []
        for ps in passes:
            ps["x"], pre, made = _mixer(ps["x"], mods, li, ps["cond_of_row"], ps["batch"], ps["seq"], P,
                                        ps["cache"])
            pending.append(pre)
            if made is not None and ps["cache"] is None:
                new_cache = made
        xs, outs = lax.optimization_barrier(([ps["x"] for ps in passes],
                                              [None if pre is None else pre[0] for pre in pending]))
        pending = [None if pre is None else (o,) + pre[1:] for pre, o in zip(pending, outs)]
        xs = [_ffn_call(x, mods, li, ps["cond_of_row"], ps["seq"], P, pre, final_norm=(li == DEPTH - 1))
              for x, ps, pre in zip(xs, passes, pending)]
        xs = lax.optimization_barrier(xs)
        for ps, x in zip(passes, xs):
            ps["x"] = x
    return new_cache


def kernel(x_prompt, x_sample, cache_ckv, cache_krope, c, c_ctx, norm1_w, norm2_w, ada_w, ada_b, mla_wdq, mla_q_norm, mla_wuq, mla_wdkv, mla_kv_norm, mla_wukv, mla_wo, pool_w, pool_scale, fnet_w, sconv_win, sconv_conv, sconv_wout, ffn_up, ffn_conv_w, ffn_conv_b, ffn_down, final_norm_w):
    row = lambda w: w.reshape(w.shape[0], 1, w.shape[1])
    P = dict(norm1_w=row(norm1_w), norm2_w=row(norm2_w), mla_wdq=mla_wdq, mla_q_norm=mla_q_norm,
             mla_wuq=mla_wuq, mla_wdkv=mla_wdkv, mla_kv_norm=mla_kv_norm, mla_wukv=mla_wukv,
             mla_wo_bf16=mla_wo.astype(BF16),
             pool_w=pool_w, pool_scale=row(pool_scale), fnet_w=fnet_w, sconv_conv=sconv_conv,
             ffn_conv_w=ffn_conv_w, ffn_conv_b=ffn_conv_b, final_norm_w=final_norm_w,
             ffn_w=functools.cache(lambda l: _layer_weights(ffn_up, ffn_down, l, 2, f"ffn{l}_weights")),
             sconv_w=functools.cache(lambda l: _layer_weights(sconv_win, sconv_wout, l, 3, f"sconv{l}_weights")))
    b_ctx, t_ctx, _ = x_prompt.shape
    b_lat, t_lat, _ = x_sample.shape
    assert ROW_TILE % t_ctx == 0 and t_lat % ROW_TILE == 0 and b_lat + 1 <= N_COND

    conds = jnp.concatenate([c_ctx[None, :], c, jnp.zeros((N_COND - 1 - b_lat, D_MODEL), F32)], axis=0)
    mods = _ada_call(conds, ada_w, ada_b).reshape(DEPTH, N_COND, 6, D_MODEL)

    ctx = dict(x=x_prompt.reshape(b_ctx * t_ctx, D_MODEL), cond_of_row=lambda r: 0, batch=b_ctx, seq=t_ctx,
               cache=None)
    lat = dict(x=x_sample.reshape(b_lat * t_lat, D_MODEL), cond_of_row=lambda r: 1 + r // t_lat, batch=b_lat,
               seq=t_lat, cache=(cache_ckv[:, 0], cache_krope[:, 0]))
    ckv, krope = _run_passes([ctx, lat], mods, P)
    return (ctx["x"].reshape(b_ctx, t_ctx, D_MODEL), lat["x"].reshape(b_lat, t_lat, D_MODEL),
            ckv.reshape(b_ctx, 1, t_ctx, KV_LORA), krope.reshape(b_ctx, 1, t_ctx, QK_ROPE))
```

```python
import functools
import math

import numpy as np
import jax
import jax.numpy as jnp
from jax import lax
from jax.experimental import pallas as pl
from jax.experimental.pallas import tpu as pltpu

F32 = jnp.float32
BF16 = jnp.bfloat16

D_MODEL = 1024
N_HEADS = 16
QK_NOPE = 64
QK_ROPE = 32
V_HEAD = 64
Q_LORA = 384
KV_LORA = 256
ROPE_THETA = 10000.0
GRID_W = 64
POOL_WINDOWS = (2, 4, 8, 16)
N_GROUPS = 4
GROUP = D_MODEL // N_GROUPS
D_FF = 2816
EPS = 1e-6
DEPTH = 4

LANES = 128
Q_SCALE = np.float32(math.log2(math.e) / math.sqrt(QK_NOPE + QK_ROPE))
HEAD_PAIRS = N_HEADS // 2
N_COND = 8
ROW_TILE = 1024
PROJ_TILE = 1024
FF_TILE = 256
ROW_BLOCK = 256
HALO = 8
RELAYOUT_ROWS = 128
CHUNK_SLOTS = 16
ADA_TILE = 1536
MIB = 1024 * 1024
VMEM_DEFAULT = 48 * MIB
VMEM_HEADROOM = 4 * MIB + 512 * 1024
ATTN_VMEM = 48 * MIB


def _params(sem, vmem_bytes=VMEM_DEFAULT):
    return pltpu.CompilerParams(dimension_semantics=sem, vmem_limit_bytes=int(vmem_bytes))


def _nbytes(a):
    return math.prod(a.shape) * jnp.dtype(a.dtype).itemsize


def _dot(a, b):
    return jnp.dot(a, b, preferred_element_type=F32)


def _rms(x, w):
    return x * lax.rsqrt(jnp.mean(x * x, axis=-1, keepdims=True) + EPS) * w


def _norm_mod(x, nw, mod, base):
    return _rms(x, nw) * (1.0 + mod[base + 1:base + 2, :]) + mod[base:base + 1, :]


def _gelu_exact(x):
    return 0.5 * x * (1.0 + lax.erf(x * np.float32(math.sqrt(0.5))))


def _ada_kernel(c_ref, w_ref, b_ref, o_ref):
    s = jax.nn.silu(c_ref[...]).astype(BF16)
    o_ref[...] = _dot(s, w_ref[...].astype(BF16)) + b_ref[...]


def _ada_call(conds, ada_w, ada_b):
    n_out = ada_w.shape[-1]
    return pl.pallas_call(
        _ada_kernel,
        grid=(DEPTH, n_out // ADA_TILE),
        in_specs=[
            pl.BlockSpec((N_COND, D_MODEL), lambda l, j: (0, 0)),
            pl.BlockSpec((None, D_MODEL, ADA_TILE), lambda l, j: (l, 0, j)),
            pl.BlockSpec((None, 1, ADA_TILE), lambda l, j: (l, 0, j)),
        ],
        out_specs=pl.BlockSpec((None, N_COND, ADA_TILE), lambda l, j: (l, 0, j)),
        out_shape=jax.ShapeDtypeStruct((DEPTH, N_COND, n_out), F32),
        compiler_params=_params(("arbitrary", "arbitrary"), 2 * D_MODEL * ADA_TILE * 4 + VMEM_HEADROOM),
        name="ada",
    )(conds, ada_w, ada_b.reshape(DEPTH, 1, n_out))


def _mod_spec(li, cond_of_row, tile, n_grid):
    if n_grid == 1:
        return pl.BlockSpec((None, None, 6, D_MODEL), lambda i: (li, cond_of_row(i * tile), 0, 0))
    return pl.BlockSpec((None, None, 6, D_MODEL), lambda i, j: (li, cond_of_row(i * tile), 0, 0))


def _conv_window(w, cw, t0, seq):
    n = w.shape[0]
    mid = slice(HALO, HALO + ROW_BLOCK)
    t = (t0 + lax.broadcasted_iota(jnp.int32, (ROW_BLOCK, 1), 0)) & (seq - 1)
    prev = jnp.where(t == 0, 0.0, pltpu.roll(w, 1, axis=0)[mid])
    nxt = jnp.where(t == seq - 1, 0.0, pltpu.roll(w, n - 1, axis=0)[mid])
    return prev * cw[0:1, :] + w[mid] * cw[1:2, :] + nxt * cw[2:3, :]


def _ffn_act(win, mid, cw, cb, t0, seq):
    return _gelu_exact(_conv_window(win(0), cw, t0, seq) + cb[0:1, :]) * mid(1)


def _sconv_act(win, mid, cw, cb, t0, seq):
    return mid(0) * _conv_window(win(1) * win(2), cw, t0, seq)


def _ffn_kernel(x_ref, mod_ref, nw_ref, up_ref, cw_ref, cb_ref, down_ref, fnw_ref, o_ref, *scratch, **cfg):
    _glu_body(x_ref, mod_ref, nw_ref, up_ref, cw_ref, cb_ref, down_ref, fnw_ref, None, o_ref, *scratch,
              n_up=2, act=_ffn_act, mod_base=3, **cfg)


def _ffn_pre_kernel(x_ref, mod_ref, nw_ref, up_ref, cw_ref, cb_ref, down_ref, fnw_ref, a_ref, wa_ref, o_ref,
                    *scratch, **cfg):
    _glu_body(x_ref, mod_ref, nw_ref, up_ref, cw_ref, cb_ref, down_ref, fnw_ref, (a_ref, wa_ref), o_ref,
              *scratch, n_up=2, act=_ffn_act, mod_base=3, **cfg)


def _sconv_kernel(x_ref, mod_ref, nw_ref, up_ref, cw_ref, down_ref, o_ref, *scratch, **cfg):
    _glu_body(x_ref, mod_ref, nw_ref, up_ref, cw_ref, None, down_ref, None, None, o_ref, *scratch,
              n_up=3, act=_sconv_act, mod_base=0, final_norm=False, **cfg)


def _glu_body(x_ref, mod_ref, nw_ref, up_ref, cw_ref, cb_ref, down_ref, fnw_ref, pre, o_ref,
              h_scr, acc_scr, gu0, gu1, *, n_up, n_chunks, act, seq, mod_base, final_norm):
    gu = (gu0, gu1)
    n_blocks = ROW_TILE // ROW_BLOCK

    def up(c, slot, r):
        rows = slice(r * ROW_BLOCK, (r + 1) * ROW_BLOCK)
        res = _dot(h_scr[rows, :], up_ref[c])
        for p in range(n_up):
            gu[slot][p, HALO + r * ROW_BLOCK:HALO + (r + 1) * ROW_BLOCK, :] = res[:, p * FF_TILE:(p + 1) * FF_TILE]

    def activate(c, slot, r):
        win = lambda p: gu[slot][p, r * ROW_BLOCK:(r + 1) * ROW_BLOCK + 2 * HALO, :]
        mid = lambda p: gu[slot][p, HALO + r * ROW_BLOCK:HALO + (r + 1) * ROW_BLOCK, :]
        cb = None if cb_ref is None else cb_ref[c]
        return act(win, mid, cw_ref[c], cb, r * ROW_BLOCK, seq).astype(BF16)

    def down(a, c, r):
        rows = slice(r * ROW_BLOCK, (r + 1) * ROW_BLOCK)
        y = _dot(a, down_ref[c])
        if isinstance(c, int) and c == 0:
            acc_scr[rows, :] = y
        else:
            acc_scr[rows, :] += y

    def stage(s, slot):
        for r in range(n_blocks):
            a = activate(s - 1, 1 - slot, r)
            up(s, slot, r)
            down(a, s - 1, r)

    def block(r):
        return slice(r * ROW_BLOCK, (r + 1) * ROW_BLOCK)

    def mixer_proj(r):
        a_ref, wa_ref = pre
        return _dot(a_ref[block(r), :], wa_ref[...])

    for g in gu:
        g[:, :HALO, :] = jnp.zeros((n_up, HALO, FF_TILE), F32)
        g[:, HALO + ROW_TILE:, :] = jnp.zeros((n_up, HALO, FF_TILE), F32)

    proj = mixer_proj(0) if pre is not None else None
    for r in range(n_blocks):
        x = x_ref[block(r), :]
        if pre is not None:
            x = x + mod_ref[2:3, :] * proj
            o_ref[block(r), :] = x
            if r + 1 < n_blocks:
                proj = mixer_proj(r + 1)
        h_scr[block(r), :] = _norm_mod(x, nw_ref[...], mod_ref[...], mod_base).astype(BF16)
        up(0, 0, r)

    if n_chunks > 1:
        stage(1, 1)
    n_pairs = (n_chunks - 2) // 2

    def pair(i, carry):
        s = 2 * i + 2
        stage(s, 0)
        stage(s + 1, 1)
        return carry

    lax.fori_loop(0, n_pairs, pair, 0)
    for s in range(2 * n_pairs + 2, n_chunks):
        stage(s, s % 2)
    last = n_chunks - 1
    res = x_ref if pre is None else o_ref
    for r in range(n_blocks):
        down(activate(last, last % 2, r), last, r)
        y = res[block(r), :] + mod_ref[mod_base + 2:mod_base + 3, :] * acc_scr[block(r), :]
        if final_norm:
            y = _rms(y, fnw_ref[0:1, :])
        o_ref[block(r), :] = y


def _chunked(w, n_chunks):
    w = w.reshape(w.shape[0], n_chunks, FF_TILE).transpose(1, 0, 2)
    return jnp.pad(w, ((0, CHUNK_SLOTS - n_chunks), (0, 8 - w.shape[1]), (0, 0)))


def _glu_call(kernel_fn, name, x, mods, li, cond_of_row, n_up, n_chunks, norm_w, operands, specs, cfg):
    m = x.shape[0]
    whole = lambda a: pl.BlockSpec(a.shape, lambda i: (0,) * a.ndim)
    in_specs = [pl.BlockSpec((ROW_TILE, D_MODEL), lambda i: (i, 0)),
                _mod_spec(li, cond_of_row, ROW_TILE, 1),
                pl.BlockSpec((None, 1, D_MODEL), lambda i: (li, 0, 0))]
    layer_w = lambda a, l: pl.BlockSpec((None,) + a.shape[1:], lambda i: (l,) + (0,) * (a.ndim - 1),
                                        pipeline_mode=pl.Buffered(1))
    rows = lambda a: pl.BlockSpec((ROW_TILE, a.shape[1]), lambda i: (i, 0))
    spec_of = {None: whole, "rows": rows, "resident": lambda a: pl.BlockSpec(memory_space=pltpu.VMEM)}
    in_specs += [layer_w(a, s) if isinstance(s, int) else spec_of[s](a) for a, s in zip(operands, specs)]
    gu_shape = (n_up, ROW_TILE + 2 * HALO, FF_TILE)
    tile = ROW_TILE * D_MODEL
    window = {None: _nbytes, "rows": lambda a: 2 * ROW_TILE * a.shape[1] * a.dtype.itemsize,
              "resident": lambda a: 0}
    vmem = (2 * 2 * tile * 4 + tile * 2 + tile * 4 + 2 * math.prod(gu_shape) * 4 + VMEM_HEADROOM
            + sum(_nbytes(a) // a.shape[0] if isinstance(s, int) else window[s](a)
                  for a, s in zip(operands, specs)))
    return pl.pallas_call(
        functools.partial(kernel_fn, n_chunks=n_chunks, **cfg),
        grid=(m // ROW_TILE,),
        in_specs=in_specs,
        out_specs=pl.BlockSpec((ROW_TILE, D_MODEL), lambda i: (i, 0)),
        out_shape=jax.ShapeDtypeStruct((m, D_MODEL), F32),
        scratch_shapes=[
            pltpu.VMEM((ROW_TILE, D_MODEL), BF16),
            pltpu.VMEM((ROW_TILE, D_MODEL), F32),
            pltpu.VMEM(gu_shape, F32), pltpu.VMEM(gu_shape, F32),
        ],
        compiler_params=_params(("arbitrary",), vmem),
        name=name,
    )(x, mods, norm_w, *operands)


def _in_vmem(a):
    return pltpu.with_memory_space_constraint(a, pltpu.VMEM)


def _layer_weights_kernel(up_ref, dn_ref, uo_ref, do_ref, *, n_up, n_chunks):
    for c in range(n_chunks):
        for p in range(n_up):
            col = (p * n_chunks + c) * FF_TILE
            uo_ref[c, :, p * FF_TILE:(p + 1) * FF_TILE] = up_ref[:, col:col + FF_TILE].astype(BF16)
    do_ref[...] = dn_ref[...].astype(BF16)


def _layer_weights(up, down, layer, n_up, name):
    _, d, cols = up.shape
    _, hidden, d_out = down.shape
    n_chunks = hidden // FF_TILE
    steps = d // RELAYOUT_ROWS
    dn_rows = hidden // steps
    up_cm, dn = pl.pallas_call(
        functools.partial(_layer_weights_kernel, n_up=n_up, n_chunks=n_chunks),
        grid=(steps,),
        in_specs=[pl.BlockSpec((None, RELAYOUT_ROWS, cols), lambda k: (layer, k, 0)),
                  pl.BlockSpec((None, dn_rows, d_out), lambda k: (layer, k, 0))],
        out_specs=[pl.BlockSpec((n_chunks, RELAYOUT_ROWS, n_up * FF_TILE), lambda k: (0, k, 0)),
                   pl.BlockSpec((dn_rows, d_out), lambda k: (k, 0))],
        out_shape=[jax.ShapeDtypeStruct((n_chunks, d, n_up * FF_TILE), BF16),
                   jax.ShapeDtypeStruct((hidden, d_out), BF16)],
        compiler_params=_params(("arbitrary",),
                                2 * (RELAYOUT_ROWS * cols + dn_rows * d_out) * (4 + 2) + VMEM_HEADROOM),
        name=name,
    )(up, down)
    return _in_vmem(up_cm), _in_vmem(dn.reshape(n_chunks, FF_TILE, d_out))


def _ffn_call(x, mods, li, cond_of_row, seq, P, pre, final_norm):
    n = D_FF // FF_TILE
    up_cm, down_cm = P["ffn_w"](li)
    operands = [up_cm, _chunked(P["ffn_conv_w"][li], n), _chunked(P["ffn_conv_b"][li][None, :], n),
                down_cm, jnp.pad(P["final_norm_w"].reshape(1, D_MODEL), ((0, 7), (0, 0)))]
    specs = ["resident", None, None, "resident", None]
    kernel_fn = _ffn_kernel
    if pre is not None:
        a, w_a, l_a = pre
        operands += [a, w_a]
        specs += ["rows", l_a]
        kernel_fn = _ffn_pre_kernel
    return _glu_call(kernel_fn, f"ffn{li}", x, mods, li, cond_of_row, 2, n, P["norm2_w"],
                     operands, specs, dict(seq=seq, final_norm=final_norm))


def _sconv_call(x, mods, li, cond_of_row, seq, P):
    lj = li // 4
    n = D_MODEL // FF_TILE
    up_cm, down_cm = P["sconv_w"](lj)
    operands = [up_cm, _chunked(P["sconv_conv"][lj], n), down_cm]
    return _glu_call(_sconv_kernel, f"sconv{li}", x, mods, li, cond_of_row, 3, n, P["norm1_w"],
                     operands, ["resident", None, "resident"], dict(seq=seq))


def _emit_kv(ckv, krope_pad, wk_ref, wv_ref, k_ref, v_ref):
    c = ckv.astype(BF16)
    kx = _dot(c, wk_ref[...])
    vx = _dot(c, wv_ref[...])
    for h in range(N_HEADS):
        k_ref[h] = (kx[:, h * LANES:(h + 1) * LANES] + krope_pad).astype(BF16)
    for p in range(HEAD_PAIRS):
        v_ref[p] = vx[:, p * LANES:(p + 1) * LANES].astype(BF16)


def _mla_proj_kernel(x_ref, mod_ref, nw_ref, w1_ref, qn_ref, kvn_ref, w2_ref, wk_ref, wv_ref, cos_ref,
                     sin_ref, q_ref, k_ref, v_ref, ckv_ref, kr_ref, *, rope):
    h = _norm_mod(x_ref[...], nw_ref[...], mod_ref[...], 0).astype(BF16)
    z = _dot(h, w1_ref[...])
    cq = _rms(z[:, :Q_LORA], qn_ref[...]).astype(BF16)
    ckv = _rms(z[:, Q_LORA:Q_LORA + KV_LORA], kvn_ref[...])
    kr = z[:, Q_LORA + KV_LORA:Q_LORA + KV_LORA + LANES]
    if rope:
        cos, sin = cos_ref[...], sin_ref[...]
        kr = kr * cos + z[:, Q_LORA + KV_LORA + LANES:] * sin
    ckv_ref[...] = ckv
    kr_ref[...] = kr[:, QK_NOPE:QK_NOPE + QK_ROPE]
    c = ckv.astype(BF16)
    off = N_HEADS * LANES

    def pair_matmuls(p):
        cols = slice(2 * p * LANES, (2 * p + 2) * LANES)
        q2 = _dot(cq, w2_ref[:, cols])
        qs = _dot(cq, w2_ref[:, off + 2 * p * LANES:off + (2 * p + 2) * LANES]) if rope else None
        k2 = _dot(c, wk_ref[:, cols])
        v2 = _dot(c, wv_ref[:, p * LANES:(p + 2) * LANES]) if p % 2 == 0 else None
        return q2, qs, k2, v2

    nxt = pair_matmuls(0)
    for p in range(HEAD_PAIRS):
        q2, qs, k2, v2 = nxt
        if p + 1 < HEAD_PAIRS:
            nxt = pair_matmuls(p + 1)
        for e in range(2):
            lanes = slice(e * LANES, (e + 1) * LANES)
            qh = q2[:, lanes]
            if rope:
                qh = qh * cos + qs[:, lanes] * sin
            q_ref[2 * p + e] = (qh * Q_SCALE).astype(BF16)
            k_ref[2 * p + e] = (k2[:, lanes] + kr).astype(BF16)
        if v2 is not None:
            v_ref[p] = v2[:, :LANES].astype(BF16)
            v_ref[p + 1] = v2[:, LANES:].astype(BF16)


def _mla_proj_call(x, mods, li, cond_of_row, W, rope):
    m = x.shape[0]
    n1 = W["w1"].shape[1]
    n2 = W["w2"].shape[1]
    const = lambda i: (0, 0)
    pos_tiles = W["cos"].shape[0] // PROJ_TILE
    pos = lambda i: (i % pos_tiles, 0)
    return pl.pallas_call(
        functools.partial(_mla_proj_kernel, rope=rope),
        grid=(m // PROJ_TILE,),
        in_specs=[
            pl.BlockSpec((PROJ_TILE, D_MODEL), lambda i: (i, 0)),
            _mod_spec(li, cond_of_row, PROJ_TILE, 1),
            pl.BlockSpec((None, 1, D_MODEL), lambda i: (li, 0, 0)),
            pl.BlockSpec((D_MODEL, n1), const),
            pl.BlockSpec((1, Q_LORA), const),
            pl.BlockSpec((1, KV_LORA), const),
            pl.BlockSpec((Q_LORA, n2), const),
            pl.BlockSpec((KV_LORA, N_HEADS * LANES), const),
            pl.BlockSpec((KV_LORA, HEAD_PAIRS * LANES), const),
            pl.BlockSpec((PROJ_TILE, LANES), pos),
            pl.BlockSpec((PROJ_TILE, LANES), pos),
        ],
        out_specs=[
            pl.BlockSpec((N_HEADS, PROJ_TILE, LANES), lambda i: (0, i, 0)),
            pl.BlockSpec((N_HEADS, PROJ_TILE, LANES), lambda i: (0, i, 0)),
            pl.BlockSpec((HEAD_PAIRS, PROJ_TILE, LANES), lambda i: (0, i, 0)),
            pl.BlockSpec((PROJ_TILE, KV_LORA), lambda i: (i, 0)),
            pl.BlockSpec((PROJ_TILE, QK_ROPE), lambda i: (i, 0)),
        ],
        out_shape=[
            jax.ShapeDtypeStruct((N_HEADS, m, LANES), BF16),
            jax.ShapeDtypeStruct((N_HEADS, m, LANES), BF16),
            jax.ShapeDtypeStruct((HEAD_PAIRS, m, LANES), BF16),
            jax.ShapeDtypeStruct((m, KV_LORA), F32),
            jax.ShapeDtypeStruct((m, QK_ROPE), F32),
        ],
        compiler_params=_params(("arbitrary",)),
        name="mla_proj_rope" if rope else "mla_proj",
    )(x, mods, W["norm1_w"], W["w1"], W["q_norm"], W["kv_norm"], W["w2"], W["wk"], W["wv"],
      W["cos"], W["sin"])


def _kv_cache_kernel(ckv_ref, kr_ref, wk_ref, wv_ref, k_ref, v_ref):
    _emit_kv(ckv_ref[...], kr_ref[...], wk_ref, wv_ref, k_ref, v_ref)


def _kv_cache_call(ckv, krope_pad, W):
    m = ckv.shape[0]
    const = lambda i: (0, 0)
    return pl.pallas_call(
        _kv_cache_kernel,
        grid=(1,),
        in_specs=[
            pl.BlockSpec((m, KV_LORA), const),
            pl.BlockSpec((m, LANES), const),
            pl.BlockSpec((KV_LORA, N_HEADS * LANES), const),
            pl.BlockSpec((KV_LORA, HEAD_PAIRS * LANES), const),
        ],
        out_specs=[
            pl.BlockSpec((N_HEADS, m, LANES), lambda i: (0, 0, 0)),
            pl.BlockSpec((HEAD_PAIRS, m, LANES), lambda i: (0, 0, 0)),
        ],
        out_shape=[
            jax.ShapeDtypeStruct((N_HEADS, m, LANES), BF16),
            jax.ShapeDtypeStruct((HEAD_PAIRS, m, LANES), BF16),
        ],
        compiler_params=_params(("arbitrary",)),
        name="kv_cache",
    )(ckv, krope_pad, W["wk"], W["wv"])


def _attn_kernel(q_ref, *refs, n_pairs):
    o_ref, kv = refs[-1], refs[:-1]
    parts = [(kv[i], kv[i + 1]) for i in range(0, len(kv), 2)]
    heads = [(p, e) for p in range(n_pairs) for e in range(2)]

    def scores(p, e):
        q = q_ref[2 * p + e]
        return [lax.dot_general(q, k_ref[2 * p + e], (((1,), (1,)), ((), ())), preferred_element_type=F32)
                for k_ref, _ in parts]

    outs = []
    s_next = scores(*heads[0])
    for i, (p, e) in enumerate(heads):
        s = s_next
        if i + 1 < len(heads):
            s_next = scores(*heads[i + 1])
        m = functools.reduce(jnp.maximum, [jnp.max(x, axis=-1, keepdims=True) for x in s])
        w = [jnp.exp2(x - m) for x in s]
        l = sum(jnp.sum(x, axis=-1, keepdims=True) for x in w)
        pv = sum(_dot(x.astype(BF16), v_ref[p]) for x, (_, v_ref) in zip(w, parts))
        outs.append(pv / l)
        if e == 1:
            lane = lax.broadcasted_iota(jnp.int32, pv.shape, 1)
            o_ref[:, p * LANES:(p + 1) * LANES] = jnp.where(lane < V_HEAD, outs[-2], outs[-1]).astype(BF16)


def _attn_call(q, kv_parts, q_tile, n_pairs):
    _, b, t, _ = q.shape
    nq = t // q_tile
    in_specs = [pl.BlockSpec((2 * n_pairs, None, q_tile, LANES), lambda bi, g, qi: (g, bi, qi, 0))]
    operands = [q]
    for k, v in kv_parts:
        s = k.shape[2]
        in_specs += [pl.BlockSpec((2 * n_pairs, None, s, LANES), lambda bi, g, qi: (g, bi, 0, 0)),
                     pl.BlockSpec((n_pairs, None, s, LANES), lambda bi, g, qi: (g, bi, 0, 0))]
        operands += [k, v]
    return pl.pallas_call(
        functools.partial(_attn_kernel, n_pairs=n_pairs),
        grid=(b, HEAD_PAIRS // n_pairs, nq),
        in_specs=in_specs,
        out_specs=pl.BlockSpec((q_tile, n_pairs * LANES), lambda bi, g, qi: (bi * nq + qi, g)),
        out_shape=jax.ShapeDtypeStruct((b * t, N_HEADS * V_HEAD), BF16),
        compiler_params=_params(("arbitrary", "arbitrary", "arbitrary"), ATTN_VMEM),
        name=f"attn_{len(kv_parts)}part",
    )(*operands)


def _pool_kernel(x_ref, mod_ref, nw_ref, w_ref, sc_ref, o_ref, band_scr, h_scr, hi_scr, lo_scr, y_scr, *, seq):
    @pl.when(pl.program_id(0) == 0)
    def _():
        r = lax.broadcasted_iota(jnp.int32, (seq, seq), 0)
        c = lax.broadcasted_iota(jnp.int32, (seq, seq), 1)
        for g, win in enumerate(POOL_WINDOWS):
            inside = (c >= r - win // 2) & (c < r + win - win // 2)
            band_scr[g] = jnp.where(inside, 1.0, 0.0).astype(BF16)

    items = [(s, g) for s in range(ROW_TILE // seq) for g in range(N_GROUPS)]
    rows = lambda s: slice(s * seq, (s + 1) * seq)
    cols = lambda g: slice(g * GROUP, (g + 1) * GROUP)
    t = lax.broadcasted_iota(jnp.int32, (seq, 1), 0)
    counts = [(jnp.minimum(t + (win - win // 2), seq) - jnp.maximum(t - win // 2, 0)).astype(F32)
              for win in POOL_WINDOWS]

    def norm_seq(s):
        h = _norm_mod(x_ref[rows(s), :], nw_ref[...], mod_ref[...], 0)
        hi = h.astype(BF16)
        h_scr[rows(s), :] = h
        hi_scr[rows(s), :] = hi
        lo_scr[rows(s), :] = (h - hi.astype(F32)).astype(BF16)

    def window_sum(s, g):
        return _dot(band_scr[g], hi_scr[rows(s), cols(g)]) + _dot(band_scr[g], lo_scr[rows(s), cols(g)])

    norm_seq(0)
    tot = window_sum(*items[0])
    for i, (s, g) in enumerate(items):
        tot_next = None
        if i + 1 < len(items):
            s2, g2 = items[i + 1]
            if s2 != s:
                norm_seq(s2)
            tot_next = window_sum(s2, g2)
        pooled = (tot / counts[g] - h_scr[rows(s), cols(g)]).astype(BF16)
        y_scr[rows(s), cols(g)] = _dot(pooled, w_ref[g].astype(BF16))
        tot = tot_next
    for r in range(ROW_TILE // ROW_BLOCK):
        blk = slice(r * ROW_BLOCK, (r + 1) * ROW_BLOCK)
        o_ref[blk, :] = x_ref[blk, :] + mod_ref[2:3, :] * (y_scr[blk, :] * sc_ref[...])


def _pool_call(x, mods, li, cond_of_row, seq, P):
    m = x.shape[0]
    lj = li // 4
    return pl.pallas_call(
        functools.partial(_pool_kernel, seq=seq),
        grid=(m // ROW_TILE,),
        in_specs=[
            pl.BlockSpec((ROW_TILE, D_MODEL), lambda i: (i, 0)),
            _mod_spec(li, cond_of_row, ROW_TILE, 1),
            pl.BlockSpec((None, 1, D_MODEL), lambda i: (li, 0, 0)),
            pl.BlockSpec((None, N_GROUPS, GROUP, GROUP), lambda i: (lj, 0, 0, 0)),
            pl.BlockSpec((None, 1, D_MODEL), lambda i: (lj, 0, 0)),
        ],
        out_specs=pl.BlockSpec((ROW_TILE, D_MODEL), lambda i: (i, 0)),
        out_shape=jax.ShapeDtypeStruct((m, D_MODEL), F32),
        scratch_shapes=[pltpu.VMEM((N_GROUPS, seq, seq), BF16), pltpu.VMEM((ROW_TILE, D_MODEL), F32),
                        pltpu.VMEM((ROW_TILE, D_MODEL), BF16), pltpu.VMEM((ROW_TILE, D_MODEL), BF16),
                        pltpu.VMEM((ROW_TILE, D_MODEL), F32)],
        compiler_params=_params(("arbitrary",)),
        name=f"pool_t{seq}",
    )(x, mods, P["norm1_w"], P["pool_w"], P["pool_scale"])


def _dft_tables(seq):
    def cs(n):
        k = np.arange(n, dtype=np.int64)
        ang = 2.0 * np.pi * ((k[:, None] * k[None, :]) % n) / n
        return np.cos(ang), np.sin(ang)
    ct, st = cs(seq)
    cc, sc = cs(GROUP)
    e_t = jnp.asarray(np.concatenate([ct, st], axis=0), dtype=F32)
    e_c = jnp.asarray(np.concatenate([cc, -sc], axis=0), dtype=F32)
    return e_t.astype(BF16), e_c.astype(BF16)


def _fourier_kernel(x_ref, mod_ref, nw_ref, et_ref, ec_ref, w_ref, o_ref, h_scr, f_scr, wb_scr, *, seq):
    norm = np.float32(1.0 / math.sqrt(seq * GROUP))
    items = [(s, g) for s in range(ROW_TILE // seq) for g in range(N_GROUPS)]
    rows = lambda s: slice(s * seq, (s + 1) * seq)
    cols = lambda g: slice(g * GROUP, (g + 1) * GROUP)

    def norm_seq(s):
        h_scr[rows(s), :] = _norm_mod(x_ref[rows(s), :], nw_ref[...], mod_ref[...], 0).astype(BF16)

    def time_dft(s, g):
        return _dot(et_ref[...], h_scr[rows(s), cols(g)]).astype(BF16)

    norm_seq(0)
    wb_scr[...] = w_ref[...].astype(BF16)
    y = time_dft(*items[0])
    for i, (s, g) in enumerate(items):
        y_next = None
        if i + 1 < len(items):
            s2, g2 = items[i + 1]
            if s2 != s:
                norm_seq(s2)
            y_next = time_dft(s2, g2)
        f = _dot(y[:seq], ec_ref[:GROUP, :]) + _dot(y[seq:], ec_ref[GROUP:, :])
        f_scr[rows(s), cols(g)] = (f * norm).astype(BF16)
        y = y_next
    for r in range(ROW_TILE // ROW_BLOCK):
        blk = slice(r * ROW_BLOCK, (r + 1) * ROW_BLOCK)
        o_ref[blk, :] = x_ref[blk, :] + mod_ref[2:3, :] * _dot(f_scr[blk, :], wb_scr[...])


def _fourier_call(x, mods, li, cond_of_row, seq, P):
    m = x.shape[0]
    lj = li // 4
    e_t, e_c = _dft_tables(seq)
    return pl.pallas_call(
        functools.partial(_fourier_kernel, seq=seq),
        grid=(m // ROW_TILE,),
        in_specs=[
            pl.BlockSpec((ROW_TILE, D_MODEL), lambda i: (i, 0)),
            _mod_spec(li, cond_of_row, ROW_TILE, 1),
            pl.BlockSpec((None, 1, D_MODEL), lambda i: (li, 0, 0)),
            pl.BlockSpec((2 * seq, seq), lambda i: (0, 0)),
            pl.BlockSpec((2 * GROUP, GROUP), lambda i: (0, 0)),
            pl.BlockSpec((None, D_MODEL, D_MODEL), lambda i: (lj, 0, 0)),
        ],
        out_specs=pl.BlockSpec((ROW_TILE, D_MODEL), lambda i: (i, 0)),
        out_shape=jax.ShapeDtypeStruct((m, D_MODEL), F32),
        scratch_shapes=[pltpu.VMEM((ROW_TILE, D_MODEL), BF16), pltpu.VMEM((ROW_TILE, D_MODEL), BF16),
                        pltpu.VMEM((D_MODEL, D_MODEL), BF16)],
        compiler_params=_params(("arbitrary",)),
        name=f"fourier_t{seq}",
    )(x, mods, P["norm1_w"], e_t, e_c, P["fnet_w"])


def _rope_tables(rows):
    r, col = jnp.meshgrid(jnp.arange(rows), jnp.arange(GRID_W), indexing="ij")
    r = r.reshape(-1).astype(F32)
    col = col.reshape(-1).astype(F32)
    n_freq = QK_ROPE // 4
    inv = ROPE_THETA ** (-jnp.arange(n_freq, dtype=F32) / n_freq)
    ang = jnp.concatenate([r[:, None] * inv, col[:, None] * inv], axis=-1)
    cos = jnp.repeat(jnp.cos(ang), 2, axis=-1)
    sin = jnp.repeat(jnp.sin(ang), 2, axis=-1)
    t = cos.shape[0]
    pad = LANES - QK_NOPE - QK_ROPE
    cos_p = jnp.concatenate([jnp.ones((t, QK_NOPE), F32), cos, jnp.ones((t, pad), F32)], axis=-1)
    sin_p = jnp.concatenate([jnp.zeros((t, QK_NOPE), F32), sin, jnp.zeros((t, pad), F32)], axis=-1)
    return cos_p, sin_p


def _pair_swap(w):
    wr = w.reshape(w.shape[:-1] + (w.shape[-1] // 2, 2))
    return jnp.stack([-wr[..., 1], wr[..., 0]], axis=-1).reshape(w.shape)


def _mla_weights(P, j, rope, seq_rows):
    pad = LANES - QK_NOPE - QK_ROPE
    wuq = P["mla_wuq"][j].reshape(Q_LORA, N_HEADS, QK_NOPE + QK_ROPE)
    q_slots = jnp.pad(wuq, ((0, 0), (0, 0), (0, pad))).reshape(Q_LORA, N_HEADS * LANES)
    wdkv = P["mla_wdkv"][j]
    kr_slot = jnp.pad(wdkv[:, KV_LORA:], ((0, 0), (QK_NOPE, pad)))
    parts1 = [P["mla_wdq"][j], wdkv[:, :KV_LORA], kr_slot]
    parts2 = [q_slots]
    if rope:
        parts1.append(jnp.pad(_pair_swap(wdkv[:, KV_LORA:]), ((0, 0), (QK_NOPE, pad))))
        q_sw = jnp.pad(_pair_swap(wuq[:, :, QK_NOPE:]), ((0, 0), (0, 0), (QK_NOPE, pad)))
        parts2.append(q_sw.reshape(Q_LORA, N_HEADS * LANES))
        cos, sin = _rope_tables(seq_rows // GRID_W)
    else:
        cos = sin = jnp.zeros((PROJ_TILE, LANES), F32)
    wukv = P["mla_wukv"][j].reshape(KV_LORA, N_HEADS, QK_NOPE + V_HEAD)
    wk = jnp.pad(wukv[:, :, :QK_NOPE], ((0, 0), (0, 0), (0, LANES - QK_NOPE)))
    return dict(
        norm1_w=P["norm1_w"], q_norm=P["mla_q_norm"][j][None, :], kv_norm=P["mla_kv_norm"][j][None, :],
        w1=jnp.concatenate(parts1, axis=1).astype(BF16), w2=jnp.concatenate(parts2, axis=1).astype(BF16),
        wk=wk.reshape(KV_LORA, N_HEADS * LANES).astype(BF16),
        wv=wukv[:, :, QK_NOPE:].reshape(KV_LORA, N_HEADS * V_HEAD).astype(BF16), cos=cos, sin=sin)


def _mla_layer(x, mods, li, cond_of_row, batch, seq, P, cache):
    j = li // 4
    W = _mla_weights(P, j, cache is not None, seq)
    q, k, v, ckv, krope = _mla_proj_call(x, mods, li, cond_of_row, W, cache is not None)
    q = q.reshape(N_HEADS, batch, seq, LANES)
    k = k.reshape(N_HEADS, batch, seq, LANES)
    v = v.reshape(HEAD_PAIRS, batch, seq, LANES)
    if cache is None:
        o = _attn_call(q, [(k, v)], seq, HEAD_PAIRS)
    else:
        c_ckv, c_krope = cache
        past = c_ckv.shape[1]
        pad = LANES - QK_NOPE - QK_ROPE
        kr_pad = jnp.pad(c_krope.reshape(batch * past, QK_ROPE), ((0, 0), (QK_NOPE, pad)))
        kc, vc = _kv_cache_call(c_ckv.reshape(batch * past, KV_LORA), kr_pad, W)
        cached = (kc.reshape(N_HEADS, batch, past, LANES), vc.reshape(HEAD_PAIRS, batch, past, LANES))
        o = _attn_call(q, [cached, (k, v)], 512, 4)
    return o, ckv, krope


def _mixer(x, mods, li, cond_of_row, batch, seq, P, cache):
    kind = li % 4
    if kind == 0:
        o, ckv, krope = _mla_layer(x, mods, li, cond_of_row, batch, seq, P, cache)
        return x, (o, P["mla_wo_bf16"], li // 4), (ckv, krope)
    if kind == 1:
        return _pool_call(x, mods, li, cond_of_row, seq, P), None, None
    if kind == 2:
        return _fourier_call(x, mods, li, cond_of_row, seq, P), None, None
    return _sconv_call(x, mods, li, cond_of_row, seq, P), None, None


def _run_passes(passes, mods, P):
    new_cache = None
    for li in range(DEPTH):
        pending = []
        for ps in passes:
            ps["x"], pre, made = _mixer(ps["x"], mods, li, ps["cond_of_row"], ps["batch"], ps["seq"], P,
                                        ps["cache"])
            pending.append(pre)
            if made is not None and ps["cache"] is None:
                new_cache = made
        xs, outs = lax.optimization_barrier(([ps["x"] for ps in passes],
                                              [None if pre is None else pre[0] for pre in pending]))
        pending = [None if pre is None else (o,) + pre[1:] for pre, o in zip(pending, outs)]
        xs = [_ffn_call(x, mods, li, ps["cond_of_row"], ps["seq"], P, pre, final_norm=(li == DEPTH - 1))
              for x, ps, pre in zip(xs, passes, pending)]
        xs = lax.optimization_barrier(xs)
        for ps, x in zip(passes, xs):
            ps["x"] = x
    return new_cache


def kernel(x_prompt, x_sample, cache_ckv, cache_krope, c, c_ctx, norm1_w, norm2_w, ada_w, ada_b, mla_wdq, mla_q_norm, mla_wuq, mla_wdkv, mla_kv_norm, mla_wukv, mla_wo, pool_w, pool_scale, fnet_w, sconv_win, sconv_conv, sconv_wout, ffn_up, ffn_conv_w, ffn_conv_b, ffn_down, final_norm_w):
    row = lambda w: w.reshape(w.shape[0], 1, w.shape[1])
    P = dict(norm1_w=row(norm1_w), norm2_w=row(norm2_w), mla_wdq=mla_wdq, mla_q_norm=mla_q_norm,
             mla_wuq=mla_wuq, mla_wdkv=mla_wdkv, mla_kv_norm=mla_kv_norm, mla_wukv=mla_wukv,
             mla_wo_bf16=mla_wo.astype(BF16),
             pool_w=pool_w, pool_scale=row(pool_scale), fnet_w=fnet_w, sconv_conv=sconv_conv,
             ffn_conv_w=ffn_conv_w, ffn_conv_b=ffn_conv_b, final_norm_w=final_norm_w,
             ffn_w=functools.cache(lambda l: _layer_weights(ffn_up, ffn_down, l, 2, f"ffn{l}_weights")),
             sconv_w=functools.cache(lambda l: _layer_weights(sconv_win, sconv_wout, l, 3, f"sconv{l}_weights")))
    b_ctx, t_ctx, _ = x_prompt.shape
    b_lat, t_lat, _ = x_sample.shape
    assert ROW_TILE % t_ctx == 0 and t_lat % ROW_TILE == 0 and b_lat + 1 <= N_COND

    conds = jnp.concatenate([c_ctx[None, :], c, jnp.zeros((N_COND - 1 - b_lat, D_MODEL), F32)], axis=0)
    mods = _ada_call(conds, ada_w, ada_b).reshape(DEPTH, N_COND, 6, D_MODEL)

    ctx = dict(x=x_prompt.reshape(b_ctx * t_ctx, D_MODEL), cond_of_row=lambda r: 0, batch=b_ctx, seq=t_ctx,
               cache=None)
    lat = dict(x=x_sample.reshape(b_lat * t_lat, D_MODEL), cond_of_row=lambda r: 1 + r // t_lat, batch=b_lat,
               seq=t_lat, cache=(cache_ckv[:, 0], cache_krope[:, 0]))
    ckv, krope = _run_passes([ctx, lat], mods, P)
    return (ctx["x"].reshape(b_ctx, t_ctx, D_MODEL), lat["x"].reshape(b_lat, t_lat, D_MODEL),
            ckv.reshape(b_ctx, 1, t_ctx, KV_LORA), krope.reshape(b_ctx, 1, t_ctx, QK_ROPE))
```

```python
import functools
import math

import numpy as np
import jax
import jax.numpy as jnp
from jax import lax
from jax.experimental import pallas as pl
from jax.experimental.pallas import tpu as pltpu

F32 = jnp.float32
BF16 = jnp.bfloat16

D_MODEL = 1024
N_HEADS = 16
QK_NOPE = 64
QK_ROPE = 32
V_HEAD = 64
Q_LORA = 384
KV_LORA = 256
ROPE_THETA = 10000.0
GRID_W = 64
POOL_WINDOWS = (2, 4, 8, 16)
N_GROUPS = 4
GROUP = D_MODEL // N_GROUPS
D_FF = 2816
EPS = 1e-6
DEPTH = 4

LANES = 128
Q_SCALE = np.float32(math.log2(math.e) / math.sqrt(QK_NOPE + QK_ROPE))
HEAD_PAIRS = N_HEADS // 2
N_COND = 8
ROW_TILE = 1024
PROJ_TILE = 512
FF_TILE = 256
ROW_BLOCK = 256
HALO = 8
RELAYOUT_ROWS = 128
CHUNK_SLOTS = 16
ADA_TILE = 3072
MIB = 1024 * 1024
VMEM_DEFAULT = 48 * MIB
VMEM_HEADROOM = 4 * MIB + 512 * 1024
ATTN_VMEM = 48 * MIB


def _params(sem, vmem_bytes=VMEM_DEFAULT):
    return pltpu.CompilerParams(dimension_semantics=sem, vmem_limit_bytes=int(vmem_bytes))


def _nbytes(a):
    return math.prod(a.shape) * jnp.dtype(a.dtype).itemsize


def _dot(a, b):
    return jnp.dot(a, b, preferred_element_type=F32)


def _rms(x, w):
    return x * lax.rsqrt(jnp.mean(x * x, axis=-1, keepdims=True) + EPS) * w


def _norm_mod(x, nw, mod, base):
    return _rms(x, nw) * (1.0 + mod[base + 1:base + 2, :]) + mod[base:base + 1, :]


def _gelu_exact(x):
    return 0.5 * x * (1.0 + lax.erf(x * np.float32(math.sqrt(0.5))))


def _ada_kernel(c_ref, w_ref, b_ref, o_ref):
    s = jax.nn.silu(c_ref[...]).astype(BF16)
    o_ref[...] = _dot(s, w_ref[...].astype(BF16)) + b_ref[...]


def _ada_call(conds, ada_w, ada_b):
    n_out = ada_w.shape[-1]
    return pl.pallas_call(
        _ada_kernel,
        grid=(DEPTH, n_out // ADA_TILE),
        in_specs=[
            pl.BlockSpec((N_COND, D_MODEL), lambda l, j: (0, 0)),
            pl.BlockSpec((None, D_MODEL, ADA_TILE), lambda l, j: (l, 0, j)),
            pl.BlockSpec((None, 1, ADA_TILE), lambda l, j: (l, 0, j)),
        ],
        out_specs=pl.BlockSpec((None, N_COND, ADA_TILE), lambda l, j: (l, 0, j)),
        out_shape=jax.ShapeDtypeStruct((DEPTH, N_COND, n_out), F32),
        compiler_params=_params(("arbitrary", "arbitrary"), 2 * D_MODEL * ADA_TILE * 4 + VMEM_HEADROOM),
        name="ada",
    )(conds, ada_w, ada_b.reshape(DEPTH, 1, n_out))


def _mod_spec(li, cond_of_row, tile, n_grid):
    if n_grid == 1:
        return pl.BlockSpec((None, None, 6, D_MODEL), lambda i: (li, cond_of_row(i * tile), 0, 0))
    return pl.BlockSpec((None, None, 6, D_MODEL), lambda i, j: (li, cond_of_row(i * tile), 0, 0))


def _conv_window(w, cw, t0, seq):
    n = w.shape[0]
    mid = slice(HALO, HALO + ROW_BLOCK)
    t = (t0 + lax.broadcasted_iota(jnp.int32, (ROW_BLOCK, 1), 0)) & (seq - 1)
    prev = jnp.where(t == 0, 0.0, pltpu.roll(w, 1, axis=0)[mid])
    nxt = jnp.where(t == seq - 1, 0.0, pltpu.roll(w, n - 1, axis=0)[mid])
    return prev * cw[0:1, :] + w[mid] * cw[1:2, :] + nxt * cw[2:3, :]


def _ffn_act(win, mid, cw, cb, t0, seq):
    return _gelu_exact(_conv_window(win(0), cw, t0, seq) + cb[0:1, :]) * mid(1)


def _sconv_act(win, mid, cw, cb, t0, seq):
    return mid(0) * _conv_window(win(1) * win(2), cw, t0, seq)


def _ffn_kernel(x_ref, mod_ref, nw_ref, up_ref, cw_ref, cb_ref, down_ref, fnw_ref, o_ref, *scratch, **cfg):
    _glu_body(x_ref, mod_ref, nw_ref, up_ref, cw_ref, cb_ref, down_ref, fnw_ref, None, o_ref, *scratch,
              n_up=2, act=_ffn_act, mod_base=3, **cfg)


def _ffn_pre_kernel(x_ref, mod_ref, nw_ref, up_ref, cw_ref, cb_ref, down_ref, fnw_ref, a_ref, wa_ref, o_ref,
                    *scratch, **cfg):
    _glu_body(x_ref, mod_ref, nw_ref, up_ref, cw_ref, cb_ref, down_ref, fnw_ref, (a_ref, wa_ref), o_ref,
              *scratch, n_up=2, act=_ffn_act, mod_base=3, **cfg)


def _sconv_kernel(x_ref, mod_ref, nw_ref, up_ref, cw_ref, down_ref, o_ref, *scratch, **cfg):
    _glu_body(x_ref, mod_ref, nw_ref, up_ref, cw_ref, None, down_ref, None, None, o_ref, *scratch,
              n_up=3, act=_sconv_act, mod_base=0, final_norm=False, **cfg)


def _glu_body(x_ref, mod_ref, nw_ref, up_ref, cw_ref, cb_ref, down_ref, fnw_ref, pre, o_ref,
              h_scr, acc_scr, gu0, gu1, *, n_up, n_chunks, act, seq, mod_base, final_norm):
    gu = (gu0, gu1)
    n_blocks = ROW_TILE // ROW_BLOCK

    def up(c, slot, r):
        rows = slice(r * ROW_BLOCK, (r + 1) * ROW_BLOCK)
        res = _dot(h_scr[rows, :], up_ref[c])
        for p in range(n_up):
            gu[slot][p, HALO + r * ROW_BLOCK:HALO + (r + 1) * ROW_BLOCK, :] = res[:, p * FF_TILE:(p + 1) * FF_TILE]

    def activate(c, slot, r):
        win = lambda p: gu[slot][p, r * ROW_BLOCK:(r + 1) * ROW_BLOCK + 2 * HALO, :]
        mid = lambda p: gu[slot][p, HALO + r * ROW_BLOCK:HALO + (r + 1) * ROW_BLOCK, :]
        cb = None if cb_ref is None else cb_ref[c]
        return act(win, mid, cw_ref[c], cb, r * ROW_BLOCK, seq).astype(BF16)

    def down(a, c, r):
        rows = slice(r * ROW_BLOCK, (r + 1) * ROW_BLOCK)
        y = _dot(a, down_ref[c])
        if isinstance(c, int) and c == 0:
            acc_scr[rows, :] = y
        else:
            acc_scr[rows, :] += y

    def stage(s, slot):
        for r in range(n_blocks):
            a = activate(s - 1, 1 - slot, r)
            up(s, slot, r)
            down(a, s - 1, r)

    def block(r):
        return slice(r * ROW_BLOCK, (r + 1) * ROW_BLOCK)

    def mixer_proj(r):
        a_ref, wa_ref = pre
        return _dot(a_ref[block(r), :], wa_ref[...])

    for g in gu:
        g[:, :HALO, :] = jnp.zeros((n_up, HALO, FF_TILE), F32)
        g[:, HALO + ROW_TILE:, :] = jnp.zeros((n_up, HALO, FF_TILE), F32)

    proj = mixer_proj(0) if pre is not None else None
    for r in range(n_blocks):
        x = x_ref[block(r), :]
        if pre is not None:
            x = x + mod_ref[2:3, :] * proj
            o_ref[block(r), :] = x
            if r + 1 < n_blocks:
                proj = mixer_proj(r + 1)
        h_scr[block(r), :] = _norm_mod(x, nw_ref[...], mod_ref[...], mod_base).astype(BF16)
        up(0, 0, r)

    if n_chunks > 1:
        stage(1, 1)
    n_pairs = (n_chunks - 2) // 2

    def pair(i, carry):
        s = 2 * i + 2
        stage(s, 0)
        stage(s + 1, 1)
        return carry

    lax.fori_loop(0, n_pairs, pair, 0)
    for s in range(2 * n_pairs + 2, n_chunks):
        stage(s, s % 2)
    last = n_chunks - 1
    res = x_ref if pre is None else o_ref
    for r in range(n_blocks):
        down(activate(last, last % 2, r), last, r)
        y = res[block(r), :] + mod_ref[mod_base + 2:mod_base + 3, :] * acc_scr[block(r), :]
        if final_norm:
            y = _rms(y, fnw_ref[0:1, :])
        o_ref[block(r), :] = y


def _chunked(w, n_chunks):
    w = w.reshape(w.shape[0], n_chunks, FF_TILE).transpose(1, 0, 2)
    return jnp.pad(w, ((0, CHUNK_SLOTS - n_chunks), (0, 8 - w.shape[1]), (0, 0)))


def _glu_call(kernel_fn, name, x, mods, li, cond_of_row, n_up, n_chunks, norm_w, operands, specs, cfg):
    m = x.shape[0]
    whole = lambda a: pl.BlockSpec(a.shape, lambda i: (0,) * a.ndim)
    in_specs = [pl.BlockSpec((ROW_TILE, D_MODEL), lambda i: (i, 0)),
                _mod_spec(li, cond_of_row, ROW_TILE, 1),
                pl.BlockSpec((None, 1, D_MODEL), lambda i: (li, 0, 0))]
    layer_w = lambda a, l: pl.BlockSpec((None,) + a.shape[1:], lambda i: (l,) + (0,) * (a.ndim - 1),
                                        pipeline_mode=pl.Buffered(1))
    rows = lambda a: pl.BlockSpec((ROW_TILE, a.shape[1]), lambda i: (i, 0))
    spec_of = {None: whole, "rows": rows, "resident": lambda a: pl.BlockSpec(memory_space=pltpu.VMEM)}
    in_specs += [layer_w(a, s) if isinstance(s, int) else spec_of[s](a) for a, s in zip(operands, specs)]
    gu_shape = (n_up, ROW_TILE + 2 * HALO, FF_TILE)
    tile = ROW_TILE * D_MODEL
    window = {None: _nbytes, "rows": lambda a: 2 * ROW_TILE * a.shape[1] * a.dtype.itemsize,
              "resident": lambda a: 0}
    vmem = (2 * 2 * tile * 4 + tile * 2 + tile * 4 + 2 * math.prod(gu_shape) * 4 + VMEM_HEADROOM
            + sum(_nbytes(a) // a.shape[0] if isinstance(s, int) else window[s](a)
                  for a, s in zip(operands, specs)))
    return pl.pallas_call(
        functools.partial(kernel_fn, n_chunks=n_chunks, **cfg),
        grid=(m // ROW_TILE,),
        in_specs=in_specs,
        out_specs=pl.BlockSpec((ROW_TILE, D_MODEL), lambda i: (i, 0)),
        out_shape=jax.ShapeDtypeStruct((m, D_MODEL), F32),
        scratch_shapes=[
            pltpu.VMEM((ROW_TILE, D_MODEL), BF16),
            pltpu.VMEM((ROW_TILE, D_MODEL), F32),
            pltpu.VMEM(gu_shape, F32), pltpu.VMEM(gu_shape, F32),
        ],
        compiler_params=_params(("arbitrary",), vmem),
        name=name,
    )(x, mods, norm_w, *operands)


def _in_vmem(a):
    return pltpu.with_memory_space_constraint(a, pltpu.VMEM)


def _layer_weights_kernel(up_ref, dn_ref, uo_ref, do_ref, *, n_up, n_chunks):
    for c in range(n_chunks):
        for p in range(n_up):
            col = (p * n_chunks + c) * FF_TILE
            uo_ref[c, :, p * FF_TILE:(p + 1) * FF_TILE] = up_ref[:, col:col + FF_TILE].astype(BF16)
    do_ref[...] = dn_ref[...].astype(BF16)


def _layer_weights(up, down, layer, n_up, name):
    _, d, cols = up.shape
    _, hidden, d_out = down.shape
    n_chunks = hidden // FF_TILE
    steps = d // RELAYOUT_ROWS
    dn_rows = hidden // steps
    up_cm, dn = pl.pallas_call(
        functools.partial(_layer_weights_kernel, n_up=n_up, n_chunks=n_chunks),
        grid=(steps,),
        in_specs=[pl.BlockSpec((None, RELAYOUT_ROWS, cols), lambda k: (layer, k, 0)),
                  pl.BlockSpec((None, dn_rows, d_out), lambda k: (layer, k, 0))],
        out_specs=[pl.BlockSpec((n_chunks, RELAYOUT_ROWS, n_up * FF_TILE), lambda k: (0, k, 0)),
                   pl.BlockSpec((dn_rows, d_out), lambda k: (k, 0))],
        out_shape=[jax.ShapeDtypeStruct((n_chunks, d, n_up * FF_TILE), BF16),
                   jax.ShapeDtypeStruct((hidden, d_out), BF16)],
        compiler_params=_params(("arbitrary",),
                                2 * (RELAYOUT_ROWS * cols + dn_rows * d_out) * (4 + 2) + VMEM_HEADROOM),
        name=name,
    )(up, down)
    return _in_vmem(up_cm), _in_vmem(dn.reshape(n_chunks, FF_TILE, d_out))


def _ffn_call(x, mods, li, cond_of_row, seq, P, pre, final_norm):
    n = D_FF // FF_TILE
    up_cm, down_cm = P["ffn_w"](li)
    operands = [up_cm, _chunked(P["ffn_conv_w"][li], n), _chunked(P["ffn_conv_b"][li][None, :], n),
                down_cm, jnp.pad(P["final_norm_w"].reshape(1, D_MODEL), ((0, 7), (0, 0)))]
    specs = ["resident", None, None, "resident", None]
    kernel_fn = _ffn_kernel
    if pre is not None:
        a, w_a, l_a = pre
        operands += [a, w_a]
        specs += ["rows", l_a]
        kernel_fn = _ffn_pre_kernel
    return _glu_call(kernel_fn, f"ffn{li}", x, mods, li, cond_of_row, 2, n, P["norm2_w"],
                     operands, specs, dict(seq=seq, final_norm=final_norm))


def _sconv_call(x, mods, li, cond_of_row, seq, P):
    lj = li // 4
    n = D_MODEL // FF_TILE
    up_cm, down_cm = P["sconv_w"](lj)
    operands = [up_cm, _chunked(P["sconv_conv"][lj], n), down_cm]
    return _glu_call(_sconv_kernel, f"sconv{li}", x, mods, li, cond_of_row, 3, n, P["norm1_w"],
                     operands, ["resident", None, "resident"], dict(seq=seq))


def _emit_kv(ckv, krope_pad, wk_ref, wv_ref, k_ref, v_ref):
    c = ckv.astype(BF16)
    kx = _dot(c, wk_ref[...])
    vx = _dot(c, wv_ref[...])
    for h in range(N_HEADS):
        k_ref[h] = (kx[:, h * LANES:(h + 1) * LANES] + krope_pad).astype(BF16)
    for p in range(HEAD_PAIRS):
        v_ref[p] = vx[:, p * LANES:(p + 1) * LANES].astype(BF16)


def _mla_proj_kernel(x_ref, mod_ref, nw_ref, w1_ref, qn_ref, kvn_ref, w2_ref, wk_ref, wv_ref, cos_ref,
                     sin_ref, q_ref, k_ref, v_ref, ckv_ref, kr_ref, *, rope):
    h = _norm_mod(x_ref[...], nw_ref[...], mod_ref[...], 0).astype(BF16)
    z = _dot(h, w1_ref[...])
    cq = _rms(z[:, :Q_LORA], qn_ref[...]).astype(BF16)
    ckv = _rms(z[:, Q_LORA:Q_LORA + KV_LORA], kvn_ref[...])
    kr = z[:, Q_LORA + KV_LORA:Q_LORA + KV_LORA + LANES]
    if rope:
        cos, sin = cos_ref[...], sin_ref[...]
        kr = kr * cos + z[:, Q_LORA + KV_LORA + LANES:] * sin
    ckv_ref[...] = ckv
    kr_ref[...] = kr[:, QK_NOPE:QK_NOPE + QK_ROPE]
    c = ckv.astype(BF16)
    off = N_HEADS * LANES

    def pair_matmuls(p):
        cols = slice(2 * p * LANES, (2 * p + 2) * LANES)
        q2 = _dot(cq, w2_ref[:, cols])
        qs = _dot(cq, w2_ref[:, off + 2 * p * LANES:off + (2 * p + 2) * LANES]) if rope else None
        k2 = _dot(c, wk_ref[:, cols])
        v2 = _dot(c, wv_ref[:, p * LANES:(p + 2) * LANES]) if p % 2 == 0 else None
        return q2, qs, k2, v2

    nxt = pair_matmuls(0)
    for p in range(HEAD_PAIRS):
        q2, qs, k2, v2 = nxt
        if p + 1 < HEAD_PAIRS:
            nxt = pair_matmuls(p + 1)
        for e in range(2):
            lanes = slice(e * LANES, (e + 1) * LANES)
            qh = q2[:, lanes]
            if rope:
                qh = qh * cos + qs[:, lanes] * sin
            q_ref[2 * p + e] = (qh * Q_SCALE).astype(BF16)
            k_ref[2 * p + e] = (k2[:, lanes] + kr).astype(BF16)
        if v2 is not None:
            v_ref[p] = v2[:, :LANES].astype(BF16)
            v_ref[p + 1] = v2[:, LANES:].astype(BF16)


def _mla_proj_call(x, mods, li, cond_of_row, W, rope):
    m = x.shape[0]
    n1 = W["w1"].shape[1]
    n2 = W["w2"].shape[1]
    const = lambda i: (0, 0)
    pos_tiles = W["cos"].shape[0] // PROJ_TILE
    pos = lambda i: (i % pos_tiles, 0)
    return pl.pallas_call(
        functools.partial(_mla_proj_kernel, rope=rope),
        grid=(m // PROJ_TILE,),
        in_specs=[
            pl.BlockSpec((PROJ_TILE, D_MODEL), lambda i: (i, 0)),
            _mod_spec(li, cond_of_row, PROJ_TILE, 1),
            pl.BlockSpec((None, 1, D_MODEL), lambda i: (li, 0, 0)),
            pl.BlockSpec((D_MODEL, n1), const),
            pl.BlockSpec((1, Q_LORA), const),
            pl.BlockSpec((1, KV_LORA), const),
            pl.BlockSpec((Q_LORA, n2), const),
            pl.BlockSpec((KV_LORA, N_HEADS * LANES), const),
            pl.BlockSpec((KV_LORA, HEAD_PAIRS * LANES), const),
            pl.BlockSpec((PROJ_TILE, LANES), pos),
            pl.BlockSpec((PROJ_TILE, LANES), pos),
        ],
        out_specs=[
            pl.BlockSpec((N_HEADS, PROJ_TILE, LANES), lambda i: (0, i, 0)),
            pl.BlockSpec((N_HEADS, PROJ_TILE, LANES), lambda i: (0, i, 0)),
            pl.BlockSpec((HEAD_PAIRS, PROJ_TILE, LANES), lambda i: (0, i, 0)),
            pl.BlockSpec((PROJ_TILE, KV_LORA), lambda i: (i, 0)),
            pl.BlockSpec((PROJ_TILE, QK_ROPE), lambda i: (i, 0)),
        ],
        out_shape=[
            jax.ShapeDtypeStruct((N_HEADS, m, LANES), BF16),
            jax.ShapeDtypeStruct((N_HEADS, m, LANES), BF16),
            jax.ShapeDtypeStruct((HEAD_PAIRS, m, LANES), BF16),
            jax.ShapeDtypeStruct((m, KV_LORA), F32),
            jax.ShapeDtypeStruct((m, QK_ROPE), F32),
        ],
        compiler_params=_params(("arbitrary",)),
        name="mla_proj_rope" if rope else "mla_proj",
    )(x, mods, W["norm1_w"], W["w1"], W["q_norm"], W["kv_norm"], W["w2"], W["wk"], W["wv"],
      W["cos"], W["sin"])


def _kv_cache_kernel(ckv_ref, kr_ref, wk_ref, wv_ref, k_ref, v_ref):
    _emit_kv(ckv_ref[...], kr_ref[...], wk_ref, wv_ref, k_ref, v_ref)


def _kv_cache_call(ckv, krope_pad, W):
    m = ckv.shape[0]
    const = lambda i: (0, 0)
    return pl.pallas_call(
        _kv_cache_kernel,
        grid=(1,),
        in_specs=[
            pl.BlockSpec((m, KV_LORA), const),
            pl.BlockSpec((m, LANES), const),
            pl.BlockSpec((KV_LORA, N_HEADS * LANES), const),
            pl.BlockSpec((KV_LORA, HEAD_PAIRS * LANES), const),
        ],
        out_specs=[
            pl.BlockSpec((N_HEADS, m, LANES), lambda i: (0, 0, 0)),
            pl.BlockSpec((HEAD_PAIRS, m, LANES), lambda i: (0, 0, 0)),
        ],
        out_shape=[
            jax.ShapeDtypeStruct((N_HEADS, m, LANES), BF16),
            jax.ShapeDtypeStruct((HEAD_PAIRS, m, LANES), BF16),
        ],
        compiler_params=_params(("arbitrary",)),
        name="kv_cache",
    )(ckv, krope_pad, W["wk"], W["wv"])


def _attn_kernel(q_ref, *refs, n_pairs):
    o_ref, kv = refs[-1], refs[:-1]
    parts = [(kv[i], kv[i + 1]) for i in range(0, len(kv), 2)]
    heads = [(p, e) for p in range(n_pairs) for e in range(2)]

    def scores(p, e):
        q = q_ref[2 * p + e]
        return [lax.dot_general(q, k_ref[2 * p + e], (((1,), (1,)), ((), ())), preferred_element_type=F32)
                for k_ref, _ in parts]

    outs = []
    s_next = scores(*heads[0])
    for i, (p, e) in enumerate(heads):
        s = s_next
        if i + 1 < len(heads):
            s_next = scores(*heads[i + 1])
        m = functools.reduce(jnp.maximum, [jnp.max(x, axis=-1, keepdims=True) for x in s])
        w = [jnp.exp2(x - m) for x in s]
        l = sum(jnp.sum(x, axis=-1, keepdims=True) for x in w)
        pv = sum(_dot(x.astype(BF16), v_ref[p]) for x, (_, v_ref) in zip(w, parts))
        outs.append(pv / l)
        if e == 1:
            lane = lax.broadcasted_iota(jnp.int32, pv.shape, 1)
            o_ref[:, p * LANES:(p + 1) * LANES] = jnp.where(lane < V_HEAD, outs[-2], outs[-1]).astype(BF16)


def _attn_call(q, kv_parts, q_tile, n_pairs):
    _, b, t, _ = q.shape
    nq = t // q_tile
    in_specs = [pl.BlockSpec((2 * n_pairs, None, q_tile, LANES), lambda bi, g, qi: (g, bi, qi, 0))]
    operands = [q]
    for k, v in kv_parts:
        s = k.shape[2]
        in_specs += [pl.BlockSpec((2 * n_pairs, None, s, LANES), lambda bi, g, qi: (g, bi, 0, 0)),
                     pl.BlockSpec((n_pairs, None, s, LANES), lambda bi, g, qi: (g, bi, 0, 0))]
        operands += [k, v]
    return pl.pallas_call(
        functools.partial(_attn_kernel, n_pairs=n_pairs),
        grid=(b, HEAD_PAIRS // n_pairs, nq),
        in_specs=in_specs,
        out_specs=pl.BlockSpec((q_tile, n_pairs * LANES), lambda bi, g, qi: (bi * nq + qi, g)),
        out_shape=jax.ShapeDtypeStruct((b * t, N_HEADS * V_HEAD), BF16),
        compiler_params=_params(("arbitrary", "arbitrary", "arbitrary"), ATTN_VMEM),
        name=f"attn_{len(kv_parts)}part",
    )(*operands)


def _pool_kernel(x_ref, mod_ref, nw_ref, w_ref, sc_ref, o_ref, band_scr, h_scr, hi_scr, lo_scr, y_scr, *, seq):
    @pl.when(pl.program_id(0) == 0)
    def _():
        r = lax.broadcasted_iota(jnp.int32, (seq, seq), 0)
        c = lax.broadcasted_iota(jnp.int32, (seq, seq), 1)
        for g, win in enumerate(POOL_WINDOWS):
            inside = (c >= r - win // 2) & (c < r + win - win // 2)
            band_scr[g] = jnp.where(inside, 1.0, 0.0).astype(BF16)

    items = [(s, g) for s in range(ROW_TILE // seq) for g in range(N_GROUPS)]
    rows = lambda s: slice(s * seq, (s + 1) * seq)
    cols = lambda g: slice(g * GROUP, (g + 1) * GROUP)
    t = lax.broadcasted_iota(jnp.int32, (seq, 1), 0)
    counts = [(jnp.minimum(t + (win - win // 2), seq) - jnp.maximum(t - win // 2, 0)).astype(F32)
              for win in POOL_WINDOWS]

    def norm_seq(s):
        h = _norm_mod(x_ref[rows(s), :], nw_ref[...], mod_ref[...], 0)
        hi = h.astype(BF16)
        h_scr[rows(s), :] = h
        hi_scr[rows(s), :] = hi
        lo_scr[rows(s), :] = (h - hi.astype(F32)).astype(BF16)

    def window_sum(s, g):
        return _dot(band_scr[g], hi_scr[rows(s), cols(g)]) + _dot(band_scr[g], lo_scr[rows(s), cols(g)])

    norm_seq(0)
    tot = window_sum(*items[0])
    for i, (s, g) in enumerate(items):
        tot_next = None
        if i + 1 < len(items):
            s2, g2 = items[i + 1]
            if s2 != s:
                norm_seq(s2)
            tot_next = window_sum(s2, g2)
        pooled = (tot / counts[g] - h_scr[rows(s), cols(g)]).astype(BF16)
        y_scr[rows(s), cols(g)] = _dot(pooled, w_ref[g].astype(BF16))
        tot = tot_next
    for r in range(ROW_TILE // ROW_BLOCK):
        blk = slice(r * ROW_BLOCK, (r + 1) * ROW_BLOCK)
        o_ref[blk, :] = x_ref[blk, :] + mod_ref[2:3, :] * (y_scr[blk, :] * sc_ref[...])


def _pool_call(x, mods, li, cond_of_row, seq, P):
    m = x.shape[0]
    lj = li // 4
    return pl.pallas_call(
        functools.partial(_pool_kernel, seq=seq),
        grid=(m // ROW_TILE,),
        in_specs=[
            pl.BlockSpec((ROW_TILE, D_MODEL), lambda i: (i, 0)),
            _mod_spec(li, cond_of_row, ROW_TILE, 1),
            pl.BlockSpec((None, 1, D_MODEL), lambda i: (li, 0, 0)),
            pl.BlockSpec((None, N_GROUPS, GROUP, GROUP), lambda i: (lj, 0, 0, 0)),
            pl.BlockSpec((None, 1, D_MODEL), lambda i: (lj, 0, 0)),
        ],
        out_specs=pl.BlockSpec((ROW_TILE, D_MODEL), lambda i: (i, 0)),
        out_shape=jax.ShapeDtypeStruct((m, D_MODEL), F32),
        scratch_shapes=[pltpu.VMEM((N_GROUPS, seq, seq), BF16), pltpu.VMEM((ROW_TILE, D_MODEL), F32),
                        pltpu.VMEM((ROW_TILE, D_MODEL), BF16), pltpu.VMEM((ROW_TILE, D_MODEL), BF16),
                        pltpu.VMEM((ROW_TILE, D_MODEL), F32)],
        compiler_params=_params(("arbitrary",)),
        name=f"pool_t{seq}",
    )(x, mods, P["norm1_w"], P["pool_w"], P["pool_scale"])


def _dft_tables(seq):
    def cs(n):
        k = np.arange(n, dtype=np.int64)
        ang = 2.0 * np.pi * ((k[:, None] * k[None, :]) % n) / n
        return np.cos(ang), np.sin(ang)
    ct, st = cs(seq)
    cc, sc = cs(GROUP)
    e_t = jnp.asarray(np.concatenate([ct, st], axis=0), dtype=F32)
    e_c = jnp.asarray(np.concatenate([cc, -sc], axis=0), dtype=F32)
    return e_t.astype(BF16), e_c.astype(BF16)


def _fourier_kernel(x_ref, mod_ref, nw_ref, et_ref, ec_ref, w_ref, o_ref, h_scr, f_scr, wb_scr, *, seq):
    norm = np.float32(1.0 / math.sqrt(seq * GROUP))
    items = [(s, g) for s in range(ROW_TILE // seq) for g in range(N_GROUPS)]
    rows = lambda s: slice(s * seq, (s + 1) * seq)
    cols = lambda g: slice(g * GROUP, (g + 1) * GROUP)

    def norm_seq(s):
        h_scr[rows(s), :] = _norm_mod(x_ref[rows(s), :], nw_ref[...], mod_ref[...], 0).astype(BF16)

    def time_dft(s, g):
        return _dot(et_ref[...], h_scr[rows(s), cols(g)]).astype(BF16)

    @pl.when(pl.program_id(0) == 0)
    def _():
        wb_scr[...] = w_ref[...].astype(BF16)

    norm_seq(0)
    y = time_dft(*items[0])
    for i, (s, g) in enumerate(items):
        y_next = None
        if i + 1 < len(items):
            s2, g2 = items[i + 1]
            if s2 != s:
                norm_seq(s2)
            y_next = time_dft(s2, g2)
        f = _dot(y[:seq], ec_ref[:GROUP, :]) + _dot(y[seq:], ec_ref[GROUP:, :])
        f_scr[rows(s), cols(g)] = (f * norm).astype(BF16)
        y = y_next
    for r in range(ROW_TILE // ROW_BLOCK):
        blk = slice(r * ROW_BLOCK, (r + 1) * ROW_BLOCK)
        o_ref[blk, :] = x_ref[blk, :] + mod_ref[2:3, :] * _dot(f_scr[blk, :], wb_scr[...])


def _fourier_call(x, mods, li, cond_of_row, seq, P):
    m = x.shape[0]
    lj = li // 4
    e_t, e_c = _dft_tables(seq)
    return pl.pallas_call(
        functools.partial(_fourier_kernel, seq=seq),
        grid=(m // ROW_TILE,),
        in_specs=[
            pl.BlockSpec((ROW_TILE, D_MODEL), lambda i: (i, 0)),
            _mod_spec(li, cond_of_row, ROW_TILE, 1),
            pl.BlockSpec((None, 1, D_MODEL), lambda i: (li, 0, 0)),
            pl.BlockSpec((2 * seq, seq), lambda i: (0, 0)),
            pl.BlockSpec((2 * GROUP, GROUP), lambda i: (0, 0)),
            pl.BlockSpec((None, D_MODEL, D_MODEL), lambda i: (lj, 0, 0)),
        ],
        out_specs=pl.BlockSpec((ROW_TILE, D_MODEL), lambda i: (i, 0)),
        out_shape=jax.ShapeDtypeStruct((m, D_MODEL), F32),
        scratch_shapes=[pltpu.VMEM((ROW_TILE, D_MODEL), BF16), pltpu.VMEM((ROW_TILE, D_MODEL), BF16),
                        pltpu.VMEM((D_MODEL, D_MODEL), BF16)],
        compiler_params=_params(("arbitrary",)),
        name=f"fourier_t{seq}",
    )(x, mods, P["norm1_w"], e_t, e_c, P["fnet_w"])


def _rope_tables(rows):
    r, col = jnp.meshgrid(jnp.arange(rows), jnp.arange(GRID_W), indexing="ij")
    r = r.reshape(-1).astype(F32)
    col = col.reshape(-1).astype(F32)
    n_freq = QK_ROPE // 4
    inv = ROPE_THETA ** (-jnp.arange(n_freq, dtype=F32) / n_freq)
    ang = jnp.concatenate([r[:, None] * inv, col[:, None] * inv], axis=-1)
    cos = jnp.repeat(jnp.cos(ang), 2, axis=-1)
    sin = jnp.repeat(jnp.sin(ang), 2, axis=-1)
    t = cos.shape[0]
    pad = LANES - QK_NOPE - QK_ROPE
    cos_p = jnp.concatenate([jnp.ones((t, QK_NOPE), F32), cos, jnp.ones((t, pad), F32)], axis=-1)
    sin_p = jnp.concatenate([jnp.zeros((t, QK_NOPE), F32), sin, jnp.zeros((t, pad), F32)], axis=-1)
    return cos_p, sin_p


def _pair_swap(w):
    wr = w.reshape(w.shape[:-1] + (w.shape[-1] // 2, 2))
    return jnp.stack([-wr[..., 1], wr[..., 0]], axis=-1).reshape(w.shape)


def _mla_weights(P, j, rope, seq_rows):
    pad = LANES - QK_NOPE - QK_ROPE
    wuq = P["mla_wuq"][j].reshape(Q_LORA, N_HEADS, QK_NOPE + QK_ROPE)
    q_slots = jnp.pad(wuq, ((0, 0), (0, 0), (0, pad))).reshape(Q_LORA, N_HEADS * LANES)
    wdkv = P["mla_wdkv"][j]
    kr_slot = jnp.pad(wdkv[:, KV_LORA:], ((0, 0), (QK_NOPE, pad)))
    parts1 = [P["mla_wdq"][j], wdkv[:, :KV_LORA], kr_slot]
    parts2 = [q_slots]
    if rope:
        parts1.append(jnp.pad(_pair_swap(wdkv[:, KV_LORA:]), ((0, 0), (QK_NOPE, pad))))
        q_sw = jnp.pad(_pair_swap(wuq[:, :, QK_NOPE:]), ((0, 0), (0, 0), (QK_NOPE, pad)))
        parts2.append(q_sw.reshape(Q_LORA, N_HEADS * LANES))
        cos, sin = _rope_tables(seq_rows // GRID_W)
    else:
        cos = sin = jnp.zeros((PROJ_TILE, LANES), F32)
    wukv = P["mla_wukv"][j].reshape(KV_LORA, N_HEADS, QK_NOPE + V_HEAD)
    wk = jnp.pad(wukv[:, :, :QK_NOPE], ((0, 0), (0, 0), (0, LANES - QK_NOPE)))
    return dict(
        norm1_w=P["norm1_w"], q_norm=P["mla_q_norm"][j][None, :], kv_norm=P["mla_kv_norm"][j][None, :],
        w1=jnp.concatenate(parts1, axis=1).astype(BF16), w2=jnp.concatenate(parts2, axis=1).astype(BF16),
        wk=wk.reshape(KV_LORA, N_HEADS * LANES).astype(BF16),
        wv=wukv[:, :, QK_NOPE:].reshape(KV_LORA, N_HEADS * V_HEAD).astype(BF16), cos=cos, sin=sin)


def _mla_layer(x, mods, li, cond_of_row, batch, seq, P, cache):
    j = li // 4
    W = _mla_weights(P, j, cache is not None, seq)
    q, k, v, ckv, krope = _mla_proj_call(x, mods, li, cond_of_row, W, cache is not None)
    q = q.reshape(N_HEADS, batch, seq, LANES)
    k = k.reshape(N_HEADS, batch, seq, LANES)
    v = v.reshape(HEAD_PAIRS, batch, seq, LANES)
    if cache is None:
        o = _attn_call(q, [(k, v)], seq, HEAD_PAIRS)
    else:
        c_ckv, c_krope = cache
        past = c_ckv.shape[1]
        pad = LANES - QK_NOPE - QK_ROPE
        kr_pad = jnp.pad(c_krope.reshape(batch * past, QK_ROPE), ((0, 0), (QK_NOPE, pad)))
        kc, vc = _kv_cache_call(c_ckv.reshape(batch * past, KV_LORA), kr_pad, W)
        cached = (kc.reshape(N_HEADS, batch, past, LANES), vc.reshape(HEAD_PAIRS, batch, past, LANES))
        o = _attn_call(q, [cached, (k, v)], 512, 4)
    return o, ckv, krope


def _mixer(x, mods, li, cond_of_row, batch, seq, P, cache):
    kind = li % 4
    if kind == 0:
        o, ckv, krope = _mla_layer(x, mods, li, cond_of_row, batch, seq, P, cache)
        return x, (o, P["mla_wo_bf16"], li // 4), (ckv, krope)
    if kind == 1:
        return _pool_call(x, mods, li, cond_of_row, seq, P), None, None
    if kind == 2:
        return _fourier_call(x, mods, li, cond_of_row, seq, P), None, None
    return _sconv_call(x, mods, li, cond_of_row, seq, P), None, None


def _run_passes(passes, mods, P):
    new_cache = None
    for li in range(DEPTH):
        pending = []
        for ps in passes:
            ps["x"], pre, made = _mixer(ps["x"], mods, li, ps["cond_of_row"], ps["batch"], ps["seq"], P,
                                        ps["cache"])
            pending.append(pre)
            if made is not None and ps["cache"] is None:
                new_cache = made
        xs, outs = lax.optimization_barrier(([ps["x"] for ps in passes],
                                              [None if pre is None else pre[0] for pre in pending]))
        pending = [None if pre is None else (o,) + pre[1:] for pre, o in zip(pending, outs)]
        xs = [_ffn_call(x, mods, li, ps["cond_of_row"], ps["seq"], P, pre, final_norm=(li == DEPTH - 1))
              for x, ps, pre in zip(xs, passes, pending)]
        xs = lax.optimization_barrier(xs)
        for ps, x in zip(passes, xs):
            ps["x"] = x
    return new_cache


def kernel(x_prompt, x_sample, cache_ckv, cache_krope, c, c_ctx, norm1_w, norm2_w, ada_w, ada_b, mla_wdq, mla_q_norm, mla_wuq, mla_wdkv, mla_kv_norm, mla_wukv, mla_wo, pool_w, pool_scale, fnet_w, sconv_win, sconv_conv, sconv_wout, ffn_up, ffn_conv_w, ffn_conv_b, ffn_down, final_norm_w):
    row = lambda w: w.reshape(w.shape[0], 1, w.shape[1])
    P = dict(norm1_w=row(norm1_w), norm2_w=row(norm2_w), mla_wdq=mla_wdq, mla_q_norm=mla_q_norm,
             mla_wuq=mla_wuq, mla_wdkv=mla_wdkv, mla_kv_norm=mla_kv_norm, mla_wukv=mla_wukv,
             mla_wo_bf16=mla_wo.astype(BF16),
             pool_w=pool_w, pool_scale=row(pool_scale), fnet_w=fnet_w, sconv_win=sconv_win,
             sconv_conv=sconv_conv, sconv_wout=sconv_wout, ffn_conv_w=ffn_conv_w,
             ffn_conv_b=ffn_conv_b, ffn_down=ffn_down, final_norm_w=final_norm_w,
             ffn_w=functools.cache(lambda l: _layer_weights(ffn_up, ffn_down, l, 2, f"ffn{l}_weights")),
             sconv_w=functools.cache(lambda l: _layer_weights(sconv_win, sconv_wout, l, 3, f"sconv{l}_weights")))
    b_ctx, t_ctx, _ = x_prompt.shape
    b_lat, t_lat, _ = x_sample.shape
    assert ROW_TILE % t_ctx == 0 and t_lat % ROW_TILE == 0 and b_lat + 1 <= N_COND

    conds = jnp.concatenate([c_ctx[None, :], c, jnp.zeros((N_COND - 1 - b_lat, D_MODEL), F32)], axis=0)
    mods = _ada_call(conds, ada_w, ada_b).reshape(DEPTH, N_COND, 6, D_MODEL)

    ctx = dict(x=x_prompt.reshape(b_ctx * t_ctx, D_MODEL), cond_of_row=lambda r: 0, batch=b_ctx, seq=t_ctx,
               cache=None)
    lat = dict(x=x_sample.reshape(b_lat * t_lat, D_MODEL), cond_of_row=lambda r: 1 + r // t_lat, batch=b_lat,
               seq=t_lat, cache=(cache_ckv[:, 0], cache_krope[:, 0]))
    ckv, krope = _run_passes([ctx, lat], mods, P)
    return (ctx["x"].reshape(b_ctx, t_ctx, D_MODEL), lat["x"].reshape(b_lat, t_lat, D_MODEL),
            ckv.reshape(b_ctx, 1, t_ctx, KV_LORA), krope.reshape(b_ctx, 1, t_ctx, QK_ROPE))
```
